```python
import jax, jax.numpy as jnp
from jax import lax
import numpy as np

D_MODEL = 2048
BATCH = 2
SEQ = 8192
DEPTH = 4

N_MIXERS = 3
N_SG_LAYERS = (DEPTH + 2) // 3
N_POOL_LAYERS = (DEPTH + 1) // 3
N_HGRN_LAYERS = DEPTH // 3

SG_WIDTH = D_MODEL
SG_HEADS = 16
SG_HEAD_DIM = SG_WIDTH // SG_HEADS
SG_CHUNK = 128
POOL_WINDOWS = (2, 4, 8, 16)
POOL_GROUPS = len(POOL_WINDOWS)
POOL_GROUP_DIM = D_MODEL // POOL_GROUPS
HGRN_EXPAND = 128
HGRN_HEADS = D_MODEL // HGRN_EXPAND
HGRN_HEAD_DIM = HGRN_EXPAND
HGRN_CHUNK = 64
N_EXPERTS = 32
TOP_K = 4
EXPERT_DIM = 768
SWIGLU_ALPHA = 1.702
SWIGLU_LIMIT = 7.0
MOE_BLOCK = 256
RMS_EPS = 1e-5
LN_EPS = 1e-5

kernel_name = "hybrid_gmlp_pool_hgrn2_moe_adaln"


def rmsnorm(x, g):
    xf = x.astype(jnp.float32)
    y = xf * lax.rsqrt(jnp.mean(xf * xf, axis=-1, keepdims=True) + RMS_EPS)
    return (y * g.astype(jnp.float32)).astype(x.dtype)


def layernorm(x, g, b):
    xf = x.astype(jnp.float32)
    mu = jnp.mean(xf, axis=-1, keepdims=True)
    xc = xf - mu
    var = jnp.mean(xc * xc, axis=-1, keepdims=True)
    return (xc * lax.rsqrt(var + LN_EPS) * g.astype(jnp.float32) + b.astype(jnp.float32)).astype(x.dtype)


def modulate(x, g, ada_w, c):
    mod = jnp.einsum('bd,de->be', jax.nn.silu(c), ada_w)
    shift, scale, gate = jnp.split(mod, 3, axis=-1)
    h = rmsnorm(x, g) * (1 + scale[:, None, :]) + shift[:, None, :]
    return h, gate[:, None, :]


def spatial_gating_mixer(h, w_in, ln_g, ln_b, w_s, b_s, w_out):
    B, S, _ = h.shape
    z = jax.nn.gelu(h @ w_in, approximate=False)
    u, v = jnp.split(z, 2, axis=-1)
    v = layernorm(v, ln_g, ln_b)
    v = v.reshape(B, S // SG_CHUNK, SG_CHUNK, SG_HEADS, SG_HEAD_DIM)
    causal = jnp.tril(jnp.ones((SG_CHUNK, SG_CHUNK), dtype=bool))
    w = jnp.where(causal[None], w_s, 0)
    v = jnp.einsum('hts,bnshd->bnthd', w, v) + jnp.transpose(b_s)[None, None, :, :, None]
    y = u * v.reshape(B, S, SG_WIDTH)
    return y @ w_out


def trailing_mean_minus_self(z, window):
    S = z.shape[1]
    cs = lax.cumsum(z, axis=1)
    lag = jnp.pad(cs, ((0, 0), (window, 0), (0, 0)))[:, :S]
    count = jnp.minimum(jnp.arange(1, S + 1), window).astype(z.dtype)
    return (cs - lag) / count[None, :, None] - z


def pooling_mixer(h, w_in, w_grp, ls, w_out):
    B, S, _ = h.shape
    z = (h @ w_in).astype(jnp.float32).reshape(B, S, POOL_GROUPS, POOL_GROUP_DIM)
    pooled = jnp.stack([trailing_mean_minus_self(z[:, :, gi], wnd) for gi, wnd in enumerate(POOL_WINDOWS)], axis=2)
    y = jnp.einsum('bsgi,gio->bsgo', pooled.astype(h.dtype), w_grp).reshape(B, S, D_MODEL) * ls
    return y @ w_out


def hgrn2_mixer(h, w_in, lb, gnorm_g, w_out):
    B, S, _ = h.shape
    q, f, i, g = jnp.split(h @ w_in, 4, axis=-1)
    q = jax.nn.silu(q.astype(jnp.float32))
    log_f = jnp.logaddexp(jnp.log(lb), jnp.log1p(-lb) + jax.nn.log_sigmoid(f.astype(jnp.float32)))
    k = -jnp.expm1(log_f)
    n_chunks = S // HGRN_CHUNK

    def to_chunks(t):
        return t.reshape(B, n_chunks, HGRN_CHUNK, HGRN_HEADS, HGRN_HEAD_DIM).transpose(1, 0, 3, 2, 4)

    qc, kc, gc, vc = to_chunks(q), to_chunks(k), to_chunks(log_f), to_chunks(i.astype(jnp.float32))
    mask = jnp.tril(jnp.ones((HGRN_CHUNK, HGRN_CHUNK), dtype=bool))[:, :, None]

    def step(state, inp):
        qt, kt, gt, vt = inp
        b = jnp.cumsum(gt, axis=2)
        o_inter = jnp.einsum('bhtk,bhkv->bhtv', qt * jnp.exp(b), state)
        decay = jnp.exp(jnp.where(mask, b[:, :, :, None, :] - b[:, :, None, :, :], -jnp.inf))
        scores = jnp.einsum('bhtsk,bhsk->bhts', qt[:, :, :, None, :] * decay, kt)
        o = o_inter + jnp.einsum('bhts,bhsv->bhtv', scores, vt)
        b_end = b[:, :, -1:, :]
        state = jnp.exp(b_end[:, :, 0, :])[..., None] * state + jnp.einsum('bhsk,bhsv->bhkv', kt * jnp.exp(b_end - b), vt)
        return state, o

    state0 = jnp.zeros((B, HGRN_HEADS, HGRN_HEAD_DIM, HGRN_HEAD_DIM), jnp.float32)
    _, o = lax.scan(step, state0, (qc, kc, gc, vc))
    o = o.transpose(1, 0, 3, 2, 4).reshape(B, S, HGRN_HEADS, HGRN_HEAD_DIM)
    o = rmsnorm(o, gnorm_g).reshape(B, S, D_MODEL)
    o = (o * jax.nn.silu(g.astype(jnp.float32))).astype(h.dtype)
    return o @ w_out


def clamped_swiglu(a, l):
    a = jnp.minimum(a, SWIGLU_LIMIT)
    l = jnp.clip(l, -SWIGLU_LIMIT, SWIGLU_LIMIT)
    return a * jax.nn.sigmoid(SWIGLU_ALPHA * a) * (l + 1)


def moe_ffn(h, r_w, r_b, w_glu, b_glu, w_lin, b_lin, w_out, b_out):
    B, S, D = h.shape
    T = B * S
    hf = h.reshape(T, D)
    logits = (hf @ r_w + r_b).astype(jnp.float32)
    top_val, top_idx = lax.top_k(logits, TOP_K)
    gates = jax.nn.softmax(top_val, axis=-1)
    flat_e = top_idx.reshape(-1)
    flat_tok = jnp.repeat(jnp.arange(T, dtype=jnp.int32), TOP_K)
    flat_gate = gates.reshape(-1)
    order = jnp.argsort(flat_e)
    e_sorted = flat_e[order]
    counts = jnp.bincount(flat_e, length=N_EXPERTS)
    padded = (counts + MOE_BLOCK - 1) // MOE_BLOCK * MOE_BLOCK
    pad_end = jnp.cumsum(padded)
    pad_start = pad_end - padded
    start = jnp.cumsum(counts) - counts
    dest = pad_start[e_sorted] + jnp.arange(T * TOP_K) - start[e_sorted]
    n_blocks = -(-(T * TOP_K + N_EXPERTS * (MOE_BLOCK - 1)) // MOE_BLOCK)
    n_pad = n_blocks * MOE_BLOCK
    tok_buf = jnp.zeros((n_pad,), jnp.int32).at[dest].set(flat_tok[order])
    gate_buf = jnp.zeros((n_pad,), jnp.float32).at[dest].set(flat_gate[order])
    blk_e = jnp.minimum(jnp.searchsorted(pad_end, jnp.arange(n_blocks) * MOE_BLOCK, side='right'), N_EXPERTS - 1)

    def expert_block(args):
        tok, gate, e = args
        xb = hf[tok]
        act = clamped_swiglu(xb @ w_glu[e] + b_glu[e], xb @ w_lin[e] + b_lin[e])
        y = act @ w_out[e] + b_out[e]
        return y * gate[:, None].astype(y.dtype)

    y = lax.map(expert_block, (tok_buf.reshape(n_blocks, MOE_BLOCK), gate_buf.reshape(n_blocks, MOE_BLOCK), blk_e))
    out = jax.ops.segment_sum(y.reshape(n_pad, D), tok_buf, num_segments=T)
    return out.reshape(B, S, D)


def setup_inputs(seed: int = 0) -> dict:
    key = jax.random.key(seed)
    ks = jax.random.split(key, 32)
    nrm = jax.random.normal
    f32 = jnp.float32
    D = D_MODEL
    return {
        "x": nrm(ks[0], (BATCH, SEQ, D), f32),
        "c": nrm(ks[1], (BATCH, D), f32),
        "norm_g": 1.0 + 0.1 * nrm(ks[2], (DEPTH, 2, D), f32),
        "ada_w": 0.5 * D ** -0.5 * nrm(ks[3], (DEPTH, 2, D, 3 * D), f32),
        "final_norm_g": 1.0 + 0.1 * nrm(ks[4], (D,), f32),
        "sg_w_in": D ** -0.5 * nrm(ks[5], (N_SG_LAYERS, D, 2 * SG_WIDTH), f32),
        "sg_ln_g": 1.0 + 0.1 * nrm(ks[6], (N_SG_LAYERS, SG_WIDTH), f32),
        "sg_ln_b": 0.1 * nrm(ks[7], (N_SG_LAYERS, SG_WIDTH), f32),
        "sg_w_s": SG_CHUNK ** -0.5 * nrm(ks[8], (N_SG_LAYERS, SG_HEADS, SG_CHUNK, SG_CHUNK), f32),
        "sg_b_s": 1.0 + 0.1 * nrm(ks[9], (N_SG_LAYERS, SG_HEADS, SG_CHUNK), f32),
        "sg_w_out": SG_WIDTH ** -0.5 * nrm(ks[10], (N_SG_LAYERS, SG_WIDTH, D), f32),
        "pool_w_in": D ** -0.5 * nrm(ks[11], (N_POOL_LAYERS, D, D), f32),
        "pool_w_grp": POOL_GROUP_DIM ** -0.5 * nrm(ks[12], (N_POOL_LAYERS, POOL_GROUPS, POOL_GROUP_DIM, POOL_GROUP_DIM), f32),
        "pool_ls": 1.0 + 0.1 * nrm(ks[13], (N_POOL_LAYERS, D), f32),
        "pool_w_out": D ** -0.5 * nrm(ks[14], (N_POOL_LAYERS, D, D), f32),
        "hgrn_w_in": D ** -0.5 * nrm(ks[15], (N_HGRN_LAYERS, D, 4 * D), f32),
        "hgrn_lb_logits": nrm(ks[16], (DEPTH, D), f32),
        "hgrn_gnorm_g": 1.0 + 0.1 * nrm(ks[17], (N_HGRN_LAYERS, HGRN_HEAD_DIM), f32),
        "hgrn_w_out": D ** -0.5 * nrm(ks[18], (N_HGRN_LAYERS, D, D), f32),
        "router_w": D ** -0.5 * nrm(ks[19], (DEPTH, D, N_EXPERTS), f32),
        "router_b": 0.01 * nrm(ks[20], (DEPTH, N_EXPERTS), f32),
        "expert_w_glu": D ** -0.5 * nrm(ks[21], (DEPTH, N_EXPERTS, D, EXPERT_DIM), f32),
        "expert_b_glu": 0.02 * nrm(ks[22], (DEPTH, N_EXPERTS, EXPERT_DIM), f32),
        "expert_w_lin": D ** -0.5 * nrm(ks[23], (DEPTH, N_EXPERTS, D, EXPERT_DIM), f32),
        "expert_b_lin": 0.02 * nrm(ks[24], (DEPTH, N_EXPERTS, EXPERT_DIM), f32),
        "expert_w_out": EXPERT_DIM ** -0.5 * nrm(ks[25], (DEPTH, N_EXPERTS, EXPERT_DIM, D), f32),
        "expert_b_out": 0.02 * nrm(ks[26], (DEPTH, N_EXPERTS, D), f32),
    }


def reference(x, c, norm_g, ada_w, final_norm_g,
              sg_w_in, sg_ln_g, sg_ln_b, sg_w_s, sg_b_s, sg_w_out,
              pool_w_in, pool_w_grp, pool_ls, pool_w_out,
              hgrn_w_in, hgrn_lb_logits, hgrn_gnorm_g, hgrn_w_out,
              router_w, router_b, expert_w_glu, expert_b_glu, expert_w_lin, expert_b_lin, expert_w_out, expert_b_out):
    lb_cum = jnp.cumsum(jax.nn.softmax(hgrn_lb_logits.astype(jnp.float32), axis=0), axis=0)
    lower_bounds = lb_cum - lb_cum[0]
    for layer in range(DEPTH):
        kind, slot = layer % N_MIXERS, layer // N_MIXERS
        h, gate = modulate(x, norm_g[layer, 0], ada_w[layer, 0], c)
        if kind == 0:
            y = spatial_gating_mixer(h, sg_w_in[slot], sg_ln_g[slot], sg_ln_b[slot], sg_w_s[slot], sg_b_s[slot], sg_w_out[slot])
        elif kind == 1:
            y = pooling_mixer(h, pool_w_in[slot], pool_w_grp[slot], pool_ls[slot], pool_w_out[slot])
        else:
            y = hgrn2_mixer(h, hgrn_w_in[slot], lower_bounds[layer], hgrn_gnorm_g[slot], hgrn_w_out[slot])
        x = x + gate * y
        h, gate = modulate(x, norm_g[layer, 1], ada_w[layer, 1], c)
        x = x + gate * moe_ffn(h, router_w[layer], router_b[layer], expert_w_glu[layer], expert_b_glu[layer],
                               expert_w_lin[layer], expert_b_lin[layer], expert_w_out[layer], expert_b_out[layer])
    return rmsnorm(x, final_norm_g)
```

```python
import functools

import numpy as np
import jax
import jax.numpy as jnp
from jax import lax
from jax.experimental import pallas as pl
from jax.experimental.pallas import tpu as pltpu
from jax.experimental.pallas import tpu_sc as plsc

F32 = jnp.float32
BF16 = jnp.bfloat16
I32 = jnp.int32
U32 = jnp.uint32

RMS_EPS = 1e-5
LN_EPS = 1e-5
SWIGLU_ALPHA = 1.702
SWIGLU_LIMIT = 7.0
TOP_K = 4
SG_CHUNK = 128
POOL_WINDOWS = (2, 4, 8, 16)
HEAD_DIM = 128
LANES = 128
SC_WORKERS = 32
SC_ROWS_PER_STEP = 64

VMEM_LIMIT = 48 * 1024 * 1024


def _cparams(sem):
    return pltpu.CompilerParams(dimension_semantics=sem, vmem_limit_bytes=VMEM_LIMIT)


def _dot(a, b):
    return jnp.dot(a, b, preferred_element_type=F32)


def _dot_nt(a, b):
    return lax.dot_general(a, b, (((1,), (1,)), ((), ())), preferred_element_type=F32)


def _dot_tn(a, b):
    return lax.dot_general(a, b, (((0,), (0,)), ((), ())), preferred_element_type=F32)


def _split_bf16(x):
    hi = x.astype(BF16)
    lo = (x - hi.astype(F32)).astype(BF16)
    return hi, lo


def _pack_halves(y):
    n = y.shape[1] // 2
    lo = lax.bitcast_convert_type(y[:, :n].astype(BF16).astype(F32), U32)
    hi = lax.bitcast_convert_type(y[:, n:].astype(BF16).astype(F32), U32)
    word = lax.shift_right_logical(lo, jnp.uint32(16)) | (hi & jnp.uint32(0xFFFF0000))
    return lax.bitcast_convert_type(word, I32)


def _unpack_halves(w):
    u = lax.bitcast_convert_type(w, U32)
    lo = lax.bitcast_convert_type(lax.shift_left(u, jnp.uint32(16)), F32)
    hi = lax.bitcast_convert_type(u & jnp.uint32(0xFFFF0000), F32)
    return lo, hi


def _modulated_norm(x, g, scale, shift):
    ms = jnp.mean(x * x, axis=-1, keepdims=True)
    y = x * lax.rsqrt(ms + RMS_EPS) * g
    return y * (1.0 + scale) + shift


def _ada_kernel(c_ref, w_ref, o_ref):
    c = c_ref[...]
    s = c * jax.nn.sigmoid(c)
    o_ref[0] = _dot(s.astype(BF16), w_ref[0].astype(BF16))


def _ada_mod(c, ada_w):
    L, D, N = ada_w.shape
    B = c.shape[0]
    rows = 8
    tn = 512
    c_pad = jnp.zeros((rows, D), F32).at[:B].set(c)
    out = pl.pallas_call(
        _ada_kernel,
        grid=(L, N // tn),
        in_specs=[pl.BlockSpec((rows, D), lambda l, j: (0, 0)),
                  pl.BlockSpec((1, D, tn), lambda l, j: (l, 0, j))],
        out_specs=pl.BlockSpec((1, rows, tn), lambda l, j: (l, 0, j)),
        out_shape=jax.ShapeDtypeStruct((L, rows, N), F32),
        compiler_params=_cparams(("parallel", "parallel")),
        name="ada_mod",
    )(c_pad, ada_w)
    return out[:, :B]


def _nm_kernel(x_ref, g_ref, sc_ref, sh_ref, w_ref, o_ref, h_ref, *, act):
    @pl.when(pl.program_id(1) == 0)
    def _():
        h = _modulated_norm(x_ref[...], g_ref[...], sc_ref[0], sh_ref[0])
        h_ref[...] = h.astype(BF16)

    acc = _dot(h_ref[...], w_ref[...])
    if act == "gelu":
        acc = 0.5 * acc * (1.0 + lax.erf(acc * np.float32(1.0 / np.sqrt(2.0))))
    o_ref[...] = acc.astype(o_ref.dtype)


def _norm_matmul(x, g, scale, shift, w, seq, *, act, out_dtype, tm=512, tn=512):
    T, D = x.shape
    N = w.shape[1]
    bmap = lambda i, j: ((i * tm) // seq, 0, 0)
    return pl.pallas_call(
        functools.partial(_nm_kernel, act=act),
        grid=(T // tm, N // tn),
        in_specs=[pl.BlockSpec((tm, D), lambda i, j: (i, 0)),
                  pl.BlockSpec((1, D), lambda i, j: (0, 0)),
                  pl.BlockSpec((1, 1, D), bmap),
                  pl.BlockSpec((1, 1, D), bmap),
                  pl.BlockSpec((D, tn), lambda i, j: (0, j))],
        out_specs=pl.BlockSpec((tm, tn), lambda i, j: (i, j)),
        out_shape=jax.ShapeDtypeStruct((T, N), out_dtype),
        scratch_shapes=[pltpu.VMEM((tm, D), BF16)],
        compiler_params=_cparams(("parallel", "arbitrary")),
        name="norm_matmul_" + str(act),
    )(x, g, scale, shift, w)


def _mr_kernel(y_ref, w_ref, x_ref, gate_ref, o_ref):
    o_ref[...] = x_ref[...] + gate_ref[0] * _dot(y_ref[...], w_ref[...])


def _matmul_residual(y, w, x, gate, seq, *, tm=512, tn=512):
    T, K = y.shape
    D = w.shape[1]
    return pl.pallas_call(
        _mr_kernel,
        grid=(T // tm, D // tn),
        in_specs=[pl.BlockSpec((tm, K), lambda i, j: (i, 0)),
                  pl.BlockSpec((K, tn), lambda i, j: (0, j)),
                  pl.BlockSpec((tm, tn), lambda i, j: (i, j)),
                  pl.BlockSpec((1, 1, tn), lambda i, j: ((i * tm) // seq, 0, j))],
        out_specs=pl.BlockSpec((tm, tn), lambda i, j: (i, j)),
        out_shape=jax.ShapeDtypeStruct((T, D), F32),
        compiler_params=_cparams(("parallel", "arbitrary")),
        name="matmul_residual",
    )(y, w, x, gate)


def _sg_kernel(u_ref, v_ref, lng_ref, lnb_ref, ws_ref, bst_ref, y_ref, vn_ref, *, heads):
    v = v_ref[...].astype(F32)
    mu = jnp.mean(v, axis=-1, keepdims=True)
    vc = v - mu
    var = jnp.mean(vc * vc, axis=-1, keepdims=True)
    vn_ref[...] = (vc * lax.rsqrt(var + LN_EPS) * lng_ref[...] + lnb_ref[...]).astype(BF16)

    n_chunks = v_ref.shape[0] // SG_CHUNK
    row = lax.broadcasted_iota(I32, (SG_CHUNK, SG_CHUNK), 0)
    col = lax.broadcasted_iota(I32, (SG_CHUNK, SG_CHUNK), 1)
    causal = row >= col
    for hd in range(heads):
        cs = slice(hd * HEAD_DIM, (hd + 1) * HEAD_DIM)
        wm = jnp.where(causal, ws_ref[hd], 0.0).astype(BF16)
        bias = bst_ref[:, hd:hd + 1]
        for ci in range(n_chunks):
            rs = slice(ci * SG_CHUNK, (ci + 1) * SG_CHUNK)
            mixed = _dot(wm, vn_ref[rs, cs]) + bias
            y_ref[rs, cs] = (u_ref[rs, cs].astype(F32) * mixed).astype(BF16)


def _spatial_gate(z, ln_g, ln_b, w_s, b_s, *, tm=256):
    T, two_w = z.shape
    W = two_w // 2
    heads = w_s.shape[0]
    return pl.pallas_call(
        functools.partial(_sg_kernel, heads=heads),
        grid=(T // tm,),
        in_specs=[pl.BlockSpec((tm, W), lambda i: (i, 0)),
                  pl.BlockSpec((tm, W), lambda i: (i, 1)),
                  pl.BlockSpec((1, W), lambda i: (0, 0)),
                  pl.BlockSpec((1, W), lambda i: (0, 0)),
                  pl.BlockSpec((heads, SG_CHUNK, SG_CHUNK), lambda i: (0, 0, 0)),
                  pl.BlockSpec((SG_CHUNK, heads), lambda i: (0, 0))],
        out_specs=pl.BlockSpec((tm, W), lambda i: (i, 0)),
        out_shape=jax.ShapeDtypeStruct((T, W), BF16),
        scratch_shapes=[pltpu.VMEM((tm, W), BF16)],
        compiler_params=_cparams(("parallel",)),
        name="spatial_gate",
    )(z, z, ln_g, ln_b, w_s, jnp.transpose(b_s))


POOL_HALO = 16


def _pool_kernel(z_ref, halo_ref, wg_ref, ls_ref, y_ref, *, seq):
    tm = z_ref.shape[0]
    gdim = wg_ref.shape[1]
    pos0 = (pl.program_id(0) * tm) % seq
    halo_on = (pos0 > 0).astype(F32)
    pos = pos0 + lax.broadcasted_iota(I32, (tm, 1), 0)
    for gi, wnd in enumerate(POOL_WINDOWS):
        cs = slice(gi * gdim, (gi + 1) * gdim)
        zg = z_ref[:, cs]
        s = jnp.concatenate([halo_ref[:, cs] * halo_on, zg], axis=0)
        k = 1
        while k < wnd:
            s = s + pltpu.roll(s, k, 0)
            k *= 2
        cnt = jnp.minimum(pos + 1, wnd).astype(F32)
        pooled = s[POOL_HALO:, :] / cnt - zg
        yg = _dot(pooled.astype(BF16), wg_ref[gi]) * ls_ref[:, cs]
        y_ref[:, cs] = yg.astype(BF16)


def _pool_mix(z, w_grp, ls, seq, *, tm=256):
    T, D = z.shape
    G, gdim, _ = w_grp.shape
    assert max(POOL_WINDOWS) <= POOL_HALO and tm % POOL_HALO == 0
    ratio = tm // POOL_HALO
    return pl.pallas_call(
        functools.partial(_pool_kernel, seq=seq),
        grid=(T // tm,),
        in_specs=[pl.BlockSpec((tm, D), lambda i: (i, 0)),
                  pl.BlockSpec((POOL_HALO, D), lambda i: (jnp.maximum(i * ratio - 1, 0), 0)),
                  pl.BlockSpec((G, gdim, gdim), lambda i: (0, 0, 0)),
                  pl.BlockSpec((1, D), lambda i: (0, 0))],
        out_specs=pl.BlockSpec((tm, D), lambda i: (i, 0)),
        out_shape=jax.ShapeDtypeStruct((T, D), BF16),
        compiler_params=_cparams(("parallel",)),
        name="pool_mix",
    )(z, z, w_grp, ls)


HGRN_CHUNK = 128


def _hgrn_tables(c=HGRN_CHUNK):
    t = np.arange(c)
    sums = [(t[None, :] <= t[:, None]), (t[None, :] > t[:, None])]
    masks = []
    h = c // 2
    while h >= 1:
        off = t % (2 * h)
        mid = (t // (2 * h)) * (2 * h) + h
        second = off >= h
        j = t[None, :]
        m_q = second[:, None] & (j >= mid[:, None]) & (j <= t[:, None])
        m_k = (~second)[:, None] & (j > t[:, None]) & (j < mid[:, None])
        sums.append(m_q | m_k)
        same = (t[:, None] // (2 * h)) == (t[None, :] // (2 * h))
        masks.append(same & second[:, None] & (~second)[None, :])
        h //= 2
    masks.append(t[:, None] == t[None, :])
    return (np.stack(sums).astype(np.float32), np.stack(masks).astype(np.float32))


def _hgrn_kernel(q_ref, f_ref, i_ref, g_ref, lb_ref, gn_ref, sums_ref, masks_ref, o_ref,
                 st_ref, qb_ref, ke_ref, qp_ref, kp_ref, vb_ref, eb_ref, *, heads):
    c = q_ref.shape[0]
    n_lvl = masks_ref.shape[0]

    @pl.when(pl.program_id(1) == 0)
    def _():
        st_ref[...] = jnp.zeros_like(st_ref)

    lb = lb_ref[...]
    sig = jax.nn.sigmoid(f_ref[...])
    log_f = jnp.log(lb + (1.0 - lb) * sig)
    kk = (1.0 - lb) * (1.0 - sig)
    lf_hi, lf_lo = _split_bf16(log_f)
    qr = q_ref[...]
    q = qr * jax.nn.sigmoid(qr)
    vb_ref[...] = i_ref[...].astype(BF16)

    for j in range(n_lvl + 1):
        m = sums_ref[j]
        x = jnp.exp(_dot(m, lf_hi) + _dot(m, lf_lo))
        if j == 0:
            qb_ref[...] = (q * x).astype(BF16)
            eb_ref[...] = x[c - 1:c, :]
        elif j == 1:
            ke_ref[...] = (kk * x).astype(BF16)
        else:
            qp_ref[j - 2] = (q * x).astype(BF16)
            kp_ref[j - 2] = (kk * x).astype(BF16)
    qp_ref[n_lvl - 1] = q.astype(BF16)
    kp_ref[n_lvl - 1] = kk.astype(BF16)

    gn = gn_ref[...]

    def head(hd, carry):
        cs = pl.ds(pl.multiple_of(hd * HEAD_DIM, HEAD_DIM), HEAD_DIM)
        scores = jnp.zeros((c, c), F32)
        for l in range(n_lvl):
            scores = scores + masks_ref[l] * _dot_nt(qp_ref[l, :, cs], kp_ref[l, :, cs])
        st = st_ref[hd]
        vh = vb_ref[:, cs]
        o = _dot_nt(qb_ref[:, cs], st.astype(BF16)) + _dot(scores.astype(BF16), vh)
        st_ref[hd] = st * eb_ref[:, cs] + _dot_tn(vh, ke_ref[:, cs])
        on = o * lax.rsqrt(jnp.mean(o * o, axis=-1, keepdims=True) + RMS_EPS) * gn
        gv = g_ref[:, cs]
        o_ref[:, cs] = (on * (gv * jax.nn.sigmoid(gv))).astype(BF16)
        return carry

    lax.fori_loop(0, heads, head, 0)


def _hgrn_mix(proj, lb, gnorm_g, batch, seq):
    T, four_d = proj.shape
    D = four_d // 4
    heads = D // HEAD_DIM
    c = HGRN_CHUNK
    n_chunks = seq // c
    sums, masks = _hgrn_tables(c)
    n_sum, n_lvl = sums.shape[0], masks.shape[0]
    row = lambda b, i: b * n_chunks + i
    return pl.pallas_call(
        functools.partial(_hgrn_kernel, heads=heads),
        grid=(batch, n_chunks),
        in_specs=[pl.BlockSpec((c, D), lambda b, i: (row(b, i), 0)),
                  pl.BlockSpec((c, D), lambda b, i: (row(b, i), 1)),
                  pl.BlockSpec((c, D), lambda b, i: (row(b, i), 2)),
                  pl.BlockSpec((c, D), lambda b, i: (row(b, i), 3)),
                  pl.BlockSpec((1, D), lambda b, i: (0, 0)),
                  pl.BlockSpec((1, HEAD_DIM), lambda b, i: (0, 0)),
                  pl.BlockSpec((n_sum, c, c), lambda b, i: (0, 0, 0)),
                  pl.BlockSpec((n_lvl, c, c), lambda b, i: (0, 0, 0))],
        out_specs=pl.BlockSpec((c, D), lambda b, i: (row(b, i), 0)),
        out_shape=jax.ShapeDtypeStruct((T, D), BF16),
        scratch_shapes=[pltpu.VMEM((heads, HEAD_DIM, HEAD_DIM), F32),
                        pltpu.VMEM((c, D), BF16),
                        pltpu.VMEM((c, D), BF16),
                        pltpu.VMEM((n_lvl, c, D), BF16),
                        pltpu.VMEM((n_lvl, c, D), BF16),
                        pltpu.VMEM((c, D), BF16),
                        pltpu.VMEM((1, D), F32)],
        compiler_params=_cparams(("parallel", "arbitrary")),
        name="hgrn_mix",
    )(proj, proj, proj, proj, lb, gnorm_g, jnp.asarray(sums, BF16), jnp.asarray(masks, F32))


def _route_kernel(x_ref, g_ref, sc_ref, sh_ref, rw_ref, rb_ref, hp_ref, idx_ref, gate_ref,
                  *, n_experts):
    h = _modulated_norm(x_ref[...], g_ref[...], sc_ref[0], sh_ref[0])
    hp_ref[...] = _pack_halves(h)

    h_hi, h_lo = _split_bf16(h)
    w_hi, w_lo = _split_bf16(rw_ref[...])
    logits = _dot(h_hi, w_hi) + _dot(h_lo, w_hi) + _dot(h_hi, w_lo) + rb_ref[...]
    lt = jnp.transpose(logits)[:n_experts, :]
    eidx = lax.broadcasted_iota(I32, lt.shape, 0)
    vals, idxs = [], []
    for _ in range(TOP_K):
        m = jnp.max(lt, axis=0, keepdims=True)
        sel = jnp.min(jnp.where(lt == m, eidx, n_experts), axis=0, keepdims=True)
        vals.append(m)
        idxs.append(sel)
        lt = jnp.where(eidx == sel, -jnp.inf, lt)
    exps = [jnp.exp(v - vals[0]) for v in vals]
    denom = exps[0]
    for e in exps[1:]:
        denom = denom + e
    idx_ref[...] = jnp.concatenate(idxs, axis=0)
    gate_ref[...] = jnp.concatenate([e / denom for e in exps], axis=0)


def _route(x, g, scale, shift, r_w, r_b, seq, *, tm=512):
    T, D = x.shape
    E = r_w.shape[1]
    rw_pad = jnp.zeros((D, LANES), F32).at[:, :E].set(r_w)
    rb_pad = jnp.full((1, LANES), -1e30, F32).at[0, :E].set(r_b)
    bmap = lambda i: ((i * tm) // seq, 0, 0)
    return pl.pallas_call(
        functools.partial(_route_kernel, n_experts=E),
        grid=(T // tm,),
        in_specs=[pl.BlockSpec((tm, D), lambda i: (i, 0)),
                  pl.BlockSpec((1, D), lambda i: (0, 0)),
                  pl.BlockSpec((1, 1, D), bmap),
                  pl.BlockSpec((1, 1, D), bmap),
                  pl.BlockSpec((D, LANES), lambda i: (0, 0)),
                  pl.BlockSpec((1, LANES), lambda i: (0, 0))],
        out_specs=[pl.BlockSpec((tm, D // 2), lambda i: (i, 0)),
                   pl.BlockSpec((TOP_K, tm), lambda i: (0, i)),
                   pl.BlockSpec((TOP_K, tm), lambda i: (0, i))],
        out_shape=[jax.ShapeDtypeStruct((T, D // 2), I32),
                   jax.ShapeDtypeStruct((TOP_K, T), I32),
                   jax.ShapeDtypeStruct((TOP_K, T), F32)],
        compiler_params=_cparams(("parallel",)),
        name="moe_route",
    )(x, g, scale, shift, rw_pad, rb_pad)


def _sc_gather(table, idx):
    R = idx.shape[0]
    W = table.shape[1]
    rows = SC_ROWS_PER_STEP
    per_worker = R // SC_WORKERS
    steps = per_worker // rows
    assert steps * rows * SC_WORKERS == R
    mesh = plsc.VectorSubcoreMesh(core_axis_name="c", subcore_axis_name="s")
    n_cores = mesh.num_cores

    @functools.partial(
        pl.kernel, mesh=mesh,
        out_type=jax.ShapeDtypeStruct((R, W), table.dtype),
        scratch_types=[pltpu.VMEM((rows,), I32),
                       pltpu.VMEM((rows, W), table.dtype),
                       pltpu.SemaphoreType.DMA],
    )
    def gather(table_hbm, idx_hbm, out_hbm, idx_v, rows_v, sem):
        wid = lax.axis_index("s") * n_cores + lax.axis_index("c")
        base = wid * per_worker

        @pl.loop(0, steps)
        def _(j):
            off = pl.multiple_of(base + j * rows, 8)
            pltpu.sync_copy(idx_hbm.at[pl.ds(off, rows)], idx_v)
            pltpu.async_copy(table_hbm.at[idx_v], rows_v, sem).wait()
            pltpu.sync_copy(rows_v, out_hbm.at[pl.ds(off, rows)])

    return gather(table, idx)


def _expert_kernel(blk_e_ref, n_used_ref, xs_ref, gate_ref, wg_ref, bg_ref, wl_ref, bl_ref,
                   wo_ref, bo_ref, y_ref):
    i = pl.program_id(0)

    @pl.when(i < n_used_ref[0])
    def _():
        half = xs_ref.shape[1]
        lo, hi = _unpack_halves(xs_ref[...])
        lo = lo.astype(BF16)
        hi = hi.astype(BF16)
        a = _dot(lo, wg_ref[0, :half, :]) + _dot(hi, wg_ref[0, half:, :]) + bg_ref[0]
        l = _dot(lo, wl_ref[0, :half, :]) + _dot(hi, wl_ref[0, half:, :]) + bl_ref[0]
        a = jnp.minimum(a, SWIGLU_LIMIT)
        l = jnp.clip(l, -SWIGLU_LIMIT, SWIGLU_LIMIT)
        act = a * jax.nn.sigmoid(SWIGLU_ALPHA * a) * (l + 1.0)
        y = _dot(act.astype(BF16), wo_ref[0]) + bo_ref[0]
        y_ref[...] = _pack_halves(y * gate_ref[...])

    @pl.when(i >= n_used_ref[0])
    def _():
        y_ref[...] = jnp.zeros_like(y_ref)


def _experts(xs, gate_buf, blk_e, n_used, w_glu, b_glu, w_lin, b_lin, w_out, b_out, *, blk):
    P, half = xs.shape
    E, D, F = w_glu.shape
    n_blocks = P // blk
    wmap = lambda i, be, nu: (be[i], 0, 0)
    grid_spec = pltpu.PrefetchScalarGridSpec(
        num_scalar_prefetch=2,
        grid=(n_blocks,),
        in_specs=[pl.BlockSpec((blk, half), lambda i, be, nu: (i, 0)),
                  pl.BlockSpec((blk, 1), lambda i, be, nu: (i, 0)),
                  pl.BlockSpec((1, D, F), wmap),
                  pl.BlockSpec((1, 1, F), wmap),
                  pl.BlockSpec((1, D, F), wmap),
                  pl.BlockSpec((1, 1, F), wmap),
                  pl.BlockSpec((1, F, D), wmap),
                  pl.BlockSpec((1, 1, D), wmap)],
        out_specs=pl.BlockSpec((blk, half), lambda i, be, nu: (i, 0)),
    )
    return pl.pallas_call(
        _expert_kernel,
        grid_spec=grid_spec,
        out_shape=jax.ShapeDtypeStruct((P, half), I32),
        compiler_params=_cparams(("arbitrary",)),
        name="moe_experts",
    )(blk_e, n_used, xs, gate_buf.reshape(P, 1), w_glu, b_glu.reshape(E, 1, F),
      w_lin, b_lin.reshape(E, 1, F), w_out, b_out.reshape(E, 1, D))


def _combine_kernel(ys_ref, x_ref, gate_ref, o_ref):
    half = ys_ref.shape[2]
    lo_sum, hi_sum = _unpack_halves(ys_ref[0])
    for k in range(1, ys_ref.shape[0]):
        lo, hi = _unpack_halves(ys_ref[k])
        lo_sum = lo_sum + lo
        hi_sum = hi_sum + hi
    o_ref[:, :half] = x_ref[:, :half] + gate_ref[0, :, :half] * lo_sum
    o_ref[:, half:] = x_ref[:, half:] + gate_ref[0, :, half:] * hi_sum


def _combine(ys, x, gate, seq, *, tm=256):
    K, T, half = ys.shape
    D = x.shape[1]
    return pl.pallas_call(
        _combine_kernel,
        grid=(T // tm,),
        in_specs=[pl.BlockSpec((K, tm, half), lambda i: (0, i, 0)),
                  pl.BlockSpec((tm, D), lambda i: (i, 0)),
                  pl.BlockSpec((1, 1, D), lambda i: ((i * tm) // seq, 0, 0))],
        out_specs=pl.BlockSpec((tm, D), lambda i: (i, 0)),
        out_shape=jax.ShapeDtypeStruct((T, D), F32),
        compiler_params=_cparams(("parallel",)),
        name="moe_combine",
    )(ys, x, gate)


def _final_kernel(x_ref, g_ref, o_ref):
    x = x_ref[...]
    ms = jnp.mean(x * x, axis=-1, keepdims=True)
    o_ref[...] = x * lax.rsqrt(ms + RMS_EPS) * g_ref[...]


def _final_norm(x, g, *, tm=512):
    T, D = x.shape
    return pl.pallas_call(
        _final_kernel,
        grid=(T // tm,),
        in_specs=[pl.BlockSpec((tm, D), lambda i: (i, 0)),
                  pl.BlockSpec((1, D), lambda i: (0, 0))],
        out_specs=pl.BlockSpec((tm, D), lambda i: (i, 0)),
        out_shape=jax.ShapeDtypeStruct((T, D), F32),
        compiler_params=_cparams(("parallel",)),
        name="final_norm",
    )(x, g)


MOE_BLOCK = 256


def _dispatch_plan(idx_t, gate_t, n_experts, blk):
    K, T = idx_t.shape
    n = K * T
    flat_e = idx_t.reshape(n)
    flat_tok = jnp.tile(jnp.arange(T, dtype=I32), K)
    flat_gate = gate_t.reshape(n)
    order = jnp.argsort(flat_e)
    e_sorted = flat_e[order]
    counts = jnp.bincount(flat_e, length=n_experts).astype(I32)
    padded = (counts + blk - 1) // blk * blk
    pad_end = jnp.cumsum(padded)
    pad_start = pad_end - padded
    start = jnp.cumsum(counts) - counts
    dest = pad_start[e_sorted] + jnp.arange(n, dtype=I32) - start[e_sorted]
    n_blocks = -(-(n + n_experts * (blk - 1)) // blk)
    n_pad = n_blocks * blk
    tok_buf = jnp.zeros((n_pad,), I32).at[dest].set(flat_tok[order])
    gate_buf = jnp.zeros((n_pad,), F32).at[dest].set(flat_gate[order])
    slot = jnp.zeros((n,), I32).at[order].set(dest)
    blk_e = jnp.minimum(
        jnp.searchsorted(pad_end, jnp.arange(n_blocks, dtype=I32) * blk, side="right"),
        n_experts - 1).astype(I32)
    n_used = (pad_end[-1] // blk).astype(I32).reshape(1)
    return tok_buf, gate_buf, slot, blk_e, n_used


def _moe(x, g, scale, shift, gate, r_w, r_b, w_glu, b_glu, w_lin, b_lin, w_out, b_out, seq):
    T, D = x.shape
    E = r_w.shape[1]
    hp, idx_t, gate_t = _route(x, g, scale, shift, r_w, r_b, seq)
    tok_buf, gate_buf, slot, blk_e, n_used = _dispatch_plan(idx_t, gate_t, E, MOE_BLOCK)
    xs = _sc_gather(hp, tok_buf)
    yp = _experts(xs, gate_buf, blk_e, n_used, w_glu, b_glu, w_lin, b_lin, w_out, b_out,
                  blk=MOE_BLOCK)
    ys = _sc_gather(yp, slot).reshape(TOP_K, T, D // 2)
    return _combine(ys, x, gate, seq)


def kernel(x, c, norm_g, ada_w, final_norm_g, sg_w_in, sg_ln_g, sg_ln_b, sg_w_s, sg_b_s, sg_w_out, pool_w_in, pool_w_grp, pool_ls, pool_w_out, hgrn_w_in, hgrn_lb_logits, hgrn_gnorm_g, hgrn_w_out, router_w, router_b, expert_w_glu, expert_b_glu, expert_w_lin, expert_b_lin, expert_w_out, expert_b_out):
    B, S, D = x.shape
    depth = norm_g.shape[0]
    T = B * S
    xf = x.reshape(T, D)

    mod = _ada_mod(c, ada_w.reshape(depth * 2, D, 3 * D))

    def mod_parts(layer, sub):
        m = mod[layer * 2 + sub][:, None, :]
        return m[..., :D], m[..., D:2 * D], m[..., 2 * D:]

    lb_cum = jnp.cumsum(jax.nn.softmax(hgrn_lb_logits.astype(F32), axis=0), axis=0)
    lower_bounds = lb_cum - lb_cum[0]

    for layer in range(depth):
        kind, slot = layer % 3, layer // 3
        shift, scale, gate = mod_parts(layer, 0)
        g = norm_g[layer, 0][None, :]
        if kind == 0:
            z = _norm_matmul(xf, g, scale, shift, sg_w_in[slot].astype(BF16), S,
                             act="gelu", out_dtype=BF16)
            y = _spatial_gate(z, sg_ln_g[slot][None, :], sg_ln_b[slot][None, :],
                              sg_w_s[slot], sg_b_s[slot])
            w_out = sg_w_out[slot]
        elif kind == 1:
            z = _norm_matmul(xf, g, scale, shift, pool_w_in[slot].astype(BF16), S,
                             act=None, out_dtype=F32)
            y = _pool_mix(z, pool_w_grp[slot].astype(BF16), pool_ls[slot][None, :], S)
            w_out = pool_w_out[slot]
        else:
            proj = _norm_matmul(xf, g, scale, shift, hgrn_w_in[slot].astype(BF16), S,
                                act=None, out_dtype=F32)
            y = _hgrn_mix(proj, lower_bounds[layer][None, :], hgrn_gnorm_g[slot][None, :], B, S)
            w_out = hgrn_w_out[slot]
        xf = _matmul_residual(y, w_out.astype(BF16), xf, gate, S)

        shift, scale, gate = mod_parts(layer, 1)
        xf = _moe(xf, norm_g[layer, 1][None, :], scale, shift, gate,
                  router_w[layer], router_b[layer],
                  expert_w_glu[layer].astype(BF16), expert_b_glu[layer],
                  expert_w_lin[layer].astype(BF16), expert_b_lin[layer],
                  expert_w_out[layer].astype(BF16), expert_b_out[layer], S)

    return _final_norm(xf, final_norm_g[None, :]).reshape(B, S, D)
```

```python
import functools

import numpy as np
import jax
import jax.numpy as jnp
from jax import lax
from jax.experimental import pallas as pl
from jax.experimental.pallas import tpu as pltpu
from jax.experimental.pallas import tpu_sc as plsc

F32 = jnp.float32
BF16 = jnp.bfloat16
I32 = jnp.int32
U32 = jnp.uint32

RMS_EPS = 1e-5
LN_EPS = 1e-5
SWIGLU_ALPHA = 1.702
SWIGLU_LIMIT = 7.0
TOP_K = 4
SG_CHUNK = 128
POOL_WINDOWS = (2, 4, 8, 16)
HEAD_DIM = 128
LANES = 128
SC_WORKERS = 32
SC_ROWS_PER_STEP = 64

VMEM_LIMIT = 48 * 1024 * 1024
EXPERT_VMEM_LIMIT = 60 * 1024 * 1024


def _cparams(sem):
    return pltpu.CompilerParams(dimension_semantics=sem, vmem_limit_bytes=VMEM_LIMIT)


def _dot(a, b):
    return jnp.dot(a, b, preferred_element_type=F32)


def _dot_nt(a, b):
    return lax.dot_general(a, b, (((1,), (1,)), ((), ())), preferred_element_type=F32)


def _dot_tn(a, b):
    return lax.dot_general(a, b, (((0,), (0,)), ((), ())), preferred_element_type=F32)


def _split_bf16(x):
    hi = x.astype(BF16)
    lo = (x - hi.astype(F32)).astype(BF16)
    return hi, lo


def _pack_halves(y):
    n = y.shape[1] // 2
    lo = lax.bitcast_convert_type(y[:, :n].astype(BF16).astype(F32), U32)
    hi = lax.bitcast_convert_type(y[:, n:].astype(BF16).astype(F32), U32)
    word = lax.shift_right_logical(lo, jnp.uint32(16)) | (hi & jnp.uint32(0xFFFF0000))
    return lax.bitcast_convert_type(word, I32)


def _unpack_halves(w):
    u = lax.bitcast_convert_type(w, U32)
    lo = lax.bitcast_convert_type(lax.shift_left(u, jnp.uint32(16)), F32)
    hi = lax.bitcast_convert_type(u & jnp.uint32(0xFFFF0000), F32)
    return lo, hi


def _modulated_norm(x, g, scale, shift):
    ms = jnp.mean(x * x, axis=-1, keepdims=True)
    y = x * lax.rsqrt(ms + RMS_EPS) * g
    return y * (1.0 + scale) + shift


def _ada_kernel(c_ref, w_ref, o_ref):
    c = c_ref[...]
    s = c * jax.nn.sigmoid(c)
    o_ref[0] = _dot(s.astype(BF16), w_ref[0].astype(BF16))


def _ada_mod(c, ada_w):
    L, D, N = ada_w.shape
    B = c.shape[0]
    rows = 8
    tn = 512
    c_pad = jnp.zeros((rows, D), F32).at[:B].set(c)
    out = pl.pallas_call(
        _ada_kernel,
        grid=(L, N // tn),
        in_specs=[pl.BlockSpec((rows, D), lambda l, j: (0, 0)),
                  pl.BlockSpec((1, D, tn), lambda l, j: (l, 0, j))],
        out_specs=pl.BlockSpec((1, rows, tn), lambda l, j: (l, 0, j)),
        out_shape=jax.ShapeDtypeStruct((L, rows, N), F32),
        compiler_params=_cparams(("parallel", "parallel")),
        name="ada_mod",
    )(c_pad, ada_w)
    return out[:, :B]


def _nm_kernel(x_ref, g_ref, sc_ref, sh_ref, w_ref, o_ref, h_ref, *, act):
    @pl.when(pl.program_id(1) == 0)
    def _():
        h = _modulated_norm(x_ref[...], g_ref[...], sc_ref[0], sh_ref[0])
        h_ref[...] = h.astype(BF16)

    acc = _dot(h_ref[...], w_ref[...])
    if act == "gelu":
        acc = 0.5 * acc * (1.0 + lax.erf(acc * np.float32(1.0 / np.sqrt(2.0))))
    o_ref[...] = acc.astype(o_ref.dtype)


def _norm_matmul(x, g, scale, shift, w, seq, *, act, out_dtype, tm=512, tn=512):
    T, D = x.shape
    N = w.shape[1]
    bmap = lambda i, j: ((i * tm) // seq, 0, 0)
    return pl.pallas_call(
        functools.partial(_nm_kernel, act=act),
        grid=(T // tm, N // tn),
        in_specs=[pl.BlockSpec((tm, D), lambda i, j: (i, 0)),
                  pl.BlockSpec((1, D), lambda i, j: (0, 0)),
                  pl.BlockSpec((1, 1, D), bmap),
                  pl.BlockSpec((1, 1, D), bmap),
                  pl.BlockSpec((D, tn), lambda i, j: (0, j))],
        out_specs=pl.BlockSpec((tm, tn), lambda i, j: (i, j)),
        out_shape=jax.ShapeDtypeStruct((T, N), out_dtype),
        scratch_shapes=[pltpu.VMEM((tm, D), BF16)],
        compiler_params=_cparams(("parallel", "arbitrary")),
        name="norm_matmul_" + str(act),
    )(x, g, scale, shift, w)


def _mr_kernel(y_ref, w_ref, x_ref, gate_ref, o_ref):
    o_ref[...] = x_ref[...] + gate_ref[0] * _dot(y_ref[...], w_ref[...])


def _matmul_residual(y, w, x, gate, seq, *, tm=512, tn=512):
    T, K = y.shape
    D = w.shape[1]
    return pl.pallas_call(
        _mr_kernel,
        grid=(T // tm, D // tn),
        in_specs=[pl.BlockSpec((tm, K), lambda i, j: (i, 0)),
                  pl.BlockSpec((K, tn), lambda i, j: (0, j)),
                  pl.BlockSpec((tm, tn), lambda i, j: (i, j)),
                  pl.BlockSpec((1, 1, tn), lambda i, j: ((i * tm) // seq, 0, j))],
        out_specs=pl.BlockSpec((tm, tn), lambda i, j: (i, j)),
        out_shape=jax.ShapeDtypeStruct((T, D), F32),
        compiler_params=_cparams(("parallel", "arbitrary")),
        name="matmul_residual",
    )(y, w, x, gate)


def _sg_kernel(u_ref, v_ref, lng_ref, lnb_ref, ws_ref, bst_ref, y_ref, vn_ref, *, heads):
    v = v_ref[...].astype(F32)
    mu = jnp.mean(v, axis=-1, keepdims=True)
    vc = v - mu
    var = jnp.mean(vc * vc, axis=-1, keepdims=True)
    vn_ref[...] = (vc * lax.rsqrt(var + LN_EPS) * lng_ref[...] + lnb_ref[...]).astype(BF16)

    n_chunks = v_ref.shape[0] // SG_CHUNK
    row = lax.broadcasted_iota(I32, (SG_CHUNK, SG_CHUNK), 0)
    col = lax.broadcasted_iota(I32, (SG_CHUNK, SG_CHUNK), 1)
    causal = row >= col
    for hd in range(heads):
        cs = slice(hd * HEAD_DIM, (hd + 1) * HEAD_DIM)
        wm = jnp.where(causal, ws_ref[hd], 0.0).astype(BF16)
        bias = bst_ref[:, hd:hd + 1]
        for ci in range(n_chunks):
            rs = slice(ci * SG_CHUNK, (ci + 1) * SG_CHUNK)
            mixed = _dot(wm, vn_ref[rs, cs]) + bias
            y_ref[rs, cs] = (u_ref[rs, cs].astype(F32) * mixed).astype(BF16)


def _spatial_gate(z, ln_g, ln_b, w_s, b_s, *, tm=256):
    T, two_w = z.shape
    W = two_w // 2
    heads = w_s.shape[0]
    return pl.pallas_call(
        functools.partial(_sg_kernel, heads=heads),
        grid=(T // tm,),
        in_specs=[pl.BlockSpec((tm, W), lambda i: (i, 0)),
                  pl.BlockSpec((tm, W), lambda i: (i, 1)),
                  pl.BlockSpec((1, W), lambda i: (0, 0)),
                  pl.BlockSpec((1, W), lambda i: (0, 0)),
                  pl.BlockSpec((heads, SG_CHUNK, SG_CHUNK), lambda i: (0, 0, 0)),
                  pl.BlockSpec((SG_CHUNK, heads), lambda i: (0, 0))],
        out_specs=pl.BlockSpec((tm, W), lambda i: (i, 0)),
        out_shape=jax.ShapeDtypeStruct((T, W), BF16),
        scratch_shapes=[pltpu.VMEM((tm, W), BF16)],
        compiler_params=_cparams(("parallel",)),
        name="spatial_gate",
    )(z, z, ln_g, ln_b, w_s, jnp.transpose(b_s))


POOL_HALO = 16


def _pool_kernel(z_ref, halo_ref, wg_ref, ls_ref, y_ref, *, seq):
    tm = z_ref.shape[0]
    gdim = wg_ref.shape[1]
    pos0 = (pl.program_id(0) * tm) % seq
    halo_on = (pos0 > 0).astype(F32)
    pos = pos0 + lax.broadcasted_iota(I32, (tm, 1), 0)
    for gi, wnd in enumerate(POOL_WINDOWS):
        cs = slice(gi * gdim, (gi + 1) * gdim)
        zg = z_ref[:, cs]
        s = jnp.concatenate([halo_ref[:, cs] * halo_on, zg], axis=0)
        k = 1
        while k < wnd:
            s = s + pltpu.roll(s, k, 0)
            k *= 2
        cnt = jnp.minimum(pos + 1, wnd).astype(F32)
        pooled = s[POOL_HALO:, :] / cnt - zg
        yg = _dot(pooled.astype(BF16), wg_ref[gi]) * ls_ref[:, cs]
        y_ref[:, cs] = yg.astype(BF16)


def _pool_mix(z, w_grp, ls, seq, *, tm=256):
    T, D = z.shape
    G, gdim, _ = w_grp.shape
    assert max(POOL_WINDOWS) <= POOL_HALO and tm % POOL_HALO == 0
    ratio = tm // POOL_HALO
    return pl.pallas_call(
        functools.partial(_pool_kernel, seq=seq),
        grid=(T // tm,),
        in_specs=[pl.BlockSpec((tm, D), lambda i: (i, 0)),
                  pl.BlockSpec((POOL_HALO, D), lambda i: (jnp.maximum(i * ratio - 1, 0), 0)),
                  pl.BlockSpec((G, gdim, gdim), lambda i: (0, 0, 0)),
                  pl.BlockSpec((1, D), lambda i: (0, 0))],
        out_specs=pl.BlockSpec((tm, D), lambda i: (i, 0)),
        out_shape=jax.ShapeDtypeStruct((T, D), BF16),
        compiler_params=_cparams(("parallel",)),
        name="pool_mix",
    )(z, z, w_grp, ls)


HGRN_CHUNK = 128


def _hgrn_tables(c=HGRN_CHUNK):
    t = np.arange(c)
    sums = [(t[None, :] <= t[:, None]), (t[None, :] > t[:, None])]
    masks = []
    h = c // 2
    while h >= 1:
        off = t % (2 * h)
        mid = (t // (2 * h)) * (2 * h) + h
        second = off >= h
        j = t[None, :]
        m_q = second[:, None] & (j >= mid[:, None]) & (j <= t[:, None])
        m_k = (~second)[:, None] & (j > t[:, None]) & (j < mid[:, None])
        sums.append(m_q | m_k)
        same = (t[:, None] // (2 * h)) == (t[None, :] // (2 * h))
        masks.append(same & second[:, None] & (~second)[None, :])
        h //= 2
    masks.append(t[:, None] == t[None, :])
    return (np.stack(sums).astype(np.float32), np.stack(masks).astype(np.float32))


def _hgrn_kernel(q_ref, f_ref, i_ref, g_ref, lb_ref, gn_ref, sums_ref, masks_ref, o_ref,
                 st_ref, qb_ref, ke_ref, qp_ref, kp_ref, vb_ref, eb_ref, *, heads):
    c = q_ref.shape[0]
    n_lvl = masks_ref.shape[0]

    @pl.when(pl.program_id(1) == 0)
    def _():
        st_ref[...] = jnp.zeros_like(st_ref)

    lb = lb_ref[...]
    sig = jax.nn.sigmoid(f_ref[...])
    log_f = jnp.log(lb + (1.0 - lb) * sig)
    kk = (1.0 - lb) * (1.0 - sig)
    lf_hi, lf_lo = _split_bf16(log_f)
    qr = q_ref[...]
    q = qr * jax.nn.sigmoid(qr)
    vb_ref[...] = i_ref[...].astype(BF16)

    for j in range(n_lvl + 1):
        m = sums_ref[j]
        x = jnp.exp(_dot(m, lf_hi) + _dot(m, lf_lo))
        if j == 0:
            qb_ref[...] = (q * x).astype(BF16)
            eb_ref[...] = x[c - 1:c, :]
        elif j == 1:
            ke_ref[...] = (kk * x).astype(BF16)
        else:
            qp_ref[j - 2] = (q * x).astype(BF16)
            kp_ref[j - 2] = (kk * x).astype(BF16)
    qp_ref[n_lvl - 1] = q.astype(BF16)
    kp_ref[n_lvl - 1] = kk.astype(BF16)

    gn = gn_ref[...]

    def head(hd, carry):
        cs = pl.ds(pl.multiple_of(hd * HEAD_DIM, HEAD_DIM), HEAD_DIM)
        scores = jnp.zeros((c, c), F32)
        for l in range(n_lvl):
            scores = scores + masks_ref[l] * _dot_nt(qp_ref[l, :, cs], kp_ref[l, :, cs])
        st = st_ref[hd]
        vh = vb_ref[:, cs]
        o = _dot_nt(qb_ref[:, cs], st.astype(BF16)) + _dot(scores.astype(BF16), vh)
        st_ref[hd] = st * eb_ref[:, cs] + _dot_tn(vh, ke_ref[:, cs])
        on = o * lax.rsqrt(jnp.mean(o * o, axis=-1, keepdims=True) + RMS_EPS) * gn
        gv = g_ref[:, cs]
        o_ref[:, cs] = (on * (gv * jax.nn.sigmoid(gv))).astype(BF16)
        return carry

    lax.fori_loop(0, heads, head, 0)


def _hgrn_mix(proj, lb, gnorm_g, batch, seq):
    T, four_d = proj.shape
    D = four_d // 4
    heads = D // HEAD_DIM
    c = HGRN_CHUNK
    n_chunks = seq // c
    sums, masks = _hgrn_tables(c)
    n_sum, n_lvl = sums.shape[0], masks.shape[0]
    row = lambda b, i: b * n_chunks + i
    return pl.pallas_call(
        functools.partial(_hgrn_kernel, heads=heads),
        grid=(batch, n_chunks),
        in_specs=[pl.BlockSpec((c, D), lambda b, i: (row(b, i), 0)),
                  pl.BlockSpec((c, D), lambda b, i: (row(b, i), 1)),
                  pl.BlockSpec((c, D), lambda b, i: (row(b, i), 2)),
                  pl.BlockSpec((c, D), lambda b, i: (row(b, i), 3)),
                  pl.BlockSpec((1, D), lambda b, i: (0, 0)),
                  pl.BlockSpec((1, HEAD_DIM), lambda b, i: (0, 0)),
                  pl.BlockSpec((n_sum, c, c), lambda b, i: (0, 0, 0)),
                  pl.BlockSpec((n_lvl, c, c), lambda b, i: (0, 0, 0))],
        out_specs=pl.BlockSpec((c, D), lambda b, i: (row(b, i), 0)),
        out_shape=jax.ShapeDtypeStruct((T, D), BF16),
        scratch_shapes=[pltpu.VMEM((heads, HEAD_DIM, HEAD_DIM), F32),
                        pltpu.VMEM((c, D), BF16),
                        pltpu.VMEM((c, D), BF16),
                        pltpu.VMEM((n_lvl, c, D), BF16),
                        pltpu.VMEM((n_lvl, c, D), BF16),
                        pltpu.VMEM((c, D), BF16),
                        pltpu.VMEM((1, D), F32)],
        compiler_params=_cparams(("parallel", "arbitrary")),
        name="hgrn_mix",
    )(proj, proj, proj, proj, lb, gnorm_g, jnp.asarray(sums, BF16), jnp.asarray(masks, F32))


def _route_kernel(x_ref, g_ref, sc_ref, sh_ref, rw_ref, rb_ref, tri_ref,
                  hp_ref, idx_ref, gate_ref, rank_ref, cnt_ref, carry_ref, *, n_experts):
    @pl.when(pl.program_id(0) == 0)
    def _():
        carry_ref[...] = jnp.zeros_like(carry_ref)

    h = _modulated_norm(x_ref[...], g_ref[...], sc_ref[0], sh_ref[0])
    hp_ref[...] = _pack_halves(h)

    h_hi, h_lo = _split_bf16(h)
    w_hi, w_lo = _split_bf16(rw_ref[...])
    logits = _dot(h_hi, w_hi) + _dot(h_lo, w_hi) + _dot(h_hi, w_lo) + rb_ref[...]
    lt = jnp.transpose(logits)[:n_experts, :]
    eidx = lax.broadcasted_iota(I32, lt.shape, 0)
    vals, idxs, hits = [], [], []
    for _ in range(TOP_K):
        m = jnp.max(lt, axis=0, keepdims=True)
        sel = jnp.min(jnp.where(lt == m, eidx, n_experts), axis=0, keepdims=True)
        hit = eidx == sel
        vals.append(m)
        idxs.append(sel)
        hits.append(hit)
        lt = jnp.where(hit, -jnp.inf, lt)
    exps = [jnp.exp(v - vals[0]) for v in vals]
    denom = exps[0]
    for e in exps[1:]:
        denom = denom + e
    idx_ref[...] = jnp.concatenate(idxs, axis=0)
    gate_ref[...] = jnp.concatenate([e / denom for e in exps], axis=0)

    onehot = hits[0].astype(F32)
    for hit in hits[1:]:
        onehot = onehot + hit.astype(F32)
    before = carry_ref[...] + _dot(onehot.astype(BF16), tri_ref[...])
    rank_ref[...] = jnp.concatenate(
        [jnp.sum(jnp.where(hit, before, 0.0), axis=0, keepdims=True) for hit in hits],
        axis=0).astype(I32)
    carry_ref[...] += jnp.sum(onehot, axis=1, keepdims=True)
    cnt_ref[...] = carry_ref[...].astype(I32)


def _route(x, g, scale, shift, r_w, r_b, seq, *, tm=512):
    T, D = x.shape
    E = r_w.shape[1]
    rw_pad = jnp.zeros((D, LANES), F32).at[:, :E].set(r_w)
    rb_pad = jnp.full((1, LANES), -1e30, F32).at[0, :E].set(r_b)
    tri = jnp.asarray(np.triu(np.ones((tm, tm), np.float32), k=1), BF16)
    bmap = lambda i: ((i * tm) // seq, 0, 0)
    kt = pl.BlockSpec((TOP_K, tm), lambda i: (0, i))
    return pl.pallas_call(
        functools.partial(_route_kernel, n_experts=E),
        grid=(T // tm,),
        in_specs=[pl.BlockSpec((tm, D), lambda i: (i, 0)),
                  pl.BlockSpec((1, D), lambda i: (0, 0)),
                  pl.BlockSpec((1, 1, D), bmap),
                  pl.BlockSpec((1, 1, D), bmap),
                  pl.BlockSpec((D, LANES), lambda i: (0, 0)),
                  pl.BlockSpec((1, LANES), lambda i: (0, 0)),
                  pl.BlockSpec((tm, tm), lambda i: (0, 0))],
        out_specs=[pl.BlockSpec((tm, D // 2), lambda i: (i, 0)), kt, kt, kt,
                   pl.BlockSpec((E, 1), lambda i: (0, 0))],
        out_shape=[jax.ShapeDtypeStruct((T, D // 2), I32),
                   jax.ShapeDtypeStruct((TOP_K, T), I32),
                   jax.ShapeDtypeStruct((TOP_K, T), F32),
                   jax.ShapeDtypeStruct((TOP_K, T), I32),
                   jax.ShapeDtypeStruct((E, 1), I32)],
        scratch_shapes=[pltpu.VMEM((E, 1), F32)],
        compiler_params=_cparams(("arbitrary",)),
        name="moe_route",
    )(x, g, scale, shift, rw_pad, rb_pad, tri)


def _sc_gather(table, idx):
    R = idx.shape[0]
    W = table.shape[1]
    rows = SC_ROWS_PER_STEP
    per_worker = R // SC_WORKERS
    steps = per_worker // rows
    assert steps * rows * SC_WORKERS == R
    mesh = plsc.VectorSubcoreMesh(core_axis_name="c", subcore_axis_name="s")
    n_cores = mesh.num_cores

    @functools.partial(
        pl.kernel, mesh=mesh,
        out_type=jax.ShapeDtypeStruct((R, W), table.dtype),
        scratch_types=[pltpu.VMEM((rows,), I32),
                       pltpu.VMEM((rows, W), table.dtype),
                       pltpu.SemaphoreType.DMA],
    )
    def gather(table_hbm, idx_hbm, out_hbm, idx_v, rows_v, sem):
        wid = lax.axis_index("s") * n_cores + lax.axis_index("c")
        base = wid * per_worker

        @pl.loop(0, steps)
        def _(j):
            off = pl.multiple_of(base + j * rows, 8)
            pltpu.sync_copy(idx_hbm.at[pl.ds(off, rows)], idx_v)
            pltpu.async_copy(table_hbm.at[idx_v], rows_v, sem).wait()
            pltpu.sync_copy(rows_v, out_hbm.at[pl.ds(off, rows)])

    return gather(table, idx)


def _sc_scatter_rows(table, slots, n_out):
    K, T = slots.shape
    W = table.shape[1]
    rows = SC_ROWS_PER_STEP
    per_worker = T // SC_WORKERS
    steps = per_worker // rows
    assert steps * rows * SC_WORKERS == T
    mesh = plsc.VectorSubcoreMesh(core_axis_name="c", subcore_axis_name="s")
    n_cores = mesh.num_cores

    @functools.partial(
        pl.kernel, mesh=mesh,
        out_type=jax.ShapeDtypeStruct((n_out, W), table.dtype),
        scratch_types=[pltpu.VMEM((K, rows), I32),
                       pltpu.VMEM((rows, W), table.dtype)],
    )
    def scatter(table_hbm, slots_hbm, out_hbm, idx_v, rows_v):
        wid = lax.axis_index("s") * n_cores + lax.axis_index("c")
        base = wid * per_worker

        @pl.loop(0, steps)
        def _(j):
            off = pl.multiple_of(base + j * rows, 8)
            pltpu.sync_copy(table_hbm.at[pl.ds(off, rows)], rows_v)
            for k in range(K):
                pltpu.sync_copy(slots_hbm.at[pl.ds(pl.multiple_of(k * T + off, 8), rows)],
                                idx_v.at[k])
                pltpu.sync_copy(rows_v, out_hbm.at[idx_v.at[k]])

    return scatter(table, slots.reshape(K * T))


def _expert_kernel(blk_e_ref, n_valid_ref, xs_ref, wg_ref, bg_ref, wl_ref, bl_ref,
                   wo_ref, bo_ref, y_ref, wgb_ref, wlb_ref, wob_ref):
    i = pl.program_id(0)
    n_valid = n_valid_ref[i]
    new_expert = jnp.logical_or(i == 0, blk_e_ref[i] != blk_e_ref[jnp.maximum(i - 1, 0)])

    @pl.when(jnp.logical_and(new_expert, n_valid > 0))
    def _():
        wgb_ref[...] = wg_ref[0].astype(BF16)
        wlb_ref[...] = wl_ref[0].astype(BF16)
        wob_ref[...] = wo_ref[0].astype(BF16)

    @pl.when(n_valid > 0)
    def _():
        half = xs_ref.shape[1]
        live = lax.broadcasted_iota(I32, (xs_ref.shape[0], 1), 0) < n_valid
        lo, hi = _unpack_halves(jnp.where(live, xs_ref[...], 0))
        lo = lo.astype(BF16)
        hi = hi.astype(BF16)
        a = _dot(lo, wgb_ref[:half, :]) + _dot(hi, wgb_ref[half:, :]) + bg_ref[0]
        l = _dot(lo, wlb_ref[:half, :]) + _dot(hi, wlb_ref[half:, :]) + bl_ref[0]
        a = jnp.minimum(a, SWIGLU_LIMIT)
        l = jnp.clip(l, -SWIGLU_LIMIT, SWIGLU_LIMIT)
        act = a * jax.nn.sigmoid(SWIGLU_ALPHA * a) * (l + 1.0)
        y_ref[...] = _pack_halves(_dot(act.astype(BF16), wob_ref[...]) + bo_ref[0])

    @pl.when(n_valid == 0)
    def _():
        y_ref[...] = jnp.zeros_like(y_ref)


def _experts(xs, blk_e, n_valid, layer, w_glu, b_glu, w_lin, b_lin, w_out, b_out, *, blk):
    P, half = xs.shape
    L, E, D, F = w_glu.shape
    n_blocks = P // blk
    wmap = lambda i, be, nv: (layer * E + be[i], 0, 0)
    w_glu, w_lin = w_glu.reshape(L * E, D, F), w_lin.reshape(L * E, D, F)
    w_out = w_out.reshape(L * E, F, D)
    b_glu, b_lin = b_glu.reshape(L * E, 1, F), b_lin.reshape(L * E, 1, F)
    b_out = b_out.reshape(L * E, 1, D)
    grid_spec = pltpu.PrefetchScalarGridSpec(
        num_scalar_prefetch=2,
        grid=(n_blocks,),
        in_specs=[pl.BlockSpec((blk, half), lambda i, be, nv: (i, 0)),
                  pl.BlockSpec((1, D, F), wmap),
                  pl.BlockSpec((1, 1, F), wmap),
                  pl.BlockSpec((1, D, F), wmap),
                  pl.BlockSpec((1, 1, F), wmap),
                  pl.BlockSpec((1, F, D), wmap),
                  pl.BlockSpec((1, 1, D), wmap)],
        out_specs=pl.BlockSpec((blk, half), lambda i, be, nv: (i, 0)),
        scratch_shapes=[pltpu.VMEM((D, F), BF16), pltpu.VMEM((D, F), BF16),
                        pltpu.VMEM((F, D), BF16)],
    )
    return pl.pallas_call(
        _expert_kernel,
        grid_spec=grid_spec,
        out_shape=jax.ShapeDtypeStruct((P, half), I32),
        compiler_params=pltpu.CompilerParams(dimension_semantics=("arbitrary",),
                                             vmem_limit_bytes=EXPERT_VMEM_LIMIT),
        name="moe_experts",
    )(blk_e, n_valid, xs, w_glu, b_glu, w_lin, b_lin, w_out, b_out)


def _combine_kernel(ys_ref, tg_ref, x_ref, gate_ref, o_ref):
    half = ys_ref.shape[2]
    tg = tg_ref[...]
    lo_sum = hi_sum = None
    for k in range(ys_ref.shape[0]):
        lo, hi = _unpack_halves(ys_ref[k])
        gk = tg[:, k:k + 1]
        lo_sum = gk * lo if lo_sum is None else lo_sum + gk * lo
        hi_sum = gk * hi if hi_sum is None else hi_sum + gk * hi
    o_ref[:, :half] = x_ref[:, :half] + gate_ref[0, :, :half] * lo_sum
    o_ref[:, half:] = x_ref[:, half:] + gate_ref[0, :, half:] * hi_sum


def _combine(ys, tok_gate, x, gate, seq, *, tm=256):
    K, T, half = ys.shape
    D = x.shape[1]
    return pl.pallas_call(
        _combine_kernel,
        grid=(T // tm,),
        in_specs=[pl.BlockSpec((K, tm, half), lambda i: (0, i, 0)),
                  pl.BlockSpec((tm, K), lambda i: (i, 0)),
                  pl.BlockSpec((tm, D), lambda i: (i, 0)),
                  pl.BlockSpec((1, 1, D), lambda i: ((i * tm) // seq, 0, 0))],
        out_specs=pl.BlockSpec((tm, D), lambda i: (i, 0)),
        out_shape=jax.ShapeDtypeStruct((T, D), F32),
        compiler_params=_cparams(("parallel",)),
        name="moe_combine",
    )(ys, tok_gate, x, gate)


def _final_kernel(x_ref, g_ref, o_ref):
    x = x_ref[...]
    ms = jnp.mean(x * x, axis=-1, keepdims=True)
    o_ref[...] = x * lax.rsqrt(ms + RMS_EPS) * g_ref[...]


def _final_norm(x, g, *, tm=512):
    T, D = x.shape
    return pl.pallas_call(
        _final_kernel,
        grid=(T // tm,),
        in_specs=[pl.BlockSpec((tm, D), lambda i: (i, 0)),
                  pl.BlockSpec((1, D), lambda i: (0, 0))],
        out_specs=pl.BlockSpec((tm, D), lambda i: (i, 0)),
        out_shape=jax.ShapeDtypeStruct((T, D), F32),
        compiler_params=_cparams(("parallel",)),
        name="final_norm",
    )(x, g)


MOE_BLOCK = 256


def _dispatch_plan(idx_t, rank_t, counts, blk):
    K, T = idx_t.shape
    E = counts.shape[0]
    padded = (counts + blk - 1) // blk * blk
    pad_end = jnp.cumsum(padded)
    pad_start = pad_end - padded
    slots = pad_start[idx_t] + rank_t
    n_blocks = -(-(K * T + E * (blk - 1)) // blk)
    blk_start = jnp.arange(n_blocks, dtype=I32) * blk
    blk_e = jnp.minimum(jnp.sum(pad_end[None, :] <= blk_start[:, None], axis=1), E - 1).astype(I32)
    n_valid = jnp.clip(pad_start[blk_e] + counts[blk_e] - blk_start, 0, blk).astype(I32)
    return slots, blk_e, n_valid, n_blocks * blk


def _moe(x, g, scale, shift, gate, r_w, r_b, layer, w_glu, b_glu, w_lin, b_lin, w_out, b_out,
         seq):
    T, D = x.shape
    hp, idx_t, gate_t, rank_t, counts = _route(x, g, scale, shift, r_w, r_b, seq)
    slots, blk_e, n_valid, n_rows = _dispatch_plan(idx_t, rank_t, counts[:, 0], MOE_BLOCK)
    xs = _sc_scatter_rows(hp, slots, n_rows)
    yp = _experts(xs, blk_e, n_valid, layer, w_glu, b_glu, w_lin, b_lin, w_out, b_out,
                  blk=MOE_BLOCK)
    ys = _sc_gather(yp, slots.reshape(TOP_K * T)).reshape(TOP_K, T, D // 2)
    return _combine(ys, jnp.transpose(gate_t), x, gate, seq)


def kernel(x, c, norm_g, ada_w, final_norm_g, sg_w_in, sg_ln_g, sg_ln_b, sg_w_s, sg_b_s, sg_w_out, pool_w_in, pool_w_grp, pool_ls, pool_w_out, hgrn_w_in, hgrn_lb_logits, hgrn_gnorm_g, hgrn_w_out, router_w, router_b, expert_w_glu, expert_b_glu, expert_w_lin, expert_b_lin, expert_w_out, expert_b_out):
    B, S, D = x.shape
    depth = norm_g.shape[0]
    T = B * S
    xf = x.reshape(T, D)

    mod = _ada_mod(c, ada_w.reshape(depth * 2, D, 3 * D))

    def mod_parts(layer, sub):
        m = mod[layer * 2 + sub][:, None, :]
        return m[..., :D], m[..., D:2 * D], m[..., 2 * D:]

    lb_cum = jnp.cumsum(jax.nn.softmax(hgrn_lb_logits.astype(F32), axis=0), axis=0)
    lower_bounds = lb_cum - lb_cum[0]

    for layer in range(depth):
        kind, slot = layer % 3, layer // 3
        shift, scale, gate = mod_parts(layer, 0)
        g = norm_g[layer, 0][None, :]
        if kind == 0:
            z = _norm_matmul(xf, g, scale, shift, sg_w_in[slot].astype(BF16), S,
                             act="gelu", out_dtype=BF16)
            y = _spatial_gate(z, sg_ln_g[slot][None, :], sg_ln_b[slot][None, :],
                              sg_w_s[slot], sg_b_s[slot])
            w_out = sg_w_out[slot]
        elif kind == 1:
            z = _norm_matmul(xf, g, scale, shift, pool_w_in[slot].astype(BF16), S,
                             act=None, out_dtype=F32)
            y = _pool_mix(z, pool_w_grp[slot].astype(BF16), pool_ls[slot][None, :], S)
            w_out = pool_w_out[slot]
        else:
            proj = _norm_matmul(xf, g, scale, shift, hgrn_w_in[slot].astype(BF16), S,
                                act=None, out_dtype=F32)
            y = _hgrn_mix(proj, lower_bounds[layer][None, :], hgrn_gnorm_g[slot][None, :], B, S)
            w_out = hgrn_w_out[slot]
        xf = _matmul_residual(y, w_out.astype(BF16), xf, gate, S)

        shift, scale, gate = mod_parts(layer, 1)
        xf = _moe(xf, norm_g[layer, 1][None, :], scale, shift, gate,
                  router_w[layer], router_b[layer], layer,
                  expert_w_glu, expert_b_glu, expert_w_lin, expert_b_lin,
                  expert_w_out, expert_b_out, S)

    return _final_norm(xf, final_norm_g[None, :]).reshape(B, S, D)
```

```python
import functools

import numpy as np
import jax
import jax.numpy as jnp
from jax import lax
from jax.experimental import pallas as pl
from jax.experimental.pallas import tpu as pltpu
from jax.experimental.pallas import tpu_sc as plsc

F32 = jnp.float32
BF16 = jnp.bfloat16
I32 = jnp.int32
U32 = jnp.uint32

RMS_EPS = 1e-5
LN_EPS = 1e-5
SWIGLU_ALPHA = 1.702
SWIGLU_LIMIT = 7.0
TOP_K = 4
SG_CHUNK = 128
POOL_WINDOWS = (2, 4, 8, 16)
HEAD_DIM = 128
LANES = 128
SC_WORKERS = 32
SC_ROWS_PER_STEP = 64

VMEM_LIMIT = 48 * 1024 * 1024
EXPERT_VMEM_LIMIT = 60 * 1024 * 1024
BIG_VMEM_LIMIT = 56 * 1024 * 1024


def _cparams(sem):
    return pltpu.CompilerParams(dimension_semantics=sem, vmem_limit_bytes=VMEM_LIMIT)


def _dot(a, b):
    return jnp.dot(a, b, preferred_element_type=F32)


def _dot_nt(a, b):
    return lax.dot_general(a, b, (((1,), (1,)), ((), ())), preferred_element_type=F32)


def _dot_tn(a, b):
    return lax.dot_general(a, b, (((0,), (0,)), ((), ())), preferred_element_type=F32)


def _split_bf16(x):
    hi = x.astype(BF16)
    lo = (x - hi.astype(F32)).astype(BF16)
    return hi, lo


def _pack_halves(y):
    n = y.shape[1] // 2
    lo = lax.bitcast_convert_type(y[:, :n].astype(BF16).astype(F32), U32)
    hi = lax.bitcast_convert_type(y[:, n:].astype(BF16).astype(F32), U32)
    word = lax.shift_right_logical(lo, jnp.uint32(16)) | (hi & jnp.uint32(0xFFFF0000))
    return lax.bitcast_convert_type(word, I32)


def _unpack_halves(w):
    u = lax.bitcast_convert_type(w, U32)
    lo = lax.bitcast_convert_type(lax.shift_left(u, jnp.uint32(16)), F32)
    hi = lax.bitcast_convert_type(u & jnp.uint32(0xFFFF0000), F32)
    return lo, hi


def _modulated_norm(x, g, scale, shift):
    ms = jnp.mean(x * x, axis=-1, keepdims=True)
    y = x * lax.rsqrt(ms + RMS_EPS) * g
    return y * (1.0 + scale) + shift


def _ada_kernel(c_ref, w_ref, o_ref):
    c = c_ref[...]
    s = c * jax.nn.sigmoid(c)
    o_ref[0] = _dot(s.astype(BF16), w_ref[0].astype(BF16))


def _ada_mod(c, ada_w):
    L, D, N = ada_w.shape
    B = c.shape[0]
    rows = 8
    tn = 512
    c_pad = jnp.zeros((rows, D), F32).at[:B].set(c)
    out = pl.pallas_call(
        _ada_kernel,
        grid=(L, N // tn),
        in_specs=[pl.BlockSpec((rows, D), lambda l, j: (0, 0)),
                  pl.BlockSpec((1, D, tn), lambda l, j: (l, 0, j))],
        out_specs=pl.BlockSpec((1, rows, tn), lambda l, j: (l, 0, j)),
        out_shape=jax.ShapeDtypeStruct((L, rows, N), F32),
        compiler_params=_cparams(("parallel", "parallel")),
        name="ada_mod",
    )(c_pad, ada_w)
    return out[:, :B]


def _nm_kernel(x_ref, g_ref, sc_ref, sh_ref, w_ref, o_ref, *rest, act, f32_col):
    h_ref = rest[-1]

    @pl.when(pl.program_id(1) == 0)
    def _():
        h = _modulated_norm(x_ref[...], g_ref[...], sc_ref[0], sh_ref[0])
        h_ref[...] = h.astype(BF16)

    acc = _dot(h_ref[...], w_ref[...])
    if act == "gelu":
        acc = 0.5 * acc * (1.0 + lax.erf(acc * np.float32(1.0 / np.sqrt(2.0))))
    o_ref[...] = acc.astype(o_ref.dtype)
    if f32_col is not None:
        @pl.when(pl.program_id(1) == f32_col)
        def _():
            rest[0][...] = acc


def _norm_matmul(x, g, scale, shift, w, seq, *, act, out_dtype, f32_col=None, tm=512, tn=2048):
    T, D = x.shape
    N = w.shape[1]
    bmap = lambda i, j: ((i * tm) // seq, 0, 0)
    out_specs = [pl.BlockSpec((tm, tn), lambda i, j: (i, j))]
    out_shape = [jax.ShapeDtypeStruct((T, N), out_dtype)]
    if f32_col is not None:
        out_specs.append(pl.BlockSpec((tm, tn), lambda i, j: (i, 0)))
        out_shape.append(jax.ShapeDtypeStruct((T, tn), F32))
    outs = pl.pallas_call(
        functools.partial(_nm_kernel, act=act, f32_col=f32_col),
        grid=(T // tm, N // tn),
        in_specs=[pl.BlockSpec((tm, D), lambda i, j: (i, 0)),
                  pl.BlockSpec((1, D), lambda i, j: (0, 0)),
                  pl.BlockSpec((1, 1, D), bmap),
                  pl.BlockSpec((1, 1, D), bmap),
                  pl.BlockSpec((D, tn), lambda i, j: (0, j))],
        out_specs=out_specs,
        out_shape=out_shape,
        scratch_shapes=[pltpu.VMEM((tm, D), BF16)],
        compiler_params=pltpu.CompilerParams(dimension_semantics=("parallel", "arbitrary"),
                                             vmem_limit_bytes=BIG_VMEM_LIMIT),
        name="norm_matmul_" + str(act),
    )(x, g, scale, shift, w)
    return outs if f32_col is not None else outs[0]


def _mr_kernel(y_ref, w_ref, x_ref, gate_ref, o_ref):
    o_ref[...] = x_ref[...] + gate_ref[0] * _dot(y_ref[...], w_ref[...])


def _matmul_residual(y, w, x, gate, seq, *, tm=512):
    T, K = y.shape
    D = w.shape[1]
    return pl.pallas_call(
        _mr_kernel,
        grid=(T // tm,),
        in_specs=[pl.BlockSpec((tm, K), lambda i: (i, 0)),
                  pl.BlockSpec((K, D), lambda i: (0, 0)),
                  pl.BlockSpec((tm, D), lambda i: (i, 0)),
                  pl.BlockSpec((1, 1, D), lambda i: ((i * tm) // seq, 0, 0))],
        out_specs=pl.BlockSpec((tm, D), lambda i: (i, 0)),
        out_shape=jax.ShapeDtypeStruct((T, D), F32),
        compiler_params=_cparams(("parallel",)),
        name="matmul_residual",
    )(y, w, x, gate)


def _sg_kernel(u_ref, v_ref, lng_ref, lnb_ref, ws_ref, bst_ref, y_ref, vn_ref, *, heads):
    v = v_ref[...].astype(F32)
    mu = jnp.mean(v, axis=-1, keepdims=True)
    vc = v - mu
    var = jnp.mean(vc * vc, axis=-1, keepdims=True)
    vn_ref[...] = (vc * lax.rsqrt(var + LN_EPS) * lng_ref[...] + lnb_ref[...]).astype(BF16)

    n_chunks = v_ref.shape[0] // SG_CHUNK
    row = lax.broadcasted_iota(I32, (SG_CHUNK, SG_CHUNK), 0)
    col = lax.broadcasted_iota(I32, (SG_CHUNK, SG_CHUNK), 1)
    causal = row >= col
    for hd in range(heads):
        cs = slice(hd * HEAD_DIM, (hd + 1) * HEAD_DIM)
        wm = jnp.where(causal, ws_ref[hd], 0.0).astype(BF16)
        bias = bst_ref[:, hd:hd + 1]
        for ci in range(n_chunks):
            rs = slice(ci * SG_CHUNK, (ci + 1) * SG_CHUNK)
            mixed = _dot(wm, vn_ref[rs, cs]) + bias
            y_ref[rs, cs] = (u_ref[rs, cs].astype(F32) * mixed).astype(BF16)


def _spatial_gate(z, ln_g, ln_b, w_s, b_s, *, tm=256):
    T, two_w = z.shape
    W = two_w // 2
    heads = w_s.shape[0]
    return pl.pallas_call(
        functools.partial(_sg_kernel, heads=heads),
        grid=(T // tm,),
        in_specs=[pl.BlockSpec((tm, W), lambda i: (i, 0)),
                  pl.BlockSpec((tm, W), lambda i: (i, 1)),
                  pl.BlockSpec((1, W), lambda i: (0, 0)),
                  pl.BlockSpec((1, W), lambda i: (0, 0)),
                  pl.BlockSpec((heads, SG_CHUNK, SG_CHUNK), lambda i: (0, 0, 0)),
                  pl.BlockSpec((SG_CHUNK, heads), lambda i: (0, 0))],
        out_specs=pl.BlockSpec((tm, W), lambda i: (i, 0)),
        out_shape=jax.ShapeDtypeStruct((T, W), BF16),
        scratch_shapes=[pltpu.VMEM((tm, W), BF16)],
        compiler_params=_cparams(("parallel",)),
        name="spatial_gate",
    )(z, z, ln_g, ln_b, w_s, jnp.transpose(b_s))


POOL_HALO = 16


def _pool_kernel(z_ref, halo_ref, wg_ref, ls_ref, y_ref, *, seq):
    tm = z_ref.shape[0]
    gdim = wg_ref.shape[1]
    pos0 = (pl.program_id(0) * tm) % seq
    halo_on = (pos0 > 0).astype(F32)
    pos = pos0 + lax.broadcasted_iota(I32, (tm, 1), 0)
    for gi, wnd in enumerate(POOL_WINDOWS):
        cs = slice(gi * gdim, (gi + 1) * gdim)
        zg = z_ref[:, cs]
        s = jnp.concatenate([halo_ref[:, cs] * halo_on, zg], axis=0)
        k = 1
        while k < wnd:
            s = s + pltpu.roll(s, k, 0)
            k *= 2
        cnt = jnp.minimum(pos + 1, wnd).astype(F32)
        pooled = s[POOL_HALO:, :] / cnt - zg
        yg = _dot(pooled.astype(BF16), wg_ref[gi]) * ls_ref[:, cs]
        y_ref[:, cs] = yg.astype(BF16)


def _pool_mix(z, w_grp, ls, seq, *, tm=256):
    T, D = z.shape
    G, gdim, _ = w_grp.shape
    assert max(POOL_WINDOWS) <= POOL_HALO and tm % POOL_HALO == 0
    ratio = tm // POOL_HALO
    return pl.pallas_call(
        functools.partial(_pool_kernel, seq=seq),
        grid=(T // tm,),
        in_specs=[pl.BlockSpec((tm, D), lambda i: (i, 0)),
                  pl.BlockSpec((POOL_HALO, D), lambda i: (jnp.maximum(i * ratio - 1, 0), 0)),
                  pl.BlockSpec((G, gdim, gdim), lambda i: (0, 0, 0)),
                  pl.BlockSpec((1, D), lambda i: (0, 0))],
        out_specs=pl.BlockSpec((tm, D), lambda i: (i, 0)),
        out_shape=jax.ShapeDtypeStruct((T, D), BF16),
        compiler_params=_cparams(("parallel",)),
        name="pool_mix",
    )(z, z, w_grp, ls)


HGRN_CHUNK = 128
LOG2_E = 1.4426950408889634


def _hgrn_tables(c=HGRN_CHUNK):
    t = np.arange(c)
    sums = [(t[None, :] <= t[:, None])]
    masks = []
    h = c // 2
    while h >= 1:
        off = t % (2 * h)
        mid = (t // (2 * h)) * (2 * h) + h
        second = off >= h
        j = t[None, :]
        if h >= 2:
            m_q = second[:, None] & (j >= mid[:, None]) & (j <= t[:, None])
            m_k = (~second)[:, None] & (j > t[:, None]) & (j < mid[:, None])
            sums.append(m_q | m_k)
        same = (t[:, None] // (2 * h)) == (t[None, :] // (2 * h))
        masks.append(same & second[:, None] & (~second)[None, :])
        h //= 2
    sums = np.stack(sums).astype(np.float32)
    return np.concatenate([sums, sums], axis=2), np.stack(masks).astype(np.float32)


def _hgrn_kernel(q_ref, f_ref, i_ref, g_ref, lb_ref, gn_ref, sums_ref, masks_ref, o_ref,
                 st_ref, qb_ref, ke_ref, z_ref, qd_ref, kd_ref, vb_ref, eb_ref, *, heads):
    c = q_ref.shape[0]
    n_sum = sums_ref.shape[0]
    n_lvl = masks_ref.shape[0]

    @pl.when(pl.program_id(1) == 0)
    def _():
        st_ref[...] = jnp.zeros_like(st_ref)

    lb = lb_ref[...]
    sig = jax.nn.sigmoid(f_ref[...])
    fg = lb + (1.0 - lb) * sig
    lf2 = jnp.log(fg) * np.float32(LOG2_E)
    kk = (1.0 - lb) * (1.0 - sig)
    lf_hi, lf_lo = _split_bf16(lf2)
    lf_cat = jnp.concatenate([lf_hi, lf_lo], axis=0)
    qr = q_ref[...].astype(F32)
    q = qr * jax.nn.sigmoid(qr)
    vb_ref[...] = i_ref[...].astype(BF16)
    q16 = q.astype(BF16)
    k16 = kk.astype(BF16)
    qd_ref[...] = q16
    kd_ref[...] = k16

    b2 = _dot(sums_ref[0], lf_cat)
    eb = jnp.exp2(b2)
    qb_ref[...] = (q * eb).astype(BF16)
    eb_ref[...] = eb[c - 1:c, :]
    ke_ref[...] = (kk * jnp.exp2(b2[c - 1:c, :] - b2)).astype(BF16)

    row = lax.broadcasted_iota(I32, (c, 1), 0)
    for l in range(n_lvl):
        half = c >> (l + 1)
        second = (row & (2 * half - 1)) >= half
        if l + 1 < n_sum:
            x = jnp.exp2(_dot(sums_ref[l + 1], lf_cat).astype(BF16))
        else:
            x = jnp.where(second, fg, 1.0).astype(BF16)
        z_ref[l] = jnp.where(second, q16, k16) * x

    gn = gn_ref[...]

    def head(hd, carry):
        cs = pl.ds(pl.multiple_of(hd * HEAD_DIM, HEAD_DIM), HEAD_DIM)
        z = z_ref[0, :, cs]
        scores = masks_ref[0] * _dot_nt(z, z)
        for l in range(1, n_lvl):
            z = z_ref[l, :, cs]
            scores = scores + masks_ref[l] * _dot_nt(z, z)
        st = st_ref[hd]
        vh = vb_ref[:, cs]
        diag = jnp.sum(qd_ref[:, cs].astype(F32) * kd_ref[:, cs].astype(F32), axis=-1,
                       keepdims=True)
        o = (_dot_nt(qb_ref[:, cs], st.astype(BF16)) + _dot(scores.astype(BF16), vh)
             + diag * vh.astype(F32))
        st_ref[hd] = st * eb_ref[:, cs] + _dot_tn(vh, ke_ref[:, cs])
        on = o * lax.rsqrt(jnp.mean(o * o, axis=-1, keepdims=True) + RMS_EPS) * gn
        gv = g_ref[:, cs].astype(F32)
        o_ref[:, cs] = (on * (gv * jax.nn.sigmoid(gv))).astype(BF16)
        return carry

    lax.fori_loop(0, heads, head, 0, unroll=4)


def _hgrn_mix(proj, f_pre, lb, gnorm_g, batch, seq):
    T, four_d = proj.shape
    D = four_d // 4
    heads = D // HEAD_DIM
    c = HGRN_CHUNK
    n_chunks = seq // c
    sums, masks = _hgrn_tables(c)
    n_sum, n_mask = sums.shape[0], masks.shape[0]
    row = lambda b, i: b * n_chunks + i
    return pl.pallas_call(
        functools.partial(_hgrn_kernel, heads=heads),
        grid=(batch, n_chunks),
        in_specs=[pl.BlockSpec((c, D), lambda b, i: (row(b, i), 0)),
                  pl.BlockSpec((c, D), lambda b, i: (row(b, i), 0)),
                  pl.BlockSpec((c, D), lambda b, i: (row(b, i), 2)),
                  pl.BlockSpec((c, D), lambda b, i: (row(b, i), 3)),
                  pl.BlockSpec((1, D), lambda b, i: (0, 0)),
                  pl.BlockSpec((1, HEAD_DIM), lambda b, i: (0, 0)),
                  pl.BlockSpec((n_sum, c, 2 * c), lambda b, i: (0, 0, 0)),
                  pl.BlockSpec((n_mask, c, c), lambda b, i: (0, 0, 0))],
        out_specs=pl.BlockSpec((c, D), lambda b, i: (row(b, i), 0)),
        out_shape=jax.ShapeDtypeStruct((T, D), BF16),
        scratch_shapes=[pltpu.VMEM((heads, HEAD_DIM, HEAD_DIM), F32),
                        pltpu.VMEM((c, D), BF16),
                        pltpu.VMEM((c, D), BF16),
                        pltpu.VMEM((n_mask, c, D), BF16),
                        pltpu.VMEM((c, D), BF16),
                        pltpu.VMEM((c, D), BF16),
                        pltpu.VMEM((c, D), BF16),
                        pltpu.VMEM((1, D), F32)],
        compiler_params=_cparams(("parallel", "arbitrary")),
        name="hgrn_mix",
    )(proj, f_pre, proj, proj, lb, gnorm_g, jnp.asarray(sums, BF16), jnp.asarray(masks, F32))


def _route_kernel(x_ref, g_ref, sc_ref, sh_ref, rw_ref, rb_ref, tri_ref,
                  hp_ref, idx_ref, gate_ref, rank_ref, cnt_ref, carry_ref, *, n_experts):
    @pl.when(pl.program_id(0) == 0)
    def _():
        carry_ref[...] = jnp.zeros_like(carry_ref)

    h = _modulated_norm(x_ref[...], g_ref[...], sc_ref[0], sh_ref[0])
    hp_ref[...] = _pack_halves(h)

    h_hi, h_lo = _split_bf16(h)
    w_hi, w_lo = _split_bf16(rw_ref[...])
    logits = _dot(h_hi, w_hi) + _dot(h_lo, w_hi) + _dot(h_hi, w_lo) + rb_ref[...]
    lt = jnp.transpose(logits)[:n_experts, :]
    eidx = lax.broadcasted_iota(I32, lt.shape, 0)
    vals, idxs, hits = [], [], []
    for _ in range(TOP_K):
        m = jnp.max(lt, axis=0, keepdims=True)
        sel = jnp.min(jnp.where(lt == m, eidx, n_experts), axis=0, keepdims=True)
        hit = eidx == sel
        vals.append(m)
        idxs.append(sel)
        hits.append(hit)
        lt = jnp.where(hit, -jnp.inf, lt)
    exps = [jnp.exp(v - vals[0]) for v in vals]
    denom = exps[0]
    for e in exps[1:]:
        denom = denom + e
    idx_ref[...] = jnp.concatenate(idxs, axis=0)
    gate_ref[...] = jnp.concatenate([e / denom for e in exps], axis=0)

    onehot = hits[0].astype(F32)
    for hit in hits[1:]:
        onehot = onehot + hit.astype(F32)
    before = carry_ref[...] + _dot(onehot.astype(BF16), tri_ref[...])
    rank_ref[...] = jnp.concatenate(
        [jnp.sum(jnp.where(hit, before, 0.0), axis=0, keepdims=True) for hit in hits],
        axis=0).astype(I32)
    carry_ref[...] += jnp.sum(onehot, axis=1, keepdims=True)
    cnt_ref[...] = carry_ref[...].astype(I32)


def _route(x, g, scale, shift, r_w, r_b, seq, *, tm=512):
    T, D = x.shape
    E = r_w.shape[1]
    rw_pad = jnp.zeros((D, LANES), F32).at[:, :E].set(r_w)
    rb_pad = jnp.full((1, LANES), -1e30, F32).at[0, :E].set(r_b)
    tri = jnp.asarray(np.triu(np.ones((tm, tm), np.float32), k=1), BF16)
    bmap = lambda i: ((i * tm) // seq, 0, 0)
    kt = pl.BlockSpec((TOP_K, tm), lambda i: (0, i))
    return pl.pallas_call(
        functools.partial(_route_kernel, n_experts=E),
        grid=(T // tm,),
        in_specs=[pl.BlockSpec((tm, D), lambda i: (i, 0)),
                  pl.BlockSpec((1, D), lambda i: (0, 0)),
                  pl.BlockSpec((1, 1, D), bmap),
                  pl.BlockSpec((1, 1, D), bmap),
                  pl.BlockSpec((D, LANES), lambda i: (0, 0)),
                  pl.BlockSpec((1, LANES), lambda i: (0, 0)),
                  pl.BlockSpec((tm, tm), lambda i: (0, 0))],
        out_specs=[pl.BlockSpec((tm, D // 2), lambda i: (i, 0)), kt, kt, kt,
                   pl.BlockSpec((E, 1), lambda i: (0, 0))],
        out_shape=[jax.ShapeDtypeStruct((T, D // 2), I32),
                   jax.ShapeDtypeStruct((TOP_K, T), I32),
                   jax.ShapeDtypeStruct((TOP_K, T), F32),
                   jax.ShapeDtypeStruct((TOP_K, T), I32),
                   jax.ShapeDtypeStruct((E, 1), I32)],
        scratch_shapes=[pltpu.VMEM((E, 1), F32)],
        compiler_params=_cparams(("arbitrary",)),
        name="moe_route",
    )(x, g, scale, shift, rw_pad, rb_pad, tri)


def _sc_gather(table, idx):
    R = idx.shape[0]
    W = table.shape[1]
    rows = SC_ROWS_PER_STEP // 2
    per_worker = R // SC_WORKERS
    steps = per_worker // rows
    assert steps * rows * SC_WORKERS == R and steps % 2 == 0
    mesh = plsc.VectorSubcoreMesh(core_axis_name="c", subcore_axis_name="s")
    n_cores = mesh.num_cores

    @functools.partial(
        pl.kernel, mesh=mesh,
        out_type=jax.ShapeDtypeStruct((R, W), table.dtype),
        scratch_types=[pltpu.VMEM((steps, rows), I32),
                       pltpu.VMEM((rows, W), table.dtype),
                       pltpu.VMEM((rows, W), table.dtype),
                       pltpu.SemaphoreType.DMA((2,)),
                       pltpu.SemaphoreType.DMA((2,))],
    )
    def gather(table_hbm, idx_hbm, out_hbm, idx_v, buf0, buf1, gsem, wsem):
        wid = lax.axis_index("s") * n_cores + lax.axis_index("c")
        base = wid * per_worker
        bufs = (buf0, buf1)
        pltpu.sync_copy(idx_hbm.at[pl.ds(wid * steps, steps)], idx_v)

        def fetch(j, b):
            return pltpu.make_async_copy(table_hbm.at[idx_v.at[j]], bufs[b], gsem.at[b])

        def flush(j, b):
            off = pl.multiple_of(base + j * rows, 8)
            return pltpu.make_async_copy(bufs[b], out_hbm.at[pl.ds(off, rows)], wsem.at[b])

        fetch(0, 0).start()

        @pl.loop(0, steps, step=2)
        def _(j0):
            for b in range(2):
                j = j0 + b
                fetch(j, b).wait()
                flush(j, b).start()

                @pl.when(j + 1 < steps)
                def _():
                    @pl.when(j >= 1)
                    def _():
                        flush(j - 1, 1 - b).wait()
                    fetch(j + 1, 1 - b).start()

        flush(steps - 2, 0).wait()
        flush(steps - 1, 1).wait()

    return gather(table, idx.reshape(R // rows, rows))


def _sc_scatter_rows(table, slots, n_out):
    K, T = slots.shape
    W = table.shape[1]
    rows = SC_ROWS_PER_STEP
    per_worker = T // SC_WORKERS
    steps = per_worker // rows
    assert steps * rows * SC_WORKERS == T
    mesh = plsc.VectorSubcoreMesh(core_axis_name="c", subcore_axis_name="s")
    n_cores = mesh.num_cores

    @functools.partial(
        pl.kernel, mesh=mesh,
        out_type=jax.ShapeDtypeStruct((n_out, W), table.dtype),
        scratch_types=[pltpu.VMEM((K, rows), I32),
                       pltpu.VMEM((rows, W), table.dtype)],
    )
    def scatter(table_hbm, slots_hbm, out_hbm, idx_v, rows_v):
        wid = lax.axis_index("s") * n_cores + lax.axis_index("c")
        base = wid * per_worker

        @pl.loop(0, steps)
        def _(j):
            off = pl.multiple_of(base + j * rows, 8)
            pltpu.sync_copy(table_hbm.at[pl.ds(off, rows)], rows_v)
            for k in range(K):
                pltpu.sync_copy(slots_hbm.at[pl.ds(pl.multiple_of(k * T + off, 8), rows)],
                                idx_v.at[k])
                pltpu.sync_copy(rows_v, out_hbm.at[idx_v.at[k]])

    return scatter(table, slots.reshape(K * T))


def _expert_kernel(blk_e_ref, n_valid_ref, xs_ref, wg_ref, bg_ref, wl_ref, bl_ref,
                   wo_ref, bo_ref, y_ref, wgb_ref, wlb_ref, wob_ref):
    i = pl.program_id(0)
    n_valid = n_valid_ref[i]
    new_expert = jnp.logical_or(i == 0, blk_e_ref[i] != blk_e_ref[jnp.maximum(i - 1, 0)])

    @pl.when(jnp.logical_and(new_expert, n_valid > 0))
    def _():
        wgb_ref[...] = wg_ref[0].astype(BF16)
        wlb_ref[...] = wl_ref[0].astype(BF16)
        wob_ref[...] = wo_ref[0].astype(BF16)

    @pl.when(n_valid > 0)
    def _():
        half = xs_ref.shape[1]
        live = lax.broadcasted_iota(I32, (xs_ref.shape[0], 1), 0) < n_valid
        lo, hi = _unpack_halves(jnp.where(live, xs_ref[...], 0))
        lo = lo.astype(BF16)
        hi = hi.astype(BF16)
        a = _dot(lo, wgb_ref[:half, :]) + _dot(hi, wgb_ref[half:, :]) + bg_ref[0]
        l = _dot(lo, wlb_ref[:half, :]) + _dot(hi, wlb_ref[half:, :]) + bl_ref[0]
        a = jnp.minimum(a, SWIGLU_LIMIT)
        l = jnp.clip(l, -SWIGLU_LIMIT, SWIGLU_LIMIT)
        act = a * jax.nn.sigmoid(SWIGLU_ALPHA * a) * (l + 1.0)
        y_ref[...] = _pack_halves(_dot(act.astype(BF16), wob_ref[...]) + bo_ref[0])

    @pl.when(n_valid == 0)
    def _():
        y_ref[...] = jnp.zeros_like(y_ref)


def _experts(xs, blk_e, n_valid, layer, w_glu, b_glu, w_lin, b_lin, w_out, b_out, *, blk):
    P, half = xs.shape
    L, E, D, F = w_glu.shape
    n_blocks = P // blk
    wmap = lambda i, be, nv: (layer * E + be[i], 0, 0)
    w_glu, w_lin = w_glu.reshape(L * E, D, F), w_lin.reshape(L * E, D, F)
    w_out = w_out.reshape(L * E, F, D)
    b_glu, b_lin = b_glu.reshape(L * E, 1, F), b_lin.reshape(L * E, 1, F)
    b_out = b_out.reshape(L * E, 1, D)
    grid_spec = pltpu.PrefetchScalarGridSpec(
        num_scalar_prefetch=2,
        grid=(n_blocks,),
        in_specs=[pl.BlockSpec((blk, half), lambda i, be, nv: (i, 0)),
                  pl.BlockSpec((1, D, F), wmap),
                  pl.BlockSpec((1, 1, F), wmap),
                  pl.BlockSpec((1, D, F), wmap),
                  pl.BlockSpec((1, 1, F), wmap),
                  pl.BlockSpec((1, F, D), wmap),
                  pl.BlockSpec((1, 1, D), wmap)],
        out_specs=pl.BlockSpec((blk, half), lambda i, be, nv: (i, 0)),
        scratch_shapes=[pltpu.VMEM((D, F), BF16), pltpu.VMEM((D, F), BF16),
                        pltpu.VMEM((F, D), BF16)],
    )
    return pl.pallas_call(
        _expert_kernel,
        grid_spec=grid_spec,
        out_shape=jax.ShapeDtypeStruct((P, half), I32),
        compiler_params=pltpu.CompilerParams(dimension_semantics=("arbitrary",),
                                             vmem_limit_bytes=EXPERT_VMEM_LIMIT),
        name="moe_experts",
    )(blk_e, n_valid, xs, w_glu, b_glu, w_lin, b_lin, w_out, b_out)


def _combine_kernel(ys_ref, tg_ref, x_ref, gate_ref, o_ref):
    half = ys_ref.shape[2]
    tg = tg_ref[...]
    lo_sum = hi_sum = None
    for k in range(ys_ref.shape[0]):
        lo, hi = _unpack_halves(ys_ref[k])
        gk = tg[:, k:k + 1]
        lo_sum = gk * lo if lo_sum is None else lo_sum + gk * lo
        hi_sum = gk * hi if hi_sum is None else hi_sum + gk * hi
    o_ref[:, :half] = x_ref[:, :half] + gate_ref[0, :, :half] * lo_sum
    o_ref[:, half:] = x_ref[:, half:] + gate_ref[0, :, half:] * hi_sum


def _combine(ys, tok_gate, x, gate, seq, *, tm=256):
    K, T, half = ys.shape
    D = x.shape[1]
    return pl.pallas_call(
        _combine_kernel,
        grid=(T // tm,),
        in_specs=[pl.BlockSpec((K, tm, half), lambda i: (0, i, 0)),
                  pl.BlockSpec((tm, K), lambda i: (i, 0)),
                  pl.BlockSpec((tm, D), lambda i: (i, 0)),
                  pl.BlockSpec((1, 1, D), lambda i: ((i * tm) // seq, 0, 0))],
        out_specs=pl.BlockSpec((tm, D), lambda i: (i, 0)),
        out_shape=jax.ShapeDtypeStruct((T, D), F32),
        compiler_params=_cparams(("parallel",)),
        name="moe_combine",
    )(ys, tok_gate, x, gate)


def _final_kernel(x_ref, g_ref, o_ref):
    x = x_ref[...]
    ms = jnp.mean(x * x, axis=-1, keepdims=True)
    o_ref[...] = x * lax.rsqrt(ms + RMS_EPS) * g_ref[...]


def _final_norm(x, g, *, tm=512):
    T, D = x.shape
    return pl.pallas_call(
        _final_kernel,
        grid=(T // tm,),
        in_specs=[pl.BlockSpec((tm, D), lambda i: (i, 0)),
                  pl.BlockSpec((1, D), lambda i: (0, 0))],
        out_specs=pl.BlockSpec((tm, D), lambda i: (i, 0)),
        out_shape=jax.ShapeDtypeStruct((T, D), F32),
        compiler_params=_cparams(("parallel",)),
        name="final_norm",
    )(x, g)


MOE_BLOCK = 256


def _dispatch_plan(idx_t, rank_t, counts, blk):
    K, T = idx_t.shape
    E = counts.shape[0]
    padded = (counts + blk - 1) // blk * blk
    pad_end = jnp.cumsum(padded)
    pad_start = pad_end - padded
    experts = jnp.arange(E, dtype=I32)
    slots = rank_t + jnp.sum(jnp.where(idx_t[..., None] == experts, pad_start, 0), axis=-1)
    n_blocks = -(-(K * T + E * (blk - 1)) // blk)
    blk_start = jnp.arange(n_blocks, dtype=I32) * blk
    blk_e = jnp.minimum(jnp.sum(pad_end[None, :] <= blk_start[:, None], axis=1), E - 1).astype(I32)
    n_valid = jnp.clip(pad_start[blk_e] + counts[blk_e] - blk_start, 0, blk).astype(I32)
    return slots, blk_e, n_valid, n_blocks * blk


def _moe(x, g, scale, shift, gate, r_w, r_b, layer, w_glu, b_glu, w_lin, b_lin, w_out, b_out,
         seq):
    T, D = x.shape
    hp, idx_t, gate_t, rank_t, counts = _route(x, g, scale, shift, r_w, r_b, seq)
    slots, blk_e, n_valid, n_rows = _dispatch_plan(idx_t, rank_t, counts[:, 0], MOE_BLOCK)
    xs = _sc_scatter_rows(hp, slots, n_rows)
    yp = _experts(xs, blk_e, n_valid, layer, w_glu, b_glu, w_lin, b_lin, w_out, b_out,
                  blk=MOE_BLOCK)
    ys = _sc_gather(yp, slots.reshape(TOP_K * T)).reshape(TOP_K, T, D // 2)
    return _combine(ys, jnp.transpose(gate_t), x, gate, seq)


def kernel(x, c, norm_g, ada_w, final_norm_g, sg_w_in, sg_ln_g, sg_ln_b, sg_w_s, sg_b_s, sg_w_out, pool_w_in, pool_w_grp, pool_ls, pool_w_out, hgrn_w_in, hgrn_lb_logits, hgrn_gnorm_g, hgrn_w_out, router_w, router_b, expert_w_glu, expert_b_glu, expert_w_lin, expert_b_lin, expert_w_out, expert_b_out):
    B, S, D = x.shape
    depth = norm_g.shape[0]
    T = B * S
    xf = x.reshape(T, D)

    mod = _ada_mod(c, ada_w.reshape(depth * 2, D, 3 * D))

    def mod_parts(layer, sub):
        m = mod[layer * 2 + sub][:, None, :]
        return m[..., :D], m[..., D:2 * D], m[..., 2 * D:]

    lb_cum = jnp.cumsum(jax.nn.softmax(hgrn_lb_logits.astype(F32), axis=0), axis=0)
    lower_bounds = lb_cum - lb_cum[0]

    for layer in range(depth):
        kind, slot = layer % 3, layer // 3
        shift, scale, gate = mod_parts(layer, 0)
        g = norm_g[layer, 0][None, :]
        if kind == 0:
            z = _norm_matmul(xf, g, scale, shift, sg_w_in[slot].astype(BF16), S,
                             act="gelu", out_dtype=BF16)
            y = _spatial_gate(z, sg_ln_g[slot][None, :], sg_ln_b[slot][None, :],
                              sg_w_s[slot], sg_b_s[slot])
            w_out = sg_w_out[slot]
        elif kind == 1:
            z = _norm_matmul(xf, g, scale, shift, pool_w_in[slot].astype(BF16), S,
                             act=None, out_dtype=F32)
            y = _pool_mix(z, pool_w_grp[slot].astype(BF16), pool_ls[slot][None, :], S)
            w_out = pool_w_out[slot]
        else:
            proj, f_pre = _norm_matmul(xf, g, scale, shift, hgrn_w_in[slot].astype(BF16), S,
                                       act=None, out_dtype=BF16, f32_col=1, tn=D)
            y = _hgrn_mix(proj, f_pre, lower_bounds[layer][None, :],
                          hgrn_gnorm_g[slot][None, :], B, S)
            w_out = hgrn_w_out[slot]
        xf = _matmul_residual(y, w_out.astype(BF16), xf, gate, S)

        shift, scale, gate = mod_parts(layer, 1)
        xf = _moe(xf, norm_g[layer, 1][None, :], scale, shift, gate,
                  router_w[layer], router_b[layer], layer,
                  expert_w_glu, expert_b_glu, expert_w_lin, expert_b_lin,
                  expert_w_out, expert_b_out, S)

    return _final_norm(xf, final_norm_g[None, :]).reshape(B, S, D)
```

```python
import functools

import numpy as np
import jax
import jax.numpy as jnp
from jax import lax
from jax.experimental import pallas as pl
from jax.experimental.pallas import tpu as pltpu
from jax.experimental.pallas import tpu_sc as plsc

F32 = jnp.float32
BF16 = jnp.bfloat16
I32 = jnp.int32
U32 = jnp.uint32

RMS_EPS = 1e-5
LN_EPS = 1e-5
SWIGLU_ALPHA = 1.702
SWIGLU_LIMIT = 7.0
TOP_K = 4
SG_CHUNK = 128
POOL_WINDOWS = (2, 4, 8, 16)
HEAD_DIM = 128
LANES = 128
SC_WORKERS = 32
SC_ROWS_PER_STEP = 64

VMEM_LIMIT = 48 * 1024 * 1024
BIG_VMEM_LIMIT = 56 * 1024 * 1024


def _cparams(sem):
    return pltpu.CompilerParams(dimension_semantics=sem, vmem_limit_bytes=VMEM_LIMIT)


def _dot(a, b):
    return jnp.dot(a, b, preferred_element_type=F32)


def _dot_nt(a, b):
    return lax.dot_general(a, b, (((1,), (1,)), ((), ())), preferred_element_type=F32)


def _dot_tn(a, b):
    return lax.dot_general(a, b, (((0,), (0,)), ((), ())), preferred_element_type=F32)


def _split_bf16(x):
    hi = x.astype(BF16)
    lo = (x - hi.astype(F32)).astype(BF16)
    return hi, lo


def _packed_layout(d):
    return d // 2, I32


def _pack_halves(y):
    n = y.shape[1] // 2
    lo = lax.bitcast_convert_type(y[:, :n].astype(BF16).astype(F32), U32)
    hi = lax.bitcast_convert_type(y[:, n:].astype(BF16).astype(F32), U32)
    word = lax.shift_right_logical(lo, jnp.uint32(16)) | (hi & jnp.uint32(0xFFFF0000))
    return lax.bitcast_convert_type(word, I32)


def _unpack_halves(w):
    u = lax.bitcast_convert_type(w, U32)
    lo = lax.bitcast_convert_type(lax.shift_left(u, jnp.uint32(16)), F32)
    hi = lax.bitcast_convert_type(u & jnp.uint32(0xFFFF0000), F32)
    return lo, hi


def _modulated_norm(x, g, scale, shift):
    ms = jnp.mean(x * x, axis=-1, keepdims=True)
    y = x * lax.rsqrt(ms + RMS_EPS) * g
    return y * (1.0 + scale) + shift


def _ada_kernel(c_ref, w_ref, o_ref):
    c = c_ref[...]
    s = c * jax.nn.sigmoid(c)
    o_ref[0] = _dot(s.astype(BF16), w_ref[0].astype(BF16))


def _ada_mod(c, ada_w):
    L, D, N = ada_w.shape
    B = c.shape[0]
    rows = 8
    tn = 512
    c_pad = jnp.zeros((rows, D), F32).at[:B].set(c)
    out = pl.pallas_call(
        _ada_kernel,
        grid=(L, N // tn),
        in_specs=[pl.BlockSpec((rows, D), lambda l, j: (0, 0)),
                  pl.BlockSpec((1, D, tn), lambda l, j: (l, 0, j))],
        out_specs=pl.BlockSpec((1, rows, tn), lambda l, j: (l, 0, j)),
        out_shape=jax.ShapeDtypeStruct((L, rows, N), F32),
        compiler_params=_cparams(("parallel", "parallel")),
        name="ada_mod",
    )(c_pad, ada_w)
    return out[:, :B]


def _nm_kernel(x_ref, g_ref, sc_ref, sh_ref, w_ref, o_ref, *rest, act, f32_col):
    h_ref = rest[-1]

    @pl.when(pl.program_id(1) == 0)
    def _():
        h = _modulated_norm(x_ref[...], g_ref[...], sc_ref[0], sh_ref[0])
        h_ref[...] = h.astype(BF16)

    acc = _dot(h_ref[...], w_ref[...])
    if act == "gelu":
        acc = 0.5 * acc * (1.0 + lax.erf(acc * np.float32(1.0 / np.sqrt(2.0))))
    o_ref[...] = acc.astype(o_ref.dtype)
    if f32_col is not None:
        @pl.when(pl.program_id(1) == f32_col)
        def _():
            rest[0][...] = acc


def _norm_matmul(x, g, scale, shift, w, seq, *, act, out_dtype, f32_col=None, tm=512, tn=2048):
    T, D = x.shape
    N = w.shape[1]
    bmap = lambda i, j: ((i * tm) // seq, 0, 0)
    out_specs = [pl.BlockSpec((tm, tn), lambda i, j: (i, j))]
    out_shape = [jax.ShapeDtypeStruct((T, N), out_dtype)]
    if f32_col is not None:
        out_specs.append(pl.BlockSpec((tm, tn), lambda i, j: (i, 0)))
        out_shape.append(jax.ShapeDtypeStruct((T, tn), F32))
    outs = pl.pallas_call(
        functools.partial(_nm_kernel, act=act, f32_col=f32_col),
        grid=(T // tm, N // tn),
        in_specs=[pl.BlockSpec((tm, D), lambda i, j: (i, 0)),
                  pl.BlockSpec((1, D), lambda i, j: (0, 0)),
                  pl.BlockSpec((1, 1, D), bmap),
                  pl.BlockSpec((1, 1, D), bmap),
                  pl.BlockSpec((D, tn), lambda i, j: (0, j))],
        out_specs=out_specs,
        out_shape=out_shape,
        scratch_shapes=[pltpu.VMEM((tm, D), BF16)],
        compiler_params=pltpu.CompilerParams(dimension_semantics=("parallel", "arbitrary"),
                                             vmem_limit_bytes=BIG_VMEM_LIMIT),
        name="norm_matmul_" + str(act),
    )(x, g, scale, shift, w)
    return outs if f32_col is not None else outs[0]


def _mr_kernel(y_ref, w_ref, x_ref, gate_ref, o_ref):
    o_ref[...] = x_ref[...] + gate_ref[0] * _dot(y_ref[...], w_ref[...])


def _matmul_residual(y, w, x, gate, seq, *, tm=512):
    T, K = y.shape
    D = w.shape[1]
    return pl.pallas_call(
        _mr_kernel,
        grid=(T // tm,),
        in_specs=[pl.BlockSpec((tm, K), lambda i: (i, 0)),
                  pl.BlockSpec((K, D), lambda i: (0, 0)),
                  pl.BlockSpec((tm, D), lambda i: (i, 0)),
                  pl.BlockSpec((1, 1, D), lambda i: ((i * tm) // seq, 0, 0))],
        out_specs=pl.BlockSpec((tm, D), lambda i: (i, 0)),
        out_shape=jax.ShapeDtypeStruct((T, D), F32),
        compiler_params=_cparams(("parallel",)),
        name="matmul_residual",
    )(y, w, x, gate)


def _sg_kernel(u_ref, v_ref, lng_ref, lnb_ref, ws_ref, bst_ref, y_ref, vn_ref, *, heads):
    v = v_ref[...].astype(F32)
    mu = jnp.mean(v, axis=-1, keepdims=True)
    vc = v - mu
    var = jnp.mean(vc * vc, axis=-1, keepdims=True)
    vn_ref[...] = (vc * lax.rsqrt(var + LN_EPS) * lng_ref[...] + lnb_ref[...]).astype(BF16)

    n_chunks = v_ref.shape[0] // SG_CHUNK
    row = lax.broadcasted_iota(I32, (SG_CHUNK, SG_CHUNK), 0)
    col = lax.broadcasted_iota(I32, (SG_CHUNK, SG_CHUNK), 1)
    causal = row >= col
    for hd in range(heads):
        cs = slice(hd * HEAD_DIM, (hd + 1) * HEAD_DIM)
        wm = jnp.where(causal, ws_ref[hd], 0.0).astype(BF16)
        bias = bst_ref[:, hd:hd + 1]
        for ci in range(n_chunks):
            rs = slice(ci * SG_CHUNK, (ci + 1) * SG_CHUNK)
            mixed = _dot(wm, vn_ref[rs, cs]) + bias
            y_ref[rs, cs] = (u_ref[rs, cs].astype(F32) * mixed).astype(BF16)


def _spatial_gate(z, ln_g, ln_b, w_s, b_s, *, tm=256):
    T, two_w = z.shape
    W = two_w // 2
    heads = w_s.shape[0]
    return pl.pallas_call(
        functools.partial(_sg_kernel, heads=heads),
        grid=(T // tm,),
        in_specs=[pl.BlockSpec((tm, W), lambda i: (i, 0)),
                  pl.BlockSpec((tm, W), lambda i: (i, 1)),
                  pl.BlockSpec((1, W), lambda i: (0, 0)),
                  pl.BlockSpec((1, W), lambda i: (0, 0)),
                  pl.BlockSpec((heads, SG_CHUNK, SG_CHUNK), lambda i: (0, 0, 0)),
                  pl.BlockSpec((SG_CHUNK, heads), lambda i: (0, 0))],
        out_specs=pl.BlockSpec((tm, W), lambda i: (i, 0)),
        out_shape=jax.ShapeDtypeStruct((T, W), BF16),
        scratch_shapes=[pltpu.VMEM((tm, W), BF16)],
        compiler_params=_cparams(("parallel",)),
        name="spatial_gate",
    )(z, z, ln_g, ln_b, w_s, jnp.transpose(b_s))


POOL_HALO = 16


def _pool_kernel(z_ref, halo_ref, wg_ref, ls_ref, y_ref, *, seq):
    tm = z_ref.shape[0]
    gdim = wg_ref.shape[1]
    pos0 = (pl.program_id(0) * tm) % seq
    halo_on = (pos0 > 0).astype(F32)
    pos = pos0 + lax.broadcasted_iota(I32, (tm, 1), 0)
    for gi, wnd in enumerate(POOL_WINDOWS):
        cs = slice(gi * gdim, (gi + 1) * gdim)
        zg = z_ref[:, cs]
        s = jnp.concatenate([halo_ref[:, cs] * halo_on, zg], axis=0)
        k = 1
        while k < wnd:
            s = s + pltpu.roll(s, k, 0)
            k *= 2
        cnt = jnp.minimum(pos + 1, wnd).astype(F32)
        pooled = s[POOL_HALO:, :] / cnt - zg
        yg = _dot(pooled.astype(BF16), wg_ref[gi]) * ls_ref[:, cs]
        y_ref[:, cs] = yg.astype(BF16)


def _pool_mix(z, w_grp, ls, seq, *, tm=256):
    T, D = z.shape
    G, gdim, _ = w_grp.shape
    assert max(POOL_WINDOWS) <= POOL_HALO and tm % POOL_HALO == 0
    ratio = tm // POOL_HALO
    return pl.pallas_call(
        functools.partial(_pool_kernel, seq=seq),
        grid=(T // tm,),
        in_specs=[pl.BlockSpec((tm, D), lambda i: (i, 0)),
                  pl.BlockSpec((POOL_HALO, D), lambda i: (jnp.maximum(i * ratio - 1, 0), 0)),
                  pl.BlockSpec((G, gdim, gdim), lambda i: (0, 0, 0)),
                  pl.BlockSpec((1, D), lambda i: (0, 0))],
        out_specs=pl.BlockSpec((tm, D), lambda i: (i, 0)),
        out_shape=jax.ShapeDtypeStruct((T, D), BF16),
        compiler_params=_cparams(("parallel",)),
        name="pool_mix",
    )(z, z, w_grp, ls)


HGRN_CHUNK = 128
LOG2_E = 1.4426950408889634


def _hgrn_tables(c=HGRN_CHUNK):
    t = np.arange(c)
    sums = [(t[None, :] <= t[:, None])]
    masks = []
    h = c // 2
    while h >= 1:
        off = t % (2 * h)
        mid = (t // (2 * h)) * (2 * h) + h
        second = off >= h
        j = t[None, :]
        if h >= 2:
            m_q = second[:, None] & (j >= mid[:, None]) & (j <= t[:, None])
            m_k = (~second)[:, None] & (j > t[:, None]) & (j < mid[:, None])
            sums.append(m_q | m_k)
        same = (t[:, None] // (2 * h)) == (t[None, :] // (2 * h))
        masks.append(same & second[:, None] & (~second)[None, :])
        h //= 2
    sums = np.stack(sums).astype(np.float32)
    return np.concatenate([sums, sums], axis=2), np.stack(masks).astype(np.float32)


def _hgrn_kernel(q_ref, f_ref, i_ref, g_ref, lb_ref, gn_ref, sums_ref, masks_ref, o_ref,
                 st_ref, qb_ref, ke_ref, z_ref, qd_ref, kd_ref, vb_ref, eb_ref, *, heads):
    c = q_ref.shape[0]
    n_sum = sums_ref.shape[0]
    n_lvl = masks_ref.shape[0]

    @pl.when(pl.program_id(1) == 0)
    def _():
        st_ref[...] = jnp.zeros_like(st_ref)

    lb = lb_ref[...]
    sig = jax.nn.sigmoid(f_ref[...])
    fg = lb + (1.0 - lb) * sig
    lf2 = jnp.log(fg) * np.float32(LOG2_E)
    kk = (1.0 - lb) * (1.0 - sig)
    lf_hi, lf_lo = _split_bf16(lf2)
    lf_cat = jnp.concatenate([lf_hi, lf_lo], axis=0)
    qr = q_ref[...].astype(F32)
    q = qr * jax.nn.sigmoid(qr)
    vb_ref[...] = i_ref[...].astype(BF16)
    q16 = q.astype(BF16)
    k16 = kk.astype(BF16)
    qd_ref[...] = q16
    kd_ref[...] = k16

    b2 = _dot(sums_ref[0], lf_cat)
    eb = jnp.exp2(b2)
    qb_ref[...] = (q * eb).astype(BF16)
    eb_ref[...] = eb[c - 1:c, :]
    ke_ref[...] = (kk * jnp.exp2(b2[c - 1:c, :] - b2)).astype(BF16)

    row = lax.broadcasted_iota(I32, (c, 1), 0)
    for l in range(n_lvl):
        half = c >> (l + 1)
        second = (row & (2 * half - 1)) >= half
        if l + 1 < n_sum:
            x = jnp.exp2(_dot(sums_ref[l + 1], lf_cat).astype(BF16))
        else:
            x = jnp.where(second, fg, 1.0).astype(BF16)
        z_ref[l] = jnp.where(second, q16, k16) * x

    gn = gn_ref[...]

    def head(hd, carry):
        cs = pl.ds(pl.multiple_of(hd * HEAD_DIM, HEAD_DIM), HEAD_DIM)
        z = z_ref[0, :, cs]
        scores = masks_ref[0] * _dot_nt(z, z).astype(BF16)
        for l in range(1, n_lvl):
            z = z_ref[l, :, cs]
            scores = scores + masks_ref[l] * _dot_nt(z, z).astype(BF16)
        st = st_ref[hd]
        vh = vb_ref[:, cs]
        diag = jnp.sum(qd_ref[:, cs].astype(F32) * kd_ref[:, cs].astype(F32), axis=-1,
                       keepdims=True)
        o = (_dot_nt(qb_ref[:, cs], st.astype(BF16)) + _dot(scores, vh)
             + diag * vh.astype(F32))
        st_ref[hd] = st * eb_ref[:, cs] + _dot_tn(vh, ke_ref[:, cs])
        on = o * lax.rsqrt(jnp.mean(o * o, axis=-1, keepdims=True) + RMS_EPS) * gn
        gv = g_ref[:, cs].astype(F32)
        o_ref[:, cs] = (on * (gv * jax.nn.sigmoid(gv))).astype(BF16)
        return carry

    lax.fori_loop(0, heads, head, 0, unroll=4)


def _hgrn_mix(proj, f_pre, lb, gnorm_g, batch, seq):
    T, four_d = proj.shape
    D = four_d // 4
    heads = D // HEAD_DIM
    c = HGRN_CHUNK
    n_chunks = seq // c
    sums, masks = _hgrn_tables(c)
    n_sum, n_mask = sums.shape[0], masks.shape[0]
    row = lambda b, i: b * n_chunks + i
    return pl.pallas_call(
        functools.partial(_hgrn_kernel, heads=heads),
        grid=(batch, n_chunks),
        in_specs=[pl.BlockSpec((c, D), lambda b, i: (row(b, i), 0)),
                  pl.BlockSpec((c, D), lambda b, i: (row(b, i), 0)),
                  pl.BlockSpec((c, D), lambda b, i: (row(b, i), 2)),
                  pl.BlockSpec((c, D), lambda b, i: (row(b, i), 3)),
                  pl.BlockSpec((1, D), lambda b, i: (0, 0)),
                  pl.BlockSpec((1, HEAD_DIM), lambda b, i: (0, 0)),
                  pl.BlockSpec((n_sum, c, 2 * c), lambda b, i: (0, 0, 0)),
                  pl.BlockSpec((n_mask, c, c), lambda b, i: (0, 0, 0))],
        out_specs=pl.BlockSpec((c, D), lambda b, i: (row(b, i), 0)),
        out_shape=jax.ShapeDtypeStruct((T, D), BF16),
        scratch_shapes=[pltpu.VMEM((heads, HEAD_DIM, HEAD_DIM), F32),
                        pltpu.VMEM((c, D), BF16),
                        pltpu.VMEM((c, D), BF16),
                        pltpu.VMEM((n_mask, c, D), BF16),
                        pltpu.VMEM((c, D), BF16),
                        pltpu.VMEM((c, D), BF16),
                        pltpu.VMEM((c, D), BF16),
                        pltpu.VMEM((1, D), F32)],
        compiler_params=_cparams(("parallel", "arbitrary")),
        name="hgrn_mix",
    )(proj, f_pre, proj, proj, lb, gnorm_g, jnp.asarray(sums, BF16), jnp.asarray(masks, BF16))


def _route_kernel(x_ref, g_ref, sc_ref, sh_ref, rw_ref, rb_ref, tri_ref,
                  hp_ref, idx_ref, gate_ref, rank_ref, cnt_ref, carry_ref, *, n_experts):
    @pl.when(pl.program_id(0) == 0)
    def _():
        carry_ref[...] = jnp.zeros_like(carry_ref)

    h = _modulated_norm(x_ref[...], g_ref[...], sc_ref[0], sh_ref[0])
    hp_ref[...] = _pack_halves(h)

    h_hi, h_lo = _split_bf16(h)
    w_hi, w_lo = _split_bf16(rw_ref[...])
    logits = _dot(h_hi, w_hi) + _dot(h_lo, w_hi) + _dot(h_hi, w_lo) + rb_ref[...]
    lt = jnp.transpose(logits)[:n_experts, :]
    eidx = lax.broadcasted_iota(I32, lt.shape, 0)
    vals, idxs, hits = [], [], []
    for _ in range(TOP_K):
        m = jnp.max(lt, axis=0, keepdims=True)
        sel = jnp.min(jnp.where(lt == m, eidx, n_experts), axis=0, keepdims=True)
        hit = eidx == sel
        vals.append(m)
        idxs.append(sel)
        hits.append(hit)
        lt = jnp.where(hit, -jnp.inf, lt)
    exps = [jnp.exp(v - vals[0]) for v in vals]
    denom = exps[0]
    for e in exps[1:]:
        denom = denom + e
    idx_ref[...] = jnp.concatenate(idxs, axis=0)
    gate_ref[...] = jnp.concatenate([e / denom for e in exps], axis=0)

    onehot = hits[0].astype(F32)
    for hit in hits[1:]:
        onehot = onehot + hit.astype(F32)
    before = carry_ref[...] + _dot(onehot.astype(BF16), tri_ref[...])
    rank_ref[...] = jnp.concatenate(
        [jnp.sum(jnp.where(hit, before, 0.0), axis=0, keepdims=True) for hit in hits],
        axis=0).astype(I32)
    carry_ref[...] += jnp.sum(onehot, axis=1, keepdims=True)
    cnt_ref[...] = carry_ref[...].astype(I32)


def _route(x, g, scale, shift, r_w, r_b, seq, *, tm=512):
    T, D = x.shape
    E = r_w.shape[1]
    rw_pad = jnp.zeros((D, LANES), F32).at[:, :E].set(r_w)
    rb_pad = jnp.full((1, LANES), -1e30, F32).at[0, :E].set(r_b)
    tri = jnp.asarray(np.triu(np.ones((tm, tm), np.float32), k=1), BF16)
    pw, pdt = _packed_layout(D)
    bmap = lambda i: ((i * tm) // seq, 0, 0)
    kt = pl.BlockSpec((TOP_K, tm), lambda i: (0, i))
    return pl.pallas_call(
        functools.partial(_route_kernel, n_experts=E),
        grid=(T // tm,),
        in_specs=[pl.BlockSpec((tm, D), lambda i: (i, 0)),
                  pl.BlockSpec((1, D), lambda i: (0, 0)),
                  pl.BlockSpec((1, 1, D), bmap),
                  pl.BlockSpec((1, 1, D), bmap),
                  pl.BlockSpec((D, LANES), lambda i: (0, 0)),
                  pl.BlockSpec((1, LANES), lambda i: (0, 0)),
                  pl.BlockSpec((tm, tm), lambda i: (0, 0))],
        out_specs=[pl.BlockSpec((tm, pw), lambda i: (i, 0)), kt, kt, kt,
                   pl.BlockSpec((E, 1), lambda i: (0, 0))],
        out_shape=[jax.ShapeDtypeStruct((T, pw), pdt),
                   jax.ShapeDtypeStruct((TOP_K, T), I32),
                   jax.ShapeDtypeStruct((TOP_K, T), F32),
                   jax.ShapeDtypeStruct((TOP_K, T), I32),
                   jax.ShapeDtypeStruct((E, 1), I32)],
        scratch_shapes=[pltpu.VMEM((E, 1), F32)],
        compiler_params=_cparams(("arbitrary",)),
        name="moe_route",
    )(x, g, scale, shift, rw_pad, rb_pad, tri)


def _sc_gather(table, idx):
    R = idx.shape[0]
    W = table.shape[1]
    rows = SC_ROWS_PER_STEP // 2
    per_worker = R // SC_WORKERS
    steps = per_worker // rows
    assert steps * rows * SC_WORKERS == R and steps % 2 == 0
    mesh = plsc.VectorSubcoreMesh(core_axis_name="c", subcore_axis_name="s")
    n_cores = mesh.num_cores

    @functools.partial(
        pl.kernel, mesh=mesh,
        out_type=jax.ShapeDtypeStruct((R, W), table.dtype),
        scratch_types=[pltpu.VMEM((steps, rows), I32),
                       pltpu.VMEM((rows, W), table.dtype),
                       pltpu.VMEM((rows, W), table.dtype),
                       pltpu.SemaphoreType.DMA((2,)),
                       pltpu.SemaphoreType.DMA((2,))],
    )
    def gather(table_hbm, idx_hbm, out_hbm, idx_v, buf0, buf1, gsem, wsem):
        wid = lax.axis_index("s") * n_cores + lax.axis_index("c")
        base = wid * per_worker
        bufs = (buf0, buf1)
        pltpu.sync_copy(idx_hbm.at[pl.ds(wid * steps, steps)], idx_v)

        def fetch(j, b):
            return pltpu.make_async_copy(table_hbm.at[idx_v.at[j]], bufs[b], gsem.at[b])

        def flush(j, b):
            off = pl.multiple_of(base + j * rows, 8)
            return pltpu.make_async_copy(bufs[b], out_hbm.at[pl.ds(off, rows)], wsem.at[b])

        fetch(0, 0).start()

        @pl.loop(0, steps, step=2)
        def _(j0):
            for b in range(2):
                j = j0 + b
                fetch(j, b).wait()
                flush(j, b).start()

                @pl.when(j + 1 < steps)
                def _():
                    @pl.when(j >= 1)
                    def _():
                        flush(j - 1, 1 - b).wait()
                    fetch(j + 1, 1 - b).start()

        flush(steps - 2, 0).wait()
        flush(steps - 1, 1).wait()

    return gather(table, idx.reshape(R // rows, rows))


def _sc_scatter_rows(table, slots, n_out):
    K, T = slots.shape
    W = table.shape[1]
    rows = SC_ROWS_PER_STEP
    per_worker = T // SC_WORKERS
    steps = per_worker // rows
    assert steps * rows * SC_WORKERS == T
    mesh = plsc.VectorSubcoreMesh(core_axis_name="c", subcore_axis_name="s")
    n_cores = mesh.num_cores

    @functools.partial(
        pl.kernel, mesh=mesh,
        out_type=jax.ShapeDtypeStruct((n_out, W), table.dtype),
        scratch_types=[pltpu.VMEM((K, rows), I32),
                       pltpu.VMEM((rows, W), table.dtype)],
    )
    def scatter(table_hbm, slots_hbm, out_hbm, idx_v, rows_v):
        wid = lax.axis_index("s") * n_cores + lax.axis_index("c")
        base = wid * per_worker

        @pl.loop(0, steps)
        def _(j):
            off = pl.multiple_of(base + j * rows, 8)
            pltpu.sync_copy(table_hbm.at[pl.ds(off, rows)], rows_v)
            for k in range(K):
                pltpu.sync_copy(slots_hbm.at[pl.ds(pl.multiple_of(k * T + off, 8), rows)],
                                idx_v.at[k])
                pltpu.sync_copy(rows_v, out_hbm.at[idx_v.at[k]])

    return scatter(table, slots.reshape(K * T))


def _expert_kernel(blk_e_ref, n_valid_ref, next_e_ref, xs_ref, wg_hbm, bg_ref, wl_hbm, bl_ref,
                   wo_hbm, bo_ref, y_ref, wg_st, wl_st, wo_st, wgb_ref, wlb_ref, wob_ref, sem,
                   *, expert_base):
    i = pl.program_id(0)
    e = blk_e_ref[i]
    n_valid = n_valid_ref[i]
    new_expert = jnp.logical_or(i == 0, e != blk_e_ref[jnp.maximum(i - 1, 0)])

    def weight_copies(expert):
        idx = expert_base + expert
        return (pltpu.make_async_copy(wg_hbm.at[idx], wg_st, sem.at[0]),
                pltpu.make_async_copy(wl_hbm.at[idx], wl_st, sem.at[1]),
                pltpu.make_async_copy(wo_hbm.at[idx], wo_st, sem.at[2]))

    @pl.when(i == 0)
    def _():
        for cp in weight_copies(e):
            cp.start()

    @pl.when(jnp.logical_and(new_expert, n_valid > 0))
    def _():
        for cp in weight_copies(e):
            cp.wait()
        wgb_ref[...] = wg_st[...].astype(BF16)
        wlb_ref[...] = wl_st[...].astype(BF16)
        wob_ref[...] = wo_st[...].astype(BF16)
        nxt = next_e_ref[i]

        @pl.when(nxt >= 0)
        def _():
            for cp in weight_copies(nxt):
                cp.start()

    @pl.when(n_valid > 0)
    def _():
        half = wgb_ref.shape[0] // 2
        live = lax.broadcasted_iota(I32, (xs_ref.shape[0], 1), 0) < n_valid
        lo, hi = _unpack_halves(jnp.where(live, xs_ref[...], 0))
        lo = lo.astype(BF16)
        hi = hi.astype(BF16)
        a = _dot(lo, wgb_ref[:half, :]) + _dot(hi, wgb_ref[half:, :]) + bg_ref[0]
        l = _dot(lo, wlb_ref[:half, :]) + _dot(hi, wlb_ref[half:, :]) + bl_ref[0]
        a = jnp.minimum(a, SWIGLU_LIMIT)
        l = jnp.clip(l, -SWIGLU_LIMIT, SWIGLU_LIMIT)
        act = a * jax.nn.sigmoid(SWIGLU_ALPHA * a) * (l + 1.0)
        y_ref[...] = _pack_halves(_dot(act.astype(BF16), wob_ref[...]) + bo_ref[0])

    @pl.when(n_valid == 0)
    def _():
        y_ref[...] = jnp.zeros_like(y_ref)


def _experts(xs, blk_e, n_valid, next_e, layer, w_glu, b_glu, w_lin, b_lin, w_out, b_out, *,
             blk):
    P, pw = xs.shape
    L, E, D, F = w_glu.shape
    n_blocks = P // blk
    bmap = lambda i, be, nv, ne: (layer * E + be[i], 0, 0)
    rows = lambda i, be, nv, ne: (i, 0)
    w_glu, w_lin = w_glu.reshape(L * E, D, F), w_lin.reshape(L * E, D, F)
    w_out = w_out.reshape(L * E, F, D)
    b_glu, b_lin = b_glu.reshape(L * E, 1, F), b_lin.reshape(L * E, 1, F)
    b_out = b_out.reshape(L * E, 1, D)
    hbm = pl.BlockSpec(memory_space=pl.ANY)
    grid_spec = pltpu.PrefetchScalarGridSpec(
        num_scalar_prefetch=3,
        grid=(n_blocks,),
        in_specs=[pl.BlockSpec((blk, pw), rows),
                  hbm, pl.BlockSpec((1, 1, F), bmap),
                  hbm, pl.BlockSpec((1, 1, F), bmap),
                  hbm, pl.BlockSpec((1, 1, D), bmap)],
        out_specs=pl.BlockSpec((blk, pw), rows),
        scratch_shapes=[pltpu.VMEM((D, F), F32), pltpu.VMEM((D, F), F32),
                        pltpu.VMEM((F, D), F32),
                        pltpu.VMEM((D, F), BF16), pltpu.VMEM((D, F), BF16),
                        pltpu.VMEM((F, D), BF16),
                        pltpu.SemaphoreType.DMA((3,))],
    )
    return pl.pallas_call(
        functools.partial(_expert_kernel, expert_base=layer * E),
        grid_spec=grid_spec,
        out_shape=jax.ShapeDtypeStruct((P, pw), xs.dtype),
        compiler_params=_cparams(("arbitrary",)),
        name="moe_experts",
    )(blk_e, n_valid, next_e, xs, w_glu, b_glu, w_lin, b_lin, w_out, b_out)


def _combine_kernel(ys_ref, tg_ref, x_ref, gate_ref, o_ref):
    half = x_ref.shape[1] // 2
    tg = tg_ref[...]
    lo_sum = hi_sum = None
    for k in range(ys_ref.shape[0]):
        lo, hi = _unpack_halves(ys_ref[k])
        gk = tg[:, k:k + 1]
        lo_sum = gk * lo if lo_sum is None else lo_sum + gk * lo
        hi_sum = gk * hi if hi_sum is None else hi_sum + gk * hi
    o_ref[:, :half] = x_ref[:, :half] + gate_ref[0, :, :half] * lo_sum
    o_ref[:, half:] = x_ref[:, half:] + gate_ref[0, :, half:] * hi_sum


def _combine(ys, tok_gate, x, gate, seq, *, tm=256):
    K, T, pw = ys.shape
    D = x.shape[1]
    return pl.pallas_call(
        _combine_kernel,
        grid=(T // tm,),
        in_specs=[pl.BlockSpec((K, tm, pw), lambda i: (0, i, 0)),
                  pl.BlockSpec((tm, K), lambda i: (i, 0)),
                  pl.BlockSpec((tm, D), lambda i: (i, 0)),
                  pl.BlockSpec((1, 1, D), lambda i: ((i * tm) // seq, 0, 0))],
        out_specs=pl.BlockSpec((tm, D), lambda i: (i, 0)),
        out_shape=jax.ShapeDtypeStruct((T, D), F32),
        compiler_params=_cparams(("parallel",)),
        name="moe_combine",
    )(ys, tok_gate, x, gate)


def _final_kernel(x_ref, g_ref, o_ref):
    x = x_ref[...]
    ms = jnp.mean(x * x, axis=-1, keepdims=True)
    o_ref[...] = x * lax.rsqrt(ms + RMS_EPS) * g_ref[...]


def _final_norm(x, g, *, tm=512):
    T, D = x.shape
    return pl.pallas_call(
        _final_kernel,
        grid=(T // tm,),
        in_specs=[pl.BlockSpec((tm, D), lambda i: (i, 0)),
                  pl.BlockSpec((1, D), lambda i: (0, 0))],
        out_specs=pl.BlockSpec((tm, D), lambda i: (i, 0)),
        out_shape=jax.ShapeDtypeStruct((T, D), F32),
        compiler_params=_cparams(("parallel",)),
        name="final_norm",
    )(x, g)


MOE_BLOCK = 256


def _dispatch_plan(idx_t, rank_t, counts, blk):
    K, T = idx_t.shape
    E = counts.shape[0]
    padded = (counts + blk - 1) // blk * blk
    pad_end = jnp.cumsum(padded)
    pad_start = pad_end - padded
    experts = jnp.arange(E, dtype=I32)
    slots = rank_t + jnp.sum(jnp.where(idx_t[..., None] == experts, pad_start, 0), axis=-1)
    n_blocks = -(-(K * T + E * (blk - 1)) // blk)
    blk_start = jnp.arange(n_blocks, dtype=I32) * blk
    blk_e = jnp.minimum(jnp.sum(pad_end[None, :] <= blk_start[:, None], axis=1), E - 1).astype(I32)
    n_valid = jnp.clip(pad_start[blk_e] + counts[blk_e] - blk_start, 0, blk).astype(I32)
    later = jnp.where(counts > 0, experts, E)
    after = lax.cummin(jnp.concatenate([later[1:], jnp.full((1,), E, I32)]), reverse=True)
    next_e = jnp.where(after < E, after, -1).astype(I32)[blk_e]
    return slots, blk_e, n_valid, next_e, n_blocks * blk


def _moe(x, g, scale, shift, gate, r_w, r_b, layer, w_glu, b_glu, w_lin, b_lin, w_out, b_out,
         seq):
    T, D = x.shape
    hp, idx_t, gate_t, rank_t, counts = _route(x, g, scale, shift, r_w, r_b, seq)
    slots, blk_e, n_valid, next_e, n_rows = _dispatch_plan(idx_t, rank_t, counts[:, 0],
                                                           MOE_BLOCK)
    xs = _sc_scatter_rows(hp, slots, n_rows)
    yp = _experts(xs, blk_e, n_valid, next_e, layer, w_glu, b_glu, w_lin, b_lin, w_out, b_out,
                  blk=MOE_BLOCK)
    ys = _sc_gather(yp, slots.reshape(TOP_K * T)).reshape(TOP_K, T, yp.shape[1])
    return _combine(ys, jnp.transpose(gate_t), x, gate, seq)


def kernel(x, c, norm_g, ada_w, final_norm_g, sg_w_in, sg_ln_g, sg_ln_b, sg_w_s, sg_b_s, sg_w_out, pool_w_in, pool_w_grp, pool_ls, pool_w_out, hgrn_w_in, hgrn_lb_logits, hgrn_gnorm_g, hgrn_w_out, router_w, router_b, expert_w_glu, expert_b_glu, expert_w_lin, expert_b_lin, expert_w_out, expert_b_out):
    B, S, D = x.shape
    depth = norm_g.shape[0]
    T = B * S
    xf = x.reshape(T, D)

    mod = _ada_mod(c, ada_w.reshape(depth * 2, D, 3 * D))

    def mod_parts(layer, sub):
        m = mod[layer * 2 + sub][:, None, :]
        return m[..., :D], m[..., D:2 * D], m[..., 2 * D:]

    lb_cum = jnp.cumsum(jax.nn.softmax(hgrn_lb_logits.astype(F32), axis=0), axis=0)
    lower_bounds = lb_cum - lb_cum[0]

    for layer in range(depth):
        kind, slot = layer % 3, layer // 3
        shift, scale, gate = mod_parts(layer, 0)
        g = norm_g[layer, 0][None, :]
        if kind == 0:
            z = _norm_matmul(xf, g, scale, shift, sg_w_in[slot].astype(BF16), S,
                             act="gelu", out_dtype=BF16)
            y = _spatial_gate(z, sg_ln_g[slot][None, :], sg_ln_b[slot][None, :],
                              sg_w_s[slot], sg_b_s[slot])
            w_out = sg_w_out[slot]
        elif kind == 1:
            z = _norm_matmul(xf, g, scale, shift, pool_w_in[slot].astype(BF16), S,
                             act=None, out_dtype=F32)
            y = _pool_mix(z, pool_w_grp[slot].astype(BF16), pool_ls[slot][None, :], S)
            w_out = pool_w_out[slot]
        else:
            proj, f_pre = _norm_matmul(xf, g, scale, shift, hgrn_w_in[slot].astype(BF16), S,
                                       act=None, out_dtype=BF16, f32_col=1, tn=D)
            y = _hgrn_mix(proj, f_pre, lower_bounds[layer][None, :],
                          hgrn_gnorm_g[slot][None, :], B, S)
            w_out = hgrn_w_out[slot]
        xf = _matmul_residual(y, w_out.astype(BF16), xf, gate, S)

        shift, scale, gate = mod_parts(layer, 1)
        xf = _moe(xf, norm_g[layer, 1][None, :], scale, shift, gate,
                  router_w[layer], router_b[layer], layer,
                  expert_w_glu, expert_b_glu, expert_w_lin, expert_b_lin,
                  expert_w_out, expert_b_out, S)

    return _final_norm(xf, final_norm_g[None, :]).reshape(B, S, D)
```

```python
import functools

import numpy as np
import jax
import jax.numpy as jnp
from jax import lax
from jax.experimental import pallas as pl
from jax.experimental.pallas import tpu as pltpu
from jax.experimental.pallas import tpu_sc as plsc

F32 = jnp.float32
BF16 = jnp.bfloat16
I32 = jnp.int32
U32 = jnp.uint32

RMS_EPS = 1e-5
LN_EPS = 1e-5
SWIGLU_ALPHA = 1.702
SWIGLU_LIMIT = 7.0
TOP_K = 4
SG_CHUNK = 128
POOL_WINDOWS = (2, 4, 8, 16)
HEAD_DIM = 128
LANES = 128
SC_WORKERS = 32
SC_ROWS_PER_STEP = 64

VMEM_LIMIT = 48 * 1024 * 1024
BIG_VMEM_LIMIT = 56 * 1024 * 1024


def _cparams(sem):
    return pltpu.CompilerParams(dimension_semantics=sem, vmem_limit_bytes=VMEM_LIMIT)


def _dot(a, b):
    return jnp.dot(a, b, preferred_element_type=F32)


def _dot_nt(a, b):
    return lax.dot_general(a, b, (((1,), (1,)), ((), ())), preferred_element_type=F32)


def _dot_tn(a, b):
    return lax.dot_general(a, b, (((0,), (0,)), ((), ())), preferred_element_type=F32)


def _split_bf16(x):
    hi = x.astype(BF16)
    lo = (x - hi.astype(F32)).astype(BF16)
    return hi, lo


def _packed_layout(d):
    return d // 2, I32


def _pack_halves(y):
    n = y.shape[1] // 2
    lo = lax.bitcast_convert_type(y[:, :n].astype(BF16).astype(F32), U32)
    hi = lax.bitcast_convert_type(y[:, n:].astype(BF16).astype(F32), U32)
    word = lax.shift_right_logical(lo, jnp.uint32(16)) | (hi & jnp.uint32(0xFFFF0000))
    return lax.bitcast_convert_type(word, I32)


def _unpack_halves(w):
    u = lax.bitcast_convert_type(w, U32)
    lo = lax.bitcast_convert_type(lax.shift_left(u, jnp.uint32(16)), F32)
    hi = lax.bitcast_convert_type(u & jnp.uint32(0xFFFF0000), F32)
    return lo, hi


def _modulated_norm(x, g, scale, shift):
    ms = jnp.mean(x * x, axis=-1, keepdims=True)
    y = x * lax.rsqrt(ms + RMS_EPS) * g
    return y * (1.0 + scale) + shift


def _ada_kernel(c_ref, w_ref, o_ref):
    c = c_ref[...]
    s = c * jax.nn.sigmoid(c)
    o_ref[0] = _dot(s.astype(BF16), w_ref[0].astype(BF16))


def _ada_mod(c, ada_w, index):
    L, D, N = ada_w.shape
    B = c.shape[0]
    rows = 8
    tn = 512
    c_pad = jnp.zeros((rows, D), F32).at[:B].set(c)
    out = pl.pallas_call(
        _ada_kernel,
        grid=(1, N // tn),
        in_specs=[pl.BlockSpec((rows, D), lambda l, j: (0, 0)),
                  pl.BlockSpec((1, D, tn), lambda l, j: (index, 0, j))],
        out_specs=pl.BlockSpec((1, rows, tn), lambda l, j: (0, 0, j)),
        out_shape=jax.ShapeDtypeStruct((1, rows, N), F32),
        compiler_params=_cparams(("parallel", "parallel")),
        name="ada_mod",
    )(c_pad, ada_w)
    return out[0, :B]


def _nm_kernel(x_ref, g_ref, sc_ref, sh_ref, w_ref, o_ref, *rest, act, f32_col):
    h_ref = rest[-1]

    @pl.when(pl.program_id(1) == 0)
    def _():
        h = _modulated_norm(x_ref[...], g_ref[...], sc_ref[0], sh_ref[0])
        h_ref[...] = h.astype(BF16)

    acc = _dot(h_ref[...], w_ref[...])
    if act == "gelu":
        acc = 0.5 * acc * (1.0 + lax.erf(acc * np.float32(1.0 / np.sqrt(2.0))))
    o_ref[...] = acc.astype(o_ref.dtype)
    if f32_col is not None:
        @pl.when(pl.program_id(1) == f32_col)
        def _():
            rest[0][...] = acc


def _norm_matmul(x, g, scale, shift, w, seq, *, act, out_dtype, f32_col=None, tm=512, tn=2048):
    T, D = x.shape
    N = w.shape[1]
    bmap = lambda i, j: ((i * tm) // seq, 0, 0)
    out_specs = [pl.BlockSpec((tm, tn), lambda i, j: (i, j))]
    out_shape = [jax.ShapeDtypeStruct((T, N), out_dtype)]
    if f32_col is not None:
        out_specs.append(pl.BlockSpec((tm, tn), lambda i, j: (i, 0)))
        out_shape.append(jax.ShapeDtypeStruct((T, tn), F32))
    outs = pl.pallas_call(
        functools.partial(_nm_kernel, act=act, f32_col=f32_col),
        grid=(T // tm, N // tn),
        in_specs=[pl.BlockSpec((tm, D), lambda i, j: (i, 0)),
                  pl.BlockSpec((1, D), lambda i, j: (0, 0)),
                  pl.BlockSpec((1, 1, D), bmap),
                  pl.BlockSpec((1, 1, D), bmap),
                  pl.BlockSpec((D, tn), lambda i, j: (0, j))],
        out_specs=out_specs,
        out_shape=out_shape,
        scratch_shapes=[pltpu.VMEM((tm, D), BF16)],
        compiler_params=pltpu.CompilerParams(dimension_semantics=("parallel", "arbitrary"),
                                             vmem_limit_bytes=BIG_VMEM_LIMIT),
        name="norm_matmul_" + str(act),
    )(x, g, scale, shift, w)
    return outs if f32_col is not None else outs[0]


def _sg_kernel(u_ref, v_ref, lng_ref, lnb_ref, ws_ref, bst_ref, y_ref, vn_ref, *, heads):
    v = v_ref[...].astype(F32)
    mu = jnp.mean(v, axis=-1, keepdims=True)
    vc = v - mu
    var = jnp.mean(vc * vc, axis=-1, keepdims=True)
    vn_ref[...] = (vc * lax.rsqrt(var + LN_EPS) * lng_ref[...] + lnb_ref[...]).astype(BF16)

    n_chunks = v_ref.shape[0] // SG_CHUNK
    row = lax.broadcasted_iota(I32, (SG_CHUNK, SG_CHUNK), 0)
    col = lax.broadcasted_iota(I32, (SG_CHUNK, SG_CHUNK), 1)
    causal = row >= col
    for hd in range(heads):
        cs = slice(hd * HEAD_DIM, (hd + 1) * HEAD_DIM)
        wm = jnp.where(causal, ws_ref[hd], 0.0).astype(BF16)
        bias = bst_ref[:, hd:hd + 1]
        for ci in range(n_chunks):
            rs = slice(ci * SG_CHUNK, (ci + 1) * SG_CHUNK)
            mixed = _dot(wm, vn_ref[rs, cs]) + bias
            y_ref[rs, cs] = (u_ref[rs, cs].astype(F32) * mixed).astype(BF16)


def _spatial_gate(z, ln_g, ln_b, w_s, b_s, *, tm=256):
    T, two_w = z.shape
    W = two_w // 2
    heads = w_s.shape[0]
    return pl.pallas_call(
        functools.partial(_sg_kernel, heads=heads),
        grid=(T // tm,),
        in_specs=[pl.BlockSpec((tm, W), lambda i: (i, 0)),
                  pl.BlockSpec((tm, W), lambda i: (i, 1)),
                  pl.BlockSpec((1, W), lambda i: (0, 0)),
                  pl.BlockSpec((1, W), lambda i: (0, 0)),
                  pl.BlockSpec((heads, SG_CHUNK, SG_CHUNK), lambda i: (0, 0, 0)),
                  pl.BlockSpec((SG_CHUNK, heads), lambda i: (0, 0))],
        out_specs=pl.BlockSpec((tm, W), lambda i: (i, 0)),
        out_shape=jax.ShapeDtypeStruct((T, W), BF16),
        scratch_shapes=[pltpu.VMEM((tm, W), BF16)],
        compiler_params=_cparams(("parallel",)),
        name="spatial_gate",
    )(z, z, ln_g, ln_b, w_s, jnp.transpose(b_s))


POOL_HALO = 16


def _pool_kernel(z_ref, halo_ref, wg_ref, ls_ref, y_ref, *, seq):
    tm = z_ref.shape[0]
    gdim = wg_ref.shape[1]
    pos0 = (pl.program_id(0) * tm) % seq
    halo_on = (pos0 > 0).astype(F32)
    pos = pos0 + lax.broadcasted_iota(I32, (tm, 1), 0)
    for gi, wnd in enumerate(POOL_WINDOWS):
        cs = slice(gi * gdim, (gi + 1) * gdim)
        zg = z_ref[:, cs]
        s = jnp.concatenate([halo_ref[:, cs] * halo_on, zg], axis=0)
        k = 1
        while k < wnd:
            s = s + pltpu.roll(s, k, 0)
            k *= 2
        cnt = jnp.minimum(pos + 1, wnd).astype(F32)
        pooled = s[POOL_HALO:, :] / cnt - zg
        yg = _dot(pooled.astype(BF16), wg_ref[gi]) * ls_ref[:, cs]
        y_ref[:, cs] = yg.astype(BF16)


def _pool_mix(z, w_grp, ls, seq, *, tm=256):
    T, D = z.shape
    G, gdim, _ = w_grp.shape
    assert max(POOL_WINDOWS) <= POOL_HALO and tm % POOL_HALO == 0
    ratio = tm // POOL_HALO
    return pl.pallas_call(
        functools.partial(_pool_kernel, seq=seq),
        grid=(T // tm,),
        in_specs=[pl.BlockSpec((tm, D), lambda i: (i, 0)),
                  pl.BlockSpec((POOL_HALO, D), lambda i: (jnp.maximum(i * ratio - 1, 0), 0)),
                  pl.BlockSpec((G, gdim, gdim), lambda i: (0, 0, 0)),
                  pl.BlockSpec((1, D), lambda i: (0, 0))],
        out_specs=pl.BlockSpec((tm, D), lambda i: (i, 0)),
        out_shape=jax.ShapeDtypeStruct((T, D), BF16),
        compiler_params=_cparams(("parallel",)),
        name="pool_mix",
    )(z, z, w_grp, ls)


HGRN_CHUNK = 128
LOG2_E = 1.4426950408889634


def _hgrn_tables(c=HGRN_CHUNK):
    t = np.arange(c)
    sums = [(t[None, :] <= t[:, None])]
    masks = []
    h = c // 2
    while h >= 1:
        off = t % (2 * h)
        mid = (t // (2 * h)) * (2 * h) + h
        second = off >= h
        j = t[None, :]
        if h >= 2:
            m_q = second[:, None] & (j >= mid[:, None]) & (j <= t[:, None])
            m_k = (~second)[:, None] & (j > t[:, None]) & (j < mid[:, None])
            sums.append(m_q | m_k)
        same = (t[:, None] // (2 * h)) == (t[None, :] // (2 * h))
        masks.append(same & second[:, None] & (~second)[None, :])
        h //= 2
    sums = np.stack(sums).astype(np.float32)
    return np.concatenate([sums, sums], axis=2), np.stack(masks).astype(np.float32)


def _hgrn_kernel(q_ref, f_ref, i_ref, g_ref, lb_ref, gn_ref, sums_ref, masks_ref, o_ref,
                 st_ref, qb_ref, ke_ref, z_ref, qd_ref, kd_ref, vb_ref, eb_ref, *, heads):
    c = q_ref.shape[0]
    n_sum = sums_ref.shape[0]
    n_lvl = masks_ref.shape[0]

    @pl.when(pl.program_id(1) == 0)
    def _():
        st_ref[...] = jnp.zeros_like(st_ref)

    lb = lb_ref[...]
    sig = jax.nn.sigmoid(f_ref[...])
    fg = lb + (1.0 - lb) * sig
    lf2 = jnp.log(fg) * np.float32(LOG2_E)
    kk = (1.0 - lb) * (1.0 - sig)
    lf_hi, lf_lo = _split_bf16(lf2)
    lf_cat = jnp.concatenate([lf_hi, lf_lo], axis=0)
    qr = q_ref[...].astype(F32)
    q = qr * jax.nn.sigmoid(qr)
    vb_ref[...] = i_ref[...].astype(BF16)
    q16 = q.astype(BF16)
    k16 = kk.astype(BF16)
    qd_ref[...] = q16
    kd_ref[...] = k16

    b2 = _dot(sums_ref[0], lf_cat)
    eb = jnp.exp2(b2)
    qb_ref[...] = (q * eb).astype(BF16)
    eb_ref[...] = eb[c - 1:c, :]
    ke_ref[...] = (kk * jnp.exp2(b2[c - 1:c, :] - b2)).astype(BF16)

    row = lax.broadcasted_iota(I32, (c, 1), 0)
    for l in range(n_lvl):
        half = c >> (l + 1)
        second = (row & (2 * half - 1)) >= half
        if l + 1 < n_sum:
            x = jnp.exp2(_dot(sums_ref[l + 1], lf_cat).astype(BF16))
        else:
            x = jnp.where(second, fg, 1.0).astype(BF16)
        z_ref[l] = jnp.where(second, q16, k16) * x

    gn = gn_ref[...]

    def head(hd, carry):
        cs = pl.ds(pl.multiple_of(hd * HEAD_DIM, HEAD_DIM), HEAD_DIM)
        z = z_ref[0, :, cs]
        scores = masks_ref[0] * _dot_nt(z, z).astype(BF16)
        for l in range(1, n_lvl):
            z = z_ref[l, :, cs]
            scores = scores + masks_ref[l] * _dot_nt(z, z).astype(BF16)
        st = st_ref[hd]
        vh = vb_ref[:, cs]
        diag = jnp.sum(qd_ref[:, cs].astype(F32) * kd_ref[:, cs].astype(F32), axis=-1,
                       keepdims=True)
        o = (_dot_nt(qb_ref[:, cs], st.astype(BF16)) + _dot(scores, vh)
             + diag * vh.astype(F32))
        st_ref[hd] = st * eb_ref[:, cs] + _dot_tn(vh, ke_ref[:, cs])
        on = o * lax.rsqrt(jnp.mean(o * o, axis=-1, keepdims=True) + RMS_EPS) * gn
        gv = g_ref[:, cs].astype(F32)
        o_ref[:, cs] = (on * (gv * jax.nn.sigmoid(gv))).astype(BF16)
        return carry

    lax.fori_loop(0, heads, head, 0, unroll=4)


def _hgrn_mix(proj, f_pre, lb, gnorm_g, batch, seq):
    T, four_d = proj.shape
    D = four_d // 4
    heads = D // HEAD_DIM
    c = HGRN_CHUNK
    n_chunks = seq // c
    sums, masks = _hgrn_tables(c)
    n_sum, n_mask = sums.shape[0], masks.shape[0]
    row = lambda b, i: b * n_chunks + i
    return pl.pallas_call(
        functools.partial(_hgrn_kernel, heads=heads),
        grid=(batch, n_chunks),
        in_specs=[pl.BlockSpec((c, D), lambda b, i: (row(b, i), 0)),
                  pl.BlockSpec((c, D), lambda b, i: (row(b, i), 0)),
                  pl.BlockSpec((c, D), lambda b, i: (row(b, i), 2)),
                  pl.BlockSpec((c, D), lambda b, i: (row(b, i), 3)),
                  pl.BlockSpec((1, D), lambda b, i: (0, 0)),
                  pl.BlockSpec((1, HEAD_DIM), lambda b, i: (0, 0)),
                  pl.BlockSpec((n_sum, c, 2 * c), lambda b, i: (0, 0, 0)),
                  pl.BlockSpec((n_mask, c, c), lambda b, i: (0, 0, 0))],
        out_specs=pl.BlockSpec((c, D), lambda b, i: (row(b, i), 0)),
        out_shape=jax.ShapeDtypeStruct((T, D), BF16),
        scratch_shapes=[pltpu.VMEM((heads, HEAD_DIM, HEAD_DIM), F32),
                        pltpu.VMEM((c, D), BF16),
                        pltpu.VMEM((c, D), BF16),
                        pltpu.VMEM((n_mask, c, D), BF16),
                        pltpu.VMEM((c, D), BF16),
                        pltpu.VMEM((c, D), BF16),
                        pltpu.VMEM((c, D), BF16),
                        pltpu.VMEM((1, D), F32)],
        compiler_params=_cparams(("parallel", "arbitrary")),
        name="hgrn_mix",
    )(proj, f_pre, proj, proj, lb, gnorm_g, jnp.asarray(sums, BF16), jnp.asarray(masks, BF16))


def _route_kernel(y_ref, w_ref, xin_ref, mgate_ref, g_ref, sc_ref, sh_ref, rw_ref,
                  rb_ref, tri_ref, x_ref, hp_ref, idx_ref, gate_ref, rank_ref, cnt_ref, carry_ref,
                  *, n_experts):
    @pl.when(pl.program_id(0) == 0)
    def _():
        carry_ref[...] = jnp.zeros_like(carry_ref)

    x = xin_ref[...] + mgate_ref[0] * _dot(y_ref[...], w_ref[...])
    x_ref[...] = x
    h = _modulated_norm(x, g_ref[...], sc_ref[0], sh_ref[0])
    hp_ref[...] = _pack_halves(h)

    h_hi, h_lo = _split_bf16(h)
    both = _dot(h_hi, rw_ref[...])
    logits = (both[:, :LANES] + both[:, LANES:] + _dot(h_lo, rw_ref[:, :LANES])) + rb_ref[...]
    lt = jnp.transpose(logits)[:n_experts, :]
    eidx = lax.broadcasted_iota(I32, lt.shape, 0)
    vals, idxs, hits = [], [], []
    for _ in range(TOP_K):
        m = jnp.max(lt, axis=0, keepdims=True)
        sel = jnp.min(jnp.where(lt == m, eidx, n_experts), axis=0, keepdims=True)
        hit = eidx == sel
        vals.append(m)
        idxs.append(sel)
        hits.append(hit)
        lt = jnp.where(hit, -jnp.inf, lt)
    exps = [jnp.exp(v - vals[0]) for v in vals]
    denom = exps[0]
    for e in exps[1:]:
        denom = denom + e
    idx_ref[...] = jnp.concatenate(idxs, axis=0)
    gate_ref[...] = jnp.concatenate([e / denom for e in exps], axis=0)

    onehot = hits[0].astype(F32)
    for hit in hits[1:]:
        onehot = onehot + hit.astype(F32)
    before = carry_ref[...] + _dot(onehot.astype(BF16), tri_ref[...])
    rank_ref[...] = jnp.concatenate(
        [jnp.sum(jnp.where(hit, before, 0.0), axis=0, keepdims=True) for hit in hits],
        axis=0).astype(I32)
    carry_ref[...] += jnp.sum(onehot, axis=1, keepdims=True)
    cnt_ref[...] = carry_ref[...].astype(I32)


def _mixer_out_route(y, w, x, mix_gate, g, scale, shift, r_w, r_b, seq, *, tm=256):
    T, K = y.shape
    D = w.shape[1]
    E = r_w.shape[1]
    rw_pad = jnp.zeros((D, LANES), F32).at[:, :E].set(r_w)
    rw_cat = jnp.concatenate(_split_bf16(rw_pad), axis=1)
    rb_pad = jnp.full((1, LANES), -1e30, F32).at[0, :E].set(r_b)
    tri = jnp.asarray(np.triu(np.ones((tm, tm), np.float32), k=1), BF16)
    pw, pdt = _packed_layout(D)
    bmap = lambda i: ((i * tm) // seq, 0, 0)
    const = lambda i: (0, 0)
    rows = lambda i: (i, 0)
    kt = pl.BlockSpec((TOP_K, tm), lambda i: (0, i))
    return pl.pallas_call(
        functools.partial(_route_kernel, n_experts=E),
        grid=(T // tm,),
        in_specs=[pl.BlockSpec((tm, K), rows),
                  pl.BlockSpec((K, D), const),
                  pl.BlockSpec((tm, D), rows),
                  pl.BlockSpec((1, 1, D), bmap),
                  pl.BlockSpec((1, D), const),
                  pl.BlockSpec((1, 1, D), bmap),
                  pl.BlockSpec((1, 1, D), bmap),
                  pl.BlockSpec((D, 2 * LANES), const),
                  pl.BlockSpec((1, LANES), const),
                  pl.BlockSpec((tm, tm), const)],
        out_specs=[pl.BlockSpec((tm, D), rows), pl.BlockSpec((tm, pw), rows), kt, kt, kt,
                   pl.BlockSpec((E, 1), const)],
        out_shape=[jax.ShapeDtypeStruct((T, D), F32),
                   jax.ShapeDtypeStruct((T, pw), pdt),
                   jax.ShapeDtypeStruct((TOP_K, T), I32),
                   jax.ShapeDtypeStruct((TOP_K, T), F32),
                   jax.ShapeDtypeStruct((TOP_K, T), I32),
                   jax.ShapeDtypeStruct((E, 1), I32)],
        scratch_shapes=[pltpu.VMEM((E, 1), F32)],
        compiler_params=_cparams(("arbitrary",)),
        name="mixer_out_route",
    )(y, w, x, mix_gate, g, scale, shift, rw_cat, rb_pad, tri)


def _sc_gather(table, idx):
    R = idx.shape[0]
    W = table.shape[1]
    rows = SC_ROWS_PER_STEP // 2
    per_worker = R // SC_WORKERS
    steps = per_worker // rows
    assert steps * rows * SC_WORKERS == R and steps % 2 == 0
    mesh = plsc.VectorSubcoreMesh(core_axis_name="c", subcore_axis_name="s")
    n_cores = mesh.num_cores

    @functools.partial(
        pl.kernel, mesh=mesh,
        out_type=jax.ShapeDtypeStruct((R, W), table.dtype),
        scratch_types=[pltpu.VMEM((steps, rows), I32),
                       pltpu.VMEM((rows, W), table.dtype),
                       pltpu.VMEM((rows, W), table.dtype),
                       pltpu.SemaphoreType.DMA((2,)),
                       pltpu.SemaphoreType.DMA((2,))],
    )
    def gather(table_hbm, idx_hbm, out_hbm, idx_v, buf0, buf1, gsem, wsem):
        wid = lax.axis_index("s") * n_cores + lax.axis_index("c")
        base = wid * per_worker
        bufs = (buf0, buf1)
        pltpu.sync_copy(idx_hbm.at[pl.ds(wid * steps, steps)], idx_v)

        def fetch(j, b):
            return pltpu.make_async_copy(table_hbm.at[idx_v.at[j]], bufs[b], gsem.at[b])

        def flush(j, b):
            off = pl.multiple_of(base + j * rows, 8)
            return pltpu.make_async_copy(bufs[b], out_hbm.at[pl.ds(off, rows)], wsem.at[b])

        fetch(0, 0).start()

        @pl.loop(0, steps, step=2)
        def _(j0):
            for b in range(2):
                j = j0 + b
                fetch(j, b).wait()
                flush(j, b).start()

                @pl.when(j + 1 < steps)
                def _():
                    @pl.when(j >= 1)
                    def _():
                        flush(j - 1, 1 - b).wait()
                    fetch(j + 1, 1 - b).start()

        flush(steps - 2, 0).wait()
        flush(steps - 1, 1).wait()

    return gather(table, idx.reshape(R // rows, rows))


def _sc_scatter_rows(table, slots, n_out):
    K, T = slots.shape
    W = table.shape[1]
    rows = SC_ROWS_PER_STEP
    per_worker = T // SC_WORKERS
    steps = per_worker // rows
    assert steps * rows * SC_WORKERS == T
    mesh = plsc.VectorSubcoreMesh(core_axis_name="c", subcore_axis_name="s")
    n_cores = mesh.num_cores

    @functools.partial(
        pl.kernel, mesh=mesh,
        out_type=jax.ShapeDtypeStruct((n_out, W), table.dtype),
        scratch_types=[pltpu.VMEM((K, rows), I32),
                       pltpu.VMEM((rows, W), table.dtype)],
    )
    def scatter(table_hbm, slots_hbm, out_hbm, idx_v, rows_v):
        wid = lax.axis_index("s") * n_cores + lax.axis_index("c")
        base = wid * per_worker

        @pl.loop(0, steps)
        def _(j):
            off = pl.multiple_of(base + j * rows, 8)
            pltpu.sync_copy(table_hbm.at[pl.ds(off, rows)], rows_v)
            for k in range(K):
                pltpu.sync_copy(slots_hbm.at[pl.ds(pl.multiple_of(k * T + off, 8), rows)],
                                idx_v.at[k])
                pltpu.sync_copy(rows_v, out_hbm.at[idx_v.at[k]])

    return scatter(table, slots.reshape(K * T))


def _expert_kernel(blk_e_ref, n_valid_ref, next_e_ref, xs_ref, wg_hbm, bg_ref, wl_hbm, bl_ref,
                   wo_hbm, bo_ref, y_ref, wg_st, wl_st, wo_st, wgb_ref, wlb_ref, wob_ref, sem,
                   *, expert_base):
    i = pl.program_id(0)
    e = blk_e_ref[i]
    n_valid = n_valid_ref[i]
    new_expert = jnp.logical_or(i == 0, e != blk_e_ref[jnp.maximum(i - 1, 0)])

    def weight_copies(expert):
        idx = expert_base + expert
        return (pltpu.make_async_copy(wg_hbm.at[idx], wg_st, sem.at[0]),
                pltpu.make_async_copy(wl_hbm.at[idx], wl_st, sem.at[1]),
                pltpu.make_async_copy(wo_hbm.at[idx], wo_st, sem.at[2]))

    @pl.when(i == 0)
    def _():
        for cp in weight_copies(e):
            cp.start()

    @pl.when(jnp.logical_and(new_expert, n_valid > 0))
    def _():
        for cp in weight_copies(e):
            cp.wait()
        wgb_ref[...] = wg_st[...].astype(BF16)
        wlb_ref[...] = wl_st[...].astype(BF16)
        wob_ref[...] = wo_st[...].astype(BF16)
        nxt = next_e_ref[i]

        @pl.when(nxt >= 0)
        def _():
            for cp in weight_copies(nxt):
                cp.start()

    @pl.when(n_valid > 0)
    def _():
        half = wgb_ref.shape[0] // 2
        live = lax.broadcasted_iota(I32, (xs_ref.shape[0], 1), 0) < n_valid
        lo, hi = _unpack_halves(jnp.where(live, xs_ref[...], 0))
        lo = lo.astype(BF16)
        hi = hi.astype(BF16)
        a = _dot(lo, wgb_ref[:half, :]) + _dot(hi, wgb_ref[half:, :]) + bg_ref[0]
        l = _dot(lo, wlb_ref[:half, :]) + _dot(hi, wlb_ref[half:, :]) + bl_ref[0]
        a = jnp.minimum(a, SWIGLU_LIMIT)
        l = jnp.clip(l, -SWIGLU_LIMIT, SWIGLU_LIMIT)
        act = a * jax.nn.sigmoid(SWIGLU_ALPHA * a) * (l + 1.0)
        y_ref[...] = _pack_halves(_dot(act.astype(BF16), wob_ref[...]) + bo_ref[0])

    @pl.when(n_valid == 0)
    def _():
        y_ref[...] = jnp.zeros_like(y_ref)


def _experts(xs, blk_e, n_valid, next_e, layer, w_glu, b_glu, w_lin, b_lin, w_out, b_out, *,
             blk):
    P, pw = xs.shape
    L, E, D, F = w_glu.shape
    n_blocks = P // blk
    bmap = lambda i, be, nv, ne: (layer * E + be[i], 0, 0)
    rows = lambda i, be, nv, ne: (i, 0)
    w_glu, w_lin = w_glu.reshape(L * E, D, F), w_lin.reshape(L * E, D, F)
    w_out = w_out.reshape(L * E, F, D)
    b_glu, b_lin = b_glu.reshape(L * E, 1, F), b_lin.reshape(L * E, 1, F)
    b_out = b_out.reshape(L * E, 1, D)
    hbm = pl.BlockSpec(memory_space=pl.ANY)
    grid_spec = pltpu.PrefetchScalarGridSpec(
        num_scalar_prefetch=3,
        grid=(n_blocks,),
        in_specs=[pl.BlockSpec((blk, pw), rows),
                  hbm, pl.BlockSpec((1, 1, F), bmap),
                  hbm, pl.BlockSpec((1, 1, F), bmap),
                  hbm, pl.BlockSpec((1, 1, D), bmap)],
        out_specs=pl.BlockSpec((blk, pw), rows),
        scratch_shapes=[pltpu.VMEM((D, F), F32), pltpu.VMEM((D, F), F32),
                        pltpu.VMEM((F, D), F32),
                        pltpu.VMEM((D, F), BF16), pltpu.VMEM((D, F), BF16),
                        pltpu.VMEM((F, D), BF16),
                        pltpu.SemaphoreType.DMA((3,))],
    )
    return pl.pallas_call(
        functools.partial(_expert_kernel, expert_base=layer * E),
        grid_spec=grid_spec,
        out_shape=jax.ShapeDtypeStruct((P, pw), xs.dtype),
        compiler_params=_cparams(("arbitrary",)),
        name="moe_experts",
    )(blk_e, n_valid, next_e, xs, w_glu, b_glu, w_lin, b_lin, w_out, b_out)


def _combine_kernel(ys_ref, tg_ref, x_ref, gate_ref, *rest, final):
    o_ref = rest[-1]
    half = x_ref.shape[1] // 2
    tg = tg_ref[...]
    lo_sum = hi_sum = None
    for k in range(ys_ref.shape[0]):
        lo, hi = _unpack_halves(ys_ref[k])
        gk = tg[:, k:k + 1]
        lo_sum = gk * lo if lo_sum is None else lo_sum + gk * lo
        hi_sum = gk * hi if hi_sum is None else hi_sum + gk * hi
    x_lo = x_ref[:, :half] + gate_ref[0, :, :half] * lo_sum
    x_hi = x_ref[:, half:] + gate_ref[0, :, half:] * hi_sum
    if final:
        g_ref = rest[0]
        ms = (jnp.sum(x_lo * x_lo, axis=-1, keepdims=True)
              + jnp.sum(x_hi * x_hi, axis=-1, keepdims=True)) / x_ref.shape[1]
        inv = lax.rsqrt(ms + RMS_EPS)
        x_lo = x_lo * inv * g_ref[:, :half]
        x_hi = x_hi * inv * g_ref[:, half:]
    o_ref[:, :half] = x_lo
    o_ref[:, half:] = x_hi


def _combine(ys, tok_gate, x, gate, seq, final_g=None, *, tm=256):
    K, T, pw = ys.shape
    D = x.shape[1]
    in_specs = [pl.BlockSpec((K, tm, pw), lambda i: (0, i, 0)),
                pl.BlockSpec((tm, K), lambda i: (i, 0)),
                pl.BlockSpec((tm, D), lambda i: (i, 0)),
                pl.BlockSpec((1, 1, D), lambda i: ((i * tm) // seq, 0, 0))]
    args = [ys, tok_gate, x, gate]
    if final_g is not None:
        in_specs.append(pl.BlockSpec((1, D), lambda i: (0, 0)))
        args.append(final_g)
    return pl.pallas_call(
        functools.partial(_combine_kernel, final=final_g is not None),
        grid=(T // tm,),
        in_specs=in_specs,
        out_specs=pl.BlockSpec((tm, D), lambda i: (i, 0)),
        out_shape=jax.ShapeDtypeStruct((T, D), F32),
        compiler_params=_cparams(("parallel",)),
        name="moe_combine",
    )(*args)


MOE_BLOCK = 256


def _dispatch_plan(idx_t, rank_t, counts, blk):
    K, T = idx_t.shape
    E = counts.shape[0]
    padded = (counts + blk - 1) // blk * blk
    pad_end = jnp.cumsum(padded)
    pad_start = pad_end - padded
    experts = jnp.arange(E, dtype=I32)
    slots = rank_t + jnp.sum(jnp.where(idx_t[..., None] == experts, pad_start, 0), axis=-1)
    n_blocks = -(-(K * T + E * (blk - 1)) // blk)
    blk_start = jnp.arange(n_blocks, dtype=I32) * blk
    blk_e = jnp.minimum(jnp.sum(pad_end[None, :] <= blk_start[:, None], axis=1), E - 1).astype(I32)
    n_valid = jnp.clip(pad_start[blk_e] + counts[blk_e] - blk_start, 0, blk).astype(I32)
    later = jnp.where(counts > 0, experts, E)
    after = lax.cummin(jnp.concatenate([later[1:], jnp.full((1,), E, I32)]), reverse=True)
    next_e = jnp.where(after < E, after, -1).astype(I32)[blk_e]
    return slots, blk_e, n_valid, next_e, n_blocks * blk


def _mixer_out_moe(y, w_mix, x, mix_gate, g, scale, shift, gate, r_w, r_b, layer,
                   w_glu, b_glu, w_lin, b_lin, w_out, b_out, seq, final_g=None):
    T, D = x.shape
    x, hp, idx_t, gate_t, rank_t, counts = _mixer_out_route(
        y, w_mix, x, mix_gate, g, scale, shift, r_w, r_b, seq)
    slots, blk_e, n_valid, next_e, n_rows = _dispatch_plan(idx_t, rank_t, counts[:, 0],
                                                           MOE_BLOCK)
    xs = _sc_scatter_rows(hp, slots, n_rows)
    yp = _experts(xs, blk_e, n_valid, next_e, layer, w_glu, b_glu, w_lin, b_lin, w_out, b_out,
                  blk=MOE_BLOCK)
    ys = _sc_gather(yp, slots.reshape(TOP_K * T)).reshape(TOP_K, T, yp.shape[1])
    return _combine(ys, jnp.transpose(gate_t), x, gate, seq, final_g)


def kernel(x, c, norm_g, ada_w, final_norm_g, sg_w_in, sg_ln_g, sg_ln_b, sg_w_s, sg_b_s, sg_w_out, pool_w_in, pool_w_grp, pool_ls, pool_w_out, hgrn_w_in, hgrn_lb_logits, hgrn_gnorm_g, hgrn_w_out, router_w, router_b, expert_w_glu, expert_b_glu, expert_w_lin, expert_b_lin, expert_w_out, expert_b_out):
    B, S, D = x.shape
    depth = norm_g.shape[0]
    T = B * S
    xf = x.reshape(T, D)

    ada_w = ada_w.reshape(depth * 2, D, 3 * D)

    def mod_parts(layer, sub):
        m = _ada_mod(c, ada_w, layer * 2 + sub)[:, None, :]
        return m[..., :D], m[..., D:2 * D], m[..., 2 * D:]

    lb_cum = jnp.cumsum(jax.nn.softmax(hgrn_lb_logits.astype(F32), axis=0), axis=0)
    lower_bounds = lb_cum - lb_cum[0]

    for layer in range(depth):
        kind, slot = layer % 3, layer // 3
        shift, scale, mix_gate = mod_parts(layer, 0)
        g = norm_g[layer, 0][None, :]
        if kind == 0:
            z = _norm_matmul(xf, g, scale, shift, sg_w_in[slot].astype(BF16), S,
                             act="gelu", out_dtype=BF16)
            y = _spatial_gate(z, sg_ln_g[slot][None, :], sg_ln_b[slot][None, :],
                              sg_w_s[slot], sg_b_s[slot])
            w_out = sg_w_out[slot]
        elif kind == 1:
            z = _norm_matmul(xf, g, scale, shift, pool_w_in[slot].astype(BF16), S,
                             act=None, out_dtype=F32)
            y = _pool_mix(z, pool_w_grp[slot].astype(BF16), pool_ls[slot][None, :], S)
            w_out = pool_w_out[slot]
        else:
            proj, f_pre = _norm_matmul(xf, g, scale, shift, hgrn_w_in[slot].astype(BF16), S,
                                       act=None, out_dtype=BF16, f32_col=1, tn=D)
            y = _hgrn_mix(proj, f_pre, lower_bounds[layer][None, :],
                          hgrn_gnorm_g[slot][None, :], B, S)
            w_out = hgrn_w_out[slot]

        shift, scale, gate = mod_parts(layer, 1)
        final_g = final_norm_g[None, :] if layer == depth - 1 else None
        xf = _mixer_out_moe(y, w_out.astype(BF16), xf, mix_gate, norm_g[layer, 1][None, :],
                            scale, shift, gate, router_w[layer], router_b[layer], layer,
                            expert_w_glu, expert_b_glu, expert_w_lin, expert_b_lin,
                            expert_w_out, expert_b_out, S, final_g)

    return xf.reshape(B, S, D)
```

```python
import functools

import numpy as np
import jax
import jax.numpy as jnp
from jax import lax
from jax.experimental import pallas as pl
from jax.experimental.pallas import tpu as pltpu
from jax.experimental.pallas import tpu_sc as plsc

F32 = jnp.float32
BF16 = jnp.bfloat16
I32 = jnp.int32
U32 = jnp.uint32

RMS_EPS = 1e-5
LN_EPS = 1e-5
SWIGLU_ALPHA = 1.702
SWIGLU_LIMIT = 7.0
TOP_K = 4
SG_CHUNK = 128
POOL_WINDOWS = (2, 4, 8, 16)
HEAD_DIM = 128
LANES = 128
SC_WORKERS = 32
SC_ROWS_PER_STEP = 64

VMEM_LIMIT = 48 * 1024 * 1024
BIG_VMEM_LIMIT = 56 * 1024 * 1024


def _cparams(sem):
    return pltpu.CompilerParams(dimension_semantics=sem, vmem_limit_bytes=VMEM_LIMIT)


def _dot(a, b):
    return jnp.dot(a, b, preferred_element_type=F32)


def _dot_nt(a, b):
    return lax.dot_general(a, b, (((1,), (1,)), ((), ())), preferred_element_type=F32)


def _dot_tn(a, b):
    return lax.dot_general(a, b, (((0,), (0,)), ((), ())), preferred_element_type=F32)


def _split_bf16(x):
    hi = x.astype(BF16)
    lo = (x - hi.astype(F32)).astype(BF16)
    return hi, lo


def _packed_layout(d):
    return d // 2, I32


def _pack_halves(y):
    n = y.shape[1] // 2
    lo = lax.bitcast_convert_type(y[:, :n].astype(BF16).astype(F32), U32)
    hi = lax.bitcast_convert_type(y[:, n:].astype(BF16).astype(F32), U32)
    word = lax.shift_right_logical(lo, jnp.uint32(16)) | (hi & jnp.uint32(0xFFFF0000))
    return lax.bitcast_convert_type(word, I32)


def _unpack_halves(w):
    u = lax.bitcast_convert_type(w, U32)
    lo = lax.bitcast_convert_type(lax.shift_left(u, jnp.uint32(16)), F32)
    hi = lax.bitcast_convert_type(u & jnp.uint32(0xFFFF0000), F32)
    return lo, hi


def _modulated_norm(x, g, scale, shift):
    ms = jnp.mean(x * x, axis=-1, keepdims=True)
    y = x * lax.rsqrt(ms + RMS_EPS) * g
    return y * (1.0 + scale) + shift


def _ada_kernel(c_ref, w_ref, o_ref):
    w = w_ref[0]
    reps = w.shape[1] // LANES
    for b in range(c_ref.shape[0]):
        cb = c_ref[b]
        s = jnp.tile(cb * jax.nn.sigmoid(cb), (1, reps))
        o_ref[b] = jnp.sum(w * s, axis=0, keepdims=True)


def _ada_mod(c, ada_w, index):
    L, D, N = ada_w.shape
    B = c.shape[0]
    tn = 512
    c_cols = jnp.broadcast_to(c[:, :, None], (B, D, LANES))
    out = pl.pallas_call(
        _ada_kernel,
        grid=(N // tn,),
        in_specs=[pl.BlockSpec((B, D, LANES), lambda j: (0, 0, 0)),
                  pl.BlockSpec((1, D, tn), lambda j: (index, 0, j))],
        out_specs=pl.BlockSpec((B, 1, tn), lambda j: (0, 0, j)),
        out_shape=jax.ShapeDtypeStruct((B, 1, N), F32),
        compiler_params=_cparams(("parallel",)),
        name="ada_mod",
    )(c_cols, ada_w)
    return out[:, 0]


def _nm_kernel(x_ref, g_ref, sc_ref, sh_ref, w_ref, o_ref, *rest, act, f32_col):
    h_ref = rest[-1]

    @pl.when(pl.program_id(1) == 0)
    def _():
        h = _modulated_norm(x_ref[...], g_ref[...], sc_ref[0], sh_ref[0])
        h_ref[...] = h.astype(BF16)

    acc = _dot(h_ref[...], w_ref[...])
    if act == "gelu":
        acc = 0.5 * acc * (1.0 + lax.erf(acc * np.float32(1.0 / np.sqrt(2.0))))
    o_ref[...] = acc.astype(o_ref.dtype)
    if f32_col is not None:
        @pl.when(pl.program_id(1) == f32_col)
        def _():
            rest[0][...] = acc


def _norm_matmul(x, g, scale, shift, w, seq, *, act, out_dtype, f32_col=None, tm=512, tn=2048):
    T, D = x.shape
    N = w.shape[1]
    bmap = lambda i, j: ((i * tm) // seq, 0, 0)
    out_specs = [pl.BlockSpec((tm, tn), lambda i, j: (i, j))]
    out_shape = [jax.ShapeDtypeStruct((T, N), out_dtype)]
    if f32_col is not None:
        out_specs.append(pl.BlockSpec((tm, tn), lambda i, j: (i, 0)))
        out_shape.append(jax.ShapeDtypeStruct((T, tn), F32))
    outs = pl.pallas_call(
        functools.partial(_nm_kernel, act=act, f32_col=f32_col),
        grid=(T // tm, N // tn),
        in_specs=[pl.BlockSpec((tm, D), lambda i, j: (i, 0)),
                  pl.BlockSpec((1, D), lambda i, j: (0, 0)),
                  pl.BlockSpec((1, 1, D), bmap),
                  pl.BlockSpec((1, 1, D), bmap),
                  pl.BlockSpec((D, tn), lambda i, j: (0, j))],
        out_specs=out_specs,
        out_shape=out_shape,
        scratch_shapes=[pltpu.VMEM((tm, D), BF16)],
        compiler_params=pltpu.CompilerParams(dimension_semantics=("parallel", "arbitrary"),
                                             vmem_limit_bytes=BIG_VMEM_LIMIT),
        name="norm_matmul_" + str(act),
    )(x, g, scale, shift, w)
    return outs if f32_col is not None else outs[0]


def _sg_kernel(u_ref, v_ref, lng_ref, lnb_ref, ws_ref, bst_ref, y_ref, vn_ref, *, heads):
    v = v_ref[...].astype(F32)
    mu = jnp.mean(v, axis=-1, keepdims=True)
    vc = v - mu
    var = jnp.mean(vc * vc, axis=-1, keepdims=True)
    vn_ref[...] = (vc * lax.rsqrt(var + LN_EPS) * lng_ref[...] + lnb_ref[...]).astype(BF16)

    n_chunks = v_ref.shape[0] // SG_CHUNK
    row = lax.broadcasted_iota(I32, (SG_CHUNK, SG_CHUNK), 0)
    col = lax.broadcasted_iota(I32, (SG_CHUNK, SG_CHUNK), 1)
    causal = row >= col
    for hd in range(heads):
        cs = slice(hd * HEAD_DIM, (hd + 1) * HEAD_DIM)
        wm = jnp.where(causal, ws_ref[hd], 0.0).astype(BF16)
        bias = bst_ref[:, hd:hd + 1]
        for ci in range(n_chunks):
            rs = slice(ci * SG_CHUNK, (ci + 1) * SG_CHUNK)
            mixed = _dot(wm, vn_ref[rs, cs]) + bias
            y_ref[rs, cs] = (u_ref[rs, cs].astype(F32) * mixed).astype(BF16)


def _spatial_gate(z, ln_g, ln_b, w_s, b_s, *, tm=256):
    T, two_w = z.shape
    W = two_w // 2
    heads = w_s.shape[0]
    return pl.pallas_call(
        functools.partial(_sg_kernel, heads=heads),
        grid=(T // tm,),
        in_specs=[pl.BlockSpec((tm, W), lambda i: (i, 0)),
                  pl.BlockSpec((tm, W), lambda i: (i, 1)),
                  pl.BlockSpec((1, W), lambda i: (0, 0)),
                  pl.BlockSpec((1, W), lambda i: (0, 0)),
                  pl.BlockSpec((heads, SG_CHUNK, SG_CHUNK), lambda i: (0, 0, 0)),
                  pl.BlockSpec((SG_CHUNK, heads), lambda i: (0, 0))],
        out_specs=pl.BlockSpec((tm, W), lambda i: (i, 0)),
        out_shape=jax.ShapeDtypeStruct((T, W), BF16),
        scratch_shapes=[pltpu.VMEM((tm, W), BF16)],
        compiler_params=_cparams(("parallel",)),
        name="spatial_gate",
    )(z, z, ln_g, ln_b, w_s, jnp.transpose(b_s))


POOL_HALO = 16


def _pool_kernel(z_ref, halo_ref, wg_ref, ls_ref, y_ref, *, seq):
    tm = z_ref.shape[0]
    gdim = wg_ref.shape[1]
    pos0 = (pl.program_id(0) * tm) % seq
    halo_on = (pos0 > 0).astype(F32)
    pos = pos0 + lax.broadcasted_iota(I32, (tm, 1), 0)
    for gi, wnd in enumerate(POOL_WINDOWS):
        cs = slice(gi * gdim, (gi + 1) * gdim)
        zg = z_ref[:, cs]
        s = jnp.concatenate([halo_ref[:, cs] * halo_on, zg], axis=0)
        k = 1
        while k < wnd:
            s = s + pltpu.roll(s, k, 0)
            k *= 2
        cnt = jnp.minimum(pos + 1, wnd).astype(F32)
        pooled = s[POOL_HALO:, :] / cnt - zg
        yg = _dot(pooled.astype(BF16), wg_ref[gi]) * ls_ref[:, cs]
        y_ref[:, cs] = yg.astype(BF16)


def _pool_mix(z, w_grp, ls, seq, *, tm=256):
    T, D = z.shape
    G, gdim, _ = w_grp.shape
    assert max(POOL_WINDOWS) <= POOL_HALO and tm % POOL_HALO == 0
    ratio = tm // POOL_HALO
    return pl.pallas_call(
        functools.partial(_pool_kernel, seq=seq),
        grid=(T // tm,),
        in_specs=[pl.BlockSpec((tm, D), lambda i: (i, 0)),
                  pl.BlockSpec((POOL_HALO, D), lambda i: (jnp.maximum(i * ratio - 1, 0), 0)),
                  pl.BlockSpec((G, gdim, gdim), lambda i: (0, 0, 0)),
                  pl.BlockSpec((1, D), lambda i: (0, 0))],
        out_specs=pl.BlockSpec((tm, D), lambda i: (i, 0)),
        out_shape=jax.ShapeDtypeStruct((T, D), BF16),
        compiler_params=_cparams(("parallel",)),
        name="pool_mix",
    )(z, z, w_grp, ls)


HGRN_CHUNK = 128
LOG2_E = 1.4426950408889634


def _hgrn_tables(c=HGRN_CHUNK):
    t = np.arange(c)
    sums = [(t[None, :] <= t[:, None])]
    masks = []
    h = c // 2
    while h >= 1:
        off = t % (2 * h)
        mid = (t // (2 * h)) * (2 * h) + h
        second = off >= h
        j = t[None, :]
        if h >= 2:
            m_q = second[:, None] & (j >= mid[:, None]) & (j <= t[:, None])
            m_k = (~second)[:, None] & (j > t[:, None]) & (j < mid[:, None])
            sums.append(m_q | m_k)
        same = (t[:, None] // (2 * h)) == (t[None, :] // (2 * h))
        masks.append(same & second[:, None] & (~second)[None, :])
        h //= 2
    sums = np.stack(sums).astype(np.float32)
    return np.concatenate([sums, sums], axis=2), np.stack(masks).astype(np.float32)


def _hgrn_kernel(q_ref, f_ref, i_ref, g_ref, lb_ref, gn_ref, sums_ref, masks_ref, o_ref,
                 st_ref, qb_ref, ke_ref, z_ref, qd_ref, kd_ref, vb_ref, eb_ref, *, heads):
    c = q_ref.shape[0]
    n_sum = sums_ref.shape[0]
    n_lvl = masks_ref.shape[0]

    @pl.when(pl.program_id(1) == 0)
    def _():
        st_ref[...] = jnp.zeros_like(st_ref)

    lb = lb_ref[...]
    sig = jax.nn.sigmoid(f_ref[...])
    fg = lb + (1.0 - lb) * sig
    lf2 = jnp.log(fg) * np.float32(LOG2_E)
    kk = (1.0 - lb) * (1.0 - sig)
    lf_hi, lf_lo = _split_bf16(lf2)
    lf_cat = jnp.concatenate([lf_hi, lf_lo], axis=0)
    qr = q_ref[...].astype(F32)
    q = qr * jax.nn.sigmoid(qr)
    vb_ref[...] = i_ref[...].astype(BF16)
    q16 = q.astype(BF16)
    k16 = kk.astype(BF16)
    qd_ref[...] = q16
    kd_ref[...] = k16

    b2 = _dot(sums_ref[0], lf_cat)
    eb = jnp.exp2(b2)
    qb_ref[...] = (q * eb).astype(BF16)
    eb_ref[...] = eb[c - 1:c, :]
    ke_ref[...] = (kk * jnp.exp2(b2[c - 1:c, :] - b2)).astype(BF16)

    row = lax.broadcasted_iota(I32, (c, 1), 0)
    for l in range(n_lvl):
        half = c >> (l + 1)
        second = (row & (2 * half - 1)) >= half
        if l + 1 < n_sum:
            x = jnp.exp2(_dot(sums_ref[l + 1], lf_cat).astype(BF16))
        else:
            x = jnp.where(second, fg, 1.0).astype(BF16)
        z_ref[l] = jnp.where(second, q16, k16) * x

    gn = gn_ref[...]

    def head(hd, carry):
        cs = pl.ds(pl.multiple_of(hd * HEAD_DIM, HEAD_DIM), HEAD_DIM)
        z = z_ref[0, :, cs]
        scores = masks_ref[0] * _dot_nt(z, z).astype(BF16)
        for l in range(1, n_lvl):
            z = z_ref[l, :, cs]
            scores = scores + masks_ref[l] * _dot_nt(z, z).astype(BF16)
        st = st_ref[hd]
        vh = vb_ref[:, cs]
        diag = jnp.sum(qd_ref[:, cs].astype(F32) * kd_ref[:, cs].astype(F32), axis=-1,
                       keepdims=True)
        o = (_dot_nt(qb_ref[:, cs], st.astype(BF16)) + _dot(scores, vh)
             + diag * vh.astype(F32))
        st_ref[hd] = st * eb_ref[:, cs] + _dot_tn(vh, ke_ref[:, cs])
        on = o * lax.rsqrt(jnp.mean(o * o, axis=-1, keepdims=True) + RMS_EPS) * gn
        gv = g_ref[:, cs].astype(F32)
        o_ref[:, cs] = (on * (gv * jax.nn.sigmoid(gv))).astype(BF16)
        return carry

    lax.fori_loop(0, heads, head, 0, unroll=4)


def _hgrn_mix(proj, f_pre, lb, gnorm_g, batch, seq):
    T, four_d = proj.shape
    D = four_d // 4
    heads = D // HEAD_DIM
    c = HGRN_CHUNK
    n_chunks = seq // c
    sums, masks = _hgrn_tables(c)
    n_sum, n_mask = sums.shape[0], masks.shape[0]
    row = lambda b, i: b * n_chunks + i
    return pl.pallas_call(
        functools.partial(_hgrn_kernel, heads=heads),
        grid=(batch, n_chunks),
        in_specs=[pl.BlockSpec((c, D), lambda b, i: (row(b, i), 0)),
                  pl.BlockSpec((c, D), lambda b, i: (row(b, i), 0)),
                  pl.BlockSpec((c, D), lambda b, i: (row(b, i), 2)),
                  pl.BlockSpec((c, D), lambda b, i: (row(b, i), 3)),
                  pl.BlockSpec((1, D), lambda b, i: (0, 0)),
                  pl.BlockSpec((1, HEAD_DIM), lambda b, i: (0, 0)),
                  pl.BlockSpec((n_sum, c, 2 * c), lambda b, i: (0, 0, 0)),
                  pl.BlockSpec((n_mask, c, c), lambda b, i: (0, 0, 0))],
        out_specs=pl.BlockSpec((c, D), lambda b, i: (row(b, i), 0)),
        out_shape=jax.ShapeDtypeStruct((T, D), BF16),
        scratch_shapes=[pltpu.VMEM((heads, HEAD_DIM, HEAD_DIM), F32),
                        pltpu.VMEM((c, D), BF16),
                        pltpu.VMEM((c, D), BF16),
                        pltpu.VMEM((n_mask, c, D), BF16),
                        pltpu.VMEM((c, D), BF16),
                        pltpu.VMEM((c, D), BF16),
                        pltpu.VMEM((c, D), BF16),
                        pltpu.VMEM((1, D), F32)],
        compiler_params=_cparams(("parallel", "arbitrary")),
        name="hgrn_mix",
    )(proj, f_pre, proj, proj, lb, gnorm_g, jnp.asarray(sums, BF16), jnp.asarray(masks, BF16))


def _route_kernel(y_ref, w_ref, xin_ref, mgate_ref, g_ref, sc_ref, sh_ref, rw_ref,
                  rb_ref, tri_ref, x_ref, hp_ref, idx_ref, gate_ref, rank_ref, cnt_ref, carry_ref,
                  *, n_experts):
    @pl.when(pl.program_id(0) == 0)
    def _():
        carry_ref[...] = jnp.zeros_like(carry_ref)

    x = xin_ref[...] + mgate_ref[0] * _dot(y_ref[...], w_ref[...])
    x_ref[...] = x
    h = _modulated_norm(x, g_ref[...], sc_ref[0], sh_ref[0])
    hp_ref[...] = _pack_halves(h)

    h_hi, h_lo = _split_bf16(h)
    both = _dot(h_hi, rw_ref[...])
    logits = (both[:, :LANES] + both[:, LANES:] + _dot(h_lo, rw_ref[:, :LANES])) + rb_ref[...]
    lt = jnp.transpose(logits)[:n_experts, :]
    eidx = lax.broadcasted_iota(I32, lt.shape, 0)
    vals, idxs, hits = [], [], []
    for _ in range(TOP_K):
        m = jnp.max(lt, axis=0, keepdims=True)
        sel = jnp.min(jnp.where(lt == m, eidx, n_experts), axis=0, keepdims=True)
        hit = eidx == sel
        vals.append(m)
        idxs.append(sel)
        hits.append(hit)
        lt = jnp.where(hit, -jnp.inf, lt)
    exps = [jnp.exp(v - vals[0]) for v in vals]
    denom = exps[0]
    for e in exps[1:]:
        denom = denom + e
    idx_ref[...] = jnp.concatenate(idxs, axis=0)
    gate_ref[...] = jnp.concatenate([e / denom for e in exps], axis=0)

    onehot = hits[0].astype(F32)
    for hit in hits[1:]:
        onehot = onehot + hit.astype(F32)
    before = carry_ref[...] + _dot(onehot.astype(BF16), tri_ref[...])
    rank_ref[...] = jnp.concatenate(
        [jnp.sum(jnp.where(hit, before, 0.0), axis=0, keepdims=True) for hit in hits],
        axis=0).astype(I32)
    carry_ref[...] += jnp.sum(onehot, axis=1, keepdims=True)
    cnt_ref[...] = carry_ref[...].astype(I32)


def _mixer_out_route(y, w, x, mix_gate, g, scale, shift, r_w, r_b, seq, *, tm=256):
    T, K = y.shape
    D = w.shape[1]
    E = r_w.shape[1]
    rw_pad = jnp.zeros((D, LANES), F32).at[:, :E].set(r_w)
    rw_cat = jnp.concatenate(_split_bf16(rw_pad), axis=1)
    rb_pad = jnp.full((1, LANES), -1e30, F32).at[0, :E].set(r_b)
    tri = jnp.asarray(np.triu(np.ones((tm, tm), np.float32), k=1), BF16)
    pw, pdt = _packed_layout(D)
    bmap = lambda i: ((i * tm) // seq, 0, 0)
    const = lambda i: (0, 0)
    rows = lambda i: (i, 0)
    kt = pl.BlockSpec((TOP_K, tm), lambda i: (0, i))
    return pl.pallas_call(
        functools.partial(_route_kernel, n_experts=E),
        grid=(T // tm,),
        in_specs=[pl.BlockSpec((tm, K), rows),
                  pl.BlockSpec((K, D), const),
                  pl.BlockSpec((tm, D), rows),
                  pl.BlockSpec((1, 1, D), bmap),
                  pl.BlockSpec((1, D), const),
                  pl.BlockSpec((1, 1, D), bmap),
                  pl.BlockSpec((1, 1, D), bmap),
                  pl.BlockSpec((D, 2 * LANES), const),
                  pl.BlockSpec((1, LANES), const),
                  pl.BlockSpec((tm, tm), const)],
        out_specs=[pl.BlockSpec((tm, D), rows), pl.BlockSpec((tm, pw), rows), kt, kt, kt,
                   pl.BlockSpec((E, 1), const)],
        out_shape=[jax.ShapeDtypeStruct((T, D), F32),
                   jax.ShapeDtypeStruct((T, pw), pdt),
                   jax.ShapeDtypeStruct((TOP_K, T), I32),
                   jax.ShapeDtypeStruct((TOP_K, T), F32),
                   jax.ShapeDtypeStruct((TOP_K, T), I32),
                   jax.ShapeDtypeStruct((E, 1), I32)],
        scratch_shapes=[pltpu.VMEM((E, 1), F32)],
        compiler_params=_cparams(("arbitrary",)),
        name="mixer_out_route",
    )(y, w, x, mix_gate, g, scale, shift, rw_cat, rb_pad, tri)


def _sc_gather(table, idx):
    R = idx.shape[0]
    W = table.shape[1]
    rows = SC_ROWS_PER_STEP // 2
    per_worker = R // SC_WORKERS
    steps = per_worker // rows
    assert steps * rows * SC_WORKERS == R and steps % 2 == 0
    mesh = plsc.VectorSubcoreMesh(core_axis_name="c", subcore_axis_name="s")
    n_cores = mesh.num_cores

    @functools.partial(
        pl.kernel, mesh=mesh,
        out_type=jax.ShapeDtypeStruct((R, W), table.dtype),
        scratch_types=[pltpu.VMEM((steps, rows), I32),
                       pltpu.VMEM((rows, W), table.dtype),
                       pltpu.VMEM((rows, W), table.dtype),
                       pltpu.SemaphoreType.DMA((2,)),
                       pltpu.SemaphoreType.DMA((2,))],
    )
    def gather(table_hbm, idx_hbm, out_hbm, idx_v, buf0, buf1, gsem, wsem):
        wid = lax.axis_index("s") * n_cores + lax.axis_index("c")
        base = wid * per_worker
        bufs = (buf0, buf1)
        pltpu.sync_copy(idx_hbm.at[pl.ds(wid * steps, steps)], idx_v)

        def fetch(j, b):
            return pltpu.make_async_copy(table_hbm.at[idx_v.at[j]], bufs[b], gsem.at[b])

        def flush(j, b):
            off = pl.multiple_of(base + j * rows, 8)
            return pltpu.make_async_copy(bufs[b], out_hbm.at[pl.ds(off, rows)], wsem.at[b])

        fetch(0, 0).start()

        @pl.loop(0, steps, step=2)
        def _(j0):
            for b in range(2):
                j = j0 + b
                fetch(j, b).wait()
                flush(j, b).start()

                @pl.when(j + 1 < steps)
                def _():
                    @pl.when(j >= 1)
                    def _():
                        flush(j - 1, 1 - b).wait()
                    fetch(j + 1, 1 - b).start()

        flush(steps - 2, 0).wait()
        flush(steps - 1, 1).wait()

    return gather(table, idx.reshape(R // rows, rows))


def _sc_scatter_rows(table, slots, n_out):
    K, T = slots.shape
    W = table.shape[1]
    rows = SC_ROWS_PER_STEP
    per_worker = T // SC_WORKERS
    steps = per_worker // rows
    assert steps * rows * SC_WORKERS == T
    mesh = plsc.VectorSubcoreMesh(core_axis_name="c", subcore_axis_name="s")
    n_cores = mesh.num_cores

    @functools.partial(
        pl.kernel, mesh=mesh,
        out_type=jax.ShapeDtypeStruct((n_out, W), table.dtype),
        scratch_types=[pltpu.VMEM((K, rows), I32),
                       pltpu.VMEM((rows, W), table.dtype)],
    )
    def scatter(table_hbm, slots_hbm, out_hbm, idx_v, rows_v):
        wid = lax.axis_index("s") * n_cores + lax.axis_index("c")
        base = wid * per_worker

        @pl.loop(0, steps)
        def _(j):
            off = pl.multiple_of(base + j * rows, 8)
            pltpu.sync_copy(table_hbm.at[pl.ds(off, rows)], rows_v)
            for k in range(K):
                pltpu.sync_copy(slots_hbm.at[pl.ds(pl.multiple_of(k * T + off, 8), rows)],
                                idx_v.at[k])
                pltpu.sync_copy(rows_v, out_hbm.at[idx_v.at[k]])

    return scatter(table, slots.reshape(K * T))


EXPERT_SUB_ROWS = 256


def _expert_kernel(blk_e_ref, n_valid_ref, next_e_ref, xs_ref, wg_hbm, bg_ref, wl_hbm, bl_ref,
                   wo_hbm, bo_ref, y_ref, wg_st, wl_st, wo_st, wgb_ref, wlb_ref, wob_ref, sem,
                   *, expert_base):
    i = pl.program_id(0)
    e = blk_e_ref[i]
    n_valid = n_valid_ref[i]
    new_expert = jnp.logical_or(i == 0, e != blk_e_ref[jnp.maximum(i - 1, 0)])

    def weight_copies(expert):
        idx = expert_base + expert
        return (pltpu.make_async_copy(wg_hbm.at[idx], wg_st, sem.at[0]),
                pltpu.make_async_copy(wl_hbm.at[idx], wl_st, sem.at[1]),
                pltpu.make_async_copy(wo_hbm.at[idx], wo_st, sem.at[2]))

    @pl.when(i == 0)
    def _():
        for cp in weight_copies(e):
            cp.start()

    @pl.when(jnp.logical_and(new_expert, n_valid > 0))
    def _():
        for cp in weight_copies(e):
            cp.wait()
        wgb_ref[...] = wg_st[...].astype(BF16)
        wlb_ref[...] = wl_st[...].astype(BF16)
        wob_ref[...] = wo_st[...].astype(BF16)
        nxt = next_e_ref[i]

        @pl.when(nxt >= 0)
        def _():
            for cp in weight_copies(nxt):
                cp.start()

    half = wgb_ref.shape[0] // 2
    for first in range(0, xs_ref.shape[0], EXPERT_SUB_ROWS):
        rs = slice(first, first + EXPERT_SUB_ROWS)

        @pl.when(n_valid > first)
        def _():
            live = first + lax.broadcasted_iota(I32, (EXPERT_SUB_ROWS, 1), 0) < n_valid
            lo, hi = _unpack_halves(jnp.where(live, xs_ref[rs, :], 0))
            lo = lo.astype(BF16)
            hi = hi.astype(BF16)
            a = _dot(lo, wgb_ref[:half, :]) + _dot(hi, wgb_ref[half:, :]) + bg_ref[0]
            l = _dot(lo, wlb_ref[:half, :]) + _dot(hi, wlb_ref[half:, :]) + bl_ref[0]
            a = jnp.minimum(a, SWIGLU_LIMIT)
            l = jnp.clip(l, -SWIGLU_LIMIT, SWIGLU_LIMIT)
            act = a * jax.nn.sigmoid(SWIGLU_ALPHA * a) * (l + 1.0)
            y_ref[rs, :] = _pack_halves(_dot(act.astype(BF16), wob_ref[...]) + bo_ref[0])

        @pl.when(n_valid <= first)
        def _():
            y_ref[rs, :] = jnp.zeros((EXPERT_SUB_ROWS, y_ref.shape[1]), y_ref.dtype)


def _experts(xs, blk_e, n_valid, next_e, layer, w_glu, b_glu, w_lin, b_lin, w_out, b_out, *,
             blk):
    P, pw = xs.shape
    L, E, D, F = w_glu.shape
    n_blocks = P // blk
    bmap = lambda i, be, nv, ne: (layer * E + be[i], 0, 0)
    rows = lambda i, be, nv, ne: (i, 0)
    w_glu, w_lin = w_glu.reshape(L * E, D, F), w_lin.reshape(L * E, D, F)
    w_out = w_out.reshape(L * E, F, D)
    b_glu, b_lin = b_glu.reshape(L * E, 1, F), b_lin.reshape(L * E, 1, F)
    b_out = b_out.reshape(L * E, 1, D)
    hbm = pl.BlockSpec(memory_space=pl.ANY)
    grid_spec = pltpu.PrefetchScalarGridSpec(
        num_scalar_prefetch=3,
        grid=(n_blocks,),
        in_specs=[pl.BlockSpec((blk, pw), rows),
                  hbm, pl.BlockSpec((1, 1, F), bmap),
                  hbm, pl.BlockSpec((1, 1, F), bmap),
                  hbm, pl.BlockSpec((1, 1, D), bmap)],
        out_specs=pl.BlockSpec((blk, pw), rows),
        scratch_shapes=[pltpu.VMEM((D, F), F32), pltpu.VMEM((D, F), F32),
                        pltpu.VMEM((F, D), F32),
                        pltpu.VMEM((D, F), BF16), pltpu.VMEM((D, F), BF16),
                        pltpu.VMEM((F, D), BF16),
                        pltpu.SemaphoreType.DMA((3,))],
    )
    return pl.pallas_call(
        functools.partial(_expert_kernel, expert_base=layer * E),
        grid_spec=grid_spec,
        out_shape=jax.ShapeDtypeStruct((P, pw), xs.dtype),
        compiler_params=_cparams(("arbitrary",)),
        name="moe_experts",
    )(blk_e, n_valid, next_e, xs, w_glu, b_glu, w_lin, b_lin, w_out, b_out)


def _combine_kernel(ys_ref, tg_ref, x_ref, gate_ref, *rest, final):
    o_ref = rest[-1]
    half = x_ref.shape[1] // 2
    tg = tg_ref[...]
    lo_sum = hi_sum = None
    for k in range(ys_ref.shape[0]):
        lo, hi = _unpack_halves(ys_ref[k])
        gk = tg[:, k:k + 1]
        lo_sum = gk * lo if lo_sum is None else lo_sum + gk * lo
        hi_sum = gk * hi if hi_sum is None else hi_sum + gk * hi
    x_lo = x_ref[:, :half] + gate_ref[0, :, :half] * lo_sum
    x_hi = x_ref[:, half:] + gate_ref[0, :, half:] * hi_sum
    if final:
        g_ref = rest[0]
        ms = (jnp.sum(x_lo * x_lo, axis=-1, keepdims=True)
              + jnp.sum(x_hi * x_hi, axis=-1, keepdims=True)) / x_ref.shape[1]
        inv = lax.rsqrt(ms + RMS_EPS)
        x_lo = x_lo * inv * g_ref[:, :half]
        x_hi = x_hi * inv * g_ref[:, half:]
    o_ref[:, :half] = x_lo
    o_ref[:, half:] = x_hi


def _combine(ys, tok_gate, x, gate, seq, final_g=None, *, tm=256):
    K, T, pw = ys.shape
    D = x.shape[1]
    in_specs = [pl.BlockSpec((K, tm, pw), lambda i: (0, i, 0)),
                pl.BlockSpec((tm, K), lambda i: (i, 0)),
                pl.BlockSpec((tm, D), lambda i: (i, 0)),
                pl.BlockSpec((1, 1, D), lambda i: ((i * tm) // seq, 0, 0))]
    args = [ys, tok_gate, x, gate]
    if final_g is not None:
        in_specs.append(pl.BlockSpec((1, D), lambda i: (0, 0)))
        args.append(final_g)
    return pl.pallas_call(
        functools.partial(_combine_kernel, final=final_g is not None),
        grid=(T // tm,),
        in_specs=in_specs,
        out_specs=pl.BlockSpec((tm, D), lambda i: (i, 0)),
        out_shape=jax.ShapeDtypeStruct((T, D), F32),
        compiler_params=_cparams(("parallel",)),
        name="moe_combine",
    )(*args)


MOE_BLOCK = 512


def _slots_kernel(start_ref, idx_ref, rank_ref, o_ref):
    idx = idx_ref[...]
    acc = rank_ref[...]
    for e in range(start_ref.shape[0]):
        acc = acc + jnp.where(idx == e, start_ref[e], 0)
    o_ref[...] = acc


def _slots(idx_t, rank_t, group_start):
    full = pl.BlockSpec(idx_t.shape, lambda i, gs: (0, 0))
    return pl.pallas_call(
        _slots_kernel,
        grid_spec=pltpu.PrefetchScalarGridSpec(num_scalar_prefetch=1, grid=(1,),
                                               in_specs=[full, full], out_specs=full),
        out_shape=jax.ShapeDtypeStruct(idx_t.shape, I32),
        compiler_params=_cparams(("arbitrary",)),
        name="moe_slots",
    )(group_start, idx_t, rank_t)


def _dispatch_plan(idx_t, rank_t, counts, blk):
    K, T = idx_t.shape
    E = counts.shape[0]
    padded = (counts + blk - 1) // blk * blk
    pad_end = jnp.cumsum(padded)
    pad_start = pad_end - padded
    experts = jnp.arange(E, dtype=I32)
    slots = _slots(idx_t, rank_t, pad_start.astype(I32))
    n_blocks = -(-(K * T + E * (blk - 1)) // blk)
    blk_start = jnp.arange(n_blocks, dtype=I32) * blk
    blk_e = jnp.minimum(jnp.sum(pad_end[None, :] <= blk_start[:, None], axis=1), E - 1).astype(I32)
    n_valid = jnp.clip(pad_start[blk_e] + counts[blk_e] - blk_start, 0, blk).astype(I32)
    later = jnp.where(counts > 0, experts, E)
    after = lax.cummin(jnp.concatenate([later[1:], jnp.full((1,), E, I32)]), reverse=True)
    next_e = jnp.where(after < E, after, -1).astype(I32)[blk_e]
    return slots, blk_e, n_valid, next_e, n_blocks * blk


def _mixer_out_moe(y, w_mix, x, mix_gate, g, scale, shift, gate, r_w, r_b, layer,
                   w_glu, b_glu, w_lin, b_lin, w_out, b_out, seq, final_g=None):
    T, D = x.shape
    x, hp, idx_t, gate_t, rank_t, counts = _mixer_out_route(
        y, w_mix, x, mix_gate, g, scale, shift, r_w, r_b, seq)
    slots, blk_e, n_valid, next_e, n_rows = _dispatch_plan(idx_t, rank_t, counts[:, 0],
                                                           MOE_BLOCK)
    xs = _sc_scatter_rows(hp, slots, n_rows)
    yp = _experts(xs, blk_e, n_valid, next_e, layer, w_glu, b_glu, w_lin, b_lin, w_out, b_out,
                  blk=MOE_BLOCK)
    ys = _sc_gather(yp, slots.reshape(TOP_K * T)).reshape(TOP_K, T, yp.shape[1])
    return _combine(ys, jnp.transpose(gate_t), x, gate, seq, final_g)


def kernel(x, c, norm_g, ada_w, final_norm_g, sg_w_in, sg_ln_g, sg_ln_b, sg_w_s, sg_b_s, sg_w_out, pool_w_in, pool_w_grp, pool_ls, pool_w_out, hgrn_w_in, hgrn_lb_logits, hgrn_gnorm_g, hgrn_w_out, router_w, router_b, expert_w_glu, expert_b_glu, expert_w_lin, expert_b_lin, expert_w_out, expert_b_out):
    B, S, D = x.shape
    depth = norm_g.shape[0]
    T = B * S
    xf = x.reshape(T, D)

    ada_w = ada_w.reshape(depth * 2, D, 3 * D)

    def mod_parts(layer, sub):
        m = _ada_mod(c, ada_w, layer * 2 + sub)[:, None, :]
        return m[..., :D], m[..., D:2 * D], m[..., 2 * D:]

    lb_cum = jnp.cumsum(jax.nn.softmax(hgrn_lb_logits.astype(F32), axis=0), axis=0)
    lower_bounds = lb_cum - lb_cum[0]

    for layer in range(depth):
        kind, slot = layer % 3, layer // 3
        shift, scale, mix_gate = mod_parts(layer, 0)
        g = norm_g[layer, 0][None, :]
        if kind == 0:
            z = _norm_matmul(xf, g, scale, shift, sg_w_in[slot].astype(BF16), S,
                             act="gelu", out_dtype=BF16)
            y = _spatial_gate(z, sg_ln_g[slot][None, :], sg_ln_b[slot][None, :],
                              sg_w_s[slot], sg_b_s[slot])
            w_out = sg_w_out[slot]
        elif kind == 1:
            z = _norm_matmul(xf, g, scale, shift, pool_w_in[slot].astype(BF16), S,
                             act=None, out_dtype=F32)
            y = _pool_mix(z, pool_w_grp[slot].astype(BF16), pool_ls[slot][None, :], S)
            w_out = pool_w_out[slot]
        else:
            proj, f_pre = _norm_matmul(xf, g, scale, shift, hgrn_w_in[slot].astype(BF16), S,
                                       act=None, out_dtype=BF16, f32_col=1, tn=D)
            y = _hgrn_mix(proj, f_pre, lower_bounds[layer][None, :],
                          hgrn_gnorm_g[slot][None, :], B, S)
            w_out = hgrn_w_out[slot]

        shift, scale, gate = mod_parts(layer, 1)
        final_g = final_norm_g[None, :] if layer == depth - 1 else None
        xf = _mixer_out_moe(y, w_out.astype(BF16), xf, mix_gate, norm_g[layer, 1][None, :],
                            scale, shift, gate, router_w[layer], router_b[layer], layer,
                            expert_w_glu, expert_b_glu, expert_w_lin, expert_b_lin,
                            expert_w_out, expert_b_out, S, final_g)

    return xf.reshape(B, S, D)
```

```python
import functools

import numpy as np
import jax
import jax.numpy as jnp
from jax import lax
from jax.experimental import pallas as pl
from jax.experimental.pallas import tpu as pltpu
from jax.experimental.pallas import tpu_sc as plsc

F32 = jnp.float32
BF16 = jnp.bfloat16
I32 = jnp.int32
U32 = jnp.uint32

RMS_EPS = 1e-5
LN_EPS = 1e-5
SWIGLU_ALPHA = 1.702
SWIGLU_LIMIT = 7.0
TOP_K = 4
SG_CHUNK = 128
POOL_WINDOWS = (2, 4, 8, 16)
HEAD_DIM = 128
LANES = 128
SC_WORKERS = 32
SC_ROWS_PER_STEP = 64

VMEM_LIMIT = 48 * 1024 * 1024
BIG_VMEM_LIMIT = 56 * 1024 * 1024


def _cparams(sem):
    return pltpu.CompilerParams(dimension_semantics=sem, vmem_limit_bytes=VMEM_LIMIT)


def _dot(a, b):
    return jnp.dot(a, b, preferred_element_type=F32)


def _dot_nt(a, b):
    return lax.dot_general(a, b, (((1,), (1,)), ((), ())), preferred_element_type=F32)


def _dot_tn(a, b):
    return lax.dot_general(a, b, (((0,), (0,)), ((), ())), preferred_element_type=F32)


def _split_bf16(x):
    hi = x.astype(BF16)
    lo = (x - hi.astype(F32)).astype(BF16)
    return hi, lo


def _packed_layout(d):
    return d // 2, I32


def _pack_halves(y):
    n = y.shape[1] // 2
    lo = lax.bitcast_convert_type(y[:, :n].astype(BF16).astype(F32), U32)
    hi = lax.bitcast_convert_type(y[:, n:].astype(BF16).astype(F32), U32)
    word = lax.shift_right_logical(lo, jnp.uint32(16)) | (hi & jnp.uint32(0xFFFF0000))
    return lax.bitcast_convert_type(word, I32)


def _unpack_halves(w):
    u = lax.bitcast_convert_type(w, U32)
    lo = lax.bitcast_convert_type(lax.shift_left(u, jnp.uint32(16)), F32)
    hi = lax.bitcast_convert_type(u & jnp.uint32(0xFFFF0000), F32)
    return lo, hi


def _modulated_norm(x, g, scale, shift):
    ms = jnp.mean(x * x, axis=-1, keepdims=True)
    y = x * lax.rsqrt(ms + RMS_EPS) * g
    return y * (1.0 + scale) + shift


def _ada_kernel(c_ref, w_ref, o_ref):
    w = w_ref[0]
    reps = w.shape[1] // LANES
    for b in range(c_ref.shape[0]):
        cb = c_ref[b]
        s = jnp.tile(cb * jax.nn.sigmoid(cb), (1, reps))
        o_ref[b] = jnp.sum(w * s, axis=0, keepdims=True)


def _ada_mod(c, ada_w, index):
    L, D, N = ada_w.shape
    B = c.shape[0]
    tn = 512
    c_cols = jnp.broadcast_to(c[:, :, None], (B, D, LANES))
    out = pl.pallas_call(
        _ada_kernel,
        grid=(N // tn,),
        in_specs=[pl.BlockSpec((B, D, LANES), lambda j: (0, 0, 0)),
                  pl.BlockSpec((1, D, tn), lambda j: (index, 0, j))],
        out_specs=pl.BlockSpec((B, 1, tn), lambda j: (0, 0, j)),
        out_shape=jax.ShapeDtypeStruct((B, 1, N), F32),
        compiler_params=_cparams(("parallel",)),
        name="ada_mod",
    )(c_cols, ada_w)
    return out[:, 0]


def _nm_kernel(x_ref, g_ref, sc_ref, sh_ref, w_ref, o_ref, *rest, act, f32_col):
    h_ref = rest[-1]

    @pl.when(pl.program_id(1) == 0)
    def _():
        h = _modulated_norm(x_ref[...], g_ref[...], sc_ref[0], sh_ref[0])
        h_ref[...] = h.astype(BF16)

    acc = _dot(h_ref[...], w_ref[...])
    if act == "gelu":
        acc = 0.5 * acc * (1.0 + lax.erf(acc * np.float32(1.0 / np.sqrt(2.0))))
    o_ref[...] = acc.astype(o_ref.dtype)
    if f32_col is not None:
        @pl.when(pl.program_id(1) == f32_col)
        def _():
            rest[0][...] = acc


def _norm_matmul(x, g, scale, shift, w, seq, *, act, out_dtype, f32_col=None, tm=512, tn=2048):
    T, D = x.shape
    N = w.shape[1]
    bmap = lambda i, j: ((i * tm) // seq, 0, 0)
    out_specs = [pl.BlockSpec((tm, tn), lambda i, j: (i, j))]
    out_shape = [jax.ShapeDtypeStruct((T, N), out_dtype)]
    if f32_col is not None:
        out_specs.append(pl.BlockSpec((tm, tn), lambda i, j: (i, 0)))
        out_shape.append(jax.ShapeDtypeStruct((T, tn), F32))
    outs = pl.pallas_call(
        functools.partial(_nm_kernel, act=act, f32_col=f32_col),
        grid=(T // tm, N // tn),
        in_specs=[pl.BlockSpec((tm, D), lambda i, j: (i, 0)),
                  pl.BlockSpec((1, D), lambda i, j: (0, 0)),
                  pl.BlockSpec((1, 1, D), bmap),
                  pl.BlockSpec((1, 1, D), bmap),
                  pl.BlockSpec((D, tn), lambda i, j: (0, j))],
        out_specs=out_specs,
        out_shape=out_shape,
        scratch_shapes=[pltpu.VMEM((tm, D), BF16)],
        compiler_params=pltpu.CompilerParams(dimension_semantics=("parallel", "arbitrary"),
                                             vmem_limit_bytes=BIG_VMEM_LIMIT),
        name="norm_matmul_" + str(act),
    )(x, g, scale, shift, w)
    return outs if f32_col is not None else outs[0]


def _sg_kernel(u_ref, v_ref, lng_ref, lnb_ref, ws_ref, bst_ref, y_ref, vn_ref, *, heads):
    v = v_ref[...].astype(F32)
    mu = jnp.mean(v, axis=-1, keepdims=True)
    vc = v - mu
    var = jnp.mean(vc * vc, axis=-1, keepdims=True)
    vn_ref[...] = (vc * lax.rsqrt(var + LN_EPS) * lng_ref[...] + lnb_ref[...]).astype(BF16)

    n_chunks = v_ref.shape[0] // SG_CHUNK
    row = lax.broadcasted_iota(I32, (SG_CHUNK, SG_CHUNK), 0)
    col = lax.broadcasted_iota(I32, (SG_CHUNK, SG_CHUNK), 1)
    causal = row >= col
    for hd in range(heads):
        cs = slice(hd * HEAD_DIM, (hd + 1) * HEAD_DIM)
        wm = jnp.where(causal, ws_ref[hd], 0.0).astype(BF16)
        bias = bst_ref[:, hd:hd + 1]
        for ci in range(n_chunks):
            rs = slice(ci * SG_CHUNK, (ci + 1) * SG_CHUNK)
            mixed = _dot(wm, vn_ref[rs, cs]) + bias
            y_ref[rs, cs] = (u_ref[rs, cs].astype(F32) * mixed).astype(BF16)


def _spatial_gate(z, ln_g, ln_b, w_s, b_s, *, tm=256):
    T, two_w = z.shape
    W = two_w // 2
    heads = w_s.shape[0]
    return pl.pallas_call(
        functools.partial(_sg_kernel, heads=heads),
        grid=(T // tm,),
        in_specs=[pl.BlockSpec((tm, W), lambda i: (i, 0)),
                  pl.BlockSpec((tm, W), lambda i: (i, 1)),
                  pl.BlockSpec((1, W), lambda i: (0, 0)),
                  pl.BlockSpec((1, W), lambda i: (0, 0)),
                  pl.BlockSpec((heads, SG_CHUNK, SG_CHUNK), lambda i: (0, 0, 0)),
                  pl.BlockSpec((SG_CHUNK, heads), lambda i: (0, 0))],
        out_specs=pl.BlockSpec((tm, W), lambda i: (i, 0)),
        out_shape=jax.ShapeDtypeStruct((T, W), BF16),
        scratch_shapes=[pltpu.VMEM((tm, W), BF16)],
        compiler_params=_cparams(("parallel",)),
        name="spatial_gate",
    )(z, z, ln_g, ln_b, w_s, jnp.transpose(b_s))


POOL_HALO = 16


def _pool_kernel(z_ref, halo_ref, wg_ref, ls_ref, y_ref, *, seq):
    tm = z_ref.shape[0]
    gdim = wg_ref.shape[1]
    pos0 = (pl.program_id(0) * tm) % seq
    halo_on = (pos0 > 0).astype(F32)
    pos = pos0 + lax.broadcasted_iota(I32, (tm, 1), 0)
    for gi, wnd in enumerate(POOL_WINDOWS):
        cs = slice(gi * gdim, (gi + 1) * gdim)
        zg = z_ref[:, cs]
        s = jnp.concatenate([halo_ref[:, cs] * halo_on, zg], axis=0)
        k = 1
        while k < wnd:
            s = s + pltpu.roll(s, k, 0)
            k *= 2
        cnt = jnp.minimum(pos + 1, wnd).astype(F32)
        pooled = s[POOL_HALO:, :] / cnt - zg
        yg = _dot(pooled.astype(BF16), wg_ref[gi]) * ls_ref[:, cs]
        y_ref[:, cs] = yg.astype(BF16)


def _pool_mix(z, w_grp, ls, seq, *, tm=256):
    T, D = z.shape
    G, gdim, _ = w_grp.shape
    assert max(POOL_WINDOWS) <= POOL_HALO and tm % POOL_HALO == 0
    ratio = tm // POOL_HALO
    return pl.pallas_call(
        functools.partial(_pool_kernel, seq=seq),
        grid=(T // tm,),
        in_specs=[pl.BlockSpec((tm, D), lambda i: (i, 0)),
                  pl.BlockSpec((POOL_HALO, D), lambda i: (jnp.maximum(i * ratio - 1, 0), 0)),
                  pl.BlockSpec((G, gdim, gdim), lambda i: (0, 0, 0)),
                  pl.BlockSpec((1, D), lambda i: (0, 0))],
        out_specs=pl.BlockSpec((tm, D), lambda i: (i, 0)),
        out_shape=jax.ShapeDtypeStruct((T, D), BF16),
        compiler_params=_cparams(("parallel",)),
        name="pool_mix",
    )(z, z, w_grp, ls)


HGRN_CHUNK = 128
HGRN_BASE = 32
HGRN_GUARD_LOG2 = 100.0
LOG2_E = 1.4426950408889634


def _hgrn_tables(c=HGRN_CHUNK, base=HGRN_BASE):
    t = np.arange(c)
    j = t[None, :]
    sums = [j <= t[:, None]]
    masks = []
    n_shallow = 0
    h = c // 2
    while h >= 1:
        off = t % (2 * h)
        mid = (t // (2 * h)) * (2 * h) + h
        second = off >= h
        if h >= 2:
            m_q = second[:, None] & (j >= mid[:, None]) & (j <= t[:, None])
            m_k = (~second)[:, None] & (j > t[:, None]) & (j < mid[:, None])
            sums.append(m_q | m_k)
        same = (t[:, None] // (2 * h)) == (t[None, :] // (2 * h))
        masks.append(same & second[:, None] & (~second)[None, :])
        n_shallow += h >= base
        h //= 2
    same_base = (t[:, None] // base) == (t[None, :] // base)
    sums.append(same_base & (j <= t[:, None]))
    sums = np.stack(sums).astype(np.float32)
    base_mask = (same_base & (j <= t[:, None])).astype(np.float32)
    return (np.concatenate([sums, sums], axis=2), np.stack(masks).astype(np.float32), base_mask,
            n_shallow)


def _hgrn_kernel(q_ref, f_ref, i_ref, g_ref, lb_ref, gn_ref, sums_ref, masks_ref, bmask_ref,
                 o_ref, st_ref, qb_ref, ke_ref, z_ref, qd_ref, kd_ref, vb_ref, eb_ref,
                 *, heads, n_shallow):
    c = q_ref.shape[0]
    n_sum = sums_ref.shape[0] - 1
    n_lvl = masks_ref.shape[0]

    @pl.when(pl.program_id(1) == 0)
    def _():
        st_ref[...] = jnp.zeros_like(st_ref)

    lb = lb_ref[...]
    sig = jax.nn.sigmoid(f_ref[...])
    fg = lb + (1.0 - lb) * sig
    lf2 = jnp.log(fg) * np.float32(LOG2_E)
    kk = (1.0 - lb) * (1.0 - sig)
    lf_hi, lf_lo = _split_bf16(lf2)
    lf_cat = jnp.concatenate([lf_hi, lf_lo], axis=0)
    qr = q_ref[...].astype(F32)
    q = qr * jax.nn.sigmoid(qr)
    vb_ref[...] = i_ref[...].astype(BF16)
    q16 = q.astype(BF16)
    k16 = kk.astype(BF16)

    b2 = _dot(sums_ref[0], lf_cat)
    eb = jnp.exp2(b2)
    qb_ref[...] = (q * eb).astype(BF16)
    eb_ref[...] = eb[c - 1:c, :]
    ke_ref[...] = (kk * jnp.exp2(b2[c - 1:c, :] - b2)).astype(BF16)
    w2 = _dot(sums_ref[n_sum], lf_cat)
    shallow = jnp.min(w2) >= -HGRN_GUARD_LOG2

    row = lax.broadcasted_iota(I32, (c, 1), 0)

    def level(l):
        half = c >> (l + 1)
        second = (row & (2 * half - 1)) >= half
        if l + 1 < n_sum:
            x = jnp.exp2(_dot(sums_ref[l + 1], lf_cat).astype(BF16))
        else:
            x = jnp.where(second, fg, 1.0).astype(BF16)
        z_ref[l] = jnp.where(second, q16, k16) * x

    gn = gn_ref[...]

    def finish(hd, cs, o):
        st = st_ref[hd]
        vh = vb_ref[:, cs]
        o = o + _dot_nt(qb_ref[:, cs], st.astype(BF16))
        st_ref[hd] = st * eb_ref[:, cs] + _dot_tn(vh, ke_ref[:, cs])
        on = o * lax.rsqrt(jnp.mean(o * o, axis=-1, keepdims=True) + RMS_EPS) * gn
        gv = g_ref[:, cs].astype(F32)
        o_ref[:, cs] = (on * (gv * jax.nn.sigmoid(gv))).astype(BF16)

    def level_scores(cs, n):
        z = z_ref[0, :, cs]
        scores = masks_ref[0] * _dot_nt(z, z).astype(BF16)
        for l in range(1, n):
            z = z_ref[l, :, cs]
            scores = scores + masks_ref[l] * _dot_nt(z, z).astype(BF16)
        return scores

    @pl.when(shallow)
    def _():
        for l in range(n_shallow):
            level(l)
        qd_ref[...] = (q * jnp.exp2(w2)).astype(BF16)
        kd_ref[...] = (kk * jnp.exp2(-w2)).astype(BF16)

        def head(hd, carry):
            cs = pl.ds(pl.multiple_of(hd * HEAD_DIM, HEAD_DIM), HEAD_DIM)
            scores = (level_scores(cs, n_shallow)
                      + bmask_ref[...] * _dot_nt(qd_ref[:, cs], kd_ref[:, cs]).astype(BF16))
            finish(hd, cs, _dot(scores, vb_ref[:, cs]))
            return carry

        lax.fori_loop(0, heads, head, 0, unroll=4)

    @pl.when(jnp.logical_not(shallow))
    def _():
        for l in range(n_lvl):
            level(l)
        qd_ref[...] = q16
        kd_ref[...] = k16

        def head(hd, carry):
            cs = pl.ds(pl.multiple_of(hd * HEAD_DIM, HEAD_DIM), HEAD_DIM)
            vh = vb_ref[:, cs]
            diag = jnp.sum(qd_ref[:, cs].astype(F32) * kd_ref[:, cs].astype(F32), axis=-1,
                           keepdims=True)
            finish(hd, cs, _dot(level_scores(cs, n_lvl), vh) + diag * vh.astype(F32))
            return carry

        lax.fori_loop(0, heads, head, 0, unroll=4)


def _hgrn_mix(proj, f_pre, lb, gnorm_g, batch, seq):
    T, four_d = proj.shape
    D = four_d // 4
    heads = D // HEAD_DIM
    c = HGRN_CHUNK
    n_chunks = seq // c
    sums, masks, base_mask, n_shallow = _hgrn_tables(c)
    n_sum, n_mask = sums.shape[0], masks.shape[0]
    row = lambda b, i: b * n_chunks + i
    return pl.pallas_call(
        functools.partial(_hgrn_kernel, heads=heads, n_shallow=n_shallow),
        grid=(batch, n_chunks),
        in_specs=[pl.BlockSpec((c, D), lambda b, i: (row(b, i), 0)),
                  pl.BlockSpec((c, D), lambda b, i: (row(b, i), 0)),
                  pl.BlockSpec((c, D), lambda b, i: (row(b, i), 2)),
                  pl.BlockSpec((c, D), lambda b, i: (row(b, i), 3)),
                  pl.BlockSpec((1, D), lambda b, i: (0, 0)),
                  pl.BlockSpec((1, HEAD_DIM), lambda b, i: (0, 0)),
                  pl.BlockSpec((n_sum, c, 2 * c), lambda b, i: (0, 0, 0)),
                  pl.BlockSpec((n_mask, c, c), lambda b, i: (0, 0, 0)),
                  pl.BlockSpec((c, c), lambda b, i: (0, 0))],
        out_specs=pl.BlockSpec((c, D), lambda b, i: (row(b, i), 0)),
        out_shape=jax.ShapeDtypeStruct((T, D), BF16),
        scratch_shapes=[pltpu.VMEM((heads, HEAD_DIM, HEAD_DIM), F32),
                        pltpu.VMEM((c, D), BF16),
                        pltpu.VMEM((c, D), BF16),
                        pltpu.VMEM((n_mask, c, D), BF16),
                        pltpu.VMEM((c, D), BF16),
                        pltpu.VMEM((c, D), BF16),
                        pltpu.VMEM((c, D), BF16),
                        pltpu.VMEM((1, D), F32)],
        compiler_params=_cparams(("parallel", "arbitrary")),
        name="hgrn_mix",
    )(proj, f_pre, proj, proj, lb, gnorm_g, jnp.asarray(sums, BF16), jnp.asarray(masks, BF16),
      jnp.asarray(base_mask, BF16))


def _route_kernel(y_ref, w_ref, xin_ref, mgate_ref, g_ref, sc_ref, sh_ref, rw_ref,
                  rb_ref, tri_ref, x_ref, hp_ref, idx_ref, gate_ref, rank_ref, cnt_ref, carry_ref,
                  *, n_experts):
    @pl.when(pl.program_id(0) == 0)
    def _():
        carry_ref[...] = jnp.zeros_like(carry_ref)

    x = xin_ref[...] + mgate_ref[0] * _dot(y_ref[...], w_ref[...])
    x_ref[...] = x
    h = _modulated_norm(x, g_ref[...], sc_ref[0], sh_ref[0])
    hp_ref[...] = _pack_halves(h)

    h_hi, h_lo = _split_bf16(h)
    both = _dot(h_hi, rw_ref[...])
    logits = (both[:, :LANES] + both[:, LANES:] + _dot(h_lo, rw_ref[:, :LANES])) + rb_ref[...]
    lt = jnp.transpose(logits)[:n_experts, :]
    eidx = lax.broadcasted_iota(I32, lt.shape, 0)
    vals, idxs, hits = [], [], []
    for _ in range(TOP_K):
        m = jnp.max(lt, axis=0, keepdims=True)
        sel = jnp.min(jnp.where(lt == m, eidx, n_experts), axis=0, keepdims=True)
        hit = eidx == sel
        vals.append(m)
        idxs.append(sel)
        hits.append(hit)
        lt = jnp.where(hit, -jnp.inf, lt)
    exps = [jnp.exp(v - vals[0]) for v in vals]
    denom = exps[0]
    for e in exps[1:]:
        denom = denom + e
    idx_ref[...] = jnp.concatenate(idxs, axis=0)
    gate_ref[...] = jnp.concatenate([e / denom for e in exps], axis=0)

    onehot = hits[0].astype(F32)
    for hit in hits[1:]:
        onehot = onehot + hit.astype(F32)
    before = carry_ref[...] + _dot(onehot.astype(BF16), tri_ref[...])
    rank_ref[...] = jnp.concatenate(
        [jnp.sum(jnp.where(hit, before, 0.0), axis=0, keepdims=True) for hit in hits],
        axis=0).astype(I32)
    carry_ref[...] += jnp.sum(onehot, axis=1, keepdims=True)
    cnt_ref[...] = carry_ref[...].astype(I32)


def _mixer_out_route(y, w, x, mix_gate, g, scale, shift, r_w, r_b, seq, *, tm=256):
    T, K = y.shape
    D = w.shape[1]
    E = r_w.shape[1]
    rw_pad = jnp.zeros((D, LANES), F32).at[:, :E].set(r_w)
    rw_cat = jnp.concatenate(_split_bf16(rw_pad), axis=1)
    rb_pad = jnp.full((1, LANES), -1e30, F32).at[0, :E].set(r_b)
    tri = jnp.asarray(np.triu(np.ones((tm, tm), np.float32), k=1), BF16)
    pw, pdt = _packed_layout(D)
    bmap = lambda i: ((i * tm) // seq, 0, 0)
    const = lambda i: (0, 0)
    rows = lambda i: (i, 0)
    kt = pl.BlockSpec((TOP_K, tm), lambda i: (0, i))
    return pl.pallas_call(
        functools.partial(_route_kernel, n_experts=E),
        grid=(T // tm,),
        in_specs=[pl.BlockSpec((tm, K), rows),
                  pl.BlockSpec((K, D), const),
                  pl.BlockSpec((tm, D), rows),
                  pl.BlockSpec((1, 1, D), bmap),
                  pl.BlockSpec((1, D), const),
                  pl.BlockSpec((1, 1, D), bmap),
                  pl.BlockSpec((1, 1, D), bmap),
                  pl.BlockSpec((D, 2 * LANES), const),
                  pl.BlockSpec((1, LANES), const),
                  pl.BlockSpec((tm, tm), const)],
        out_specs=[pl.BlockSpec((tm, D), rows), pl.BlockSpec((tm, pw), rows), kt, kt, kt,
                   pl.BlockSpec((E, 1), const)],
        out_shape=[jax.ShapeDtypeStruct((T, D), F32),
                   jax.ShapeDtypeStruct((T, pw), pdt),
                   jax.ShapeDtypeStruct((TOP_K, T), I32),
                   jax.ShapeDtypeStruct((TOP_K, T), F32),
                   jax.ShapeDtypeStruct((TOP_K, T), I32),
                   jax.ShapeDtypeStruct((E, 1), I32)],
        scratch_shapes=[pltpu.VMEM((E, 1), F32)],
        compiler_params=_cparams(("arbitrary",)),
        name="mixer_out_route",
    )(y, w, x, mix_gate, g, scale, shift, rw_cat, rb_pad, tri)


def _sc_gather(table, idx):
    R = idx.shape[0]
    W = table.shape[1]
    rows = SC_ROWS_PER_STEP // 2
    per_worker = R // SC_WORKERS
    steps = per_worker // rows
    assert steps * rows * SC_WORKERS == R and steps % 2 == 0
    mesh = plsc.VectorSubcoreMesh(core_axis_name="c", subcore_axis_name="s")
    n_cores = mesh.num_cores

    @functools.partial(
        pl.kernel, mesh=mesh,
        out_type=jax.ShapeDtypeStruct((R, W), table.dtype),
        scratch_types=[pltpu.VMEM((steps, rows), I32),
                       pltpu.VMEM((rows, W), table.dtype),
                       pltpu.VMEM((rows, W), table.dtype),
                       pltpu.SemaphoreType.DMA((2,)),
                       pltpu.SemaphoreType.DMA((2,))],
    )
    def gather(table_hbm, idx_hbm, out_hbm, idx_v, buf0, buf1, gsem, wsem):
        wid = lax.axis_index("s") * n_cores + lax.axis_index("c")
        base = wid * per_worker
        bufs = (buf0, buf1)
        pltpu.sync_copy(idx_hbm.at[pl.ds(wid * steps, steps)], idx_v)

        def fetch(j, b):
            return pltpu.make_async_copy(table_hbm.at[idx_v.at[j]], bufs[b], gsem.at[b])

        def flush(j, b):
            off = pl.multiple_of(base + j * rows, 8)
            return pltpu.make_async_copy(bufs[b], out_hbm.at[pl.ds(off, rows)], wsem.at[b])

        fetch(0, 0).start()

        @pl.loop(0, steps, step=2)
        def _(j0):
            for b in range(2):
                j = j0 + b
                fetch(j, b).wait()
                flush(j, b).start()

                @pl.when(j + 1 < steps)
                def _():
                    @pl.when(j >= 1)
                    def _():
                        flush(j - 1, 1 - b).wait()
                    fetch(j + 1, 1 - b).start()

        flush(steps - 2, 0).wait()
        flush(steps - 1, 1).wait()

    return gather(table, idx.reshape(R // rows, rows))


def _sc_scatter_rows(table, slots, n_out):
    K, T = slots.shape
    W = table.shape[1]
    rows = SC_ROWS_PER_STEP
    per_worker = T // SC_WORKERS
    steps = per_worker // rows
    assert steps * rows * SC_WORKERS == T
    mesh = plsc.VectorSubcoreMesh(core_axis_name="c", subcore_axis_name="s")
    n_cores = mesh.num_cores

    @functools.partial(
        pl.kernel, mesh=mesh,
        out_type=jax.ShapeDtypeStruct((n_out, W), table.dtype),
        scratch_types=[pltpu.VMEM((K, rows), I32),
                       pltpu.VMEM((rows, W), table.dtype)],
    )
    def scatter(table_hbm, slots_hbm, out_hbm, idx_v, rows_v):
        wid = lax.axis_index("s") * n_cores + lax.axis_index("c")
        base = wid * per_worker

        @pl.loop(0, steps)
        def _(j):
            off = pl.multiple_of(base + j * rows, 8)
            pltpu.sync_copy(table_hbm.at[pl.ds(off, rows)], rows_v)
            for k in range(K):
                pltpu.sync_copy(slots_hbm.at[pl.ds(pl.multiple_of(k * T + off, 8), rows)],
                                idx_v.at[k])
                pltpu.sync_copy(rows_v, out_hbm.at[idx_v.at[k]])

    return scatter(table, slots.reshape(K * T))


EXPERT_SUB_ROWS = 256


def _expert_kernel(blk_e_ref, n_valid_ref, next_e_ref, xs_ref, wg_hbm, bg_ref, wl_hbm, bl_ref,
                   wo_hbm, bo_ref, y_ref, wg_st, wl_st, wo_st, wgb_ref, wlb_ref, wob_ref, sem,
                   *, expert_base):
    i = pl.program_id(0)
    e = blk_e_ref[i]
    n_valid = n_valid_ref[i]
    new_expert = jnp.logical_or(i == 0, e != blk_e_ref[jnp.maximum(i - 1, 0)])

    def weight_copies(expert):
        idx = expert_base + expert
        return (pltpu.make_async_copy(wg_hbm.at[idx], wg_st, sem.at[0]),
                pltpu.make_async_copy(wl_hbm.at[idx], wl_st, sem.at[1]),
                pltpu.make_async_copy(wo_hbm.at[idx], wo_st, sem.at[2]))

    @pl.when(i == 0)
    def _():
        for cp in weight_copies(e):
            cp.start()

    @pl.when(jnp.logical_and(new_expert, n_valid > 0))
    def _():
        for cp in weight_copies(e):
            cp.wait()
        wgb_ref[...] = wg_st[...].astype(BF16)
        wlb_ref[...] = wl_st[...].astype(BF16)
        wob_ref[...] = wo_st[...].astype(BF16)
        nxt = next_e_ref[i]

        @pl.when(nxt >= 0)
        def _():
            for cp in weight_copies(nxt):
                cp.start()

    half = wgb_ref.shape[0] // 2
    for first in range(0, xs_ref.shape[0], EXPERT_SUB_ROWS):
        rs = slice(first, first + EXPERT_SUB_ROWS)

        @pl.when(n_valid > first)
        def _():
            live = first + lax.broadcasted_iota(I32, (EXPERT_SUB_ROWS, 1), 0) < n_valid
            lo, hi = _unpack_halves(jnp.where(live, xs_ref[rs, :], 0))
            lo = lo.astype(BF16)
            hi = hi.astype(BF16)
            a = _dot(lo, wgb_ref[:half, :]) + _dot(hi, wgb_ref[half:, :]) + bg_ref[0]
            l = _dot(lo, wlb_ref[:half, :]) + _dot(hi, wlb_ref[half:, :]) + bl_ref[0]
            a = jnp.minimum(a, SWIGLU_LIMIT)
            l = jnp.clip(l, -SWIGLU_LIMIT, SWIGLU_LIMIT)
            act = a * jax.nn.sigmoid(SWIGLU_ALPHA * a) * (l + 1.0)
            y_ref[rs, :] = _pack_halves(_dot(act.astype(BF16), wob_ref[...]) + bo_ref[0])

        @pl.when(n_valid <= first)
        def _():
            y_ref[rs, :] = jnp.zeros((EXPERT_SUB_ROWS, y_ref.shape[1]), y_ref.dtype)


def _experts(xs, blk_e, n_valid, next_e, layer, w_glu, b_glu, w_lin, b_lin, w_out, b_out, *,
             blk):
    P, pw = xs.shape
    L, E, D, F = w_glu.shape
    n_blocks = P // blk
    bmap = lambda i, be, nv, ne: (layer * E + be[i], 0, 0)
    rows = lambda i, be, nv, ne: (i, 0)
    w_glu, w_lin = w_glu.reshape(L * E, D, F), w_lin.reshape(L * E, D, F)
    w_out = w_out.reshape(L * E, F, D)
    b_glu, b_lin = b_glu.reshape(L * E, 1, F), b_lin.reshape(L * E, 1, F)
    b_out = b_out.reshape(L * E, 1, D)
    hbm = pl.BlockSpec(memory_space=pl.ANY)
    grid_spec = pltpu.PrefetchScalarGridSpec(
        num_scalar_prefetch=3,
        grid=(n_blocks,),
        in_specs=[pl.BlockSpec((blk, pw), rows),
                  hbm, pl.BlockSpec((1, 1, F), bmap),
                  hbm, pl.BlockSpec((1, 1, F), bmap),
                  hbm, pl.BlockSpec((1, 1, D), bmap)],
        out_specs=pl.BlockSpec((blk, pw), rows),
        scratch_shapes=[pltpu.VMEM((D, F), F32), pltpu.VMEM((D, F), F32),
                        pltpu.VMEM((F, D), F32),
                        pltpu.VMEM((D, F), BF16), pltpu.VMEM((D, F), BF16),
                        pltpu.VMEM((F, D), BF16),
                        pltpu.SemaphoreType.DMA((3,))],
    )
    return pl.pallas_call(
        functools.partial(_expert_kernel, expert_base=layer * E),
        grid_spec=grid_spec,
        out_shape=jax.ShapeDtypeStruct((P, pw), xs.dtype),
        compiler_params=_cparams(("arbitrary",)),
        name="moe_experts",
    )(blk_e, n_valid, next_e, xs, w_glu, b_glu, w_lin, b_lin, w_out, b_out)


def _combine_kernel(ys_ref, tg_ref, x_ref, gate_ref, *rest, final):
    o_ref = rest[-1]
    half = x_ref.shape[1] // 2
    tg = tg_ref[...]
    lo_sum = hi_sum = None
    for k in range(ys_ref.shape[0]):
        lo, hi = _unpack_halves(ys_ref[k])
        gk = tg[:, k:k + 1]
        lo_sum = gk * lo if lo_sum is None else lo_sum + gk * lo
        hi_sum = gk * hi if hi_sum is None else hi_sum + gk * hi
    x_lo = x_ref[:, :half] + gate_ref[0, :, :half] * lo_sum
    x_hi = x_ref[:, half:] + gate_ref[0, :, half:] * hi_sum
    if final:
        g_ref = rest[0]
        ms = (jnp.sum(x_lo * x_lo, axis=-1, keepdims=True)
              + jnp.sum(x_hi * x_hi, axis=-1, keepdims=True)) / x_ref.shape[1]
        inv = lax.rsqrt(ms + RMS_EPS)
        x_lo = x_lo * inv * g_ref[:, :half]
        x_hi = x_hi * inv * g_ref[:, half:]
    o_ref[:, :half] = x_lo
    o_ref[:, half:] = x_hi


def _combine(ys, tok_gate, x, gate, seq, final_g=None, *, tm=256):
    K, T, pw = ys.shape
    D = x.shape[1]
    in_specs = [pl.BlockSpec((K, tm, pw), lambda i: (0, i, 0)),
                pl.BlockSpec((tm, K), lambda i: (i, 0)),
                pl.BlockSpec((tm, D), lambda i: (i, 0)),
                pl.BlockSpec((1, 1, D), lambda i: ((i * tm) // seq, 0, 0))]
    args = [ys, tok_gate, x, gate]
    if final_g is not None:
        in_specs.append(pl.BlockSpec((1, D), lambda i: (0, 0)))
        args.append(final_g)
    return pl.pallas_call(
        functools.partial(_combine_kernel, final=final_g is not None),
        grid=(T // tm,),
        in_specs=in_specs,
        out_specs=pl.BlockSpec((tm, D), lambda i: (i, 0)),
        out_shape=jax.ShapeDtypeStruct((T, D), F32),
        compiler_params=_cparams(("parallel",)),
        name="moe_combine",
    )(*args)


MOE_BLOCK = 512


def _slots_kernel(start_ref, idx_ref, rank_ref, o_ref):
    idx = idx_ref[...]
    acc = rank_ref[...]
    for e in range(start_ref.shape[0]):
        acc = acc + jnp.where(idx == e, start_ref[e], 0)
    o_ref[...] = acc


def _slots(idx_t, rank_t, group_start):
    full = pl.BlockSpec(idx_t.shape, lambda i, gs: (0, 0))
    return pl.pallas_call(
        _slots_kernel,
        grid_spec=pltpu.PrefetchScalarGridSpec(num_scalar_prefetch=1, grid=(1,),
                                               in_specs=[full, full], out_specs=full),
        out_shape=jax.ShapeDtypeStruct(idx_t.shape, I32),
        compiler_params=_cparams(("arbitrary",)),
        name="moe_slots",
    )(group_start, idx_t, rank_t)


def _dispatch_plan(idx_t, rank_t, counts, blk):
    K, T = idx_t.shape
    E = counts.shape[0]
    padded = (counts + blk - 1) // blk * blk
    pad_end = jnp.cumsum(padded)
    pad_start = pad_end - padded
    experts = jnp.arange(E, dtype=I32)
    slots = _slots(idx_t, rank_t, pad_start.astype(I32))
    n_blocks = -(-(K * T + E * (blk - 1)) // blk)
    blk_start = jnp.arange(n_blocks, dtype=I32) * blk
    blk_e = jnp.minimum(jnp.sum(pad_end[None, :] <= blk_start[:, None], axis=1), E - 1).astype(I32)
    n_valid = jnp.clip(pad_start[blk_e] + counts[blk_e] - blk_start, 0, blk).astype(I32)
    later = jnp.where(counts > 0, experts, E)
    after = lax.cummin(jnp.concatenate([later[1:], jnp.full((1,), E, I32)]), reverse=True)
    next_e = jnp.where(after < E, after, -1).astype(I32)[blk_e]
    return slots, blk_e, n_valid, next_e, n_blocks * blk


def _mixer_out_moe(y, w_mix, x, mix_gate, g, scale, shift, gate, r_w, r_b, layer,
                   w_glu, b_glu, w_lin, b_lin, w_out, b_out, seq, final_g=None):
    T, D = x.shape
    x, hp, idx_t, gate_t, rank_t, counts = _mixer_out_route(
        y, w_mix, x, mix_gate, g, scale, shift, r_w, r_b, seq)
    slots, blk_e, n_valid, next_e, n_rows = _dispatch_plan(idx_t, rank_t, counts[:, 0],
                                                           MOE_BLOCK)
    xs = _sc_scatter_rows(hp, slots, n_rows)
    yp = _experts(xs, blk_e, n_valid, next_e, layer, w_glu, b_glu, w_lin, b_lin, w_out, b_out,
                  blk=MOE_BLOCK)
    ys = _sc_gather(yp, slots.reshape(TOP_K * T)).reshape(TOP_K, T, yp.shape[1])
    return _combine(ys, jnp.transpose(gate_t), x, gate, seq, final_g)


def kernel(x, c, norm_g, ada_w, final_norm_g, sg_w_in, sg_ln_g, sg_ln_b, sg_w_s, sg_b_s, sg_w_out, pool_w_in, pool_w_grp, pool_ls, pool_w_out, hgrn_w_in, hgrn_lb_logits, hgrn_gnorm_g, hgrn_w_out, router_w, router_b, expert_w_glu, expert_b_glu, expert_w_lin, expert_b_lin, expert_w_out, expert_b_out):
    B, S, D = x.shape
    depth = norm_g.shape[0]
    T = B * S
    xf = x.reshape(T, D)

    ada_w = ada_w.reshape(depth * 2, D, 3 * D)

    def mod_parts(layer, sub):
        m = _ada_mod(c, ada_w, layer * 2 + sub)[:, None, :]
        return m[..., :D], m[..., D:2 * D], m[..., 2 * D:]

    lb_cum = jnp.cumsum(jax.nn.softmax(hgrn_lb_logits.astype(F32), axis=0), axis=0)
    lower_bounds = lb_cum - lb_cum[0]

    for layer in range(depth):
        kind, slot = layer % 3, layer // 3
        shift, scale, mix_gate = mod_parts(layer, 0)
        g = norm_g[layer, 0][None, :]
        if kind == 0:
            z = _norm_matmul(xf, g, scale, shift, sg_w_in[slot].astype(BF16), S,
                             act="gelu", out_dtype=BF16)
            y = _spatial_gate(z, sg_ln_g[slot][None, :], sg_ln_b[slot][None, :],
                              sg_w_s[slot], sg_b_s[slot])
            w_out = sg_w_out[slot]
        elif kind == 1:
            z = _norm_matmul(xf, g, scale, shift, pool_w_in[slot].astype(BF16), S,
                             act=None, out_dtype=F32)
            y = _pool_mix(z, pool_w_grp[slot].astype(BF16), pool_ls[slot][None, :], S)
            w_out = pool_w_out[slot]
        else:
            proj, f_pre = _norm_matmul(xf, g, scale, shift, hgrn_w_in[slot].astype(BF16), S,
                                       act=None, out_dtype=BF16, f32_col=1, tn=D)
            y = _hgrn_mix(proj, f_pre, lower_bounds[layer][None, :],
                          hgrn_gnorm_g[slot][None, :], B, S)
            w_out = hgrn_w_out[slot]

        shift, scale, gate = mod_parts(layer, 1)
        final_g = final_norm_g[None, :] if layer == depth - 1 else None
        xf = _mixer_out_moe(y, w_out.astype(BF16), xf, mix_gate, norm_g[layer, 1][None, :],
                            scale, shift, gate, router_w[layer], router_b[layer], layer,
                            expert_w_glu, expert_b_glu, expert_w_lin, expert_b_lin,
                            expert_w_out, expert_b_out, S, final_g)

    return xf.reshape(B, S, D)
```

```python
import functools

import numpy as np
import jax
import jax.numpy as jnp
from jax import lax
from jax.experimental import pallas as pl
from jax.experimental.pallas import tpu as pltpu
from jax.experimental.pallas import tpu_sc as plsc

F32 = jnp.float32
BF16 = jnp.bfloat16
I32 = jnp.int32
U32 = jnp.uint32

RMS_EPS = 1e-5
LN_EPS = 1e-5
SWIGLU_ALPHA = 1.702
SWIGLU_LIMIT = 7.0
TOP_K = 4
SG_CHUNK = 128
POOL_WINDOWS = (2, 4, 8, 16)
HEAD_DIM = 128
LANES = 128
SC_WORKERS = 32
SC_ROWS_PER_STEP = 64

VMEM_LIMIT = 48 * 1024 * 1024
BIG_VMEM_LIMIT = 56 * 1024 * 1024


def _cparams(sem):
    return pltpu.CompilerParams(dimension_semantics=sem, vmem_limit_bytes=VMEM_LIMIT)


def _dot(a, b):
    return jnp.dot(a, b, preferred_element_type=F32)


def _dot_nt(a, b):
    return lax.dot_general(a, b, (((1,), (1,)), ((), ())), preferred_element_type=F32)


def _dot_tn(a, b):
    return lax.dot_general(a, b, (((0,), (0,)), ((), ())), preferred_element_type=F32)


def _split_bf16(x):
    hi = x.astype(BF16)
    lo = (x - hi.astype(F32)).astype(BF16)
    return hi, lo


def _packed_layout(d):
    return d // 2, I32


def _pack_halves(y):
    n = y.shape[1] // 2
    lo = lax.bitcast_convert_type(y[:, :n].astype(BF16).astype(F32), U32)
    hi = lax.bitcast_convert_type(y[:, n:].astype(BF16).astype(F32), U32)
    word = lax.shift_right_logical(lo, jnp.uint32(16)) | (hi & jnp.uint32(0xFFFF0000))
    return lax.bitcast_convert_type(word, I32)


def _unpack_halves(w):
    u = lax.bitcast_convert_type(w, U32)
    lo = lax.bitcast_convert_type(lax.shift_left(u, jnp.uint32(16)), F32)
    hi = lax.bitcast_convert_type(u & jnp.uint32(0xFFFF0000), F32)
    return lo, hi


def _modulated_norm(x, g, scale, shift):
    ms = jnp.mean(x * x, axis=-1, keepdims=True)
    y = x * lax.rsqrt(ms + RMS_EPS) * g
    return y * (1.0 + scale) + shift


def _ada_kernel(c_ref, w_ref, o_ref):
    w = w_ref[0]
    reps = w.shape[1] // LANES
    for b in range(c_ref.shape[0]):
        cb = c_ref[b]
        s = jnp.tile(cb * jax.nn.sigmoid(cb), (1, reps))
        o_ref[b] = jnp.sum(w * s, axis=0, keepdims=True)


def _ada_mod(c, ada_w, index):
    L, D, N = ada_w.shape
    B = c.shape[0]
    tn = 512
    c_cols = jnp.broadcast_to(c[:, :, None], (B, D, LANES))
    out = pl.pallas_call(
        _ada_kernel,
        grid=(N // tn,),
        in_specs=[pl.BlockSpec((B, D, LANES), lambda j: (0, 0, 0)),
                  pl.BlockSpec((1, D, tn), lambda j: (index, 0, j))],
        out_specs=pl.BlockSpec((B, 1, tn), lambda j: (0, 0, j)),
        out_shape=jax.ShapeDtypeStruct((B, 1, N), F32),
        compiler_params=_cparams(("parallel",)),
        name="ada_mod",
    )(c_cols, ada_w)
    return out[:, 0]


def _nm_kernel(*refs, act, f32_col, prenormed):
    if prenormed:
        h_ref, w_ref, o_ref = refs[:3]
        outs = refs[3:]
    else:
        x_ref, g_ref, sc_ref, sh_ref, w_ref, o_ref = refs[:6]
        outs, h_ref = refs[6:-1], refs[-1]

        @pl.when(pl.program_id(1) == 0)
        def _():
            h = _modulated_norm(x_ref[...], g_ref[...], sc_ref[0], sh_ref[0])
            h_ref[...] = h.astype(BF16)

    acc = _dot(h_ref[...], w_ref[...])
    if act == "gelu":
        acc = 0.5 * acc * (1.0 + lax.erf(acc * np.float32(1.0 / np.sqrt(2.0))))
    o_ref[...] = acc.astype(o_ref.dtype)
    if f32_col is not None:
        @pl.when(pl.program_id(1) == f32_col)
        def _():
            outs[0][...] = acc


def _norm_matmul(x, norm, w, seq, *, act, out_dtype, f32_col=None, tm=512, tn=2048):
    T, D = x.shape
    N = w.shape[1]
    prenormed = norm is None
    bmap = lambda i, j: ((i * tm) // seq, 0, 0)
    out_specs = [pl.BlockSpec((tm, tn), lambda i, j: (i, j))]
    out_shape = [jax.ShapeDtypeStruct((T, N), out_dtype)]
    if f32_col is not None:
        out_specs.append(pl.BlockSpec((tm, tn), lambda i, j: (i, 0)))
        out_shape.append(jax.ShapeDtypeStruct((T, tn), F32))
    in_specs = [pl.BlockSpec((tm, D), lambda i, j: (i, 0))]
    args = [x]
    if not prenormed:
        in_specs += [pl.BlockSpec((1, D), lambda i, j: (0, 0)),
                     pl.BlockSpec((1, 1, D), bmap),
                     pl.BlockSpec((1, 1, D), bmap)]
        args += list(norm)
    in_specs.append(pl.BlockSpec((D, tn), lambda i, j: (0, j)))
    args.append(w)
    outs = pl.pallas_call(
        functools.partial(_nm_kernel, act=act, f32_col=f32_col, prenormed=prenormed),
        grid=(T // tm, N // tn),
        in_specs=in_specs,
        out_specs=out_specs,
        out_shape=out_shape,
        scratch_shapes=[] if prenormed else [pltpu.VMEM((tm, D), BF16)],
        compiler_params=pltpu.CompilerParams(dimension_semantics=("parallel", "arbitrary"),
                                             vmem_limit_bytes=BIG_VMEM_LIMIT),
        name="norm_matmul_" + str(act),
    )(*args)
    return outs if f32_col is not None else outs[0]


def _sg_kernel(u_ref, v_ref, lng_ref, lnb_ref, ws_ref, bst_ref, y_ref, vn_ref, *, heads):
    v = v_ref[...].astype(F32)
    mu = jnp.mean(v, axis=-1, keepdims=True)
    vc = v - mu
    var = jnp.mean(vc * vc, axis=-1, keepdims=True)
    vn_ref[...] = (vc * lax.rsqrt(var + LN_EPS) * lng_ref[...] + lnb_ref[...]).astype(BF16)

    n_chunks = v_ref.shape[0] // SG_CHUNK
    row = lax.broadcasted_iota(I32, (SG_CHUNK, SG_CHUNK), 0)
    col = lax.broadcasted_iota(I32, (SG_CHUNK, SG_CHUNK), 1)
    causal = row >= col
    for hd in range(heads):
        cs = slice(hd * HEAD_DIM, (hd + 1) * HEAD_DIM)
        wm = jnp.where(causal, ws_ref[hd], 0.0).astype(BF16)
        bias = bst_ref[:, hd:hd + 1]
        for ci in range(n_chunks):
            rs = slice(ci * SG_CHUNK, (ci + 1) * SG_CHUNK)
            mixed = _dot(wm, vn_ref[rs, cs]) + bias
            y_ref[rs, cs] = (u_ref[rs, cs].astype(F32) * mixed).astype(BF16)


def _spatial_gate(z, ln_g, ln_b, w_s, b_s, *, tm=256):
    T, two_w = z.shape
    W = two_w // 2
    heads = w_s.shape[0]
    return pl.pallas_call(
        functools.partial(_sg_kernel, heads=heads),
        grid=(T // tm,),
        in_specs=[pl.BlockSpec((tm, W), lambda i: (i, 0)),
                  pl.BlockSpec((tm, W), lambda i: (i, 1)),
                  pl.BlockSpec((1, W), lambda i: (0, 0)),
                  pl.BlockSpec((1, W), lambda i: (0, 0)),
                  pl.BlockSpec((heads, SG_CHUNK, SG_CHUNK), lambda i: (0, 0, 0)),
                  pl.BlockSpec((SG_CHUNK, heads), lambda i: (0, 0))],
        out_specs=pl.BlockSpec((tm, W), lambda i: (i, 0)),
        out_shape=jax.ShapeDtypeStruct((T, W), BF16),
        scratch_shapes=[pltpu.VMEM((tm, W), BF16)],
        compiler_params=_cparams(("parallel",)),
        name="spatial_gate",
    )(z, z, ln_g, ln_b, w_s, jnp.transpose(b_s))


POOL_HALO = 16


def _pool_kernel(z_ref, halo_ref, wg_ref, ls_ref, y_ref, *, seq):
    tm = z_ref.shape[0]
    gdim = wg_ref.shape[1]
    pos0 = (pl.program_id(0) * tm) % seq
    halo_on = (pos0 > 0).astype(F32)
    pos = pos0 + lax.broadcasted_iota(I32, (tm, 1), 0)
    for gi, wnd in enumerate(POOL_WINDOWS):
        cs = slice(gi * gdim, (gi + 1) * gdim)
        zg = z_ref[:, cs]
        s = jnp.concatenate([halo_ref[:, cs] * halo_on, zg], axis=0)
        k = 1
        while k < wnd:
            s = s + pltpu.roll(s, k, 0)
            k *= 2
        cnt = jnp.minimum(pos + 1, wnd).astype(F32)
        pooled = s[POOL_HALO:, :] / cnt - zg
        yg = _dot(pooled.astype(BF16), wg_ref[gi]) * ls_ref[:, cs]
        y_ref[:, cs] = yg.astype(BF16)


def _pool_mix(z, w_grp, ls, seq, *, tm=256):
    T, D = z.shape
    G, gdim, _ = w_grp.shape
    assert max(POOL_WINDOWS) <= POOL_HALO and tm % POOL_HALO == 0
    ratio = tm // POOL_HALO
    return pl.pallas_call(
        functools.partial(_pool_kernel, seq=seq),
        grid=(T // tm,),
        in_specs=[pl.BlockSpec((tm, D), lambda i: (i, 0)),
                  pl.BlockSpec((POOL_HALO, D), lambda i: (jnp.maximum(i * ratio - 1, 0), 0)),
                  pl.BlockSpec((G, gdim, gdim), lambda i: (0, 0, 0)),
                  pl.BlockSpec((1, D), lambda i: (0, 0))],
        out_specs=pl.BlockSpec((tm, D), lambda i: (i, 0)),
        out_shape=jax.ShapeDtypeStruct((T, D), BF16),
        compiler_params=_cparams(("parallel",)),
        name="pool_mix",
    )(z, z, w_grp, ls)


HGRN_CHUNK = 128
HGRN_BASE = 32
HGRN_GUARD_LOG2 = 100.0
LOG2_E = 1.4426950408889634


def _hgrn_tables(c=HGRN_CHUNK, base=HGRN_BASE):
    t = np.arange(c)
    j = t[None, :]
    sums = [j <= t[:, None]]
    masks = []
    n_shallow = 0
    h = c // 2
    while h >= 1:
        off = t % (2 * h)
        mid = (t // (2 * h)) * (2 * h) + h
        second = off >= h
        if h >= 2:
            m_q = second[:, None] & (j >= mid[:, None]) & (j <= t[:, None])
            m_k = (~second)[:, None] & (j > t[:, None]) & (j < mid[:, None])
            sums.append(m_q | m_k)
        same = (t[:, None] // (2 * h)) == (t[None, :] // (2 * h))
        masks.append(same & second[:, None] & (~second)[None, :])
        n_shallow += h >= base
        h //= 2
    same_base = (t[:, None] // base) == (t[None, :] // base)
    sums.append(same_base & (j <= t[:, None]))
    sums = np.stack(sums).astype(np.float32)
    base_mask = (same_base & (j <= t[:, None])).astype(np.float32)
    return (np.concatenate([sums, sums], axis=2), np.stack(masks).astype(np.float32), base_mask,
            n_shallow)


def _hgrn_kernel(q_ref, f_ref, i_ref, g_ref, lb_ref, gn_ref, sums_ref, masks_ref, bmask_ref,
                 o_ref, st_ref, qb_ref, ke_ref, z_ref, qd_ref, kd_ref, vb_ref, eb_ref,
                 *, heads, n_shallow):
    c = q_ref.shape[0]
    n_sum = sums_ref.shape[0] - 1
    n_lvl = masks_ref.shape[0]

    @pl.when(pl.program_id(1) == 0)
    def _():
        st_ref[...] = jnp.zeros_like(st_ref)

    lb = lb_ref[...]
    sig = jax.nn.sigmoid(f_ref[...])
    fg = lb + (1.0 - lb) * sig
    lf2 = jnp.log(fg) * np.float32(LOG2_E)
    kk = (1.0 - lb) * (1.0 - sig)
    lf_hi, lf_lo = _split_bf16(lf2)
    lf_cat = jnp.concatenate([lf_hi, lf_lo], axis=0)
    qr = q_ref[...].astype(F32)
    q = qr * jax.nn.sigmoid(qr)
    vb_ref[...] = i_ref[...].astype(BF16)
    q16 = q.astype(BF16)
    k16 = kk.astype(BF16)

    b2 = _dot(sums_ref[0], lf_cat)
    eb = jnp.exp2(b2)
    qb_ref[...] = (q * eb).astype(BF16)
    eb_ref[...] = eb[c - 1:c, :]
    ke_ref[...] = (kk * jnp.exp2(b2[c - 1:c, :] - b2)).astype(BF16)
    w2 = _dot(sums_ref[n_sum], lf_cat)
    shallow = jnp.min(w2) >= -HGRN_GUARD_LOG2

    row = lax.broadcasted_iota(I32, (c, 1), 0)

    def level(l):
        half = c >> (l + 1)
        second = (row & (2 * half - 1)) >= half
        if l + 1 < n_sum:
            x = jnp.exp2(_dot(sums_ref[l + 1], lf_cat).astype(BF16))
        else:
            x = jnp.where(second, fg, 1.0).astype(BF16)
        z_ref[l] = jnp.where(second, q16, k16) * x

    gn = gn_ref[...]

    def finish(hd, cs, o):
        st = st_ref[hd]
        vh = vb_ref[:, cs]
        o = o + _dot_nt(qb_ref[:, cs], st.astype(BF16))
        st_ref[hd] = st * eb_ref[:, cs] + _dot_tn(vh, ke_ref[:, cs])
        on = o * lax.rsqrt(jnp.mean(o * o, axis=-1, keepdims=True) + RMS_EPS) * gn
        gv = g_ref[:, cs].astype(F32)
        o_ref[:, cs] = (on * (gv * jax.nn.sigmoid(gv))).astype(BF16)

    def level_scores(cs, n):
        z = z_ref[0, :, cs]
        scores = masks_ref[0] * _dot_nt(z, z).astype(BF16)
        for l in range(1, n):
            z = z_ref[l, :, cs]
            scores = scores + masks_ref[l] * _dot_nt(z, z).astype(BF16)
        return scores

    @pl.when(shallow)
    def _():
        for l in range(n_shallow):
            level(l)
        qd_ref[...] = (q * jnp.exp2(w2)).astype(BF16)
        kd_ref[...] = (kk * jnp.exp2(-w2)).astype(BF16)

        def head(hd, carry):
            cs = pl.ds(pl.multiple_of(hd * HEAD_DIM, HEAD_DIM), HEAD_DIM)
            scores = (level_scores(cs, n_shallow)
                      + bmask_ref[...] * _dot_nt(qd_ref[:, cs], kd_ref[:, cs]).astype(BF16))
            finish(hd, cs, _dot(scores, vb_ref[:, cs]))
            return carry

        lax.fori_loop(0, heads, head, 0, unroll=4)

    @pl.when(jnp.logical_not(shallow))
    def _():
        for l in range(n_lvl):
            level(l)
        qd_ref[...] = q16
        kd_ref[...] = k16

        def head(hd, carry):
            cs = pl.ds(pl.multiple_of(hd * HEAD_DIM, HEAD_DIM), HEAD_DIM)
            vh = vb_ref[:, cs]
            diag = jnp.sum(qd_ref[:, cs].astype(F32) * kd_ref[:, cs].astype(F32), axis=-1,
                           keepdims=True)
            finish(hd, cs, _dot(level_scores(cs, n_lvl), vh) + diag * vh.astype(F32))
            return carry

        lax.fori_loop(0, heads, head, 0, unroll=4)


def _hgrn_mix(proj, f_pre, lb, gnorm_g, batch, seq):
    T, four_d = proj.shape
    D = four_d // 4
    heads = D // HEAD_DIM
    c = HGRN_CHUNK
    n_chunks = seq // c
    sums, masks, base_mask, n_shallow = _hgrn_tables(c)
    n_sum, n_mask = sums.shape[0], masks.shape[0]
    row = lambda b, i: b * n_chunks + i
    return pl.pallas_call(
        functools.partial(_hgrn_kernel, heads=heads, n_shallow=n_shallow),
        grid=(batch, n_chunks),
        in_specs=[pl.BlockSpec((c, D), lambda b, i: (row(b, i), 0)),
                  pl.BlockSpec((c, D), lambda b, i: (row(b, i), 0)),
                  pl.BlockSpec((c, D), lambda b, i: (row(b, i), 2)),
                  pl.BlockSpec((c, D), lambda b, i: (row(b, i), 3)),
                  pl.BlockSpec((1, D), lambda b, i: (0, 0)),
                  pl.BlockSpec((1, HEAD_DIM), lambda b, i: (0, 0)),
                  pl.BlockSpec((n_sum, c, 2 * c), lambda b, i: (0, 0, 0)),
                  pl.BlockSpec((n_mask, c, c), lambda b, i: (0, 0, 0)),
                  pl.BlockSpec((c, c), lambda b, i: (0, 0))],
        out_specs=pl.BlockSpec((c, D), lambda b, i: (row(b, i), 0)),
        out_shape=jax.ShapeDtypeStruct((T, D), BF16),
        scratch_shapes=[pltpu.VMEM((heads, HEAD_DIM, HEAD_DIM), F32),
                        pltpu.VMEM((c, D), BF16),
                        pltpu.VMEM((c, D), BF16),
                        pltpu.VMEM((n_mask, c, D), BF16),
                        pltpu.VMEM((c, D), BF16),
                        pltpu.VMEM((c, D), BF16),
                        pltpu.VMEM((c, D), BF16),
                        pltpu.VMEM((1, D), F32)],
        compiler_params=_cparams(("parallel", "arbitrary")),
        name="hgrn_mix",
    )(proj, f_pre, proj, proj, lb, gnorm_g, jnp.asarray(sums, BF16), jnp.asarray(masks, BF16),
      jnp.asarray(base_mask, BF16))


def _route_kernel(y_ref, w_ref, xin_ref, mgate_ref, g_ref, sc_ref, sh_ref, rw_ref,
                  rb_ref, tri_ref, x_ref, hp_ref, idx_ref, gate_ref, rank_ref, cnt_ref, carry_ref,
                  *, n_experts):
    @pl.when(pl.program_id(0) == 0)
    def _():
        carry_ref[...] = jnp.zeros_like(carry_ref)

    x = xin_ref[...] + mgate_ref[0] * _dot(y_ref[...], w_ref[...])
    x_ref[...] = x
    h = _modulated_norm(x, g_ref[...], sc_ref[0], sh_ref[0])
    hp_ref[...] = _pack_halves(h)

    h_hi, h_lo = _split_bf16(h)
    both = _dot(h_hi, rw_ref[...])
    logits = (both[:, :LANES] + both[:, LANES:] + _dot(h_lo, rw_ref[:, :LANES])) + rb_ref[...]
    lt = jnp.transpose(logits)[:n_experts, :]
    eidx = lax.broadcasted_iota(I32, lt.shape, 0)
    vals, idxs, hits = [], [], []
    for _ in range(TOP_K):
        m = jnp.max(lt, axis=0, keepdims=True)
        sel = jnp.min(jnp.where(lt == m, eidx, n_experts), axis=0, keepdims=True)
        hit = eidx == sel
        vals.append(m)
        idxs.append(sel)
        hits.append(hit)
        lt = jnp.where(hit, -jnp.inf, lt)
    exps = [jnp.exp(v - vals[0]) for v in vals]
    denom = exps[0]
    for e in exps[1:]:
        denom = denom + e
    idx_ref[...] = jnp.concatenate(idxs, axis=0)
    gate_ref[...] = jnp.concatenate([e / denom for e in exps], axis=0)

    onehot = hits[0].astype(F32)
    for hit in hits[1:]:
        onehot = onehot + hit.astype(F32)
    before = carry_ref[...] + _dot(onehot.astype(BF16), tri_ref[...])
    rank_ref[...] = jnp.concatenate(
        [jnp.sum(jnp.where(hit, before, 0.0), axis=0, keepdims=True) for hit in hits],
        axis=0).astype(I32)
    carry_ref[...] += jnp.sum(onehot, axis=1, keepdims=True)
    cnt_ref[...] = carry_ref[...].astype(I32)


def _mixer_out_route(y, w, x, mix_gate, g, scale, shift, r_w, r_b, seq, *, tm=256):
    T, K = y.shape
    D = w.shape[1]
    E = r_w.shape[1]
    rw_pad = jnp.zeros((D, LANES), F32).at[:, :E].set(r_w)
    rw_cat = jnp.concatenate(_split_bf16(rw_pad), axis=1)
    rb_pad = jnp.full((1, LANES), -1e30, F32).at[0, :E].set(r_b)
    tri = jnp.asarray(np.triu(np.ones((tm, tm), np.float32), k=1), BF16)
    pw, pdt = _packed_layout(D)
    bmap = lambda i: ((i * tm) // seq, 0, 0)
    const = lambda i: (0, 0)
    rows = lambda i: (i, 0)
    kt = pl.BlockSpec((TOP_K, tm), lambda i: (0, i))
    return pl.pallas_call(
        functools.partial(_route_kernel, n_experts=E),
        grid=(T // tm,),
        in_specs=[pl.BlockSpec((tm, K), rows),
                  pl.BlockSpec((K, D), const),
                  pl.BlockSpec((tm, D), rows),
                  pl.BlockSpec((1, 1, D), bmap),
                  pl.BlockSpec((1, D), const),
                  pl.BlockSpec((1, 1, D), bmap),
                  pl.BlockSpec((1, 1, D), bmap),
                  pl.BlockSpec((D, 2 * LANES), const),
                  pl.BlockSpec((1, LANES), const),
                  pl.BlockSpec((tm, tm), const)],
        out_specs=[pl.BlockSpec((tm, D), rows), pl.BlockSpec((tm, pw), rows), kt, kt, kt,
                   pl.BlockSpec((E, 1), const)],
        out_shape=[jax.ShapeDtypeStruct((T, D), F32),
                   jax.ShapeDtypeStruct((T, pw), pdt),
                   jax.ShapeDtypeStruct((TOP_K, T), I32),
                   jax.ShapeDtypeStruct((TOP_K, T), F32),
                   jax.ShapeDtypeStruct((TOP_K, T), I32),
                   jax.ShapeDtypeStruct((E, 1), I32)],
        scratch_shapes=[pltpu.VMEM((E, 1), F32)],
        compiler_params=_cparams(("arbitrary",)),
        name="mixer_out_route",
    )(y, w, x, mix_gate, g, scale, shift, rw_cat, rb_pad, tri)


def _sc_gather(table, idx):
    R = idx.shape[0]
    W = table.shape[1]
    rows = SC_ROWS_PER_STEP // 2
    per_worker = R // SC_WORKERS
    steps = per_worker // rows
    assert steps * rows * SC_WORKERS == R and steps % 2 == 0
    mesh = plsc.VectorSubcoreMesh(core_axis_name="c", subcore_axis_name="s")
    n_cores = mesh.num_cores

    @functools.partial(
        pl.kernel, mesh=mesh,
        out_type=jax.ShapeDtypeStruct((R, W), table.dtype),
        scratch_types=[pltpu.VMEM((steps, rows), I32),
                       pltpu.VMEM((rows, W), table.dtype),
                       pltpu.VMEM((rows, W), table.dtype),
                       pltpu.SemaphoreType.DMA((2,)),
                       pltpu.SemaphoreType.DMA((2,))],
    )
    def gather(table_hbm, idx_hbm, out_hbm, idx_v, buf0, buf1, gsem, wsem):
        wid = lax.axis_index("s") * n_cores + lax.axis_index("c")
        base = wid * per_worker
        bufs = (buf0, buf1)
        pltpu.sync_copy(idx_hbm.at[pl.ds(wid * steps, steps)], idx_v)

        def fetch(j, b):
            return pltpu.make_async_copy(table_hbm.at[idx_v.at[j]], bufs[b], gsem.at[b])

        def flush(j, b):
            off = pl.multiple_of(base + j * rows, 8)
            return pltpu.make_async_copy(bufs[b], out_hbm.at[pl.ds(off, rows)], wsem.at[b])

        fetch(0, 0).start()

        @pl.loop(0, steps, step=2)
        def _(j0):
            for b in range(2):
                j = j0 + b
                fetch(j, b).wait()
                flush(j, b).start()

                @pl.when(j + 1 < steps)
                def _():
                    @pl.when(j >= 1)
                    def _():
                        flush(j - 1, 1 - b).wait()
                    fetch(j + 1, 1 - b).start()

        flush(steps - 2, 0).wait()
        flush(steps - 1, 1).wait()

    return gather(table, idx.reshape(R // rows, rows))


def _sc_scatter_rows(table, slots, n_out):
    K, T = slots.shape
    W = table.shape[1]
    rows = SC_ROWS_PER_STEP
    per_worker = T // SC_WORKERS
    steps = per_worker // rows
    assert steps * rows * SC_WORKERS == T
    mesh = plsc.VectorSubcoreMesh(core_axis_name="c", subcore_axis_name="s")
    n_cores = mesh.num_cores

    @functools.partial(
        pl.kernel, mesh=mesh,
        out_type=jax.ShapeDtypeStruct((n_out, W), table.dtype),
        scratch_types=[pltpu.VMEM((K, rows), I32),
                       pltpu.VMEM((rows, W), table.dtype)],
    )
    def scatter(table_hbm, slots_hbm, out_hbm, idx_v, rows_v):
        wid = lax.axis_index("s") * n_cores + lax.axis_index("c")
        base = wid * per_worker

        @pl.loop(0, steps)
        def _(j):
            off = pl.multiple_of(base + j * rows, 8)
            pltpu.sync_copy(table_hbm.at[pl.ds(off, rows)], rows_v)
            for k in range(K):
                pltpu.sync_copy(slots_hbm.at[pl.ds(pl.multiple_of(k * T + off, 8), rows)],
                                idx_v.at[k])
                pltpu.sync_copy(rows_v, out_hbm.at[idx_v.at[k]])

    return scatter(table, slots.reshape(K * T))


EXPERT_SUB_ROWS = 256


def _expert_kernel(blk_e_ref, n_valid_ref, next_e_ref, xs_ref, wg_hbm, bg_ref, wl_hbm, bl_ref,
                   wo_hbm, bo_ref, y_ref, wg_st, wl_st, wo_st, wgb_ref, wlb_ref, wob_ref, sem,
                   *, expert_base):
    i = pl.program_id(0)
    e = blk_e_ref[i]
    n_valid = n_valid_ref[i]
    new_expert = jnp.logical_or(i == 0, e != blk_e_ref[jnp.maximum(i - 1, 0)])

    def weight_copies(expert):
        idx = expert_base + expert
        return (pltpu.make_async_copy(wg_hbm.at[idx], wg_st, sem.at[0]),
                pltpu.make_async_copy(wl_hbm.at[idx], wl_st, sem.at[1]),
                pltpu.make_async_copy(wo_hbm.at[idx], wo_st, sem.at[2]))

    @pl.when(i == 0)
    def _():
        for cp in weight_copies(e):
            cp.start()

    @pl.when(jnp.logical_and(new_expert, n_valid > 0))
    def _():
        for cp in weight_copies(e):
            cp.wait()
        wgb_ref[...] = wg_st[...].astype(BF16)
        wlb_ref[...] = wl_st[...].astype(BF16)
        wob_ref[...] = wo_st[...].astype(BF16)
        nxt = next_e_ref[i]

        @pl.when(nxt >= 0)
        def _():
            for cp in weight_copies(nxt):
                cp.start()

    half = wgb_ref.shape[0] // 2
    for first in range(0, xs_ref.shape[0], EXPERT_SUB_ROWS):
        rs = slice(first, first + EXPERT_SUB_ROWS)

        @pl.when(n_valid > first)
        def _():
            live = first + lax.broadcasted_iota(I32, (EXPERT_SUB_ROWS, 1), 0) < n_valid
            lo, hi = _unpack_halves(jnp.where(live, xs_ref[rs, :], 0))
            lo = lo.astype(BF16)
            hi = hi.astype(BF16)
            a = _dot(lo, wgb_ref[:half, :]) + _dot(hi, wgb_ref[half:, :]) + bg_ref[0]
            l = _dot(lo, wlb_ref[:half, :]) + _dot(hi, wlb_ref[half:, :]) + bl_ref[0]
            a = jnp.minimum(a, SWIGLU_LIMIT)
            l = jnp.clip(l, -SWIGLU_LIMIT, SWIGLU_LIMIT)
            act = a * jax.nn.sigmoid(SWIGLU_ALPHA * a) * (l + 1.0)
            y_ref[rs, :] = _pack_halves(_dot(act.astype(BF16), wob_ref[...]) + bo_ref[0])

        @pl.when(n_valid <= first)
        def _():
            y_ref[rs, :] = jnp.zeros((EXPERT_SUB_ROWS, y_ref.shape[1]), y_ref.dtype)


def _experts(xs, blk_e, n_valid, next_e, layer, w_glu, b_glu, w_lin, b_lin, w_out, b_out, *,
             blk):
    P, pw = xs.shape
    L, E, D, F = w_glu.shape
    n_blocks = P // blk
    bmap = lambda i, be, nv, ne: (layer * E + be[i], 0, 0)
    rows = lambda i, be, nv, ne: (i, 0)
    w_glu, w_lin = w_glu.reshape(L * E, D, F), w_lin.reshape(L * E, D, F)
    w_out = w_out.reshape(L * E, F, D)
    b_glu, b_lin = b_glu.reshape(L * E, 1, F), b_lin.reshape(L * E, 1, F)
    b_out = b_out.reshape(L * E, 1, D)
    hbm = pl.BlockSpec(memory_space=pl.ANY)
    grid_spec = pltpu.PrefetchScalarGridSpec(
        num_scalar_prefetch=3,
        grid=(n_blocks,),
        in_specs=[pl.BlockSpec((blk, pw), rows),
                  hbm, pl.BlockSpec((1, 1, F), bmap),
                  hbm, pl.BlockSpec((1, 1, F), bmap),
                  hbm, pl.BlockSpec((1, 1, D), bmap)],
        out_specs=pl.BlockSpec((blk, pw), rows),
        scratch_shapes=[pltpu.VMEM((D, F), F32), pltpu.VMEM((D, F), F32),
                        pltpu.VMEM((F, D), F32),
                        pltpu.VMEM((D, F), BF16), pltpu.VMEM((D, F), BF16),
                        pltpu.VMEM((F, D), BF16),
                        pltpu.SemaphoreType.DMA((3,))],
    )
    return pl.pallas_call(
        functools.partial(_expert_kernel, expert_base=layer * E),
        grid_spec=grid_spec,
        out_shape=jax.ShapeDtypeStruct((P, pw), xs.dtype),
        compiler_params=_cparams(("arbitrary",)),
        name="moe_experts",
    )(blk_e, n_valid, next_e, xs, w_glu, b_glu, w_lin, b_lin, w_out, b_out)


def _combine_kernel(ys_ref, tg_ref, x_ref, gate_ref, *rest, tail):
    half = x_ref.shape[1] // 2
    tg = tg_ref[...]
    lo_sum = hi_sum = None
    for k in range(ys_ref.shape[0]):
        lo, hi = _unpack_halves(ys_ref[k])
        gk = tg[:, k:k + 1]
        lo_sum = gk * lo if lo_sum is None else lo_sum + gk * lo
        hi_sum = gk * hi if hi_sum is None else hi_sum + gk * hi
    x_lo = x_ref[:, :half] + gate_ref[0, :, :half] * lo_sum
    x_hi = x_ref[:, half:] + gate_ref[0, :, half:] * hi_sum
    ms = (jnp.sum(x_lo * x_lo, axis=-1, keepdims=True)
          + jnp.sum(x_hi * x_hi, axis=-1, keepdims=True)) / x_ref.shape[1]
    inv = lax.rsqrt(ms + RMS_EPS)
    if tail == "final":
        g_ref, o_ref = rest
        o_ref[:, :half] = x_lo * inv * g_ref[:, :half]
        o_ref[:, half:] = x_hi * inv * g_ref[:, half:]
    else:
        g_ref, sc_ref, sh_ref, o_ref, h_ref = rest
        o_ref[:, :half] = x_lo
        o_ref[:, half:] = x_hi
        for cols, xs in ((slice(0, half), x_lo), (slice(half, 2 * half), x_hi)):
            h = xs * inv * g_ref[:, cols] * (1.0 + sc_ref[0, :, cols]) + sh_ref[0, :, cols]
            h_ref[:, cols] = h.astype(BF16)


def _combine(ys, tok_gate, x, gate, seq, *, final_g=None, next_norm=None, tm=256):
    K, T, pw = ys.shape
    D = x.shape[1]
    bmap = lambda i: ((i * tm) // seq, 0, 0)
    rows = pl.BlockSpec((tm, D), lambda i: (i, 0))
    vec = pl.BlockSpec((1, D), lambda i: (0, 0))
    per_batch = pl.BlockSpec((1, 1, D), bmap)
    in_specs = [pl.BlockSpec((K, tm, pw), lambda i: (0, i, 0)),
                pl.BlockSpec((tm, K), lambda i: (i, 0)), rows, per_batch]
    args = [ys, tok_gate, x, gate]
    if final_g is not None:
        tail = "final"
        in_specs.append(vec)
        args.append(final_g)
        out_specs, out_shape = rows, jax.ShapeDtypeStruct((T, D), F32)
    else:
        tail = "next"
        in_specs += [vec, per_batch, per_batch]
        args += list(next_norm)
        out_specs = [rows, rows]
        out_shape = [jax.ShapeDtypeStruct((T, D), F32), jax.ShapeDtypeStruct((T, D), BF16)]
    return pl.pallas_call(
        functools.partial(_combine_kernel, tail=tail),
        grid=(T // tm,),
        in_specs=in_specs,
        out_specs=out_specs,
        out_shape=out_shape,
        compiler_params=_cparams(("parallel",)),
        name="moe_combine",
    )(*args)


MOE_BLOCK = 512


def _slots_kernel(start_ref, idx_ref, rank_ref, o_ref):
    idx = idx_ref[...]
    acc = rank_ref[...]
    for e in range(start_ref.shape[0]):
        acc = acc + jnp.where(idx == e, start_ref[e], 0)
    o_ref[...] = acc


def _slots(idx_t, rank_t, group_start):
    full = pl.BlockSpec(idx_t.shape, lambda i, gs: (0, 0))
    return pl.pallas_call(
        _slots_kernel,
        grid_spec=pltpu.PrefetchScalarGridSpec(num_scalar_prefetch=1, grid=(1,),
                                               in_specs=[full, full], out_specs=full),
        out_shape=jax.ShapeDtypeStruct(idx_t.shape, I32),
        compiler_params=_cparams(("arbitrary",)),
        name="moe_slots",
    )(group_start, idx_t, rank_t)


def _dispatch_plan(idx_t, rank_t, counts, blk):
    K, T = idx_t.shape
    E = counts.shape[0]
    padded = (counts + blk - 1) // blk * blk
    pad_end = jnp.cumsum(padded)
    pad_start = pad_end - padded
    experts = jnp.arange(E, dtype=I32)
    slots = _slots(idx_t, rank_t, pad_start.astype(I32))
    n_blocks = -(-(K * T + E * (blk - 1)) // blk)
    blk_start = jnp.arange(n_blocks, dtype=I32) * blk
    blk_e = jnp.minimum(jnp.sum(pad_end[None, :] <= blk_start[:, None], axis=1), E - 1).astype(I32)
    n_valid = jnp.clip(pad_start[blk_e] + counts[blk_e] - blk_start, 0, blk).astype(I32)
    later = jnp.where(counts > 0, experts, E)
    after = lax.cummin(jnp.concatenate([later[1:], jnp.full((1,), E, I32)]), reverse=True)
    next_e = jnp.where(after < E, after, -1).astype(I32)[blk_e]
    return slots, blk_e, n_valid, next_e, n_blocks * blk


def _mixer_out_moe(y, w_mix, x, mix_gate, g, scale, shift, gate, r_w, r_b, layer,
                   w_glu, b_glu, w_lin, b_lin, w_out, b_out, seq, **tail):
    T, D = x.shape
    x, hp, idx_t, gate_t, rank_t, counts = _mixer_out_route(
        y, w_mix, x, mix_gate, g, scale, shift, r_w, r_b, seq)
    slots, blk_e, n_valid, next_e, n_rows = _dispatch_plan(idx_t, rank_t, counts[:, 0],
                                                           MOE_BLOCK)
    xs = _sc_scatter_rows(hp, slots, n_rows)
    yp = _experts(xs, blk_e, n_valid, next_e, layer, w_glu, b_glu, w_lin, b_lin, w_out, b_out,
                  blk=MOE_BLOCK)
    ys = _sc_gather(yp, slots.reshape(TOP_K * T)).reshape(TOP_K, T, yp.shape[1])
    return _combine(ys, jnp.transpose(gate_t), x, gate, seq, **tail)


def kernel(x, c, norm_g, ada_w, final_norm_g, sg_w_in, sg_ln_g, sg_ln_b, sg_w_s, sg_b_s, sg_w_out, pool_w_in, pool_w_grp, pool_ls, pool_w_out, hgrn_w_in, hgrn_lb_logits, hgrn_gnorm_g, hgrn_w_out, router_w, router_b, expert_w_glu, expert_b_glu, expert_w_lin, expert_b_lin, expert_w_out, expert_b_out):
    B, S, D = x.shape
    depth = norm_g.shape[0]
    T = B * S
    xf = x.reshape(T, D)

    ada_w = ada_w.reshape(depth * 2, D, 3 * D)

    def mod_parts(layer, sub):
        m = _ada_mod(c, ada_w, layer * 2 + sub)[:, None, :]
        return m[..., :D], m[..., D:2 * D], m[..., 2 * D:]

    def mixer_norm(layer, shift, scale):
        return norm_g[layer, 0][None, :], scale, shift

    lb_cum = jnp.cumsum(jax.nn.softmax(hgrn_lb_logits.astype(F32), axis=0), axis=0)
    lower_bounds = lb_cum - lb_cum[0]

    shift, scale, mix_gate = mod_parts(0, 0)
    src, norm = xf, mixer_norm(0, shift, scale)
    for layer in range(depth):
        kind, slot = layer % 3, layer // 3
        if kind == 0:
            z = _norm_matmul(src, norm, sg_w_in[slot].astype(BF16), S, act="gelu",
                             out_dtype=BF16)
            y = _spatial_gate(z, sg_ln_g[slot][None, :], sg_ln_b[slot][None, :],
                              sg_w_s[slot], sg_b_s[slot])
            w_out = sg_w_out[slot]
        elif kind == 1:
            z = _norm_matmul(src, norm, pool_w_in[slot].astype(BF16), S, act=None,
                             out_dtype=F32)
            y = _pool_mix(z, pool_w_grp[slot].astype(BF16), pool_ls[slot][None, :], S)
            w_out = pool_w_out[slot]
        else:
            proj, f_pre = _norm_matmul(src, norm, hgrn_w_in[slot].astype(BF16), S, act=None,
                                       out_dtype=BF16, f32_col=1, tn=D)
            y = _hgrn_mix(proj, f_pre, lower_bounds[layer][None, :],
                          hgrn_gnorm_g[slot][None, :], B, S)
            w_out = hgrn_w_out[slot]

        moe_shift, moe_scale, moe_gate = mod_parts(layer, 1)
        moe_args = (y, w_out.astype(BF16), xf, mix_gate, norm_g[layer, 1][None, :],
                    moe_scale, moe_shift, moe_gate, router_w[layer], router_b[layer], layer,
                    expert_w_glu, expert_b_glu, expert_w_lin, expert_b_lin,
                    expert_w_out, expert_b_out, S)
        if layer == depth - 1:
            xf = _mixer_out_moe(*moe_args, final_g=final_norm_g[None, :])
        else:
            shift, scale, mix_gate = mod_parts(layer + 1, 0)
            xf, src = _mixer_out_moe(*moe_args, next_norm=mixer_norm(layer + 1, shift, scale))
            norm = None

    return xf.reshape(B, S, D)
```

```python
import functools

import numpy as np
import jax
import jax.numpy as jnp
from jax import lax
from jax.experimental import pallas as pl
from jax.experimental.pallas import tpu as pltpu
from jax.experimental.pallas import tpu_sc as plsc

F32 = jnp.float32
BF16 = jnp.bfloat16
I32 = jnp.int32
U32 = jnp.uint32

RMS_EPS = 1e-5
LN_EPS = 1e-5
SWIGLU_ALPHA = 1.702
SWIGLU_LIMIT = 7.0
TOP_K = 4
SG_CHUNK = 128
POOL_WINDOWS = (2, 4, 8, 16)
HEAD_DIM = 128
LANES = 128
SC_WORKERS = 32
SC_ROWS_PER_STEP = 64

VMEM_LIMIT = 48 * 1024 * 1024
BIG_VMEM_LIMIT = 56 * 1024 * 1024


def _cparams(sem):
    return pltpu.CompilerParams(dimension_semantics=sem, vmem_limit_bytes=VMEM_LIMIT)


def _dot(a, b):
    return jnp.dot(a, b, preferred_element_type=F32)


def _dot_nt(a, b):
    return lax.dot_general(a, b, (((1,), (1,)), ((), ())), preferred_element_type=F32)


def _dot_tn(a, b):
    return lax.dot_general(a, b, (((0,), (0,)), ((), ())), preferred_element_type=F32)


def _split_bf16(x):
    hi = x.astype(BF16)
    lo = (x - hi.astype(F32)).astype(BF16)
    return hi, lo


def _packed_layout(d):
    return d // 2, I32


def _pack_halves(y):
    n = y.shape[1] // 2
    lo = lax.bitcast_convert_type(y[:, :n].astype(BF16).astype(F32), U32)
    hi = lax.bitcast_convert_type(y[:, n:].astype(BF16).astype(F32), U32)
    word = lax.shift_right_logical(lo, jnp.uint32(16)) | (hi & jnp.uint32(0xFFFF0000))
    return lax.bitcast_convert_type(word, I32)


def _unpack_halves(w):
    u = lax.bitcast_convert_type(w, U32)
    lo = lax.bitcast_convert_type(lax.shift_left(u, jnp.uint32(16)), F32)
    hi = lax.bitcast_convert_type(u & jnp.uint32(0xFFFF0000), F32)
    return lo, hi


def _modulated_norm(x, g, scale, shift):
    ms = jnp.mean(x * x, axis=-1, keepdims=True)
    y = x * lax.rsqrt(ms + RMS_EPS) * g
    return y * (1.0 + scale) + shift


def _ada_kernel(c_ref, w_ref, o_ref):
    w = w_ref[0]
    reps = w.shape[1] // LANES
    for b in range(c_ref.shape[0]):
        cb = c_ref[b]
        s = jnp.tile(cb * jax.nn.sigmoid(cb), (1, reps))
        o_ref[b] = jnp.sum(w * s, axis=0, keepdims=True)


def _ada_mod(c, ada_w, index):
    L, D, N = ada_w.shape
    B = c.shape[0]
    tn = 512
    c_cols = jnp.broadcast_to(c[:, :, None], (B, D, LANES))
    out = pl.pallas_call(
        _ada_kernel,
        grid=(N // tn,),
        in_specs=[pl.BlockSpec((B, D, LANES), lambda j: (0, 0, 0)),
                  pl.BlockSpec((1, D, tn), lambda j: (index, 0, j))],
        out_specs=pl.BlockSpec((B, 1, tn), lambda j: (0, 0, j)),
        out_shape=jax.ShapeDtypeStruct((B, 1, N), F32),
        compiler_params=_cparams(("parallel",)),
        name="ada_mod",
    )(c_cols, ada_w)
    return out[:, 0]


def _nm_kernel(*refs, act, f32_col, prenormed):
    if prenormed:
        h_ref, w_ref, o_ref = refs[:3]
        outs = refs[3:]
    else:
        x_ref, g_ref, sc_ref, sh_ref, w_ref, o_ref = refs[:6]
        outs, h_ref = refs[6:-1], refs[-1]

        @pl.when(pl.program_id(1) == 0)
        def _():
            h = _modulated_norm(x_ref[...], g_ref[...], sc_ref[0], sh_ref[0])
            h_ref[...] = h.astype(BF16)

    acc = _dot(h_ref[...], w_ref[...])
    if act == "gelu":
        acc = 0.5 * acc * (1.0 + lax.erf(acc * np.float32(1.0 / np.sqrt(2.0))))
    o_ref[...] = acc.astype(o_ref.dtype)
    if f32_col is not None:
        @pl.when(pl.program_id(1) == f32_col)
        def _():
            outs[0][...] = acc


def _norm_matmul(x, norm, w, seq, *, act, out_dtype, f32_col=None, tm=512, tn=2048):
    T, D = x.shape
    N = w.shape[1]
    prenormed = norm is None
    bmap = lambda i, j: ((i * tm) // seq, 0, 0)
    out_specs = [pl.BlockSpec((tm, tn), lambda i, j: (i, j))]
    out_shape = [jax.ShapeDtypeStruct((T, N), out_dtype)]
    if f32_col is not None:
        out_specs.append(pl.BlockSpec((tm, tn), lambda i, j: (i, 0)))
        out_shape.append(jax.ShapeDtypeStruct((T, tn), F32))
    in_specs = [pl.BlockSpec((tm, D), lambda i, j: (i, 0))]
    args = [x]
    if not prenormed:
        in_specs += [pl.BlockSpec((1, D), lambda i, j: (0, 0)),
                     pl.BlockSpec((1, 1, D), bmap),
                     pl.BlockSpec((1, 1, D), bmap)]
        args += list(norm)
    in_specs.append(pl.BlockSpec((D, tn), lambda i, j: (0, j)))
    args.append(w)
    outs = pl.pallas_call(
        functools.partial(_nm_kernel, act=act, f32_col=f32_col, prenormed=prenormed),
        grid=(T // tm, N // tn),
        in_specs=in_specs,
        out_specs=out_specs,
        out_shape=out_shape,
        scratch_shapes=[] if prenormed else [pltpu.VMEM((tm, D), BF16)],
        compiler_params=pltpu.CompilerParams(dimension_semantics=("parallel", "arbitrary"),
                                             vmem_limit_bytes=BIG_VMEM_LIMIT),
        name="norm_matmul_" + str(act),
    )(*args)
    return outs if f32_col is not None else outs[0]


def _sg_kernel(u_ref, v_ref, lng_ref, lnb_ref, ws_ref, bst_ref, y_ref, vn_ref, *, heads):
    v = v_ref[...].astype(F32)
    mu = jnp.mean(v, axis=-1, keepdims=True)
    vc = v - mu
    var = jnp.mean(vc * vc, axis=-1, keepdims=True)
    vn_ref[...] = (vc * lax.rsqrt(var + LN_EPS) * lng_ref[...] + lnb_ref[...]).astype(BF16)

    n_chunks = v_ref.shape[0] // SG_CHUNK
    row = lax.broadcasted_iota(I32, (SG_CHUNK, SG_CHUNK), 0)
    col = lax.broadcasted_iota(I32, (SG_CHUNK, SG_CHUNK), 1)
    causal = row >= col
    for hd in range(heads):
        cs = slice(hd * HEAD_DIM, (hd + 1) * HEAD_DIM)
        wm = jnp.where(causal, ws_ref[hd], 0.0).astype(BF16)
        bias = bst_ref[:, hd:hd + 1]
        for ci in range(n_chunks):
            rs = slice(ci * SG_CHUNK, (ci + 1) * SG_CHUNK)
            mixed = _dot(wm, vn_ref[rs, cs]) + bias
            y_ref[rs, cs] = (u_ref[rs, cs].astype(F32) * mixed).astype(BF16)


def _spatial_gate(z, ln_g, ln_b, w_s, b_s, *, tm=256):
    T, two_w = z.shape
    W = two_w // 2
    heads = w_s.shape[0]
    return pl.pallas_call(
        functools.partial(_sg_kernel, heads=heads),
        grid=(T // tm,),
        in_specs=[pl.BlockSpec((tm, W), lambda i: (i, 0)),
                  pl.BlockSpec((tm, W), lambda i: (i, 1)),
                  pl.BlockSpec((1, W), lambda i: (0, 0)),
                  pl.BlockSpec((1, W), lambda i: (0, 0)),
                  pl.BlockSpec((heads, SG_CHUNK, SG_CHUNK), lambda i: (0, 0, 0)),
                  pl.BlockSpec((SG_CHUNK, heads), lambda i: (0, 0))],
        out_specs=pl.BlockSpec((tm, W), lambda i: (i, 0)),
        out_shape=jax.ShapeDtypeStruct((T, W), BF16),
        scratch_shapes=[pltpu.VMEM((tm, W), BF16)],
        compiler_params=_cparams(("parallel",)),
        name="spatial_gate",
    )(z, z, ln_g, ln_b, w_s, jnp.transpose(b_s))


POOL_HALO = 16


def _pool_kernel(z_ref, halo_ref, wg_ref, ls_ref, y_ref, *, seq):
    tm = z_ref.shape[0]
    gdim = wg_ref.shape[1]
    pos0 = (pl.program_id(0) * tm) % seq
    halo_on = (pos0 > 0).astype(F32)
    pos = pos0 + lax.broadcasted_iota(I32, (tm, 1), 0)
    for gi, wnd in enumerate(POOL_WINDOWS):
        cs = slice(gi * gdim, (gi + 1) * gdim)
        zg = z_ref[:, cs]
        s = jnp.concatenate([halo_ref[:, cs] * halo_on, zg], axis=0)
        k = 1
        while k < wnd:
            s = s + pltpu.roll(s, k, 0)
            k *= 2
        cnt = jnp.minimum(pos + 1, wnd).astype(F32)
        pooled = s[POOL_HALO:, :] / cnt - zg
        yg = _dot(pooled.astype(BF16), wg_ref[gi]) * ls_ref[:, cs]
        y_ref[:, cs] = yg.astype(BF16)


def _pool_mix(z, w_grp, ls, seq, *, tm=256):
    T, D = z.shape
    G, gdim, _ = w_grp.shape
    assert max(POOL_WINDOWS) <= POOL_HALO and tm % POOL_HALO == 0
    ratio = tm // POOL_HALO
    return pl.pallas_call(
        functools.partial(_pool_kernel, seq=seq),
        grid=(T // tm,),
        in_specs=[pl.BlockSpec((tm, D), lambda i: (i, 0)),
                  pl.BlockSpec((POOL_HALO, D), lambda i: (jnp.maximum(i * ratio - 1, 0), 0)),
                  pl.BlockSpec((G, gdim, gdim), lambda i: (0, 0, 0)),
                  pl.BlockSpec((1, D), lambda i: (0, 0))],
        out_specs=pl.BlockSpec((tm, D), lambda i: (i, 0)),
        out_shape=jax.ShapeDtypeStruct((T, D), BF16),
        compiler_params=_cparams(("parallel",)),
        name="pool_mix",
    )(z, z, w_grp, ls)


HGRN_CHUNK = 128
HGRN_BASE = 32
HGRN_GUARD_LOG2 = 100.0
LOG2_E = 1.4426950408889634


def _hgrn_tables(c=HGRN_CHUNK, base=HGRN_BASE):
    t = np.arange(c)
    j = t[None, :]
    sums = [j <= t[:, None]]
    masks = []
    n_shallow = 0
    h = c // 2
    while h >= 1:
        off = t % (2 * h)
        mid = (t // (2 * h)) * (2 * h) + h
        second = off >= h
        if h >= 2:
            m_q = second[:, None] & (j >= mid[:, None]) & (j <= t[:, None])
            m_k = (~second)[:, None] & (j > t[:, None]) & (j < mid[:, None])
            sums.append(m_q | m_k)
        same = (t[:, None] // (2 * h)) == (t[None, :] // (2 * h))
        masks.append(same & second[:, None] & (~second)[None, :])
        n_shallow += h >= base
        h //= 2
    same_base = (t[:, None] // base) == (t[None, :] // base)
    sums.append(same_base & (j <= t[:, None]))
    sums = np.stack(sums).astype(np.float32)
    base_mask = (same_base & (j <= t[:, None])).astype(np.float32)
    return (np.concatenate([sums, sums], axis=2), np.stack(masks).astype(np.float32), base_mask,
            n_shallow)


def _hgrn_kernel(q_ref, f_ref, i_ref, g_ref, lb_ref, gn_ref, sums_ref, masks_ref, bmask_ref,
                 o_ref, st_ref, qb_ref, ke_ref, z_ref, qd_ref, kd_ref, vb_ref, eb_ref, sc_ref,
                 *, heads, n_shallow):
    c = q_ref.shape[0]
    n_sum = sums_ref.shape[0] - 1
    n_lvl = masks_ref.shape[0]

    @pl.when(pl.program_id(1) == 0)
    def _():
        st_ref[...] = jnp.zeros_like(st_ref)

    lb = lb_ref[...]
    sig = jax.nn.sigmoid(f_ref[...])
    fg = lb + (1.0 - lb) * sig
    lf2 = jnp.log(fg) * np.float32(LOG2_E)
    kk = (1.0 - lb) * (1.0 - sig)
    lf_hi, lf_lo = _split_bf16(lf2)
    lf_cat = jnp.concatenate([lf_hi, lf_lo], axis=0)
    qr = q_ref[...].astype(F32)
    q = qr * jax.nn.sigmoid(qr)
    vb_ref[...] = i_ref[...].astype(BF16)
    q16 = q.astype(BF16)
    k16 = kk.astype(BF16)

    b2 = _dot(sums_ref[0], lf_cat)
    eb = jnp.exp2(b2)
    qb_ref[...] = (q * eb).astype(BF16)
    eb_ref[...] = eb[c - 1:c, :]
    ke_ref[...] = (kk * jnp.exp2(b2[c - 1:c, :] - b2)).astype(BF16)
    w2 = _dot(sums_ref[n_sum], lf_cat)
    shallow = jnp.min(w2) >= -HGRN_GUARD_LOG2

    row = lax.broadcasted_iota(I32, (c, 1), 0)

    def level(l):
        half = c >> (l + 1)
        second = (row & (2 * half - 1)) >= half
        if l + 1 < n_sum:
            x = jnp.exp2(_dot(sums_ref[l + 1], lf_cat).astype(BF16))
        else:
            x = jnp.where(second, fg, 1.0).astype(BF16)
        z_ref[l] = jnp.where(second, q16, k16) * x

    gn = gn_ref[...]

    def finish(hd, cs, o):
        st = st_ref[hd]
        vh = vb_ref[:, cs]
        o = o + _dot_nt(qb_ref[:, cs], st.astype(BF16))
        st_ref[hd] = st * eb_ref[:, cs] + _dot_tn(vh, ke_ref[:, cs])
        on = o * lax.rsqrt(jnp.mean(o * o, axis=-1, keepdims=True) + RMS_EPS) * gn
        gv = g_ref[:, cs].astype(F32)
        o_ref[:, cs] = (on * (gv * jax.nn.sigmoid(gv))).astype(BF16)

    def level_scores(cs, n):
        scores = None
        for l in range(n):
            half = c >> (l + 1)
            zl = z_ref[l, :, cs]
            if half >= 16:
                starts = list(range(half, c, 2 * half))
                zq = jnp.concatenate([z_ref[l, r:r + half, cs] for r in starts], axis=0)
                p = _dot_nt(zq, zl).astype(BF16)
                parts = []
                for n_i, r in enumerate(starts):
                    parts.append(jnp.zeros((half, c), BF16))
                    parts.append(masks_ref[l, r:r + half, :] * p[n_i * half:(n_i + 1) * half])
                term = jnp.concatenate(parts, axis=0)
            else:
                term = masks_ref[l] * _dot_nt(zl, zl).astype(BF16)
            scores = term if scores is None else scores + term
        return scores

    @pl.when(shallow)
    def _():
        for l in range(n_shallow):
            level(l)
        qd_ref[...] = (q * jnp.exp2(w2)).astype(BF16)
        kd_ref[...] = (kk * jnp.exp2(-w2)).astype(BF16)

        def head_scores(hd, carry):
            cs = pl.ds(pl.multiple_of(hd * HEAD_DIM, HEAD_DIM), HEAD_DIM)
            sc_ref[hd] = (level_scores(cs, n_shallow)
                          + bmask_ref[...] * _dot_nt(qd_ref[:, cs], kd_ref[:, cs]).astype(BF16))
            return carry

        def head_out(hd, carry):
            cs = pl.ds(pl.multiple_of(hd * HEAD_DIM, HEAD_DIM), HEAD_DIM)
            finish(hd, cs, _dot(sc_ref[hd], vb_ref[:, cs]))
            return carry

        lax.fori_loop(0, heads, head_scores, 0, unroll=16)
        lax.fori_loop(0, heads, head_out, 0, unroll=16)

    @pl.when(jnp.logical_not(shallow))
    def _():
        for l in range(n_lvl):
            level(l)
        qd_ref[...] = q16
        kd_ref[...] = k16

        def head(hd, carry):
            cs = pl.ds(pl.multiple_of(hd * HEAD_DIM, HEAD_DIM), HEAD_DIM)
            vh = vb_ref[:, cs]
            diag = jnp.sum(qd_ref[:, cs].astype(F32) * kd_ref[:, cs].astype(F32), axis=-1,
                           keepdims=True)
            finish(hd, cs, _dot(level_scores(cs, n_lvl), vh) + diag * vh.astype(F32))
            return carry

        lax.fori_loop(0, heads, head, 0, unroll=4)


def _hgrn_mix(proj, f_pre, lb, gnorm_g, batch, seq):
    T, four_d = proj.shape
    D = four_d // 4
    heads = D // HEAD_DIM
    c = HGRN_CHUNK
    n_chunks = seq // c
    sums, masks, base_mask, n_shallow = _hgrn_tables(c)
    n_sum, n_mask = sums.shape[0], masks.shape[0]
    row = lambda b, i: b * n_chunks + i
    return pl.pallas_call(
        functools.partial(_hgrn_kernel, heads=heads, n_shallow=n_shallow),
        grid=(batch, n_chunks),
        in_specs=[pl.BlockSpec((c, D), lambda b, i: (row(b, i), 0)),
                  pl.BlockSpec((c, D), lambda b, i: (row(b, i), 0)),
                  pl.BlockSpec((c, D), lambda b, i: (row(b, i), 2)),
                  pl.BlockSpec((c, D), lambda b, i: (row(b, i), 3)),
                  pl.BlockSpec((1, D), lambda b, i: (0, 0)),
                  pl.BlockSpec((1, HEAD_DIM), lambda b, i: (0, 0)),
                  pl.BlockSpec((n_sum, c, 2 * c), lambda b, i: (0, 0, 0)),
                  pl.BlockSpec((n_mask, c, c), lambda b, i: (0, 0, 0)),
                  pl.BlockSpec((c, c), lambda b, i: (0, 0))],
        out_specs=pl.BlockSpec((c, D), lambda b, i: (row(b, i), 0)),
        out_shape=jax.ShapeDtypeStruct((T, D), BF16),
        scratch_shapes=[pltpu.VMEM((heads, HEAD_DIM, HEAD_DIM), F32),
                        pltpu.VMEM((c, D), BF16),
                        pltpu.VMEM((c, D), BF16),
                        pltpu.VMEM((n_mask, c, D), BF16),
                        pltpu.VMEM((c, D), BF16),
                        pltpu.VMEM((c, D), BF16),
                        pltpu.VMEM((c, D), BF16),
                        pltpu.VMEM((1, D), F32),
                        pltpu.VMEM((heads, c, c), BF16)],
        compiler_params=_cparams(("parallel", "arbitrary")),
        name="hgrn_mix",
    )(proj, f_pre, proj, proj, lb, gnorm_g, jnp.asarray(sums, BF16), jnp.asarray(masks, BF16),
      jnp.asarray(base_mask, BF16))


def _route_kernel(y_ref, w_ref, xin_ref, mgate_ref, g_ref, sc_ref, sh_ref, rw_ref,
                  rb_ref, tri_ref, x_ref, hp_ref, idx_ref, gate_ref, rank_ref, cnt_ref, carry_ref,
                  *, n_experts):
    @pl.when(pl.program_id(0) == 0)
    def _():
        carry_ref[...] = jnp.zeros_like(carry_ref)

    x = xin_ref[...] + mgate_ref[0] * _dot(y_ref[...], w_ref[...])
    x_ref[...] = x
    h = _modulated_norm(x, g_ref[...], sc_ref[0], sh_ref[0])
    hp_ref[...] = _pack_halves(h)

    h_hi, h_lo = _split_bf16(h)
    both = _dot(h_hi, rw_ref[...])
    logits = (both[:, :LANES] + both[:, LANES:] + _dot(h_lo, rw_ref[:, :LANES])) + rb_ref[...]
    lt = jnp.transpose(logits)[:n_experts, :]
    eidx = lax.broadcasted_iota(I32, lt.shape, 0)
    vals, idxs, hits = [], [], []
    for _ in range(TOP_K):
        m = jnp.max(lt, axis=0, keepdims=True)
        sel = jnp.min(jnp.where(lt == m, eidx, n_experts), axis=0, keepdims=True)
        hit = eidx == sel
        vals.append(m)
        idxs.append(sel)
        hits.append(hit)
        lt = jnp.where(hit, -jnp.inf, lt)
    exps = [jnp.exp(v - vals[0]) for v in vals]
    denom = exps[0]
    for e in exps[1:]:
        denom = denom + e
    idx_ref[...] = jnp.concatenate(idxs, axis=0)
    gate_ref[...] = jnp.concatenate([e / denom for e in exps], axis=0)

    onehot = hits[0].astype(F32)
    for hit in hits[1:]:
        onehot = onehot + hit.astype(F32)
    before = carry_ref[...] + _dot(onehot.astype(BF16), tri_ref[...])
    rank_ref[...] = jnp.concatenate(
        [jnp.sum(jnp.where(hit, before, 0.0), axis=0, keepdims=True) for hit in hits],
        axis=0).astype(I32)
    carry_ref[...] += jnp.sum(onehot, axis=1, keepdims=True)
    cnt_ref[...] = carry_ref[...].astype(I32)


def _mixer_out_route(y, w, x, mix_gate, g, scale, shift, r_w, r_b, seq, *, tm=256):
    T, K = y.shape
    D = w.shape[1]
    E = r_w.shape[1]
    rw_pad = jnp.zeros((D, LANES), F32).at[:, :E].set(r_w)
    rw_cat = jnp.concatenate(_split_bf16(rw_pad), axis=1)
    rb_pad = jnp.full((1, LANES), -1e30, F32).at[0, :E].set(r_b)
    tri = jnp.asarray(np.triu(np.ones((tm, tm), np.float32), k=1), BF16)
    pw, pdt = _packed_layout(D)
    bmap = lambda i: ((i * tm) // seq, 0, 0)
    const = lambda i: (0, 0)
    rows = lambda i: (i, 0)
    kt = pl.BlockSpec((TOP_K, tm), lambda i: (0, i))
    return pl.pallas_call(
        functools.partial(_route_kernel, n_experts=E),
        grid=(T // tm,),
        in_specs=[pl.BlockSpec((tm, K), rows),
                  pl.BlockSpec((K, D), const),
                  pl.BlockSpec((tm, D), rows),
                  pl.BlockSpec((1, 1, D), bmap),
                  pl.BlockSpec((1, D), const),
                  pl.BlockSpec((1, 1, D), bmap),
                  pl.BlockSpec((1, 1, D), bmap),
                  pl.BlockSpec((D, 2 * LANES), const),
                  pl.BlockSpec((1, LANES), const),
                  pl.BlockSpec((tm, tm), const)],
        out_specs=[pl.BlockSpec((tm, D), rows), pl.BlockSpec((tm, pw), rows), kt, kt, kt,
                   pl.BlockSpec((E, 1), const)],
        out_shape=[jax.ShapeDtypeStruct((T, D), F32),
                   jax.ShapeDtypeStruct((T, pw), pdt),
                   jax.ShapeDtypeStruct((TOP_K, T), I32),
                   jax.ShapeDtypeStruct((TOP_K, T), F32),
                   jax.ShapeDtypeStruct((TOP_K, T), I32),
                   jax.ShapeDtypeStruct((E, 1), I32)],
        scratch_shapes=[pltpu.VMEM((E, 1), F32)],
        compiler_params=_cparams(("arbitrary",)),
        name="mixer_out_route",
    )(y, w, x, mix_gate, g, scale, shift, rw_cat, rb_pad, tri)


def _sc_gather(table, idx):
    R = idx.shape[0]
    W = table.shape[1]
    rows = SC_ROWS_PER_STEP // 2
    per_worker = R // SC_WORKERS
    steps = per_worker // rows
    assert steps * rows * SC_WORKERS == R and steps % 2 == 0
    mesh = plsc.VectorSubcoreMesh(core_axis_name="c", subcore_axis_name="s")
    n_cores = mesh.num_cores

    @functools.partial(
        pl.kernel, mesh=mesh,
        out_type=jax.ShapeDtypeStruct((R, W), table.dtype),
        scratch_types=[pltpu.VMEM((steps, rows), I32),
                       pltpu.VMEM((rows, W), table.dtype),
                       pltpu.VMEM((rows, W), table.dtype),
                       pltpu.SemaphoreType.DMA((2,)),
                       pltpu.SemaphoreType.DMA((2,))],
    )
    def gather(table_hbm, idx_hbm, out_hbm, idx_v, buf0, buf1, gsem, wsem):
        wid = lax.axis_index("s") * n_cores + lax.axis_index("c")
        base = wid * per_worker
        bufs = (buf0, buf1)
        pltpu.sync_copy(idx_hbm.at[pl.ds(wid * steps, steps)], idx_v)

        def fetch(j, b):
            return pltpu.make_async_copy(table_hbm.at[idx_v.at[j]], bufs[b], gsem.at[b])

        def flush(j, b):
            off = pl.multiple_of(base + j * rows, 8)
            return pltpu.make_async_copy(bufs[b], out_hbm.at[pl.ds(off, rows)], wsem.at[b])

        fetch(0, 0).start()

        @pl.loop(0, steps, step=2)
        def _(j0):
            for b in range(2):
                j = j0 + b
                fetch(j, b).wait()
                flush(j, b).start()

                @pl.when(j + 1 < steps)
                def _():
                    @pl.when(j >= 1)
                    def _():
                        flush(j - 1, 1 - b).wait()
                    fetch(j + 1, 1 - b).start()

        flush(steps - 2, 0).wait()
        flush(steps - 1, 1).wait()

    return gather(table, idx.reshape(R // rows, rows))


def _sc_scatter_rows(table, slots, n_out):
    K, T = slots.shape
    W = table.shape[1]
    rows = SC_ROWS_PER_STEP
    per_worker = T // SC_WORKERS
    steps = per_worker // rows
    assert steps * rows * SC_WORKERS == T
    mesh = plsc.VectorSubcoreMesh(core_axis_name="c", subcore_axis_name="s")
    n_cores = mesh.num_cores

    @functools.partial(
        pl.kernel, mesh=mesh,
        out_type=jax.ShapeDtypeStruct((n_out, W), table.dtype),
        scratch_types=[pltpu.VMEM((K, rows), I32),
                       pltpu.VMEM((rows, W), table.dtype)],
    )
    def scatter(table_hbm, slots_hbm, out_hbm, idx_v, rows_v):
        wid = lax.axis_index("s") * n_cores + lax.axis_index("c")
        base = wid * per_worker

        @pl.loop(0, steps)
        def _(j):
            off = pl.multiple_of(base + j * rows, 8)
            pltpu.sync_copy(table_hbm.at[pl.ds(off, rows)], rows_v)
            for k in range(K):
                pltpu.sync_copy(slots_hbm.at[pl.ds(pl.multiple_of(k * T + off, 8), rows)],
                                idx_v.at[k])
                pltpu.sync_copy(rows_v, out_hbm.at[idx_v.at[k]])

    return scatter(table, slots.reshape(K * T))


EXPERT_SUB_ROWS = 256


def _expert_kernel(blk_e_ref, n_valid_ref, next_e_ref, xs_ref, wg_hbm, bg_ref, wl_hbm, bl_ref,
                   wo_hbm, bo_ref, y_ref, wg_st, wl_st, wo_st, wgb_ref, wlb_ref, wob_ref, sem,
                   *, expert_base):
    i = pl.program_id(0)
    e = blk_e_ref[i]
    n_valid = n_valid_ref[i]
    new_expert = jnp.logical_or(i == 0, e != blk_e_ref[jnp.maximum(i - 1, 0)])

    def weight_copies(expert):
        idx = expert_base + expert
        return (pltpu.make_async_copy(wg_hbm.at[idx], wg_st, sem.at[0]),
                pltpu.make_async_copy(wl_hbm.at[idx], wl_st, sem.at[1]),
                pltpu.make_async_copy(wo_hbm.at[idx], wo_st, sem.at[2]))

    @pl.when(i == 0)
    def _():
        for cp in weight_copies(e):
            cp.start()

    @pl.when(jnp.logical_and(new_expert, n_valid > 0))
    def _():
        for cp in weight_copies(e):
            cp.wait()
        wgb_ref[...] = wg_st[...].astype(BF16)
        wlb_ref[...] = wl_st[...].astype(BF16)
        wob_ref[...] = wo_st[...].astype(BF16)
        nxt = next_e_ref[i]

        @pl.when(nxt >= 0)
        def _():
            for cp in weight_copies(nxt):
                cp.start()

    half = wgb_ref.shape[0] // 2
    for first in range(0, xs_ref.shape[0], EXPERT_SUB_ROWS):
        rs = slice(first, first + EXPERT_SUB_ROWS)

        @pl.when(n_valid > first)
        def _():
            live = first + lax.broadcasted_iota(I32, (EXPERT_SUB_ROWS, 1), 0) < n_valid
            lo, hi = _unpack_halves(jnp.where(live, xs_ref[rs, :], 0))
            lo = lo.astype(BF16)
            hi = hi.astype(BF16)
            a = _dot(lo, wgb_ref[:half, :]) + _dot(hi, wgb_ref[half:, :]) + bg_ref[0]
            l = _dot(lo, wlb_ref[:half, :]) + _dot(hi, wlb_ref[half:, :]) + bl_ref[0]
            a = jnp.minimum(a, SWIGLU_LIMIT)
            l = jnp.clip(l, -SWIGLU_LIMIT, SWIGLU_LIMIT)
            act = a * jax.nn.sigmoid(SWIGLU_ALPHA * a) * (l + 1.0)
            y_ref[rs, :] = _pack_halves(_dot(act.astype(BF16), wob_ref[...]) + bo_ref[0])

        @pl.when(n_valid <= first)
        def _():
            y_ref[rs, :] = jnp.zeros((EXPERT_SUB_ROWS, y_ref.shape[1]), y_ref.dtype)


def _experts(xs, blk_e, n_valid, next_e, layer, w_glu, b_glu, w_lin, b_lin, w_out, b_out, *,
             blk):
    P, pw = xs.shape
    L, E, D, F = w_glu.shape
    n_blocks = P // blk
    bmap = lambda i, be, nv, ne: (layer * E + be[i], 0, 0)
    rows = lambda i, be, nv, ne: (i, 0)
    w_glu, w_lin = w_glu.reshape(L * E, D, F), w_lin.reshape(L * E, D, F)
    w_out = w_out.reshape(L * E, F, D)
    b_glu, b_lin = b_glu.reshape(L * E, 1, F), b_lin.reshape(L * E, 1, F)
    b_out = b_out.reshape(L * E, 1, D)
    hbm = pl.BlockSpec(memory_space=pl.ANY)
    grid_spec = pltpu.PrefetchScalarGridSpec(
        num_scalar_prefetch=3,
        grid=(n_blocks,),
        in_specs=[pl.BlockSpec((blk, pw), rows),
                  hbm, pl.BlockSpec((1, 1, F), bmap),
                  hbm, pl.BlockSpec((1, 1, F), bmap),
                  hbm, pl.BlockSpec((1, 1, D), bmap)],
        out_specs=pl.BlockSpec((blk, pw), rows),
        scratch_shapes=[pltpu.VMEM((D, F), F32), pltpu.VMEM((D, F), F32),
                        pltpu.VMEM((F, D), F32),
                        pltpu.VMEM((D, F), BF16), pltpu.VMEM((D, F), BF16),
                        pltpu.VMEM((F, D), BF16),
                        pltpu.SemaphoreType.DMA((3,))],
    )
    return pl.pallas_call(
        functools.partial(_expert_kernel, expert_base=layer * E),
        grid_spec=grid_spec,
        out_shape=jax.ShapeDtypeStruct((P, pw), xs.dtype),
        compiler_params=_cparams(("arbitrary",)),
        name="moe_experts",
    )(blk_e, n_valid, next_e, xs, w_glu, b_glu, w_lin, b_lin, w_out, b_out)


def _combine_kernel(ys_ref, tg_ref, x_ref, gate_ref, *rest, tail):
    half = x_ref.shape[1] // 2
    tg = tg_ref[...]
    lo_sum = hi_sum = None
    for k in range(ys_ref.shape[0]):
        lo, hi = _unpack_halves(ys_ref[k])
        gk = tg[:, k:k + 1]
        lo_sum = gk * lo if lo_sum is None else lo_sum + gk * lo
        hi_sum = gk * hi if hi_sum is None else hi_sum + gk * hi
    x_lo = x_ref[:, :half] + gate_ref[0, :, :half] * lo_sum
    x_hi = x_ref[:, half:] + gate_ref[0, :, half:] * hi_sum
    ms = (jnp.sum(x_lo * x_lo, axis=-1, keepdims=True)
          + jnp.sum(x_hi * x_hi, axis=-1, keepdims=True)) / x_ref.shape[1]
    inv = lax.rsqrt(ms + RMS_EPS)
    if tail == "final":
        g_ref, o_ref = rest
        o_ref[:, :half] = x_lo * inv * g_ref[:, :half]
        o_ref[:, half:] = x_hi * inv * g_ref[:, half:]
    else:
        g_ref, sc_ref, sh_ref, o_ref, h_ref = rest
        o_ref[:, :half] = x_lo
        o_ref[:, half:] = x_hi
        for cols, xs in ((slice(0, half), x_lo), (slice(half, 2 * half), x_hi)):
            h = xs * inv * g_ref[:, cols] * (1.0 + sc_ref[0, :, cols]) + sh_ref[0, :, cols]
            h_ref[:, cols] = h.astype(BF16)


def _combine(ys, tok_gate, x, gate, seq, *, final_g=None, next_norm=None, tm=256):
    K, T, pw = ys.shape
    D = x.shape[1]
    bmap = lambda i: ((i * tm) // seq, 0, 0)
    rows = pl.BlockSpec((tm, D), lambda i: (i, 0))
    vec = pl.BlockSpec((1, D), lambda i: (0, 0))
    per_batch = pl.BlockSpec((1, 1, D), bmap)
    in_specs = [pl.BlockSpec((K, tm, pw), lambda i: (0, i, 0)),
                pl.BlockSpec((tm, K), lambda i: (i, 0)), rows, per_batch]
    args = [ys, tok_gate, x, gate]
    if final_g is not None:
        tail = "final"
        in_specs.append(vec)
        args.append(final_g)
        out_specs, out_shape = rows, jax.ShapeDtypeStruct((T, D), F32)
    else:
        tail = "next"
        in_specs += [vec, per_batch, per_batch]
        args += list(next_norm)
        out_specs = [rows, rows]
        out_shape = [jax.ShapeDtypeStruct((T, D), F32), jax.ShapeDtypeStruct((T, D), BF16)]
    return pl.pallas_call(
        functools.partial(_combine_kernel, tail=tail),
        grid=(T // tm,),
        in_specs=in_specs,
        out_specs=out_specs,
        out_shape=out_shape,
        compiler_params=_cparams(("parallel",)),
        name="moe_combine",
    )(*args)


MOE_BLOCK = 512


def _slots_kernel(start_ref, idx_ref, rank_ref, o_ref):
    idx = idx_ref[...]
    acc = rank_ref[...]
    for e in range(start_ref.shape[0]):
        acc = acc + jnp.where(idx == e, start_ref[e], 0)
    o_ref[...] = acc


def _slots(idx_t, rank_t, group_start):
    full = pl.BlockSpec(idx_t.shape, lambda i, gs: (0, 0))
    return pl.pallas_call(
        _slots_kernel,
        grid_spec=pltpu.PrefetchScalarGridSpec(num_scalar_prefetch=1, grid=(1,),
                                               in_specs=[full, full], out_specs=full),
        out_shape=jax.ShapeDtypeStruct(idx_t.shape, I32),
        compiler_params=_cparams(("arbitrary",)),
        name="moe_slots",
    )(group_start, idx_t, rank_t)


def _dispatch_plan(idx_t, rank_t, counts, blk):
    K, T = idx_t.shape
    E = counts.shape[0]
    padded = (counts + blk - 1) // blk * blk
    pad_end = jnp.cumsum(padded)
    pad_start = pad_end - padded
    experts = jnp.arange(E, dtype=I32)
    slots = _slots(idx_t, rank_t, pad_start.astype(I32))
    n_blocks = -(-(K * T + E * (blk - 1)) // blk)
    blk_start = jnp.arange(n_blocks, dtype=I32) * blk
    blk_e = jnp.minimum(jnp.sum(pad_end[None, :] <= blk_start[:, None], axis=1), E - 1).astype(I32)
    n_valid = jnp.clip(pad_start[blk_e] + counts[blk_e] - blk_start, 0, blk).astype(I32)
    later = jnp.where(counts > 0, experts, E)
    after = lax.cummin(jnp.concatenate([later[1:], jnp.full((1,), E, I32)]), reverse=True)
    next_e = jnp.where(after < E, after, -1).astype(I32)[blk_e]
    return slots, blk_e, n_valid, next_e, n_blocks * blk


def _mixer_out_moe(y, w_mix, x, mix_gate, g, scale, shift, gate, r_w, r_b, layer,
                   w_glu, b_glu, w_lin, b_lin, w_out, b_out, seq, **tail):
    T, D = x.shape
    x, hp, idx_t, gate_t, rank_t, counts = _mixer_out_route(
        y, w_mix, x, mix_gate, g, scale, shift, r_w, r_b, seq)
    slots, blk_e, n_valid, next_e, n_rows = _dispatch_plan(idx_t, rank_t, counts[:, 0],
                                                           MOE_BLOCK)
    xs = _sc_scatter_rows(hp, slots, n_rows)
    yp = _experts(xs, blk_e, n_valid, next_e, layer, w_glu, b_glu, w_lin, b_lin, w_out, b_out,
                  blk=MOE_BLOCK)
    ys = _sc_gather(yp, slots.reshape(TOP_K * T)).reshape(TOP_K, T, yp.shape[1])
    return _combine(ys, jnp.transpose(gate_t), x, gate, seq, **tail)


def kernel(x, c, norm_g, ada_w, final_norm_g, sg_w_in, sg_ln_g, sg_ln_b, sg_w_s, sg_b_s, sg_w_out, pool_w_in, pool_w_grp, pool_ls, pool_w_out, hgrn_w_in, hgrn_lb_logits, hgrn_gnorm_g, hgrn_w_out, router_w, router_b, expert_w_glu, expert_b_glu, expert_w_lin, expert_b_lin, expert_w_out, expert_b_out):
    B, S, D = x.shape
    depth = norm_g.shape[0]
    T = B * S
    xf = x.reshape(T, D)

    ada_w = ada_w.reshape(depth * 2, D, 3 * D)

    def mod_parts(layer, sub):
        m = _ada_mod(c, ada_w, layer * 2 + sub)[:, None, :]
        return m[..., :D], m[..., D:2 * D], m[..., 2 * D:]

    def mixer_norm(layer, shift, scale):
        return norm_g[layer, 0][None, :], scale, shift

    lb_cum = jnp.cumsum(jax.nn.softmax(hgrn_lb_logits.astype(F32), axis=0), axis=0)
    lower_bounds = lb_cum - lb_cum[0]

    shift, scale, mix_gate = mod_parts(0, 0)
    src, norm = xf, mixer_norm(0, shift, scale)
    for layer in range(depth):
        kind, slot = layer % 3, layer // 3
        if kind == 0:
            z = _norm_matmul(src, norm, sg_w_in[slot].astype(BF16), S, act="gelu",
                             out_dtype=BF16)
            y = _spatial_gate(z, sg_ln_g[slot][None, :], sg_ln_b[slot][None, :],
                              sg_w_s[slot], sg_b_s[slot])
            w_out = sg_w_out[slot]
        elif kind == 1:
            z = _norm_matmul(src, norm, pool_w_in[slot].astype(BF16), S, act=None,
                             out_dtype=F32)
            y = _pool_mix(z, pool_w_grp[slot].astype(BF16), pool_ls[slot][None, :], S)
            w_out = pool_w_out[slot]
        else:
            proj, f_pre = _norm_matmul(src, norm, hgrn_w_in[slot].astype(BF16), S, act=None,
                                       out_dtype=BF16, f32_col=1, tn=D)
            y = _hgrn_mix(proj, f_pre, lower_bounds[layer][None, :],
                          hgrn_gnorm_g[slot][None, :], B, S)
            w_out = hgrn_w_out[slot]

        moe_shift, moe_scale, moe_gate = mod_parts(layer, 1)
        moe_args = (y, w_out.astype(BF16), xf, mix_gate, norm_g[layer, 1][None, :],
                    moe_scale, moe_shift, moe_gate, router_w[layer], router_b[layer], layer,
                    expert_w_glu, expert_b_glu, expert_w_lin, expert_b_lin,
                    expert_w_out, expert_b_out, S)
        if layer == depth - 1:
            xf = _mixer_out_moe(*moe_args, final_g=final_norm_g[None, :])
        else:
            shift, scale, mix_gate = mod_parts(layer + 1, 0)
            xf, src = _mixer_out_moe(*moe_args, next_norm=mixer_norm(layer + 1, shift, scale))
            norm = None

    return xf.reshape(B, S, D)
```

```python
import functools

import numpy as np
import jax
import jax.numpy as jnp
from jax import lax
from jax.experimental import pallas as pl
from jax.experimental.pallas import tpu as pltpu
from jax.experimental.pallas import tpu_sc as plsc

F32 = jnp.float32
BF16 = jnp.bfloat16
I32 = jnp.int32
U32 = jnp.uint32

RMS_EPS = 1e-5
LN_EPS = 1e-5
SWIGLU_ALPHA = 1.702
SWIGLU_LIMIT = 7.0
TOP_K = 4
SG_CHUNK = 128
POOL_WINDOWS = (2, 4, 8, 16)
HEAD_DIM = 128
LANES = 128
SC_WORKERS = 32
SC_ROWS_PER_STEP = 64

VMEM_LIMIT = 48 * 1024 * 1024
BIG_VMEM_LIMIT = 56 * 1024 * 1024


def _cparams(sem):
    return pltpu.CompilerParams(dimension_semantics=sem, vmem_limit_bytes=VMEM_LIMIT)


def _dot(a, b):
    return jnp.dot(a, b, preferred_element_type=F32)


def _dot_nt(a, b):
    return lax.dot_general(a, b, (((1,), (1,)), ((), ())), preferred_element_type=F32)


def _dot_tn(a, b):
    return lax.dot_general(a, b, (((0,), (0,)), ((), ())), preferred_element_type=F32)


def _split_bf16(x):
    hi = x.astype(BF16)
    lo = (x - hi.astype(F32)).astype(BF16)
    return hi, lo


def _packed_layout(d):
    return d // 2, I32


def _pack_halves(y):
    n = y.shape[1] // 2
    lo = lax.bitcast_convert_type(y[:, :n].astype(BF16).astype(F32), U32)
    hi = lax.bitcast_convert_type(y[:, n:].astype(BF16).astype(F32), U32)
    word = lax.shift_right_logical(lo, jnp.uint32(16)) | (hi & jnp.uint32(0xFFFF0000))
    return lax.bitcast_convert_type(word, I32)


def _unpack_halves(w):
    u = lax.bitcast_convert_type(w, U32)
    lo = lax.bitcast_convert_type(lax.shift_left(u, jnp.uint32(16)), F32)
    hi = lax.bitcast_convert_type(u & jnp.uint32(0xFFFF0000), F32)
    return lo, hi


def _modulated_norm(x, g, scale, shift):
    ms = jnp.mean(x * x, axis=-1, keepdims=True)
    y = x * lax.rsqrt(ms + RMS_EPS) * g
    return y * (1.0 + scale) + shift


def _ada_kernel(c_ref, w_ref, o_ref):
    w = w_ref[0]
    reps = w.shape[1] // LANES
    for b in range(c_ref.shape[0]):
        cb = c_ref[b]
        s = jnp.tile(cb * jax.nn.sigmoid(cb), (1, reps))
        o_ref[b] = jnp.sum(w * s, axis=0, keepdims=True)


def _ada_mod(c, ada_w, index):
    L, D, N = ada_w.shape
    B = c.shape[0]
    tn = 512
    c_cols = jnp.broadcast_to(c[:, :, None], (B, D, LANES))
    out = pl.pallas_call(
        _ada_kernel,
        grid=(N // tn,),
        in_specs=[pl.BlockSpec((B, D, LANES), lambda j: (0, 0, 0)),
                  pl.BlockSpec((1, D, tn), lambda j: (index, 0, j))],
        out_specs=pl.BlockSpec((B, 1, tn), lambda j: (0, 0, j)),
        out_shape=jax.ShapeDtypeStruct((B, 1, N), F32),
        compiler_params=_cparams(("parallel",)),
        name="ada_mod",
    )(c_cols, ada_w)
    return out[:, 0]


def _nm_kernel(*refs, act, f32_col, prenormed):
    if prenormed:
        h_ref, w_ref, o_ref = refs[:3]
        outs = refs[3:]
    else:
        x_ref, g_ref, sc_ref, sh_ref, w_ref, o_ref = refs[:6]
        outs, h_ref = refs[6:-1], refs[-1]

        @pl.when(pl.program_id(1) == 0)
        def _():
            h = _modulated_norm(x_ref[...], g_ref[...], sc_ref[0], sh_ref[0])
            h_ref[...] = h.astype(BF16)

    acc = _dot(h_ref[...], w_ref[...])
    if act == "gelu":
        acc = 0.5 * acc * (1.0 + lax.erf(acc * np.float32(1.0 / np.sqrt(2.0))))
    o_ref[...] = acc.astype(o_ref.dtype)
    if f32_col is not None:
        @pl.when(pl.program_id(1) == f32_col)
        def _():
            outs[0][...] = acc


def _norm_matmul(x, norm, w, seq, *, act, out_dtype, f32_col=None, tm=512, tn=2048):
    T, D = x.shape
    N = w.shape[1]
    prenormed = norm is None
    bmap = lambda i, j: ((i * tm) // seq, 0, 0)
    out_specs = [pl.BlockSpec((tm, tn), lambda i, j: (i, j))]
    out_shape = [jax.ShapeDtypeStruct((T, N), out_dtype)]
    if f32_col is not None:
        out_specs.append(pl.BlockSpec((tm, tn), lambda i, j: (i, 0)))
        out_shape.append(jax.ShapeDtypeStruct((T, tn), F32))
    in_specs = [pl.BlockSpec((tm, D), lambda i, j: (i, 0))]
    args = [x]
    if not prenormed:
        in_specs += [pl.BlockSpec((1, D), lambda i, j: (0, 0)),
                     pl.BlockSpec((1, 1, D), bmap),
                     pl.BlockSpec((1, 1, D), bmap)]
        args += list(norm)
    in_specs.append(pl.BlockSpec((D, tn), lambda i, j: (0, j)))
    args.append(w)
    outs = pl.pallas_call(
        functools.partial(_nm_kernel, act=act, f32_col=f32_col, prenormed=prenormed),
        grid=(T // tm, N // tn),
        in_specs=in_specs,
        out_specs=out_specs,
        out_shape=out_shape,
        scratch_shapes=[] if prenormed else [pltpu.VMEM((tm, D), BF16)],
        compiler_params=pltpu.CompilerParams(dimension_semantics=("parallel", "arbitrary"),
                                             vmem_limit_bytes=BIG_VMEM_LIMIT),
        name="norm_matmul_" + str(act),
    )(*args)
    return outs if f32_col is not None else outs[0]


def _sg_kernel(u_ref, v_ref, lng_ref, lnb_ref, ws_ref, bst_ref, y_ref, vn_ref, *, heads):
    v = v_ref[...].astype(F32)
    mu = jnp.mean(v, axis=-1, keepdims=True)
    vc = v - mu
    var = jnp.mean(vc * vc, axis=-1, keepdims=True)
    vn_ref[...] = (vc * lax.rsqrt(var + LN_EPS) * lng_ref[...] + lnb_ref[...]).astype(BF16)

    n_chunks = v_ref.shape[0] // SG_CHUNK
    row = lax.broadcasted_iota(I32, (SG_CHUNK, SG_CHUNK), 0)
    col = lax.broadcasted_iota(I32, (SG_CHUNK, SG_CHUNK), 1)
    causal = row >= col
    for hd in range(heads):
        cs = slice(hd * HEAD_DIM, (hd + 1) * HEAD_DIM)
        wm = jnp.where(causal, ws_ref[hd], 0.0).astype(BF16)
        bias = bst_ref[:, hd:hd + 1]
        for ci in range(n_chunks):
            rs = slice(ci * SG_CHUNK, (ci + 1) * SG_CHUNK)
            mixed = _dot(wm, vn_ref[rs, cs]) + bias
            y_ref[rs, cs] = (u_ref[rs, cs].astype(F32) * mixed).astype(BF16)


def _spatial_gate(z, ln_g, ln_b, w_s, b_s, *, tm=512):
    T, two_w = z.shape
    W = two_w // 2
    heads = w_s.shape[0]
    return pl.pallas_call(
        functools.partial(_sg_kernel, heads=heads),
        grid=(T // tm,),
        in_specs=[pl.BlockSpec((tm, W), lambda i: (i, 0)),
                  pl.BlockSpec((tm, W), lambda i: (i, 1)),
                  pl.BlockSpec((1, W), lambda i: (0, 0)),
                  pl.BlockSpec((1, W), lambda i: (0, 0)),
                  pl.BlockSpec((heads, SG_CHUNK, SG_CHUNK), lambda i: (0, 0, 0)),
                  pl.BlockSpec((SG_CHUNK, heads), lambda i: (0, 0))],
        out_specs=pl.BlockSpec((tm, W), lambda i: (i, 0)),
        out_shape=jax.ShapeDtypeStruct((T, W), BF16),
        scratch_shapes=[pltpu.VMEM((tm, W), BF16)],
        compiler_params=_cparams(("parallel",)),
        name="spatial_gate",
    )(z, z, ln_g, ln_b, w_s, jnp.transpose(b_s))


POOL_HALO = 16


def _pool_kernel(z_ref, halo_ref, wg_ref, ls_ref, y_ref, *, seq):
    tm = z_ref.shape[0]
    gdim = wg_ref.shape[1]
    pos0 = (pl.program_id(0) * tm) % seq
    halo_on = (pos0 > 0).astype(F32)
    pos = pos0 + lax.broadcasted_iota(I32, (tm, 1), 0)
    for gi, wnd in enumerate(POOL_WINDOWS):
        cs = slice(gi * gdim, (gi + 1) * gdim)
        zg = z_ref[:, cs]
        s = jnp.concatenate([halo_ref[:, cs] * halo_on, zg], axis=0)
        k = 1
        while k < wnd:
            s = s + pltpu.roll(s, k, 0)
            k *= 2
        cnt = jnp.minimum(pos + 1, wnd).astype(F32)
        pooled = s[POOL_HALO:, :] / cnt - zg
        yg = _dot(pooled.astype(BF16), wg_ref[gi]) * ls_ref[:, cs]
        y_ref[:, cs] = yg.astype(BF16)


def _pool_mix(z, w_grp, ls, seq, *, tm=512):
    T, D = z.shape
    G, gdim, _ = w_grp.shape
    assert max(POOL_WINDOWS) <= POOL_HALO and tm % POOL_HALO == 0
    ratio = tm // POOL_HALO
    return pl.pallas_call(
        functools.partial(_pool_kernel, seq=seq),
        grid=(T // tm,),
        in_specs=[pl.BlockSpec((tm, D), lambda i: (i, 0)),
                  pl.BlockSpec((POOL_HALO, D), lambda i: (jnp.maximum(i * ratio - 1, 0), 0)),
                  pl.BlockSpec((G, gdim, gdim), lambda i: (0, 0, 0)),
                  pl.BlockSpec((1, D), lambda i: (0, 0))],
        out_specs=pl.BlockSpec((tm, D), lambda i: (i, 0)),
        out_shape=jax.ShapeDtypeStruct((T, D), BF16),
        compiler_params=_cparams(("parallel",)),
        name="pool_mix",
    )(z, z, w_grp, ls)


HGRN_CHUNK = 128
HGRN_BASE = 32
HGRN_GUARD_LOG2 = 100.0
LOG2_E = 1.4426950408889634


def _hgrn_tables(c=HGRN_CHUNK, base=HGRN_BASE):
    t = np.arange(c)
    j = t[None, :]
    sums = [j <= t[:, None]]
    masks = []
    n_shallow = 0
    h = c // 2
    while h >= 1:
        off = t % (2 * h)
        mid = (t // (2 * h)) * (2 * h) + h
        second = off >= h
        if h >= 2:
            m_q = second[:, None] & (j >= mid[:, None]) & (j <= t[:, None])
            m_k = (~second)[:, None] & (j > t[:, None]) & (j < mid[:, None])
            sums.append(m_q | m_k)
        same = (t[:, None] // (2 * h)) == (t[None, :] // (2 * h))
        masks.append(same & second[:, None] & (~second)[None, :])
        n_shallow += h >= base
        h //= 2
    same_base = (t[:, None] // base) == (t[None, :] // base)
    sums.append(same_base & (j <= t[:, None]))
    sums = np.stack(sums).astype(np.float32)
    base_mask = (same_base & (j <= t[:, None])).astype(np.float32)
    return (np.concatenate([sums, sums], axis=2), np.stack(masks).astype(np.float32), base_mask,
            n_shallow)


def _hgrn_kernel(q_ref, f_ref, i_ref, g_ref, lb_ref, gn_ref, sums_ref, masks_ref, bmask_ref,
                 o_ref, st_ref, qb_ref, ke_ref, z_ref, qd_ref, kd_ref, vb_ref, eb_ref, sc_ref,
                 *, heads, n_shallow):
    c = q_ref.shape[0]
    n_sum = sums_ref.shape[0] - 1
    n_lvl = masks_ref.shape[0]

    @pl.when(pl.program_id(1) == 0)
    def _():
        st_ref[...] = jnp.zeros_like(st_ref)

    lb = lb_ref[...]
    sig = jax.nn.sigmoid(f_ref[...])
    fg = lb + (1.0 - lb) * sig
    lf2 = jnp.log(fg) * np.float32(LOG2_E)
    kk = (1.0 - lb) * (1.0 - sig)
    lf_hi, lf_lo = _split_bf16(lf2)
    lf_cat = jnp.concatenate([lf_hi, lf_lo], axis=0)
    qr = q_ref[...].astype(F32)
    q = qr * jax.nn.sigmoid(qr)
    vb_ref[...] = i_ref[...].astype(BF16)
    q16 = q.astype(BF16)
    k16 = kk.astype(BF16)

    b2 = _dot(sums_ref[0], lf_cat)
    eb = jnp.exp2(b2)
    qb_ref[...] = (q * eb).astype(BF16)
    eb_ref[...] = eb[c - 1:c, :]
    ke_ref[...] = (kk * jnp.exp2(b2[c - 1:c, :] - b2)).astype(BF16)
    w2 = _dot(sums_ref[n_sum], lf_cat)
    shallow = jnp.min(w2) >= -HGRN_GUARD_LOG2

    row = lax.broadcasted_iota(I32, (c, 1), 0)

    def level(l):
        half = c >> (l + 1)
        second = (row & (2 * half - 1)) >= half
        if l + 1 < n_sum:
            x = jnp.exp2(_dot(sums_ref[l + 1], lf_cat).astype(BF16))
        else:
            x = jnp.where(second, fg, 1.0).astype(BF16)
        z_ref[l] = jnp.where(second, q16, k16) * x

    gn = gn_ref[...]

    def finish(hd, cs, o):
        st = st_ref[hd]
        vh = vb_ref[:, cs]
        o = o + _dot_nt(qb_ref[:, cs], st.astype(BF16))
        st_ref[hd] = st * eb_ref[:, cs] + _dot_tn(vh, ke_ref[:, cs])
        on = o * lax.rsqrt(jnp.mean(o * o, axis=-1, keepdims=True) + RMS_EPS) * gn
        gv = g_ref[:, cs].astype(F32)
        o_ref[:, cs] = (on * (gv * jax.nn.sigmoid(gv))).astype(BF16)

    def level_scores(cs, n):
        scores = None
        for l in range(n):
            half = c >> (l + 1)
            zl = z_ref[l, :, cs]
            if half >= 16:
                starts = list(range(half, c, 2 * half))
                zq = jnp.concatenate([z_ref[l, r:r + half, cs] for r in starts], axis=0)
                p = _dot_nt(zq, zl).astype(BF16)
                parts = []
                for n_i, r in enumerate(starts):
                    parts.append(jnp.zeros((half, c), BF16))
                    parts.append(masks_ref[l, r:r + half, :] * p[n_i * half:(n_i + 1) * half])
                term = jnp.concatenate(parts, axis=0)
            else:
                term = masks_ref[l] * _dot_nt(zl, zl).astype(BF16)
            scores = term if scores is None else scores + term
        return scores

    @pl.when(shallow)
    def _():
        for l in range(n_shallow):
            level(l)
        qd_ref[...] = (q * jnp.exp2(w2)).astype(BF16)
        kd_ref[...] = (kk * jnp.exp2(-w2)).astype(BF16)

        def head_scores(hd, carry):
            cs = pl.ds(pl.multiple_of(hd * HEAD_DIM, HEAD_DIM), HEAD_DIM)
            sc_ref[hd] = (level_scores(cs, n_shallow)
                          + bmask_ref[...] * _dot_nt(qd_ref[:, cs], kd_ref[:, cs]).astype(BF16))
            return carry

        def head_out(hd, carry):
            cs = pl.ds(pl.multiple_of(hd * HEAD_DIM, HEAD_DIM), HEAD_DIM)
            finish(hd, cs, _dot(sc_ref[hd], vb_ref[:, cs]))
            return carry

        lax.fori_loop(0, heads, head_scores, 0, unroll=16)
        lax.fori_loop(0, heads, head_out, 0, unroll=16)

    @pl.when(jnp.logical_not(shallow))
    def _():
        for l in range(n_lvl):
            level(l)
        qd_ref[...] = q16
        kd_ref[...] = k16

        def head(hd, carry):
            cs = pl.ds(pl.multiple_of(hd * HEAD_DIM, HEAD_DIM), HEAD_DIM)
            vh = vb_ref[:, cs]
            diag = jnp.sum(qd_ref[:, cs].astype(F32) * kd_ref[:, cs].astype(F32), axis=-1,
                           keepdims=True)
            finish(hd, cs, _dot(level_scores(cs, n_lvl), vh) + diag * vh.astype(F32))
            return carry

        lax.fori_loop(0, heads, head, 0, unroll=4)


def _hgrn_mix(proj, f_pre, lb, gnorm_g, batch, seq):
    T, four_d = proj.shape
    D = four_d // 4
    heads = D // HEAD_DIM
    c = HGRN_CHUNK
    n_chunks = seq // c
    sums, masks, base_mask, n_shallow = _hgrn_tables(c)
    n_sum, n_mask = sums.shape[0], masks.shape[0]
    row = lambda b, i: b * n_chunks + i
    return pl.pallas_call(
        functools.partial(_hgrn_kernel, heads=heads, n_shallow=n_shallow),
        grid=(batch, n_chunks),
        in_specs=[pl.BlockSpec((c, D), lambda b, i: (row(b, i), 0)),
                  pl.BlockSpec((c, D), lambda b, i: (row(b, i), 0)),
                  pl.BlockSpec((c, D), lambda b, i: (row(b, i), 2)),
                  pl.BlockSpec((c, D), lambda b, i: (row(b, i), 3)),
                  pl.BlockSpec((1, D), lambda b, i: (0, 0)),
                  pl.BlockSpec((1, HEAD_DIM), lambda b, i: (0, 0)),
                  pl.BlockSpec((n_sum, c, 2 * c), lambda b, i: (0, 0, 0)),
                  pl.BlockSpec((n_mask, c, c), lambda b, i: (0, 0, 0)),
                  pl.BlockSpec((c, c), lambda b, i: (0, 0))],
        out_specs=pl.BlockSpec((c, D), lambda b, i: (row(b, i), 0)),
        out_shape=jax.ShapeDtypeStruct((T, D), BF16),
        scratch_shapes=[pltpu.VMEM((heads, HEAD_DIM, HEAD_DIM), F32),
                        pltpu.VMEM((c, D), BF16),
                        pltpu.VMEM((c, D), BF16),
                        pltpu.VMEM((n_mask, c, D), BF16),
                        pltpu.VMEM((c, D), BF16),
                        pltpu.VMEM((c, D), BF16),
                        pltpu.VMEM((c, D), BF16),
                        pltpu.VMEM((1, D), F32),
                        pltpu.VMEM((heads, c, c), BF16)],
        compiler_params=_cparams(("parallel", "arbitrary")),
        name="hgrn_mix",
    )(proj, f_pre, proj, proj, lb, gnorm_g, jnp.asarray(sums, BF16), jnp.asarray(masks, BF16),
      jnp.asarray(base_mask, BF16))


def _route_kernel(y_ref, w_ref, xin_ref, mgate_ref, g_ref, sc_ref, sh_ref, rw_ref,
                  rb_ref, tri_ref, x_ref, hp_ref, idx_ref, gate_ref, rank_ref, cnt_ref, carry_ref,
                  *, n_experts):
    @pl.when(pl.program_id(0) == 0)
    def _():
        carry_ref[...] = jnp.zeros_like(carry_ref)

    x = xin_ref[...] + mgate_ref[0] * _dot(y_ref[...], w_ref[...])
    x_ref[...] = x
    h = _modulated_norm(x, g_ref[...], sc_ref[0], sh_ref[0])
    hp_ref[...] = _pack_halves(h)

    h_hi, h_lo = _split_bf16(h)
    both = _dot(h_hi, rw_ref[...])
    logits = (both[:, :LANES] + both[:, LANES:] + _dot(h_lo, rw_ref[:, :LANES])) + rb_ref[...]
    lt = jnp.transpose(logits)[:n_experts, :]
    eidx = lax.broadcasted_iota(I32, lt.shape, 0)
    vals, idxs, hits = [], [], []
    for _ in range(TOP_K):
        m = jnp.max(lt, axis=0, keepdims=True)
        sel = jnp.min(jnp.where(lt == m, eidx, n_experts), axis=0, keepdims=True)
        hit = eidx == sel
        vals.append(m)
        idxs.append(sel)
        hits.append(hit)
        lt = jnp.where(hit, -jnp.inf, lt)
    exps = [jnp.exp(v - vals[0]) for v in vals]
    denom = exps[0]
    for e in exps[1:]:
        denom = denom + e
    idx_ref[...] = jnp.concatenate(idxs, axis=0)
    gate_ref[...] = jnp.concatenate([e / denom for e in exps], axis=0)

    onehot = hits[0].astype(F32)
    for hit in hits[1:]:
        onehot = onehot + hit.astype(F32)
    before = carry_ref[...] + _dot(onehot.astype(BF16), tri_ref[...])
    rank_ref[...] = jnp.concatenate(
        [jnp.sum(jnp.where(hit, before, 0.0), axis=0, keepdims=True) for hit in hits],
        axis=0).astype(I32)
    carry_ref[...] += jnp.sum(onehot, axis=1, keepdims=True)
    cnt_ref[...] = carry_ref[...].astype(I32)


def _mixer_out_route(y, w, x, mix_gate, g, scale, shift, r_w, r_b, seq, *, tm=256):
    T, K = y.shape
    D = w.shape[1]
    E = r_w.shape[1]
    rw_pad = jnp.zeros((D, LANES), F32).at[:, :E].set(r_w)
    rw_cat = jnp.concatenate(_split_bf16(rw_pad), axis=1)
    rb_pad = jnp.full((1, LANES), -1e30, F32).at[0, :E].set(r_b)
    tri = jnp.asarray(np.triu(np.ones((tm, tm), np.float32), k=1), BF16)
    pw, pdt = _packed_layout(D)
    bmap = lambda i: ((i * tm) // seq, 0, 0)
    const = lambda i: (0, 0)
    rows = lambda i: (i, 0)
    kt = pl.BlockSpec((TOP_K, tm), lambda i: (0, i))
    return pl.pallas_call(
        functools.partial(_route_kernel, n_experts=E),
        grid=(T // tm,),
        in_specs=[pl.BlockSpec((tm, K), rows),
                  pl.BlockSpec((K, D), const),
                  pl.BlockSpec((tm, D), rows),
                  pl.BlockSpec((1, 1, D), bmap),
                  pl.BlockSpec((1, D), const),
                  pl.BlockSpec((1, 1, D), bmap),
                  pl.BlockSpec((1, 1, D), bmap),
                  pl.BlockSpec((D, 2 * LANES), const),
                  pl.BlockSpec((1, LANES), const),
                  pl.BlockSpec((tm, tm), const)],
        out_specs=[pl.BlockSpec((tm, D), rows), pl.BlockSpec((tm, pw), rows), kt, kt, kt,
                   pl.BlockSpec((E, 1), const)],
        out_shape=[jax.ShapeDtypeStruct((T, D), F32),
                   jax.ShapeDtypeStruct((T, pw), pdt),
                   jax.ShapeDtypeStruct((TOP_K, T), I32),
                   jax.ShapeDtypeStruct((TOP_K, T), F32),
                   jax.ShapeDtypeStruct((TOP_K, T), I32),
                   jax.ShapeDtypeStruct((E, 1), I32)],
        scratch_shapes=[pltpu.VMEM((E, 1), F32)],
        compiler_params=_cparams(("arbitrary",)),
        name="mixer_out_route",
    )(y, w, x, mix_gate, g, scale, shift, rw_cat, rb_pad, tri)


def _sc_gather(table, idx):
    R = idx.shape[0]
    W = table.shape[1]
    rows = SC_ROWS_PER_STEP // 2
    per_worker = R // SC_WORKERS
    steps = per_worker // rows
    assert steps * rows * SC_WORKERS == R and steps % 2 == 0
    mesh = plsc.VectorSubcoreMesh(core_axis_name="c", subcore_axis_name="s")
    n_cores = mesh.num_cores

    @functools.partial(
        pl.kernel, mesh=mesh,
        out_type=jax.ShapeDtypeStruct((R, W), table.dtype),
        scratch_types=[pltpu.VMEM((steps, rows), I32),
                       pltpu.VMEM((rows, W), table.dtype),
                       pltpu.VMEM((rows, W), table.dtype),
                       pltpu.SemaphoreType.DMA((2,)),
                       pltpu.SemaphoreType.DMA((2,))],
    )
    def gather(table_hbm, idx_hbm, out_hbm, idx_v, buf0, buf1, gsem, wsem):
        wid = lax.axis_index("s") * n_cores + lax.axis_index("c")
        base = wid * per_worker
        bufs = (buf0, buf1)
        pltpu.sync_copy(idx_hbm.at[pl.ds(wid * steps, steps)], idx_v)

        def fetch(j, b):
            return pltpu.make_async_copy(table_hbm.at[idx_v.at[j]], bufs[b], gsem.at[b])

        def flush(j, b):
            off = pl.multiple_of(base + j * rows, 8)
            return pltpu.make_async_copy(bufs[b], out_hbm.at[pl.ds(off, rows)], wsem.at[b])

        fetch(0, 0).start()

        @pl.loop(0, steps, step=2)
        def _(j0):
            for b in range(2):
                j = j0 + b
                fetch(j, b).wait()
                flush(j, b).start()

                @pl.when(j + 1 < steps)
                def _():
                    @pl.when(j >= 1)
                    def _():
                        flush(j - 1, 1 - b).wait()
                    fetch(j + 1, 1 - b).start()

        flush(steps - 2, 0).wait()
        flush(steps - 1, 1).wait()

    return gather(table, idx.reshape(R // rows, rows))


def _sc_scatter_rows(table, slots, n_out):
    K, T = slots.shape
    W = table.shape[1]
    rows = SC_ROWS_PER_STEP
    per_worker = T // SC_WORKERS
    steps = per_worker // rows
    assert steps * rows * SC_WORKERS == T
    mesh = plsc.VectorSubcoreMesh(core_axis_name="c", subcore_axis_name="s")
    n_cores = mesh.num_cores

    @functools.partial(
        pl.kernel, mesh=mesh,
        out_type=jax.ShapeDtypeStruct((n_out, W), table.dtype),
        scratch_types=[pltpu.VMEM((K, rows), I32),
                       pltpu.VMEM((rows, W), table.dtype)],
    )
    def scatter(table_hbm, slots_hbm, out_hbm, idx_v, rows_v):
        wid = lax.axis_index("s") * n_cores + lax.axis_index("c")
        base = wid * per_worker

        @pl.loop(0, steps)
        def _(j):
            off = pl.multiple_of(base + j * rows, 8)
            pltpu.sync_copy(table_hbm.at[pl.ds(off, rows)], rows_v)
            for k in range(K):
                pltpu.sync_copy(slots_hbm.at[pl.ds(pl.multiple_of(k * T + off, 8), rows)],
                                idx_v.at[k])
                pltpu.sync_copy(rows_v, out_hbm.at[idx_v.at[k]])

    return scatter(table, slots.reshape(K * T))


EXPERT_SUB_ROWS = 256
CAST_ELEMS = 32 * 1024
BF16_TILE_ROWS = 16


def _expert_kernel(blk_e_ref, n_valid_ref, next_e_ref, xs_ref, wg_hbm, bg_ref, wl_hbm, bl_ref,
                   wo_hbm, bo_ref, y_ref, wg_st, wl_st, wo_st, wgb_ref, wlb_ref, wob_ref, sem,
                   *, expert_base):
    i = pl.program_id(0)
    e = blk_e_ref[i]
    n_valid = n_valid_ref[i]
    new_expert = jnp.logical_or(i == 0, e != blk_e_ref[jnp.maximum(i - 1, 0)])

    def weight_copies(expert):
        idx = expert_base + expert
        return (pltpu.make_async_copy(wg_hbm.at[idx], wg_st, sem.at[0]),
                pltpu.make_async_copy(wl_hbm.at[idx], wl_st, sem.at[1]),
                pltpu.make_async_copy(wo_hbm.at[idx], wo_st, sem.at[2]))

    @pl.when(i == 0)
    def _():
        for cp in weight_copies(e):
            cp.start()

    @pl.when(jnp.logical_and(new_expert, n_valid > 0))
    def _():
        for cp in weight_copies(e):
            cp.wait()
        for src, dst in ((wg_st, wgb_ref), (wl_st, wlb_ref), (wo_st, wob_ref)):
            n_rows = CAST_ELEMS // src.shape[1] // BF16_TILE_ROWS * BF16_TILE_ROWS
            assert n_rows > 0 and src.shape[0] % n_rows == 0

            def cast_rows(r, carry, src=src, dst=dst, n_rows=n_rows):
                rows = pl.ds(pl.multiple_of(r * n_rows, n_rows), n_rows)
                dst[rows, :] = src[rows, :].astype(BF16)
                return carry

            lax.fori_loop(0, src.shape[0] // n_rows, cast_rows, 0, unroll=4)
        nxt = next_e_ref[i]

        @pl.when(nxt >= 0)
        def _():
            for cp in weight_copies(nxt):
                cp.start()

    half = wgb_ref.shape[0] // 2
    for first in range(0, xs_ref.shape[0], EXPERT_SUB_ROWS):
        rs = slice(first, first + EXPERT_SUB_ROWS)

        @pl.when(n_valid > first)
        def _():
            live = first + lax.broadcasted_iota(I32, (EXPERT_SUB_ROWS, 1), 0) < n_valid
            lo, hi = _unpack_halves(jnp.where(live, xs_ref[rs, :], 0))
            lo = lo.astype(BF16)
            hi = hi.astype(BF16)
            a = _dot(lo, wgb_ref[:half, :]) + _dot(hi, wgb_ref[half:, :]) + bg_ref[0]
            l = _dot(lo, wlb_ref[:half, :]) + _dot(hi, wlb_ref[half:, :]) + bl_ref[0]
            a = jnp.minimum(a, SWIGLU_LIMIT)
            l = jnp.clip(l, -SWIGLU_LIMIT, SWIGLU_LIMIT)
            act = a * jax.nn.sigmoid(SWIGLU_ALPHA * a) * (l + 1.0)
            y_ref[rs, :] = _pack_halves(_dot(act.astype(BF16), wob_ref[...]) + bo_ref[0])

        @pl.when(n_valid <= first)
        def _():
            y_ref[rs, :] = jnp.zeros((EXPERT_SUB_ROWS, y_ref.shape[1]), y_ref.dtype)


def _experts(xs, blk_e, n_valid, next_e, layer, w_glu, b_glu, w_lin, b_lin, w_out, b_out, *,
             blk):
    P, pw = xs.shape
    L, E, D, F = w_glu.shape
    n_blocks = P // blk
    bmap = lambda i, be, nv, ne: (layer * E + be[i], 0, 0)
    rows = lambda i, be, nv, ne: (i, 0)
    w_glu, w_lin = w_glu.reshape(L * E, D, F), w_lin.reshape(L * E, D, F)
    w_out = w_out.reshape(L * E, F, D)
    b_glu, b_lin = b_glu.reshape(L * E, 1, F), b_lin.reshape(L * E, 1, F)
    b_out = b_out.reshape(L * E, 1, D)
    hbm = pl.BlockSpec(memory_space=pl.ANY)
    grid_spec = pltpu.PrefetchScalarGridSpec(
        num_scalar_prefetch=3,
        grid=(n_blocks,),
        in_specs=[pl.BlockSpec((blk, pw), rows),
                  hbm, pl.BlockSpec((1, 1, F), bmap),
                  hbm, pl.BlockSpec((1, 1, F), bmap),
                  hbm, pl.BlockSpec((1, 1, D), bmap)],
        out_specs=pl.BlockSpec((blk, pw), rows),
        scratch_shapes=[pltpu.VMEM((D, F), F32), pltpu.VMEM((D, F), F32),
                        pltpu.VMEM((F, D), F32),
                        pltpu.VMEM((D, F), BF16), pltpu.VMEM((D, F), BF16),
                        pltpu.VMEM((F, D), BF16),
                        pltpu.SemaphoreType.DMA((3,))],
    )
    return pl.pallas_call(
        functools.partial(_expert_kernel, expert_base=layer * E),
        grid_spec=grid_spec,
        out_shape=jax.ShapeDtypeStruct((P, pw), xs.dtype),
        compiler_params=_cparams(("arbitrary",)),
        name="moe_experts",
    )(blk_e, n_valid, next_e, xs, w_glu, b_glu, w_lin, b_lin, w_out, b_out)


def _combine_kernel(ys_ref, tg_ref, x_ref, gate_ref, *rest, tail):
    half = x_ref.shape[1] // 2
    tg = tg_ref[...]
    lo_sum = hi_sum = None
    for k in range(ys_ref.shape[0]):
        lo, hi = _unpack_halves(ys_ref[k])
        gk = tg[:, k:k + 1]
        lo_sum = gk * lo if lo_sum is None else lo_sum + gk * lo
        hi_sum = gk * hi if hi_sum is None else hi_sum + gk * hi
    x_lo = x_ref[:, :half] + gate_ref[0, :, :half] * lo_sum
    x_hi = x_ref[:, half:] + gate_ref[0, :, half:] * hi_sum
    ms = (jnp.sum(x_lo * x_lo, axis=-1, keepdims=True)
          + jnp.sum(x_hi * x_hi, axis=-1, keepdims=True)) / x_ref.shape[1]
    inv = lax.rsqrt(ms + RMS_EPS)
    if tail == "final":
        g_ref, o_ref = rest
        o_ref[:, :half] = x_lo * inv * g_ref[:, :half]
        o_ref[:, half:] = x_hi * inv * g_ref[:, half:]
    else:
        g_ref, sc_ref, sh_ref, o_ref, h_ref = rest
        o_ref[:, :half] = x_lo
        o_ref[:, half:] = x_hi
        for cols, xs in ((slice(0, half), x_lo), (slice(half, 2 * half), x_hi)):
            h = xs * inv * g_ref[:, cols] * (1.0 + sc_ref[0, :, cols]) + sh_ref[0, :, cols]
            h_ref[:, cols] = h.astype(BF16)


def _combine(ys, tok_gate, x, gate, seq, *, final_g=None, next_norm=None, tm=256):
    K, T, pw = ys.shape
    D = x.shape[1]
    bmap = lambda i: ((i * tm) // seq, 0, 0)
    rows = pl.BlockSpec((tm, D), lambda i: (i, 0))
    vec = pl.BlockSpec((1, D), lambda i: (0, 0))
    per_batch = pl.BlockSpec((1, 1, D), bmap)
    in_specs = [pl.BlockSpec((K, tm, pw), lambda i: (0, i, 0)),
                pl.BlockSpec((tm, K), lambda i: (i, 0)), rows, per_batch]
    args = [ys, tok_gate, x, gate]
    if final_g is not None:
        tail = "final"
        in_specs.append(vec)
        args.append(final_g)
        out_specs, out_shape = rows, jax.ShapeDtypeStruct((T, D), F32)
    else:
        tail = "next"
        in_specs += [vec, per_batch, per_batch]
        args += list(next_norm)
        out_specs = [rows, rows]
        out_shape = [jax.ShapeDtypeStruct((T, D), F32), jax.ShapeDtypeStruct((T, D), BF16)]
    return pl.pallas_call(
        functools.partial(_combine_kernel, tail=tail),
        grid=(T // tm,),
        in_specs=in_specs,
        out_specs=out_specs,
        out_shape=out_shape,
        compiler_params=_cparams(("parallel",)),
        name="moe_combine",
    )(*args)


MOE_BLOCK = 512


def _slots_kernel(start_ref, idx_ref, rank_ref, o_ref):
    idx = idx_ref[...]
    acc = rank_ref[...]
    for e in range(start_ref.shape[0]):
        acc = acc + jnp.where(idx == e, start_ref[e], 0)
    o_ref[...] = acc


def _slots(idx_t, rank_t, group_start):
    full = pl.BlockSpec(idx_t.shape, lambda i, gs: (0, 0))
    return pl.pallas_call(
        _slots_kernel,
        grid_spec=pltpu.PrefetchScalarGridSpec(num_scalar_prefetch=1, grid=(1,),
                                               in_specs=[full, full], out_specs=full),
        out_shape=jax.ShapeDtypeStruct(idx_t.shape, I32),
        compiler_params=_cparams(("arbitrary",)),
        name="moe_slots",
    )(group_start, idx_t, rank_t)


def _dispatch_plan(idx_t, rank_t, counts, blk):
    K, T = idx_t.shape
    E = counts.shape[0]
    padded = (counts + blk - 1) // blk * blk
    pad_end = jnp.cumsum(padded)
    pad_start = pad_end - padded
    experts = jnp.arange(E, dtype=I32)
    slots = _slots(idx_t, rank_t, pad_start.astype(I32))
    n_blocks = -(-(K * T + E * (blk - 1)) // blk)
    blk_start = jnp.arange(n_blocks, dtype=I32) * blk
    blk_e = jnp.minimum(jnp.sum(pad_end[None, :] <= blk_start[:, None], axis=1), E - 1).astype(I32)
    n_valid = jnp.clip(pad_start[blk_e] + counts[blk_e] - blk_start, 0, blk).astype(I32)
    later = jnp.where(counts > 0, experts, E)
    after = lax.cummin(jnp.concatenate([later[1:], jnp.full((1,), E, I32)]), reverse=True)
    next_e = jnp.where(after < E, after, -1).astype(I32)[blk_e]
    return slots, blk_e, n_valid, next_e, n_blocks * blk


def _mixer_out_moe(y, w_mix, x, mix_gate, g, scale, shift, gate, r_w, r_b, layer,
                   w_glu, b_glu, w_lin, b_lin, w_out, b_out, seq, **tail):
    T, D = x.shape
    x, hp, idx_t, gate_t, rank_t, counts = _mixer_out_route(
        y, w_mix, x, mix_gate, g, scale, shift, r_w, r_b, seq)
    slots, blk_e, n_valid, next_e, n_rows = _dispatch_plan(idx_t, rank_t, counts[:, 0],
                                                           MOE_BLOCK)
    xs = _sc_scatter_rows(hp, slots, n_rows)
    yp = _experts(xs, blk_e, n_valid, next_e, layer, w_glu, b_glu, w_lin, b_lin, w_out, b_out,
                  blk=MOE_BLOCK)
    ys = _sc_gather(yp, slots.reshape(TOP_K * T)).reshape(TOP_K, T, yp.shape[1])
    return _combine(ys, jnp.transpose(gate_t), x, gate, seq, **tail)


def kernel(x, c, norm_g, ada_w, final_norm_g, sg_w_in, sg_ln_g, sg_ln_b, sg_w_s, sg_b_s, sg_w_out, pool_w_in, pool_w_grp, pool_ls, pool_w_out, hgrn_w_in, hgrn_lb_logits, hgrn_gnorm_g, hgrn_w_out, router_w, router_b, expert_w_glu, expert_b_glu, expert_w_lin, expert_b_lin, expert_w_out, expert_b_out):
    B, S, D = x.shape
    depth = norm_g.shape[0]
    T = B * S
    xf = x.reshape(T, D)

    ada_w = ada_w.reshape(depth * 2, D, 3 * D)

    def mod_parts(layer, sub):
        m = _ada_mod(c, ada_w, layer * 2 + sub)[:, None, :]
        return m[..., :D], m[..., D:2 * D], m[..., 2 * D:]

    def mixer_norm(layer, shift, scale):
        return norm_g[layer, 0][None, :], scale, shift

    lb_cum = jnp.cumsum(jax.nn.softmax(hgrn_lb_logits.astype(F32), axis=0), axis=0)
    lower_bounds = lb_cum - lb_cum[0]

    shift, scale, mix_gate = mod_parts(0, 0)
    src, norm = xf, mixer_norm(0, shift, scale)
    for layer in range(depth):
        kind, slot = layer % 3, layer // 3
        if kind == 0:
            z = _norm_matmul(src, norm, sg_w_in[slot].astype(BF16), S, act="gelu",
                             out_dtype=BF16)
            y = _spatial_gate(z, sg_ln_g[slot][None, :], sg_ln_b[slot][None, :],
                              sg_w_s[slot], sg_b_s[slot])
            w_out = sg_w_out[slot]
        elif kind == 1:
            z = _norm_matmul(src, norm, pool_w_in[slot].astype(BF16), S, act=None,
                             out_dtype=F32)
            y = _pool_mix(z, pool_w_grp[slot].astype(BF16), pool_ls[slot][None, :], S)
            w_out = pool_w_out[slot]
        else:
            proj, f_pre = _norm_matmul(src, norm, hgrn_w_in[slot].astype(BF16), S, act=None,
                                       out_dtype=BF16, f32_col=1, tn=D)
            y = _hgrn_mix(proj, f_pre, lower_bounds[layer][None, :],
                          hgrn_gnorm_g[slot][None, :], B, S)
            w_out = hgrn_w_out[slot]

        moe_shift, moe_scale, moe_gate = mod_parts(layer, 1)
        moe_args = (y, w_out.astype(BF16), xf, mix_gate, norm_g[layer, 1][None, :],
                    moe_scale, moe_shift, moe_gate, router_w[layer], router_b[layer], layer,
                    expert_w_glu, expert_b_glu, expert_w_lin, expert_b_lin,
                    expert_w_out, expert_b_out, S)
        if layer == depth - 1:
            xf = _mixer_out_moe(*moe_args, final_g=final_norm_g[None, :])
        else:
            shift, scale, mix_gate = mod_parts(layer + 1, 0)
            xf, src = _mixer_out_moe(*moe_args, next_norm=mixer_norm(layer + 1, shift, scale))
            norm = None

    return xf.reshape(B, S, D)
```

```python
import functools

import numpy as np
import jax
import jax.numpy as jnp
from jax import lax
from jax.experimental import pallas as pl
from jax.experimental.pallas import tpu as pltpu
from jax.experimental.pallas import tpu_sc as plsc

F32 = jnp.float32
BF16 = jnp.bfloat16
I32 = jnp.int32
U32 = jnp.uint32

RMS_EPS = 1e-5
LN_EPS = 1e-5
SWIGLU_ALPHA = 1.702
SWIGLU_LIMIT = 7.0
TOP_K = 4
SG_CHUNK = 128
POOL_WINDOWS = (2, 4, 8, 16)
HEAD_DIM = 128
LANES = 128
SC_WORKERS = 32
SC_ROWS_PER_STEP = 64

VMEM_LIMIT = 48 * 1024 * 1024
BIG_VMEM_LIMIT = 56 * 1024 * 1024


def _cparams(sem):
    return pltpu.CompilerParams(dimension_semantics=sem, vmem_limit_bytes=VMEM_LIMIT)


def _dot(a, b):
    return jnp.dot(a, b, preferred_element_type=F32)


def _dot_nt(a, b):
    return lax.dot_general(a, b, (((1,), (1,)), ((), ())), preferred_element_type=F32)


def _dot_tn(a, b):
    return lax.dot_general(a, b, (((0,), (0,)), ((), ())), preferred_element_type=F32)


def _split_bf16(x):
    hi = x.astype(BF16)
    lo = (x - hi.astype(F32)).astype(BF16)
    return hi, lo


def _packed_layout(d):
    return d // 2, I32


def _pack_halves(y):
    n = y.shape[1] // 2
    lo = lax.bitcast_convert_type(y[:, :n].astype(BF16).astype(F32), U32)
    hi = lax.bitcast_convert_type(y[:, n:].astype(BF16).astype(F32), U32)
    word = lax.shift_right_logical(lo, jnp.uint32(16)) | (hi & jnp.uint32(0xFFFF0000))
    return lax.bitcast_convert_type(word, I32)


def _unpack_halves(w):
    u = lax.bitcast_convert_type(w, U32)
    lo = lax.bitcast_convert_type(lax.shift_left(u, jnp.uint32(16)), F32)
    hi = lax.bitcast_convert_type(u & jnp.uint32(0xFFFF0000), F32)
    return lo, hi


def _modulated_norm(x, g, scale, shift):
    ms = jnp.mean(x * x, axis=-1, keepdims=True)
    y = x * lax.rsqrt(ms + RMS_EPS) * g
    return y * (1.0 + scale) + shift


def _ada_kernel(c_ref, w_ref, o_ref):
    w = w_ref[0]
    reps = w.shape[1] // LANES
    for b in range(c_ref.shape[0]):
        cb = c_ref[b]
        s = jnp.tile(cb * jax.nn.sigmoid(cb), (1, reps))
        o_ref[b] = jnp.sum(w * s, axis=0, keepdims=True)


def _ada_mod(c, ada_w, index):
    L, D, N = ada_w.shape
    B = c.shape[0]
    tn = 512
    c_cols = jnp.broadcast_to(c[:, :, None], (B, D, LANES))
    out = pl.pallas_call(
        _ada_kernel,
        grid=(N // tn,),
        in_specs=[pl.BlockSpec((B, D, LANES), lambda j: (0, 0, 0)),
                  pl.BlockSpec((1, D, tn), lambda j: (index, 0, j))],
        out_specs=pl.BlockSpec((B, 1, tn), lambda j: (0, 0, j)),
        out_shape=jax.ShapeDtypeStruct((B, 1, N), F32),
        compiler_params=_cparams(("parallel",)),
        name="ada_mod",
    )(c_cols, ada_w)
    return out[:, 0]


def _nm_kernel(*refs, act, f32_col, prenormed):
    if prenormed:
        h_ref, w_ref, o_ref = refs[:3]
        outs = refs[3:]
    else:
        x_ref, g_ref, sc_ref, sh_ref, w_ref, o_ref = refs[:6]
        outs, h_ref = refs[6:-1], refs[-1]

        @pl.when(pl.program_id(1) == 0)
        def _():
            h = _modulated_norm(x_ref[...], g_ref[...], sc_ref[0], sh_ref[0])
            h_ref[...] = h.astype(BF16)

    acc = _dot(h_ref[...], w_ref[...])
    if act == "gelu":
        acc = 0.5 * acc * (1.0 + lax.erf(acc * np.float32(1.0 / np.sqrt(2.0))))
    o_ref[...] = acc.astype(o_ref.dtype)
    if f32_col is not None:
        @pl.when(pl.program_id(1) == f32_col)
        def _():
            outs[0][...] = acc


def _norm_matmul(x, norm, w, seq, *, act, out_dtype, f32_col=None, tm=512, tn=2048):
    T, D = x.shape
    N = w.shape[1]
    prenormed = norm is None
    bmap = lambda i, j: ((i * tm) // seq, 0, 0)
    out_specs = [pl.BlockSpec((tm, tn), lambda i, j: (i, j))]
    out_shape = [jax.ShapeDtypeStruct((T, N), out_dtype)]
    if f32_col is not None:
        out_specs.append(pl.BlockSpec((tm, tn), lambda i, j: (i, 0)))
        out_shape.append(jax.ShapeDtypeStruct((T, tn), F32))
    in_specs = [pl.BlockSpec((tm, D), lambda i, j: (i, 0))]
    args = [x]
    if not prenormed:
        in_specs += [pl.BlockSpec((1, D), lambda i, j: (0, 0)),
                     pl.BlockSpec((1, 1, D), bmap),
                     pl.BlockSpec((1, 1, D), bmap)]
        args += list(norm)
    in_specs.append(pl.BlockSpec((D, tn), lambda i, j: (0, j)))
    args.append(w)
    outs = pl.pallas_call(
        functools.partial(_nm_kernel, act=act, f32_col=f32_col, prenormed=prenormed),
        grid=(T // tm, N // tn),
        in_specs=in_specs,
        out_specs=out_specs,
        out_shape=out_shape,
        scratch_shapes=[] if prenormed else [pltpu.VMEM((tm, D), BF16)],
        compiler_params=pltpu.CompilerParams(dimension_semantics=("parallel", "arbitrary"),
                                             vmem_limit_bytes=BIG_VMEM_LIMIT),
        name="norm_matmul_" + str(act),
    )(*args)
    return outs if f32_col is not None else outs[0]


def _sg_kernel(u_ref, v_ref, lng_ref, lnb_ref, ws_ref, bst_ref, y_ref, vn_ref, *, heads):
    v = v_ref[...].astype(F32)
    mu = jnp.mean(v, axis=-1, keepdims=True)
    vc = v - mu
    var = jnp.mean(vc * vc, axis=-1, keepdims=True)
    vn_ref[...] = (vc * lax.rsqrt(var + LN_EPS) * lng_ref[...] + lnb_ref[...]).astype(BF16)

    n_chunks = v_ref.shape[0] // SG_CHUNK
    row = lax.broadcasted_iota(I32, (SG_CHUNK, SG_CHUNK), 0)
    col = lax.broadcasted_iota(I32, (SG_CHUNK, SG_CHUNK), 1)
    causal = row >= col
    for hd in range(heads):
        cs = slice(hd * HEAD_DIM, (hd + 1) * HEAD_DIM)
        wm = jnp.where(causal, ws_ref[hd], 0.0).astype(BF16)
        bias = bst_ref[:, hd:hd + 1]
        for ci in range(n_chunks):
            rs = slice(ci * SG_CHUNK, (ci + 1) * SG_CHUNK)
            mixed = _dot(wm, vn_ref[rs, cs]) + bias
            y_ref[rs, cs] = (u_ref[rs, cs].astype(F32) * mixed).astype(BF16)


def _spatial_gate(z, ln_g, ln_b, w_s, b_s, *, tm=512):
    T, two_w = z.shape
    W = two_w // 2
    heads = w_s.shape[0]
    return pl.pallas_call(
        functools.partial(_sg_kernel, heads=heads),
        grid=(T // tm,),
        in_specs=[pl.BlockSpec((tm, W), lambda i: (i, 0)),
                  pl.BlockSpec((tm, W), lambda i: (i, 1)),
                  pl.BlockSpec((1, W), lambda i: (0, 0)),
                  pl.BlockSpec((1, W), lambda i: (0, 0)),
                  pl.BlockSpec((heads, SG_CHUNK, SG_CHUNK), lambda i: (0, 0, 0)),
                  pl.BlockSpec((SG_CHUNK, heads), lambda i: (0, 0))],
        out_specs=pl.BlockSpec((tm, W), lambda i: (i, 0)),
        out_shape=jax.ShapeDtypeStruct((T, W), BF16),
        scratch_shapes=[pltpu.VMEM((tm, W), BF16)],
        compiler_params=_cparams(("parallel",)),
        name="spatial_gate",
    )(z, z, ln_g, ln_b, w_s, jnp.transpose(b_s))


POOL_HALO = 16


def _pool_kernel(z_ref, halo_ref, wg_ref, ls_ref, y_ref, *, seq):
    tm = z_ref.shape[0]
    gdim = wg_ref.shape[1]
    pos0 = (pl.program_id(0) * tm) % seq
    halo_on = (pos0 > 0).astype(F32)
    pos = pos0 + lax.broadcasted_iota(I32, (tm, 1), 0)
    for gi, wnd in enumerate(POOL_WINDOWS):
        cs = slice(gi * gdim, (gi + 1) * gdim)
        zg = z_ref[:, cs]
        s = jnp.concatenate([halo_ref[:, cs] * halo_on, zg], axis=0)
        k = 1
        while k < wnd:
            s = s + pltpu.roll(s, k, 0)
            k *= 2
        cnt = jnp.minimum(pos + 1, wnd).astype(F32)
        pooled = s[POOL_HALO:, :] / cnt - zg
        yg = _dot(pooled.astype(BF16), wg_ref[gi]) * ls_ref[:, cs]
        y_ref[:, cs] = yg.astype(BF16)


def _pool_mix(z, w_grp, ls, seq, *, tm=512):
    T, D = z.shape
    G, gdim, _ = w_grp.shape
    assert max(POOL_WINDOWS) <= POOL_HALO and tm % POOL_HALO == 0
    ratio = tm // POOL_HALO
    return pl.pallas_call(
        functools.partial(_pool_kernel, seq=seq),
        grid=(T // tm,),
        in_specs=[pl.BlockSpec((tm, D), lambda i: (i, 0)),
                  pl.BlockSpec((POOL_HALO, D), lambda i: (jnp.maximum(i * ratio - 1, 0), 0)),
                  pl.BlockSpec((G, gdim, gdim), lambda i: (0, 0, 0)),
                  pl.BlockSpec((1, D), lambda i: (0, 0))],
        out_specs=pl.BlockSpec((tm, D), lambda i: (i, 0)),
        out_shape=jax.ShapeDtypeStruct((T, D), BF16),
        compiler_params=_cparams(("parallel",)),
        name="pool_mix",
    )(z, z, w_grp, ls)


HGRN_CHUNK = 128
HGRN_BASE = 32
HGRN_GUARD_LOG2 = 100.0
LOG2_E = 1.4426950408889634


def _hgrn_tables(c=HGRN_CHUNK, base=HGRN_BASE):
    t = np.arange(c)
    j = t[None, :]
    sums = [j <= t[:, None]]
    masks = []
    n_shallow = 0
    h = c // 2
    while h >= 1:
        off = t % (2 * h)
        mid = (t // (2 * h)) * (2 * h) + h
        second = off >= h
        if h >= 2:
            m_q = second[:, None] & (j >= mid[:, None]) & (j <= t[:, None])
            m_k = (~second)[:, None] & (j > t[:, None]) & (j < mid[:, None])
            sums.append(m_q | m_k)
        same = (t[:, None] // (2 * h)) == (t[None, :] // (2 * h))
        masks.append(same & second[:, None] & (~second)[None, :])
        n_shallow += h >= base
        h //= 2
    same_base = (t[:, None] // base) == (t[None, :] // base)
    sums.append(same_base & (j <= t[:, None]))
    sums = np.stack(sums).astype(np.float32)
    base_mask = (same_base & (j <= t[:, None])).astype(np.float32)
    return (np.concatenate([sums, sums], axis=2), np.stack(masks).astype(np.float32), base_mask,
            n_shallow)


def _hgrn_kernel(q_ref, f_ref, i_ref, g_ref, lb_ref, gn_ref, sums_ref, masks_ref, bmask_ref,
                 o_ref, st_ref, qb_ref, ke_ref, z_ref, qd_ref, kd_ref, vb_ref, eb_ref, sc_ref,
                 *, heads, n_shallow):
    c = q_ref.shape[0]
    n_sum = sums_ref.shape[0] - 1
    n_lvl = masks_ref.shape[0]

    @pl.when(pl.program_id(1) == 0)
    def _():
        st_ref[...] = jnp.zeros_like(st_ref)

    lb = lb_ref[...]
    sig = jax.nn.sigmoid(f_ref[...])
    fg = lb + (1.0 - lb) * sig
    lf2 = jnp.log(fg) * np.float32(LOG2_E)
    kk = (1.0 - lb) * (1.0 - sig)
    lf_hi, lf_lo = _split_bf16(lf2)
    lf_cat = jnp.concatenate([lf_hi, lf_lo], axis=0)
    qr = q_ref[...].astype(F32)
    q = qr * jax.nn.sigmoid(qr)
    vb_ref[...] = i_ref[...].astype(BF16)
    q16 = q.astype(BF16)
    k16 = kk.astype(BF16)

    b2 = _dot(sums_ref[0], lf_cat)
    eb = jnp.exp2(b2)
    qb_ref[...] = (q * eb).astype(BF16)
    eb_ref[...] = eb[c - 1:c, :]
    ke_ref[...] = (kk * jnp.exp2(b2[c - 1:c, :] - b2)).astype(BF16)
    w2 = _dot(sums_ref[n_sum], lf_cat)
    shallow = jnp.min(w2) >= -HGRN_GUARD_LOG2

    row = lax.broadcasted_iota(I32, (c, 1), 0)

    def level(l):
        half = c >> (l + 1)
        second = (row & (2 * half - 1)) >= half
        if l + 1 < n_sum:
            x = jnp.exp2(_dot(sums_ref[l + 1], lf_cat).astype(BF16))
        else:
            x = jnp.where(second, fg, 1.0).astype(BF16)
        z_ref[l] = jnp.where(second, q16, k16) * x

    gn = gn_ref[...]

    def finish(hd, cs, o):
        st = st_ref[hd]
        vh = vb_ref[:, cs]
        o = o + _dot_nt(qb_ref[:, cs], st.astype(BF16))
        st_ref[hd] = st * eb_ref[:, cs] + _dot_tn(vh, ke_ref[:, cs])
        on = o * lax.rsqrt(jnp.mean(o * o, axis=-1, keepdims=True) + RMS_EPS) * gn
        gv = g_ref[:, cs].astype(F32)
        o_ref[:, cs] = (on * (gv * jax.nn.sigmoid(gv))).astype(BF16)

    def level_scores(cs, n):
        scores = None
        for l in range(n):
            half = c >> (l + 1)
            zl = z_ref[l, :, cs]
            if half >= 16:
                starts = list(range(half, c, 2 * half))
                zq = jnp.concatenate([z_ref[l, r:r + half, cs] for r in starts], axis=0)
                p = _dot_nt(zq, zl).astype(BF16)
                parts = []
                for n_i, r in enumerate(starts):
                    parts.append(jnp.zeros((half, c), BF16))
                    parts.append(masks_ref[l, r:r + half, :] * p[n_i * half:(n_i + 1) * half])
                term = jnp.concatenate(parts, axis=0)
            else:
                term = masks_ref[l] * _dot_nt(zl, zl).astype(BF16)
            scores = term if scores is None else scores + term
        return scores

    @pl.when(shallow)
    def _():
        for l in range(n_shallow):
            level(l)
        qd_ref[...] = (q * jnp.exp2(w2)).astype(BF16)
        kd_ref[...] = (kk * jnp.exp2(-w2)).astype(BF16)

        def head_scores(hd, carry):
            cs = pl.ds(pl.multiple_of(hd * HEAD_DIM, HEAD_DIM), HEAD_DIM)
            sc_ref[hd] = (level_scores(cs, n_shallow)
                          + bmask_ref[...] * _dot_nt(qd_ref[:, cs], kd_ref[:, cs]).astype(BF16))
            return carry

        def head_out(hd, carry):
            cs = pl.ds(pl.multiple_of(hd * HEAD_DIM, HEAD_DIM), HEAD_DIM)
            finish(hd, cs, _dot(sc_ref[hd], vb_ref[:, cs]))
            return carry

        lax.fori_loop(0, heads, head_scores, 0, unroll=16)
        lax.fori_loop(0, heads, head_out, 0, unroll=16)

    @pl.when(jnp.logical_not(shallow))
    def _():
        for l in range(n_lvl):
            level(l)
        qd_ref[...] = q16
        kd_ref[...] = k16

        def head(hd, carry):
            cs = pl.ds(pl.multiple_of(hd * HEAD_DIM, HEAD_DIM), HEAD_DIM)
            vh = vb_ref[:, cs]
            diag = jnp.sum(qd_ref[:, cs].astype(F32) * kd_ref[:, cs].astype(F32), axis=-1,
                           keepdims=True)
            finish(hd, cs, _dot(level_scores(cs, n_lvl), vh) + diag * vh.astype(F32))
            return carry

        lax.fori_loop(0, heads, head, 0, unroll=4)


def _hgrn_mix(proj, f_pre, lb, gnorm_g, batch, seq):
    T, four_d = proj.shape
    D = four_d // 4
    heads = D // HEAD_DIM
    c = HGRN_CHUNK
    n_chunks = seq // c
    sums, masks, base_mask, n_shallow = _hgrn_tables(c)
    n_sum, n_mask = sums.shape[0], masks.shape[0]
    row = lambda b, i: b * n_chunks + i
    return pl.pallas_call(
        functools.partial(_hgrn_kernel, heads=heads, n_shallow=n_shallow),
        grid=(batch, n_chunks),
        in_specs=[pl.BlockSpec((c, D), lambda b, i: (row(b, i), 0)),
                  pl.BlockSpec((c, D), lambda b, i: (row(b, i), 0)),
                  pl.BlockSpec((c, D), lambda b, i: (row(b, i), 2)),
                  pl.BlockSpec((c, D), lambda b, i: (row(b, i), 3)),
                  pl.BlockSpec((1, D), lambda b, i: (0, 0)),
                  pl.BlockSpec((1, HEAD_DIM), lambda b, i: (0, 0)),
                  pl.BlockSpec((n_sum, c, 2 * c), lambda b, i: (0, 0, 0)),
                  pl.BlockSpec((n_mask, c, c), lambda b, i: (0, 0, 0)),
                  pl.BlockSpec((c, c), lambda b, i: (0, 0))],
        out_specs=pl.BlockSpec((c, D), lambda b, i: (row(b, i), 0)),
        out_shape=jax.ShapeDtypeStruct((T, D), BF16),
        scratch_shapes=[pltpu.VMEM((heads, HEAD_DIM, HEAD_DIM), F32),
                        pltpu.VMEM((c, D), BF16),
                        pltpu.VMEM((c, D), BF16),
                        pltpu.VMEM((n_mask, c, D), BF16),
                        pltpu.VMEM((c, D), BF16),
                        pltpu.VMEM((c, D), BF16),
                        pltpu.VMEM((c, D), BF16),
                        pltpu.VMEM((1, D), F32),
                        pltpu.VMEM((heads, c, c), BF16)],
        compiler_params=_cparams(("parallel", "arbitrary")),
        name="hgrn_mix",
    )(proj, f_pre, proj, proj, lb, gnorm_g, jnp.asarray(sums, BF16), jnp.asarray(masks, BF16),
      jnp.asarray(base_mask, BF16))


def _route_kernel(y_ref, w_ref, xin_ref, mgate_ref, g_ref, sc_ref, sh_ref, rw_ref,
                  rb_ref, tri_ref, x_ref, hp_ref, idx_ref, gate_ref, rank_ref, cnt_ref, carry_ref,
                  *, n_experts):
    @pl.when(pl.program_id(0) == 0)
    def _():
        carry_ref[...] = jnp.zeros_like(carry_ref)

    x = xin_ref[...] + mgate_ref[0] * _dot(y_ref[...], w_ref[...])
    x_ref[...] = x
    h = _modulated_norm(x, g_ref[...], sc_ref[0], sh_ref[0])
    hp_ref[...] = _pack_halves(h)

    h_hi, h_lo = _split_bf16(h)
    both = _dot(h_hi, rw_ref[...])
    logits = (both[:, :LANES] + both[:, LANES:] + _dot(h_lo, rw_ref[:, :LANES])) + rb_ref[...]
    lt = jnp.transpose(logits)[:n_experts, :]
    eidx = lax.broadcasted_iota(I32, lt.shape, 0)
    vals, idxs, hits = [], [], []
    for _ in range(TOP_K):
        m = jnp.max(lt, axis=0, keepdims=True)
        sel = jnp.min(jnp.where(lt == m, eidx, n_experts), axis=0, keepdims=True)
        hit = eidx == sel
        vals.append(m)
        idxs.append(sel)
        hits.append(hit)
        lt = jnp.where(hit, -jnp.inf, lt)
    exps = [jnp.exp(v - vals[0]) for v in vals]
    denom = exps[0]
    for e in exps[1:]:
        denom = denom + e
    idx_ref[...] = jnp.concatenate(idxs, axis=0)
    gate_ref[...] = jnp.concatenate([e / denom for e in exps], axis=0)

    onehot = hits[0].astype(F32)
    for hit in hits[1:]:
        onehot = onehot + hit.astype(F32)
    before = carry_ref[...] + _dot(onehot.astype(BF16), tri_ref[...])
    rank_ref[...] = jnp.concatenate(
        [jnp.sum(jnp.where(hit, before, 0.0), axis=0, keepdims=True) for hit in hits],
        axis=0).astype(I32)
    carry_ref[...] += jnp.sum(onehot, axis=1, keepdims=True)
    cnt_ref[...] = carry_ref[...].astype(I32)


def _mixer_out_route(y, w, x, mix_gate, g, scale, shift, r_w, r_b, seq, *, tm=256):
    T, K = y.shape
    D = w.shape[1]
    E = r_w.shape[1]
    rw_pad = jnp.zeros((D, LANES), F32).at[:, :E].set(r_w)
    rw_cat = jnp.concatenate(_split_bf16(rw_pad), axis=1)
    rb_pad = jnp.full((1, LANES), -1e30, F32).at[0, :E].set(r_b)
    tri = jnp.asarray(np.triu(np.ones((tm, tm), np.float32), k=1), BF16)
    pw, pdt = _packed_layout(D)
    bmap = lambda i: ((i * tm) // seq, 0, 0)
    const = lambda i: (0, 0)
    rows = lambda i: (i, 0)
    kt = pl.BlockSpec((TOP_K, tm), lambda i: (0, i))
    return pl.pallas_call(
        functools.partial(_route_kernel, n_experts=E),
        grid=(T // tm,),
        in_specs=[pl.BlockSpec((tm, K), rows),
                  pl.BlockSpec((K, D), const),
                  pl.BlockSpec((tm, D), rows),
                  pl.BlockSpec((1, 1, D), bmap),
                  pl.BlockSpec((1, D), const),
                  pl.BlockSpec((1, 1, D), bmap),
                  pl.BlockSpec((1, 1, D), bmap),
                  pl.BlockSpec((D, 2 * LANES), const),
                  pl.BlockSpec((1, LANES), const),
                  pl.BlockSpec((tm, tm), const)],
        out_specs=[pl.BlockSpec((tm, D), rows), pl.BlockSpec((tm, pw), rows), kt, kt, kt,
                   pl.BlockSpec((E, 1), const)],
        out_shape=[jax.ShapeDtypeStruct((T, D), F32),
                   jax.ShapeDtypeStruct((T, pw), pdt),
                   jax.ShapeDtypeStruct((TOP_K, T), I32),
                   jax.ShapeDtypeStruct((TOP_K, T), F32),
                   jax.ShapeDtypeStruct((TOP_K, T), I32),
                   jax.ShapeDtypeStruct((E, 1), I32)],
        scratch_shapes=[pltpu.VMEM((E, 1), F32)],
        compiler_params=_cparams(("arbitrary",)),
        name="mixer_out_route",
    )(y, w, x, mix_gate, g, scale, shift, rw_cat, rb_pad, tri)


def _sc_gather(table, idx):
    R = idx.shape[0]
    W = table.shape[1]
    rows = SC_ROWS_PER_STEP // 2
    per_worker = R // SC_WORKERS
    steps = per_worker // rows
    assert steps * rows * SC_WORKERS == R and steps % 2 == 0
    mesh = plsc.VectorSubcoreMesh(core_axis_name="c", subcore_axis_name="s")
    n_cores = mesh.num_cores

    @functools.partial(
        pl.kernel, mesh=mesh,
        out_type=jax.ShapeDtypeStruct((R, W), table.dtype),
        scratch_types=[pltpu.VMEM((steps, rows), I32),
                       pltpu.VMEM((rows, W), table.dtype),
                       pltpu.VMEM((rows, W), table.dtype),
                       pltpu.SemaphoreType.DMA((2,)),
                       pltpu.SemaphoreType.DMA((2,))],
    )
    def gather(table_hbm, idx_hbm, out_hbm, idx_v, buf0, buf1, gsem, wsem):
        wid = lax.axis_index("s") * n_cores + lax.axis_index("c")
        base = wid * per_worker
        bufs = (buf0, buf1)
        pltpu.sync_copy(idx_hbm.at[pl.ds(wid * steps, steps)], idx_v)

        def fetch(j, b):
            return pltpu.make_async_copy(table_hbm.at[idx_v.at[j]], bufs[b], gsem.at[b])

        def flush(j, b):
            off = pl.multiple_of(base + j * rows, 8)
            return pltpu.make_async_copy(bufs[b], out_hbm.at[pl.ds(off, rows)], wsem.at[b])

        fetch(0, 0).start()

        @pl.loop(0, steps, step=2)
        def _(j0):
            for b in range(2):
                j = j0 + b
                fetch(j, b).wait()
                flush(j, b).start()

                @pl.when(j + 1 < steps)
                def _():
                    @pl.when(j >= 1)
                    def _():
                        flush(j - 1, 1 - b).wait()
                    fetch(j + 1, 1 - b).start()

        flush(steps - 2, 0).wait()
        flush(steps - 1, 1).wait()

    return gather(table, idx.reshape(R // rows, rows))


def _sc_scatter_rows(table, slots, n_out):
    K, T = slots.shape
    W = table.shape[1]
    rows = SC_ROWS_PER_STEP
    per_worker = T // SC_WORKERS
    steps = per_worker // rows
    assert steps * rows * SC_WORKERS == T
    mesh = plsc.VectorSubcoreMesh(core_axis_name="c", subcore_axis_name="s")
    n_cores = mesh.num_cores

    @functools.partial(
        pl.kernel, mesh=mesh,
        out_type=jax.ShapeDtypeStruct((n_out, W), table.dtype),
        scratch_types=[pltpu.VMEM((K, rows), I32),
                       pltpu.VMEM((rows, W), table.dtype)],
    )
    def scatter(table_hbm, slots_hbm, out_hbm, idx_v, rows_v):
        wid = lax.axis_index("s") * n_cores + lax.axis_index("c")
        base = wid * per_worker

        @pl.loop(0, steps)
        def _(j):
            off = pl.multiple_of(base + j * rows, 8)
            pltpu.sync_copy(table_hbm.at[pl.ds(off, rows)], rows_v)
            for k in range(K):
                pltpu.sync_copy(slots_hbm.at[pl.ds(pl.multiple_of(k * T + off, 8), rows)],
                                idx_v.at[k])
                pltpu.sync_copy(rows_v, out_hbm.at[idx_v.at[k]])

    return scatter(table, slots.reshape(K * T))


EXPERT_SUB_ROWS = 256
CAST_ELEMS = 32 * 1024
BF16_TILE_ROWS = 16


def _expert_kernel(blk_e_ref, n_valid_ref, next_e_ref, xs_ref, wg_hbm, bg_ref, wl_hbm, bl_ref,
                   wo_hbm, bo_ref, y_ref, wg_st, wl_st, wo_st, wgb_ref, wlb_ref, wob_ref, sem,
                   *, expert_base):
    i = pl.program_id(0)
    e = blk_e_ref[i]
    n_valid = n_valid_ref[i]
    new_expert = jnp.logical_or(i == 0, e != blk_e_ref[jnp.maximum(i - 1, 0)])

    def weight_copies(expert):
        idx = expert_base + expert
        return (pltpu.make_async_copy(wg_hbm.at[idx], wg_st, sem.at[0]),
                pltpu.make_async_copy(wl_hbm.at[idx], wl_st, sem.at[1]),
                pltpu.make_async_copy(wo_hbm.at[idx], wo_st, sem.at[2]))

    @pl.when(i == 0)
    def _():
        for cp in weight_copies(e):
            cp.start()

    @pl.when(jnp.logical_and(new_expert, n_valid > 0))
    def _():
        for cp in weight_copies(e):
            cp.wait()
        for src, dst in ((wg_st, wgb_ref), (wl_st, wlb_ref), (wo_st, wob_ref)):
            n_rows = CAST_ELEMS // src.shape[1] // BF16_TILE_ROWS * BF16_TILE_ROWS
            assert n_rows > 0 and src.shape[0] % n_rows == 0

            def cast_rows(r, carry, src=src, dst=dst, n_rows=n_rows):
                rows = pl.ds(pl.multiple_of(r * n_rows, n_rows), n_rows)
                dst[rows, :] = src[rows, :].astype(BF16)
                return carry

            lax.fori_loop(0, src.shape[0] // n_rows, cast_rows, 0, unroll=4)
        nxt = next_e_ref[i]

        @pl.when(nxt >= 0)
        def _():
            for cp in weight_copies(nxt):
                cp.start()

    half = wgb_ref.shape[0] // 2
    for first in range(0, xs_ref.shape[0], EXPERT_SUB_ROWS):
        rs = slice(first, first + EXPERT_SUB_ROWS)

        @pl.when(n_valid > first)
        def _():
            live = first + lax.broadcasted_iota(I32, (EXPERT_SUB_ROWS, 1), 0) < n_valid
            lo, hi = _unpack_halves(jnp.where(live, xs_ref[rs, :], 0))
            lo = lo.astype(BF16)
            hi = hi.astype(BF16)
            a = _dot(lo, wgb_ref[:half, :]) + _dot(hi, wgb_ref[half:, :]) + bg_ref[0]
            l = _dot(lo, wlb_ref[:half, :]) + _dot(hi, wlb_ref[half:, :]) + bl_ref[0]
            a = jnp.minimum(a, SWIGLU_LIMIT)
            l = jnp.clip(l, -SWIGLU_LIMIT, SWIGLU_LIMIT)
            act = a * jax.nn.sigmoid(SWIGLU_ALPHA * a) * (l + 1.0)
            y_ref[rs, :] = _pack_halves(_dot(act.astype(BF16), wob_ref[...]) + bo_ref[0])

        @pl.when(n_valid <= first)
        def _():
            y_ref[rs, :] = jnp.zeros((EXPERT_SUB_ROWS, y_ref.shape[1]), y_ref.dtype)


def _experts(xs, blk_e, n_valid, next_e, layer, w_glu, b_glu, w_lin, b_lin, w_out, b_out, *,
             blk):
    P, pw = xs.shape
    L, E, D, F = w_glu.shape
    n_blocks = P // blk
    bmap = lambda i, be, nv, ne: (layer * E + be[i], 0, 0)
    rows = lambda i, be, nv, ne: (i, 0)
    w_glu, w_lin = w_glu.reshape(L * E, D, F), w_lin.reshape(L * E, D, F)
    w_out = w_out.reshape(L * E, F, D)
    b_glu, b_lin = b_glu.reshape(L * E, 1, F), b_lin.reshape(L * E, 1, F)
    b_out = b_out.reshape(L * E, 1, D)
    hbm = pl.BlockSpec(memory_space=pl.ANY)
    grid_spec = pltpu.PrefetchScalarGridSpec(
        num_scalar_prefetch=3,
        grid=(n_blocks,),
        in_specs=[pl.BlockSpec((blk, pw), rows),
                  hbm, pl.BlockSpec((1, 1, F), bmap),
                  hbm, pl.BlockSpec((1, 1, F), bmap),
                  hbm, pl.BlockSpec((1, 1, D), bmap)],
        out_specs=pl.BlockSpec((blk, pw), rows),
        scratch_shapes=[pltpu.VMEM((D, F), F32), pltpu.VMEM((D, F), F32),
                        pltpu.VMEM((F, D), F32),
                        pltpu.VMEM((D, F), BF16), pltpu.VMEM((D, F), BF16),
                        pltpu.VMEM((F, D), BF16),
                        pltpu.SemaphoreType.DMA((3,))],
    )
    return pl.pallas_call(
        functools.partial(_expert_kernel, expert_base=layer * E),
        grid_spec=grid_spec,
        out_shape=jax.ShapeDtypeStruct((P, pw), xs.dtype),
        compiler_params=_cparams(("arbitrary",)),
        name="moe_experts",
    )(blk_e, n_valid, next_e, xs, w_glu, b_glu, w_lin, b_lin, w_out, b_out)


def _combine_kernel(ys_ref, tg_ref, x_ref, gate_ref, *rest, tail):
    half = x_ref.shape[1] // 2
    tg = tg_ref[...]
    lo_sum = hi_sum = None
    for k in range(ys_ref.shape[0]):
        lo, hi = _unpack_halves(ys_ref[k])
        gk = tg[:, k:k + 1]
        lo_sum = gk * lo if lo_sum is None else lo_sum + gk * lo
        hi_sum = gk * hi if hi_sum is None else hi_sum + gk * hi
    x_lo = x_ref[:, :half] + gate_ref[0, :, :half] * lo_sum
    x_hi = x_ref[:, half:] + gate_ref[0, :, half:] * hi_sum
    ms = (jnp.sum(x_lo * x_lo, axis=-1, keepdims=True)
          + jnp.sum(x_hi * x_hi, axis=-1, keepdims=True)) / x_ref.shape[1]
    inv = lax.rsqrt(ms + RMS_EPS)
    if tail == "final":
        g_ref, o_ref = rest
        o_ref[:, :half] = x_lo * inv * g_ref[:, :half]
        o_ref[:, half:] = x_hi * inv * g_ref[:, half:]
    else:
        g_ref, sc_ref, sh_ref, o_ref, h_ref = rest
        o_ref[:, :half] = x_lo
        o_ref[:, half:] = x_hi
        for cols, xs in ((slice(0, half), x_lo), (slice(half, 2 * half), x_hi)):
            h = xs * inv * g_ref[:, cols] * (1.0 + sc_ref[0, :, cols]) + sh_ref[0, :, cols]
            h_ref[:, cols] = h.astype(BF16)


def _combine(ys, tok_gate, x, gate, seq, *, final_g=None, next_norm=None, tm=512):
    K, T, pw = ys.shape
    D = x.shape[1]
    bmap = lambda i: ((i * tm) // seq, 0, 0)
    rows = pl.BlockSpec((tm, D), lambda i: (i, 0))
    vec = pl.BlockSpec((1, D), lambda i: (0, 0))
    per_batch = pl.BlockSpec((1, 1, D), bmap)
    in_specs = [pl.BlockSpec((K, tm, pw), lambda i: (0, i, 0)),
                pl.BlockSpec((tm, K), lambda i: (i, 0)), rows, per_batch]
    args = [ys, tok_gate, x, gate]
    if final_g is not None:
        tail = "final"
        in_specs.append(vec)
        args.append(final_g)
        out_specs, out_shape = rows, jax.ShapeDtypeStruct((T, D), F32)
    else:
        tail = "next"
        in_specs += [vec, per_batch, per_batch]
        args += list(next_norm)
        out_specs = [rows, rows]
        out_shape = [jax.ShapeDtypeStruct((T, D), F32), jax.ShapeDtypeStruct((T, D), BF16)]
    return pl.pallas_call(
        functools.partial(_combine_kernel, tail=tail),
        grid=(T // tm,),
        in_specs=in_specs,
        out_specs=out_specs,
        out_shape=out_shape,
        compiler_params=_cparams(("parallel",)),
        name="moe_combine",
    )(*args)


MOE_BLOCK = 512


def _slots_kernel(start_ref, idx_ref, rank_ref, o_ref):
    idx = idx_ref[...]
    acc = rank_ref[...]
    for e in range(start_ref.shape[0]):
        acc = acc + jnp.where(idx == e, start_ref[e], 0)
    o_ref[...] = acc


def _slots(idx_t, rank_t, group_start):
    full = pl.BlockSpec(idx_t.shape, lambda i, gs: (0, 0))
    return pl.pallas_call(
        _slots_kernel,
        grid_spec=pltpu.PrefetchScalarGridSpec(num_scalar_prefetch=1, grid=(1,),
                                               in_specs=[full, full], out_specs=full),
        out_shape=jax.ShapeDtypeStruct(idx_t.shape, I32),
        compiler_params=_cparams(("arbitrary",)),
        name="moe_slots",
    )(group_start, idx_t, rank_t)


def _dispatch_plan(idx_t, rank_t, counts, blk):
    K, T = idx_t.shape
    E = counts.shape[0]
    padded = (counts + blk - 1) // blk * blk
    pad_end = jnp.cumsum(padded)
    pad_start = pad_end - padded
    experts = jnp.arange(E, dtype=I32)
    slots = _slots(idx_t, rank_t, pad_start.astype(I32))
    n_blocks = -(-(K * T + E * (blk - 1)) // blk)
    blk_start = jnp.arange(n_blocks, dtype=I32) * blk
    blk_e = jnp.minimum(jnp.sum(pad_end[None, :] <= blk_start[:, None], axis=1), E - 1).astype(I32)
    n_valid = jnp.clip(pad_start[blk_e] + counts[blk_e] - blk_start, 0, blk).astype(I32)
    later = jnp.where(counts > 0, experts, E)
    after = lax.cummin(jnp.concatenate([later[1:], jnp.full((1,), E, I32)]), reverse=True)
    next_e = jnp.where(after < E, after, -1).astype(I32)[blk_e]
    return slots, blk_e, n_valid, next_e, n_blocks * blk


def _mixer_out_moe(y, w_mix, x, mix_gate, g, scale, shift, gate, r_w, r_b, layer,
                   w_glu, b_glu, w_lin, b_lin, w_out, b_out, seq, **tail):
    T, D = x.shape
    x, hp, idx_t, gate_t, rank_t, counts = _mixer_out_route(
        y, w_mix, x, mix_gate, g, scale, shift, r_w, r_b, seq)
    slots, blk_e, n_valid, next_e, n_rows = _dispatch_plan(idx_t, rank_t, counts[:, 0],
                                                           MOE_BLOCK)
    xs = _sc_scatter_rows(hp, slots, n_rows)
    yp = _experts(xs, blk_e, n_valid, next_e, layer, w_glu, b_glu, w_lin, b_lin, w_out, b_out,
                  blk=MOE_BLOCK)
    ys = _sc_gather(yp, slots.reshape(TOP_K * T)).reshape(TOP_K, T, yp.shape[1])
    return _combine(ys, jnp.transpose(gate_t), x, gate, seq, **tail)


def kernel(x, c, norm_g, ada_w, final_norm_g, sg_w_in, sg_ln_g, sg_ln_b, sg_w_s, sg_b_s, sg_w_out, pool_w_in, pool_w_grp, pool_ls, pool_w_out, hgrn_w_in, hgrn_lb_logits, hgrn_gnorm_g, hgrn_w_out, router_w, router_b, expert_w_glu, expert_b_glu, expert_w_lin, expert_b_lin, expert_w_out, expert_b_out):
    B, S, D = x.shape
    depth = norm_g.shape[0]
    T = B * S
    xf = x.reshape(T, D)

    ada_w = ada_w.reshape(depth * 2, D, 3 * D)

    def mod_parts(layer, sub):
        m = _ada_mod(c, ada_w, layer * 2 + sub)[:, None, :]
        return m[..., :D], m[..., D:2 * D], m[..., 2 * D:]

    def mixer_norm(layer, shift, scale):
        return norm_g[layer, 0][None, :], scale, shift

    lb_cum = jnp.cumsum(jax.nn.softmax(hgrn_lb_logits.astype(F32), axis=0), axis=0)
    lower_bounds = lb_cum - lb_cum[0]

    shift, scale, mix_gate = mod_parts(0, 0)
    src, norm = xf, mixer_norm(0, shift, scale)
    for layer in range(depth):
        kind, slot = layer % 3, layer // 3
        if kind == 0:
            z = _norm_matmul(src, norm, sg_w_in[slot].astype(BF16), S, act="gelu",
                             out_dtype=BF16)
            y = _spatial_gate(z, sg_ln_g[slot][None, :], sg_ln_b[slot][None, :],
                              sg_w_s[slot], sg_b_s[slot])
            w_out = sg_w_out[slot]
        elif kind == 1:
            z = _norm_matmul(src, norm, pool_w_in[slot].astype(BF16), S, act=None,
                             out_dtype=F32)
            y = _pool_mix(z, pool_w_grp[slot].astype(BF16), pool_ls[slot][None, :], S)
            w_out = pool_w_out[slot]
        else:
            proj, f_pre = _norm_matmul(src, norm, hgrn_w_in[slot].astype(BF16), S, act=None,
                                       out_dtype=BF16, f32_col=1, tn=D)
            y = _hgrn_mix(proj, f_pre, lower_bounds[layer][None, :],
                          hgrn_gnorm_g[slot][None, :], B, S)
            w_out = hgrn_w_out[slot]

        moe_shift, moe_scale, moe_gate = mod_parts(layer, 1)
        moe_args = (y, w_out.astype(BF16), xf, mix_gate, norm_g[layer, 1][None, :],
                    moe_scale, moe_shift, moe_gate, router_w[layer], router_b[layer], layer,
                    expert_w_glu, expert_b_glu, expert_w_lin, expert_b_lin,
                    expert_w_out, expert_b_out, S)
        if layer == depth - 1:
            xf = _mixer_out_moe(*moe_args, final_g=final_norm_g[None, :])
        else:
            shift, scale, mix_gate = mod_parts(layer + 1, 0)
            xf, src = _mixer_out_moe(*moe_args, next_norm=mixer_norm(layer + 1, shift, scale))
            norm = None

    return xf.reshape(B, S, D)
```

```python
import functools

import numpy as np
import jax
import jax.numpy as jnp
from jax import lax
from jax.experimental import pallas as pl
from jax.experimental.pallas import tpu as pltpu
from jax.experimental.pallas import tpu_sc as plsc

F32 = jnp.float32
BF16 = jnp.bfloat16
I32 = jnp.int32
U32 = jnp.uint32

RMS_EPS = 1e-5
LN_EPS = 1e-5
SWIGLU_ALPHA = 1.702
SWIGLU_LIMIT = 7.0
TOP_K = 4
SG_CHUNK = 128
POOL_WINDOWS = (2, 4, 8, 16)
HEAD_DIM = 128
LANES = 128
SC_WORKERS = 32
SC_ROWS_PER_STEP = 64

VMEM_LIMIT = 48 * 1024 * 1024
BIG_VMEM_LIMIT = 56 * 1024 * 1024


def _cparams(sem):
    return pltpu.CompilerParams(dimension_semantics=sem, vmem_limit_bytes=VMEM_LIMIT)


def _dot(a, b):
    return jnp.dot(a, b, preferred_element_type=F32)


def _dot_nt(a, b):
    return lax.dot_general(a, b, (((1,), (1,)), ((), ())), preferred_element_type=F32)


def _dot_tn(a, b):
    return lax.dot_general(a, b, (((0,), (0,)), ((), ())), preferred_element_type=F32)


def _split_bf16(x):
    hi = x.astype(BF16)
    lo = (x - hi.astype(F32)).astype(BF16)
    return hi, lo


def _packed_layout(d):
    return d // 2, I32


def _pack_halves(y):
    n = y.shape[1] // 2
    lo = lax.bitcast_convert_type(y[:, :n].astype(BF16).astype(F32), U32)
    hi = lax.bitcast_convert_type(y[:, n:].astype(BF16).astype(F32), U32)
    word = lax.shift_right_logical(lo, jnp.uint32(16)) | (hi & jnp.uint32(0xFFFF0000))
    return lax.bitcast_convert_type(word, I32)


def _unpack_halves(w):
    u = lax.bitcast_convert_type(w, U32)
    lo = lax.bitcast_convert_type(lax.shift_left(u, jnp.uint32(16)), F32)
    hi = lax.bitcast_convert_type(u & jnp.uint32(0xFFFF0000), F32)
    return lo, hi


def _modulated_norm(x, g, scale, shift):
    ms = jnp.mean(x * x, axis=-1, keepdims=True)
    y = x * lax.rsqrt(ms + RMS_EPS) * g
    return y * (1.0 + scale) + shift


def _ada_kernel(c_ref, w_ref, o_ref):
    w = w_ref[0]
    reps = w.shape[1] // LANES
    for b in range(c_ref.shape[0]):
        cb = c_ref[b]
        s = jnp.tile(cb * jax.nn.sigmoid(cb), (1, reps))
        o_ref[b] = jnp.sum(w * s, axis=0, keepdims=True)


def _ada_mod(c, ada_w, index):
    L, D, N = ada_w.shape
    B = c.shape[0]
    tn = 512
    c_cols = jnp.broadcast_to(c[:, :, None], (B, D, LANES))
    out = pl.pallas_call(
        _ada_kernel,
        grid=(N // tn,),
        in_specs=[pl.BlockSpec((B, D, LANES), lambda j: (0, 0, 0)),
                  pl.BlockSpec((1, D, tn), lambda j: (index, 0, j))],
        out_specs=pl.BlockSpec((B, 1, tn), lambda j: (0, 0, j)),
        out_shape=jax.ShapeDtypeStruct((B, 1, N), F32),
        compiler_params=_cparams(("parallel",)),
        name="ada_mod",
    )(c_cols, ada_w)
    return out[:, 0]


def _nm_kernel(*refs, act, f32_col, prenormed):
    if prenormed:
        h_ref, w_ref, o_ref = refs[:3]
        outs = refs[3:]
    else:
        x_ref, g_ref, sc_ref, sh_ref, w_ref, o_ref = refs[:6]
        outs, h_ref = refs[6:-1], refs[-1]

        @pl.when(pl.program_id(1) == 0)
        def _():
            h = _modulated_norm(x_ref[...], g_ref[...], sc_ref[0], sh_ref[0])
            h_ref[...] = h.astype(BF16)

    acc = _dot(h_ref[...], w_ref[...])
    if act == "gelu":
        acc = 0.5 * acc * (1.0 + lax.erf(acc * np.float32(1.0 / np.sqrt(2.0))))
    o_ref[...] = acc.astype(o_ref.dtype)
    if f32_col is not None:
        @pl.when(pl.program_id(1) == f32_col)
        def _():
            outs[0][...] = acc


def _norm_matmul(x, norm, w, seq, *, act, out_dtype, f32_col=None, tm=512, tn=2048):
    T, D = x.shape
    N = w.shape[1]
    prenormed = norm is None
    bmap = lambda i, j: ((i * tm) // seq, 0, 0)
    out_specs = [pl.BlockSpec((tm, tn), lambda i, j: (i, j))]
    out_shape = [jax.ShapeDtypeStruct((T, N), out_dtype)]
    if f32_col is not None:
        out_specs.append(pl.BlockSpec((tm, tn), lambda i, j: (i, 0)))
        out_shape.append(jax.ShapeDtypeStruct((T, tn), F32))
    in_specs = [pl.BlockSpec((tm, D), lambda i, j: (i, 0))]
    args = [x]
    if not prenormed:
        in_specs += [pl.BlockSpec((1, D), lambda i, j: (0, 0)),
                     pl.BlockSpec((1, 1, D), bmap),
                     pl.BlockSpec((1, 1, D), bmap)]
        args += list(norm)
    in_specs.append(pl.BlockSpec((D, tn), lambda i, j: (0, j)))
    args.append(w)
    outs = pl.pallas_call(
        functools.partial(_nm_kernel, act=act, f32_col=f32_col, prenormed=prenormed),
        grid=(T // tm, N // tn),
        in_specs=in_specs,
        out_specs=out_specs,
        out_shape=out_shape,
        scratch_shapes=[] if prenormed else [pltpu.VMEM((tm, D), BF16)],
        compiler_params=pltpu.CompilerParams(dimension_semantics=("parallel", "arbitrary"),
                                             vmem_limit_bytes=BIG_VMEM_LIMIT),
        name="norm_matmul_" + str(act),
    )(*args)
    return outs if f32_col is not None else outs[0]


def _sg_kernel(u_ref, v_ref, lng_ref, lnb_ref, ws_ref, bst_ref, y_ref, vn_ref, *, heads):
    v = v_ref[...].astype(F32)
    mu = jnp.mean(v, axis=-1, keepdims=True)
    vc = v - mu
    var = jnp.mean(vc * vc, axis=-1, keepdims=True)
    vn_ref[...] = (vc * lax.rsqrt(var + LN_EPS) * lng_ref[...] + lnb_ref[...]).astype(BF16)

    n_chunks = v_ref.shape[0] // SG_CHUNK
    row = lax.broadcasted_iota(I32, (SG_CHUNK, SG_CHUNK), 0)
    col = lax.broadcasted_iota(I32, (SG_CHUNK, SG_CHUNK), 1)
    causal = row >= col
    for hd in range(heads):
        cs = slice(hd * HEAD_DIM, (hd + 1) * HEAD_DIM)
        wm = jnp.where(causal, ws_ref[hd], 0.0).astype(BF16)
        bias = bst_ref[:, hd:hd + 1]
        for ci in range(n_chunks):
            rs = slice(ci * SG_CHUNK, (ci + 1) * SG_CHUNK)
            mixed = _dot(wm, vn_ref[rs, cs]) + bias
            y_ref[rs, cs] = (u_ref[rs, cs].astype(F32) * mixed).astype(BF16)


def _spatial_gate(z, ln_g, ln_b, w_s, b_s, *, tm=512):
    T, two_w = z.shape
    W = two_w // 2
    heads = w_s.shape[0]
    return pl.pallas_call(
        functools.partial(_sg_kernel, heads=heads),
        grid=(T // tm,),
        in_specs=[pl.BlockSpec((tm, W), lambda i: (i, 0)),
                  pl.BlockSpec((tm, W), lambda i: (i, 1)),
                  pl.BlockSpec((1, W), lambda i: (0, 0)),
                  pl.BlockSpec((1, W), lambda i: (0, 0)),
                  pl.BlockSpec((heads, SG_CHUNK, SG_CHUNK), lambda i: (0, 0, 0)),
                  pl.BlockSpec((SG_CHUNK, heads), lambda i: (0, 0))],
        out_specs=pl.BlockSpec((tm, W), lambda i: (i, 0)),
        out_shape=jax.ShapeDtypeStruct((T, W), BF16),
        scratch_shapes=[pltpu.VMEM((tm, W), BF16)],
        compiler_params=_cparams(("parallel",)),
        name="spatial_gate",
    )(z, z, ln_g, ln_b, w_s, jnp.transpose(b_s))


POOL_HALO = 16


def _pool_kernel(z_ref, halo_ref, wg_ref, ls_ref, y_ref, *, seq):
    tm = z_ref.shape[0]
    gdim = wg_ref.shape[1]
    pos0 = (pl.program_id(0) * tm) % seq
    halo_on = (pos0 > 0).astype(F32)
    pos = pos0 + lax.broadcasted_iota(I32, (tm, 1), 0)
    for gi, wnd in enumerate(POOL_WINDOWS):
        cs = slice(gi * gdim, (gi + 1) * gdim)
        zg = z_ref[:, cs]
        s = jnp.concatenate([halo_ref[:, cs] * halo_on, zg], axis=0)
        k = 1
        while k < wnd:
            s = s + pltpu.roll(s, k, 0)
            k *= 2
        cnt = jnp.minimum(pos + 1, wnd).astype(F32)
        pooled = s[POOL_HALO:, :] / cnt - zg
        yg = _dot(pooled.astype(BF16), wg_ref[gi]) * ls_ref[:, cs]
        y_ref[:, cs] = yg.astype(BF16)


def _pool_mix(z, w_grp, ls, seq, *, tm=512):
    T, D = z.shape
    G, gdim, _ = w_grp.shape
    assert max(POOL_WINDOWS) <= POOL_HALO and tm % POOL_HALO == 0
    ratio = tm // POOL_HALO
    return pl.pallas_call(
        functools.partial(_pool_kernel, seq=seq),
        grid=(T // tm,),
        in_specs=[pl.BlockSpec((tm, D), lambda i: (i, 0)),
                  pl.BlockSpec((POOL_HALO, D), lambda i: (jnp.maximum(i * ratio - 1, 0), 0)),
                  pl.BlockSpec((G, gdim, gdim), lambda i: (0, 0, 0)),
                  pl.BlockSpec((1, D), lambda i: (0, 0))],
        out_specs=pl.BlockSpec((tm, D), lambda i: (i, 0)),
        out_shape=jax.ShapeDtypeStruct((T, D), BF16),
        compiler_params=_cparams(("parallel",)),
        name="pool_mix",
    )(z, z, w_grp, ls)


HGRN_CHUNK = 128
HGRN_BASE = 32
HGRN_GUARD_LOG2 = 100.0
LOG2_E = 1.4426950408889634


def _hgrn_tables(c=HGRN_CHUNK, base=HGRN_BASE):
    t = np.arange(c)
    j = t[None, :]
    sums = [j <= t[:, None]]
    masks = []
    n_shallow = 0
    h = c // 2
    while h >= 1:
        off = t % (2 * h)
        mid = (t // (2 * h)) * (2 * h) + h
        second = off >= h
        if h >= 2:
            m_q = second[:, None] & (j >= mid[:, None]) & (j <= t[:, None])
            m_k = (~second)[:, None] & (j > t[:, None]) & (j < mid[:, None])
            sums.append(m_q | m_k)
        same = (t[:, None] // (2 * h)) == (t[None, :] // (2 * h))
        masks.append(same & second[:, None] & (~second)[None, :])
        n_shallow += h >= base
        h //= 2
    same_base = (t[:, None] // base) == (t[None, :] // base)
    sums.append(same_base & (j <= t[:, None]))
    sums = np.stack(sums).astype(np.float32)
    base_mask = (same_base & (j <= t[:, None])).astype(np.float32)
    return (np.concatenate([sums, sums], axis=2), np.stack(masks).astype(np.float32), base_mask,
            n_shallow)


def _hgrn_kernel(q_ref, f_ref, i_ref, g_ref, lb_ref, gn_ref, sums_ref, masks_ref, bmask_ref,
                 o_ref, st_ref, qb_ref, ke_ref, z_ref, qd_ref, kd_ref, vb_ref, eb_ref, sc_ref,
                 *, heads, n_shallow):
    c = q_ref.shape[0]
    n_sum = sums_ref.shape[0] - 1
    n_lvl = masks_ref.shape[0]

    @pl.when(pl.program_id(1) == 0)
    def _():
        st_ref[...] = jnp.zeros_like(st_ref)

    lb = lb_ref[...]
    sig = jax.nn.sigmoid(f_ref[...])
    fg = lb + (1.0 - lb) * sig
    lf2 = jnp.log(fg) * np.float32(LOG2_E)
    kk = (1.0 - lb) * (1.0 - sig)
    lf_hi, lf_lo = _split_bf16(lf2)
    lf_cat = jnp.concatenate([lf_hi, lf_lo], axis=0)
    qr = q_ref[...].astype(F32)
    q = qr * jax.nn.sigmoid(qr)
    vb_ref[...] = i_ref[...].astype(BF16)
    q16 = q.astype(BF16)
    k16 = kk.astype(BF16)

    b2 = _dot(sums_ref[0], lf_cat)
    eb = jnp.exp2(b2)
    qb_ref[...] = (q * eb).astype(BF16)
    eb_ref[...] = eb[c - 1:c, :]
    ke_ref[...] = (kk * jnp.exp2(b2[c - 1:c, :] - b2)).astype(BF16)
    w2 = _dot(sums_ref[n_sum], lf_cat)
    shallow = jnp.min(w2) >= -HGRN_GUARD_LOG2

    row = lax.broadcasted_iota(I32, (c, 1), 0)

    def level(l):
        half = c >> (l + 1)
        second = (row & (2 * half - 1)) >= half
        if l + 1 < n_sum:
            x = jnp.exp2(_dot(sums_ref[l + 1], lf_cat).astype(BF16))
        else:
            x = jnp.where(second, fg, 1.0).astype(BF16)
        z_ref[l] = jnp.where(second, q16, k16) * x

    gn = gn_ref[...]

    def finish(hd, cs, o):
        st = st_ref[hd]
        vh = vb_ref[:, cs]
        o = o + _dot_nt(qb_ref[:, cs], st.astype(BF16))
        st_ref[hd] = st * eb_ref[:, cs] + _dot_tn(vh, ke_ref[:, cs])
        on = o * lax.rsqrt(jnp.mean(o * o, axis=-1, keepdims=True) + RMS_EPS) * gn
        gv = g_ref[:, cs].astype(F32)
        o_ref[:, cs] = (on * (gv * jax.nn.sigmoid(gv))).astype(BF16)

    def level_scores(cs, n):
        scores = None
        for l in range(n):
            half = c >> (l + 1)
            zl = z_ref[l, :, cs]
            if half >= 16:
                starts = list(range(half, c, 2 * half))
                zq = jnp.concatenate([z_ref[l, r:r + half, cs] for r in starts], axis=0)
                p = _dot_nt(zq, zl).astype(BF16)
                parts = []
                for n_i, r in enumerate(starts):
                    parts.append(jnp.zeros((half, c), BF16))
                    parts.append(masks_ref[l, r:r + half, :] * p[n_i * half:(n_i + 1) * half])
                term = jnp.concatenate(parts, axis=0)
            else:
                term = masks_ref[l] * _dot_nt(zl, zl).astype(BF16)
            scores = term if scores is None else scores + term
        return scores

    @pl.when(shallow)
    def _():
        for l in range(n_shallow):
            level(l)
        qd_ref[...] = (q * jnp.exp2(w2)).astype(BF16)
        kd_ref[...] = (kk * jnp.exp2(-w2)).astype(BF16)

        def head_scores(hd, carry):
            cs = pl.ds(pl.multiple_of(hd * HEAD_DIM, HEAD_DIM), HEAD_DIM)
            sc_ref[hd] = (level_scores(cs, n_shallow)
                          + bmask_ref[...] * _dot_nt(qd_ref[:, cs], kd_ref[:, cs]).astype(BF16))
            return carry

        def head_out(hd, carry):
            cs = pl.ds(pl.multiple_of(hd * HEAD_DIM, HEAD_DIM), HEAD_DIM)
            finish(hd, cs, _dot(sc_ref[hd], vb_ref[:, cs]))
            return carry

        lax.fori_loop(0, heads, head_scores, 0, unroll=16)
        lax.fori_loop(0, heads, head_out, 0, unroll=16)

    @pl.when(jnp.logical_not(shallow))
    def _():
        for l in range(n_lvl):
            level(l)
        qd_ref[...] = q16
        kd_ref[...] = k16

        def head(hd, carry):
            cs = pl.ds(pl.multiple_of(hd * HEAD_DIM, HEAD_DIM), HEAD_DIM)
            vh = vb_ref[:, cs]
            diag = jnp.sum(qd_ref[:, cs].astype(F32) * kd_ref[:, cs].astype(F32), axis=-1,
                           keepdims=True)
            finish(hd, cs, _dot(level_scores(cs, n_lvl), vh) + diag * vh.astype(F32))
            return carry

        lax.fori_loop(0, heads, head, 0, unroll=4)


def _hgrn_mix(proj, f_pre, lb, gnorm_g, batch, seq):
    T, four_d = proj.shape
    D = four_d // 4
    heads = D // HEAD_DIM
    c = HGRN_CHUNK
    n_chunks = seq // c
    sums, masks, base_mask, n_shallow = _hgrn_tables(c)
    n_sum, n_mask = sums.shape[0], masks.shape[0]
    row = lambda b, i: b * n_chunks + i
    return pl.pallas_call(
        functools.partial(_hgrn_kernel, heads=heads, n_shallow=n_shallow),
        grid=(batch, n_chunks),
        in_specs=[pl.BlockSpec((c, D), lambda b, i: (row(b, i), 0)),
                  pl.BlockSpec((c, D), lambda b, i: (row(b, i), 0)),
                  pl.BlockSpec((c, D), lambda b, i: (row(b, i), 2)),
                  pl.BlockSpec((c, D), lambda b, i: (row(b, i), 3)),
                  pl.BlockSpec((1, D), lambda b, i: (0, 0)),
                  pl.BlockSpec((1, HEAD_DIM), lambda b, i: (0, 0)),
                  pl.BlockSpec((n_sum, c, 2 * c), lambda b, i: (0, 0, 0)),
                  pl.BlockSpec((n_mask, c, c), lambda b, i: (0, 0, 0)),
                  pl.BlockSpec((c, c), lambda b, i: (0, 0))],
        out_specs=pl.BlockSpec((c, D), lambda b, i: (row(b, i), 0)),
        out_shape=jax.ShapeDtypeStruct((T, D), BF16),
        scratch_shapes=[pltpu.VMEM((heads, HEAD_DIM, HEAD_DIM), F32),
                        pltpu.VMEM((c, D), BF16),
                        pltpu.VMEM((c, D), BF16),
                        pltpu.VMEM((n_mask, c, D), BF16),
                        pltpu.VMEM((c, D), BF16),
                        pltpu.VMEM((c, D), BF16),
                        pltpu.VMEM((c, D), BF16),
                        pltpu.VMEM((1, D), F32),
                        pltpu.VMEM((heads, c, c), BF16)],
        compiler_params=_cparams(("parallel", "arbitrary")),
        name="hgrn_mix",
    )(proj, f_pre, proj, proj, lb, gnorm_g, jnp.asarray(sums, BF16), jnp.asarray(masks, BF16),
      jnp.asarray(base_mask, BF16))


def _route_kernel(y_ref, w_ref, xin_ref, mgate_ref, g_ref, sc_ref, sh_ref, rw_ref,
                  rb_ref, tri_ref, x_ref, hp_ref, idx_ref, gate_ref, rank_ref, cnt_ref, carry_ref,
                  *, n_experts):
    @pl.when(pl.program_id(0) == 0)
    def _():
        carry_ref[...] = jnp.zeros_like(carry_ref)

    x = xin_ref[...] + mgate_ref[0] * _dot(y_ref[...], w_ref[...])
    x_ref[...] = x
    h = _modulated_norm(x, g_ref[...], sc_ref[0], sh_ref[0])
    hp_ref[...] = _pack_halves(h)

    h_hi, h_lo = _split_bf16(h)
    both = _dot(h_hi, rw_ref[...])
    logits = (both[:, :LANES] + both[:, LANES:] + _dot(h_lo, rw_ref[:, :LANES])) + rb_ref[...]
    lt = jnp.transpose(logits)[:n_experts, :]
    eidx = lax.broadcasted_iota(I32, lt.shape, 0)
    vals, idxs, hits = [], [], []
    for _ in range(TOP_K):
        m = jnp.max(lt, axis=0, keepdims=True)
        sel = jnp.min(jnp.where(lt == m, eidx, n_experts), axis=0, keepdims=True)
        hit = eidx == sel
        vals.append(m)
        idxs.append(sel)
        hits.append(hit)
        lt = jnp.where(hit, -jnp.inf, lt)
    exps = [jnp.exp(v - vals[0]) for v in vals]
    denom = exps[0]
    for e in exps[1:]:
        denom = denom + e
    idx_ref[...] = jnp.concatenate(idxs, axis=0)
    gate_ref[...] = jnp.concatenate([e / denom for e in exps], axis=0)

    onehot = hits[0].astype(F32)
    for hit in hits[1:]:
        onehot = onehot + hit.astype(F32)
    before = carry_ref[...] + _dot(onehot.astype(BF16), tri_ref[...])
    rank_ref[...] = jnp.concatenate(
        [jnp.sum(jnp.where(hit, before, 0.0), axis=0, keepdims=True) for hit in hits],
        axis=0).astype(I32)
    carry_ref[...] += jnp.sum(onehot, axis=1, keepdims=True)
    cnt_ref[...] = carry_ref[...].astype(I32)


def _mixer_out_route(y, w, x, mix_gate, g, scale, shift, r_w, r_b, seq, *, tm=512):
    T, K = y.shape
    D = w.shape[1]
    E = r_w.shape[1]
    rw_pad = jnp.zeros((D, LANES), F32).at[:, :E].set(r_w)
    rw_cat = jnp.concatenate(_split_bf16(rw_pad), axis=1)
    rb_pad = jnp.full((1, LANES), -1e30, F32).at[0, :E].set(r_b)
    tri = jnp.asarray(np.triu(np.ones((tm, tm), np.float32), k=1), BF16)
    pw, pdt = _packed_layout(D)
    bmap = lambda i: ((i * tm) // seq, 0, 0)
    const = lambda i: (0, 0)
    rows = lambda i: (i, 0)
    kt = pl.BlockSpec((TOP_K, tm), lambda i: (0, i))
    return pl.pallas_call(
        functools.partial(_route_kernel, n_experts=E),
        grid=(T // tm,),
        in_specs=[pl.BlockSpec((tm, K), rows),
                  pl.BlockSpec((K, D), const, pipeline_mode=pl.Buffered(1)),
                  pl.BlockSpec((tm, D), rows),
                  pl.BlockSpec((1, 1, D), bmap),
                  pl.BlockSpec((1, D), const),
                  pl.BlockSpec((1, 1, D), bmap),
                  pl.BlockSpec((1, 1, D), bmap),
                  pl.BlockSpec((D, 2 * LANES), const, pipeline_mode=pl.Buffered(1)),
                  pl.BlockSpec((1, LANES), const),
                  pl.BlockSpec((tm, tm), const, pipeline_mode=pl.Buffered(1))],
        out_specs=[pl.BlockSpec((tm, D), rows), pl.BlockSpec((tm, pw), rows), kt, kt, kt,
                   pl.BlockSpec((E, 1), const)],
        out_shape=[jax.ShapeDtypeStruct((T, D), F32),
                   jax.ShapeDtypeStruct((T, pw), pdt),
                   jax.ShapeDtypeStruct((TOP_K, T), I32),
                   jax.ShapeDtypeStruct((TOP_K, T), F32),
                   jax.ShapeDtypeStruct((TOP_K, T), I32),
                   jax.ShapeDtypeStruct((E, 1), I32)],
        scratch_shapes=[pltpu.VMEM((E, 1), F32)],
        compiler_params=_cparams(("arbitrary",)),
        name="mixer_out_route",
    )(y, w, x, mix_gate, g, scale, shift, rw_cat, rb_pad, tri)


def _sc_gather(table, idx):
    R = idx.shape[0]
    W = table.shape[1]
    rows = SC_ROWS_PER_STEP // 2
    per_worker = R // SC_WORKERS
    steps = per_worker // rows
    assert steps * rows * SC_WORKERS == R and steps % 2 == 0
    mesh = plsc.VectorSubcoreMesh(core_axis_name="c", subcore_axis_name="s")
    n_cores = mesh.num_cores

    @functools.partial(
        pl.kernel, mesh=mesh,
        out_type=jax.ShapeDtypeStruct((R, W), table.dtype),
        scratch_types=[pltpu.VMEM((steps, rows), I32),
                       pltpu.VMEM((rows, W), table.dtype),
                       pltpu.VMEM((rows, W), table.dtype),
                       pltpu.SemaphoreType.DMA((2,)),
                       pltpu.SemaphoreType.DMA((2,))],
    )
    def gather(table_hbm, idx_hbm, out_hbm, idx_v, buf0, buf1, gsem, wsem):
        wid = lax.axis_index("s") * n_cores + lax.axis_index("c")
        base = wid * per_worker
        bufs = (buf0, buf1)
        pltpu.sync_copy(idx_hbm.at[pl.ds(wid * steps, steps)], idx_v)

        def fetch(j, b):
            return pltpu.make_async_copy(table_hbm.at[idx_v.at[j]], bufs[b], gsem.at[b])

        def flush(j, b):
            off = pl.multiple_of(base + j * rows, 8)
            return pltpu.make_async_copy(bufs[b], out_hbm.at[pl.ds(off, rows)], wsem.at[b])

        fetch(0, 0).start()

        @pl.loop(0, steps, step=2)
        def _(j0):
            for b in range(2):
                j = j0 + b
                fetch(j, b).wait()
                flush(j, b).start()

                @pl.when(j + 1 < steps)
                def _():
                    @pl.when(j >= 1)
                    def _():
                        flush(j - 1, 1 - b).wait()
                    fetch(j + 1, 1 - b).start()

        flush(steps - 2, 0).wait()
        flush(steps - 1, 1).wait()

    return gather(table, idx.reshape(R // rows, rows))


def _sc_scatter_rows(table, slots, n_out):
    K, T = slots.shape
    W = table.shape[1]
    rows = SC_ROWS_PER_STEP
    per_worker = T // SC_WORKERS
    steps = per_worker // rows
    assert steps * rows * SC_WORKERS == T
    mesh = plsc.VectorSubcoreMesh(core_axis_name="c", subcore_axis_name="s")
    n_cores = mesh.num_cores

    @functools.partial(
        pl.kernel, mesh=mesh,
        out_type=jax.ShapeDtypeStruct((n_out, W), table.dtype),
        scratch_types=[pltpu.VMEM((K, rows), I32),
                       pltpu.VMEM((rows, W), table.dtype)],
    )
    def scatter(table_hbm, slots_hbm, out_hbm, idx_v, rows_v):
        wid = lax.axis_index("s") * n_cores + lax.axis_index("c")
        base = wid * per_worker

        @pl.loop(0, steps)
        def _(j):
            off = pl.multiple_of(base + j * rows, 8)
            pltpu.sync_copy(table_hbm.at[pl.ds(off, rows)], rows_v)
            for k in range(K):
                pltpu.sync_copy(slots_hbm.at[pl.ds(pl.multiple_of(k * T + off, 8), rows)],
                                idx_v.at[k])
                pltpu.sync_copy(rows_v, out_hbm.at[idx_v.at[k]])

    return scatter(table, slots.reshape(K * T))


EXPERT_SUB_ROWS = 256
CAST_ELEMS = 32 * 1024
BF16_TILE_ROWS = 16


def _expert_kernel(blk_e_ref, n_valid_ref, next_e_ref, xs_ref, wg_hbm, bg_ref, wl_hbm, bl_ref,
                   wo_hbm, bo_ref, y_ref, wg_st, wl_st, wo_st, wgb_ref, wlb_ref, wob_ref, sem,
                   *, expert_base):
    i = pl.program_id(0)
    e = blk_e_ref[i]
    n_valid = n_valid_ref[i]
    new_expert = jnp.logical_or(i == 0, e != blk_e_ref[jnp.maximum(i - 1, 0)])

    def weight_copies(expert):
        idx = expert_base + expert
        return (pltpu.make_async_copy(wg_hbm.at[idx], wg_st, sem.at[0]),
                pltpu.make_async_copy(wl_hbm.at[idx], wl_st, sem.at[1]),
                pltpu.make_async_copy(wo_hbm.at[idx], wo_st, sem.at[2]))

    @pl.when(i == 0)
    def _():
        for cp in weight_copies(e):
            cp.start()

    @pl.when(jnp.logical_and(new_expert, n_valid > 0))
    def _():
        for cp in weight_copies(e):
            cp.wait()
        for src, dst in ((wg_st, wgb_ref), (wl_st, wlb_ref), (wo_st, wob_ref)):
            n_rows = CAST_ELEMS // src.shape[1] // BF16_TILE_ROWS * BF16_TILE_ROWS
            assert n_rows > 0 and src.shape[0] % n_rows == 0

            def cast_rows(r, carry, src=src, dst=dst, n_rows=n_rows):
                rows = pl.ds(pl.multiple_of(r * n_rows, n_rows), n_rows)
                dst[rows, :] = src[rows, :].astype(BF16)
                return carry

            lax.fori_loop(0, src.shape[0] // n_rows, cast_rows, 0, unroll=4)
        nxt = next_e_ref[i]

        @pl.when(nxt >= 0)
        def _():
            for cp in weight_copies(nxt):
                cp.start()

    half = wgb_ref.shape[0] // 2
    for first in range(0, xs_ref.shape[0], EXPERT_SUB_ROWS):
        rs = slice(first, first + EXPERT_SUB_ROWS)

        @pl.when(n_valid > first)
        def _():
            live = first + lax.broadcasted_iota(I32, (EXPERT_SUB_ROWS, 1), 0) < n_valid
            lo, hi = _unpack_halves(jnp.where(live, xs_ref[rs, :], 0))
            lo = lo.astype(BF16)
            hi = hi.astype(BF16)
            a = _dot(lo, wgb_ref[:half, :]) + _dot(hi, wgb_ref[half:, :]) + bg_ref[0]
            l = _dot(lo, wlb_ref[:half, :]) + _dot(hi, wlb_ref[half:, :]) + bl_ref[0]
            a = jnp.minimum(a, SWIGLU_LIMIT)
            l = jnp.clip(l, -SWIGLU_LIMIT, SWIGLU_LIMIT)
            act = a * jax.nn.sigmoid(SWIGLU_ALPHA * a) * (l + 1.0)
            y_ref[rs, :] = _pack_halves(_dot(act.astype(BF16), wob_ref[...]) + bo_ref[0])

        @pl.when(n_valid <= first)
        def _():
            y_ref[rs, :] = jnp.zeros((EXPERT_SUB_ROWS, y_ref.shape[1]), y_ref.dtype)


def _experts(xs, blk_e, n_valid, next_e, layer, w_glu, b_glu, w_lin, b_lin, w_out, b_out, *,
             blk):
    P, pw = xs.shape
    L, E, D, F = w_glu.shape
    n_blocks = P // blk
    bmap = lambda i, be, nv, ne: (layer * E + be[i], 0, 0)
    rows = lambda i, be, nv, ne: (i, 0)
    w_glu, w_lin = w_glu.reshape(L * E, D, F), w_lin.reshape(L * E, D, F)
    w_out = w_out.reshape(L * E, F, D)
    b_glu, b_lin = b_glu.reshape(L * E, 1, F), b_lin.reshape(L * E, 1, F)
    b_out = b_out.reshape(L * E, 1, D)
    hbm = pl.BlockSpec(memory_space=pl.ANY)
    grid_spec = pltpu.PrefetchScalarGridSpec(
        num_scalar_prefetch=3,
        grid=(n_blocks,),
        in_specs=[pl.BlockSpec((blk, pw), rows),
                  hbm, pl.BlockSpec((1, 1, F), bmap),
                  hbm, pl.BlockSpec((1, 1, F), bmap),
                  hbm, pl.BlockSpec((1, 1, D), bmap)],
        out_specs=pl.BlockSpec((blk, pw), rows),
        scratch_shapes=[pltpu.VMEM((D, F), F32), pltpu.VMEM((D, F), F32),
                        pltpu.VMEM((F, D), F32),
                        pltpu.VMEM((D, F), BF16), pltpu.VMEM((D, F), BF16),
                        pltpu.VMEM((F, D), BF16),
                        pltpu.SemaphoreType.DMA((3,))],
    )
    return pl.pallas_call(
        functools.partial(_expert_kernel, expert_base=layer * E),
        grid_spec=grid_spec,
        out_shape=jax.ShapeDtypeStruct((P, pw), xs.dtype),
        compiler_params=_cparams(("arbitrary",)),
        name="moe_experts",
    )(blk_e, n_valid, next_e, xs, w_glu, b_glu, w_lin, b_lin, w_out, b_out)


def _combine_kernel(ys_ref, tg_ref, x_ref, gate_ref, *rest, tail):
    half = x_ref.shape[1] // 2
    tg = tg_ref[...]
    lo_sum = hi_sum = None
    for k in range(ys_ref.shape[0]):
        lo, hi = _unpack_halves(ys_ref[k])
        gk = tg[:, k:k + 1]
        lo_sum = gk * lo if lo_sum is None else lo_sum + gk * lo
        hi_sum = gk * hi if hi_sum is None else hi_sum + gk * hi
    x_lo = x_ref[:, :half] + gate_ref[0, :, :half] * lo_sum
    x_hi = x_ref[:, half:] + gate_ref[0, :, half:] * hi_sum
    ms = (jnp.sum(x_lo * x_lo, axis=-1, keepdims=True)
          + jnp.sum(x_hi * x_hi, axis=-1, keepdims=True)) / x_ref.shape[1]
    inv = lax.rsqrt(ms + RMS_EPS)
    if tail == "final":
        g_ref, o_ref = rest
        o_ref[:, :half] = x_lo * inv * g_ref[:, :half]
        o_ref[:, half:] = x_hi * inv * g_ref[:, half:]
    else:
        g_ref, sc_ref, sh_ref, o_ref, h_ref = rest
        o_ref[:, :half] = x_lo
        o_ref[:, half:] = x_hi
        for cols, xs in ((slice(0, half), x_lo), (slice(half, 2 * half), x_hi)):
            h = xs * inv * g_ref[:, cols] * (1.0 + sc_ref[0, :, cols]) + sh_ref[0, :, cols]
            h_ref[:, cols] = h.astype(BF16)


def _combine(ys, tok_gate, x, gate, seq, *, final_g=None, next_norm=None, tm=512):
    K, T, pw = ys.shape
    D = x.shape[1]
    bmap = lambda i: ((i * tm) // seq, 0, 0)
    rows = pl.BlockSpec((tm, D), lambda i: (i, 0))
    vec = pl.BlockSpec((1, D), lambda i: (0, 0))
    per_batch = pl.BlockSpec((1, 1, D), bmap)
    in_specs = [pl.BlockSpec((K, tm, pw), lambda i: (0, i, 0)),
                pl.BlockSpec((tm, K), lambda i: (i, 0)), rows, per_batch]
    args = [ys, tok_gate, x, gate]
    if final_g is not None:
        tail = "final"
        in_specs.append(vec)
        args.append(final_g)
        out_specs, out_shape = rows, jax.ShapeDtypeStruct((T, D), F32)
    else:
        tail = "next"
        in_specs += [vec, per_batch, per_batch]
        args += list(next_norm)
        out_specs = [rows, rows]
        out_shape = [jax.ShapeDtypeStruct((T, D), F32), jax.ShapeDtypeStruct((T, D), BF16)]
    return pl.pallas_call(
        functools.partial(_combine_kernel, tail=tail),
        grid=(T // tm,),
        in_specs=in_specs,
        out_specs=out_specs,
        out_shape=out_shape,
        compiler_params=_cparams(("parallel",)),
        name="moe_combine",
    )(*args)


MOE_BLOCK = 512


def _slots_kernel(start_ref, idx_ref, rank_ref, o_ref):
    idx = idx_ref[...]
    acc = rank_ref[...]
    for e in range(start_ref.shape[0]):
        acc = acc + jnp.where(idx == e, start_ref[e], 0)
    o_ref[...] = acc


def _slots(idx_t, rank_t, group_start):
    full = pl.BlockSpec(idx_t.shape, lambda i, gs: (0, 0))
    return pl.pallas_call(
        _slots_kernel,
        grid_spec=pltpu.PrefetchScalarGridSpec(num_scalar_prefetch=1, grid=(1,),
                                               in_specs=[full, full], out_specs=full),
        out_shape=jax.ShapeDtypeStruct(idx_t.shape, I32),
        compiler_params=_cparams(("arbitrary",)),
        name="moe_slots",
    )(group_start, idx_t, rank_t)


def _dispatch_plan(idx_t, rank_t, counts, blk):
    K, T = idx_t.shape
    E = counts.shape[0]
    padded = (counts + blk - 1) // blk * blk
    pad_end = jnp.cumsum(padded)
    pad_start = pad_end - padded
    experts = jnp.arange(E, dtype=I32)
    slots = _slots(idx_t, rank_t, pad_start.astype(I32))
    n_blocks = -(-(K * T + E * (blk - 1)) // blk)
    blk_start = jnp.arange(n_blocks, dtype=I32) * blk
    blk_e = jnp.minimum(jnp.sum(pad_end[None, :] <= blk_start[:, None], axis=1), E - 1).astype(I32)
    n_valid = jnp.clip(pad_start[blk_e] + counts[blk_e] - blk_start, 0, blk).astype(I32)
    later = jnp.where(counts > 0, experts, E)
    after = lax.cummin(jnp.concatenate([later[1:], jnp.full((1,), E, I32)]), reverse=True)
    next_e = jnp.where(after < E, after, -1).astype(I32)[blk_e]
    return slots, blk_e, n_valid, next_e, n_blocks * blk


def _mixer_out_moe(y, w_mix, x, mix_gate, g, scale, shift, gate, r_w, r_b, layer,
                   w_glu, b_glu, w_lin, b_lin, w_out, b_out, seq, **tail):
    T, D = x.shape
    x, hp, idx_t, gate_t, rank_t, counts = _mixer_out_route(
        y, w_mix, x, mix_gate, g, scale, shift, r_w, r_b, seq)
    slots, blk_e, n_valid, next_e, n_rows = _dispatch_plan(idx_t, rank_t, counts[:, 0],
                                                           MOE_BLOCK)
    xs = _sc_scatter_rows(hp, slots, n_rows)
    yp = _experts(xs, blk_e, n_valid, next_e, layer, w_glu, b_glu, w_lin, b_lin, w_out, b_out,
                  blk=MOE_BLOCK)
    ys = _sc_gather(yp, slots.reshape(TOP_K * T)).reshape(TOP_K, T, yp.shape[1])
    return _combine(ys, jnp.transpose(gate_t), x, gate, seq, **tail)


def kernel(x, c, norm_g, ada_w, final_norm_g, sg_w_in, sg_ln_g, sg_ln_b, sg_w_s, sg_b_s, sg_w_out, pool_w_in, pool_w_grp, pool_ls, pool_w_out, hgrn_w_in, hgrn_lb_logits, hgrn_gnorm_g, hgrn_w_out, router_w, router_b, expert_w_glu, expert_b_glu, expert_w_lin, expert_b_lin, expert_w_out, expert_b_out):
    B, S, D = x.shape
    depth = norm_g.shape[0]
    T = B * S
    xf = x.reshape(T, D)

    ada_w = ada_w.reshape(depth * 2, D, 3 * D)

    def mod_parts(layer, sub):
        m = _ada_mod(c, ada_w, layer * 2 + sub)[:, None, :]
        return m[..., :D], m[..., D:2 * D], m[..., 2 * D:]

    def mixer_norm(layer, shift, scale):
        return norm_g[layer, 0][None, :], scale, shift

    lb_cum = jnp.cumsum(jax.nn.softmax(hgrn_lb_logits.astype(F32), axis=0), axis=0)
    lower_bounds = lb_cum - lb_cum[0]

    shift, scale, mix_gate = mod_parts(0, 0)
    src, norm = xf, mixer_norm(0, shift, scale)
    for layer in range(depth):
        kind, slot = layer % 3, layer // 3
        if kind == 0:
            z = _norm_matmul(src, norm, sg_w_in[slot].astype(BF16), S, act="gelu",
                             out_dtype=BF16)
            y = _spatial_gate(z, sg_ln_g[slot][None, :], sg_ln_b[slot][None, :],
                              sg_w_s[slot], sg_b_s[slot])
            w_out = sg_w_out[slot]
        elif kind == 1:
            z = _norm_matmul(src, norm, pool_w_in[slot].astype(BF16), S, act=None,
                             out_dtype=F32)
            y = _pool_mix(z, pool_w_grp[slot].astype(BF16), pool_ls[slot][None, :], S)
            w_out = pool_w_out[slot]
        else:
            proj, f_pre = _norm_matmul(src, norm, hgrn_w_in[slot].astype(BF16), S, act=None,
                                       out_dtype=BF16, f32_col=1, tn=D)
            y = _hgrn_mix(proj, f_pre, lower_bounds[layer][None, :],
                          hgrn_gnorm_g[slot][None, :], B, S)
            w_out = hgrn_w_out[slot]

        moe_shift, moe_scale, moe_gate = mod_parts(layer, 1)
        moe_args = (y, w_out.astype(BF16), xf, mix_gate, norm_g[layer, 1][None, :],
                    moe_scale, moe_shift, moe_gate, router_w[layer], router_b[layer], layer,
                    expert_w_glu, expert_b_glu, expert_w_lin, expert_b_lin,
                    expert_w_out, expert_b_out, S)
        if layer == depth - 1:
            xf = _mixer_out_moe(*moe_args, final_g=final_norm_g[None, :])
        else:
            shift, scale, mix_gate = mod_parts(layer + 1, 0)
            xf, src = _mixer_out_moe(*moe_args, next_norm=mixer_norm(layer + 1, shift, scale))
            norm = None

    return xf.reshape(B, S, D)
```

```python
import functools

import numpy as np
import jax
import jax.numpy as jnp
from jax import lax
from jax.experimental import pallas as pl
from jax.experimental.pallas import tpu as pltpu
from jax.experimental.pallas import tpu_sc as plsc

F32 = jnp.float32
BF16 = jnp.bfloat16
I32 = jnp.int32
U32 = jnp.uint32

RMS_EPS = 1e-5
LN_EPS = 1e-5
SWIGLU_ALPHA = 1.702
SWIGLU_LIMIT = 7.0
TOP_K = 4
SG_CHUNK = 128
POOL_WINDOWS = (2, 4, 8, 16)
HEAD_DIM = 128
LANES = 128
SC_WORKERS = 32
SC_ROWS_PER_STEP = 64

VMEM_LIMIT = 48 * 1024 * 1024
BIG_VMEM_LIMIT = 56 * 1024 * 1024


def _cparams(sem):
    return pltpu.CompilerParams(dimension_semantics=sem, vmem_limit_bytes=VMEM_LIMIT)


def _dot(a, b):
    return jnp.dot(a, b, preferred_element_type=F32)


def _dot_nt(a, b):
    return lax.dot_general(a, b, (((1,), (1,)), ((), ())), preferred_element_type=F32)


def _dot_tn(a, b):
    return lax.dot_general(a, b, (((0,), (0,)), ((), ())), preferred_element_type=F32)


def _split_bf16(x):
    hi = x.astype(BF16)
    lo = (x - hi.astype(F32)).astype(BF16)
    return hi, lo


def _packed_layout(d):
    return d // 2, I32


def _pack_halves(y):
    n = y.shape[1] // 2
    lo = lax.bitcast_convert_type(y[:, :n].astype(BF16).astype(F32), U32)
    hi = lax.bitcast_convert_type(y[:, n:].astype(BF16).astype(F32), U32)
    word = lax.shift_right_logical(lo, jnp.uint32(16)) | (hi & jnp.uint32(0xFFFF0000))
    return lax.bitcast_convert_type(word, I32)


def _unpack_halves(w):
    u = lax.bitcast_convert_type(w, U32)
    lo = lax.bitcast_convert_type(lax.shift_left(u, jnp.uint32(16)), F32)
    hi = lax.bitcast_convert_type(u & jnp.uint32(0xFFFF0000), F32)
    return lo, hi


def _modulated_norm(x, g, scale, shift):
    ms = jnp.mean(x * x, axis=-1, keepdims=True)
    y = x * lax.rsqrt(ms + RMS_EPS) * g
    return y * (1.0 + scale) + shift


def _ada_kernel(c_ref, w_ref, o_ref):
    w = w_ref[0]
    reps = w.shape[1] // LANES
    for b in range(c_ref.shape[0]):
        cb = c_ref[b]
        s = jnp.tile(cb * jax.nn.sigmoid(cb), (1, reps))
        o_ref[b] = jnp.sum(w * s, axis=0, keepdims=True)


def _ada_mod(c, ada_w, index):
    L, D, N = ada_w.shape
    B = c.shape[0]
    tn = 512
    c_cols = jnp.broadcast_to(c[:, :, None], (B, D, LANES))
    out = pl.pallas_call(
        _ada_kernel,
        grid=(N // tn,),
        in_specs=[pl.BlockSpec((B, D, LANES), lambda j: (0, 0, 0)),
                  pl.BlockSpec((1, D, tn), lambda j: (index, 0, j))],
        out_specs=pl.BlockSpec((B, 1, tn), lambda j: (0, 0, j)),
        out_shape=jax.ShapeDtypeStruct((B, 1, N), F32),
        compiler_params=_cparams(("parallel",)),
        name="ada_mod",
    )(c_cols, ada_w)
    return out[:, 0]


def _nm_kernel(*refs, act, f32_col, prenormed):
    if prenormed:
        h_ref, w_ref, o_ref = refs[:3]
        outs = refs[3:]
    else:
        x_ref, g_ref, sc_ref, sh_ref, w_ref, o_ref = refs[:6]
        outs, h_ref = refs[6:-1], refs[-1]

        @pl.when(pl.program_id(1) == 0)
        def _():
            h = _modulated_norm(x_ref[...], g_ref[...], sc_ref[0], sh_ref[0])
            h_ref[...] = h.astype(BF16)

    acc = _dot(h_ref[...], w_ref[...])
    if act == "gelu":
        acc = 0.5 * acc * (1.0 + lax.erf(acc * np.float32(1.0 / np.sqrt(2.0))))
    o_ref[...] = acc.astype(o_ref.dtype)
    if f32_col is not None:
        @pl.when(pl.program_id(1) == f32_col)
        def _():
            outs[0][...] = acc


def _norm_matmul(x, norm, w, seq, *, act, out_dtype, f32_col=None, tm=512, tn=2048):
    T, D = x.shape
    N = w.shape[1]
    prenormed = norm is None
    bmap = lambda i, j: ((i * tm) // seq, 0, 0)
    out_specs = [pl.BlockSpec((tm, tn), lambda i, j: (i, j))]
    out_shape = [jax.ShapeDtypeStruct((T, N), out_dtype)]
    if f32_col is not None:
        out_specs.append(pl.BlockSpec((tm, tn), lambda i, j: (i, 0)))
        out_shape.append(jax.ShapeDtypeStruct((T, tn), F32))
    in_specs = [pl.BlockSpec((tm, D), lambda i, j: (i, 0))]
    args = [x]
    if not prenormed:
        in_specs += [pl.BlockSpec((1, D), lambda i, j: (0, 0)),
                     pl.BlockSpec((1, 1, D), bmap),
                     pl.BlockSpec((1, 1, D), bmap)]
        args += list(norm)
    in_specs.append(pl.BlockSpec((D, tn), lambda i, j: (0, j)))
    args.append(w)
    outs = pl.pallas_call(
        functools.partial(_nm_kernel, act=act, f32_col=f32_col, prenormed=prenormed),
        grid=(T // tm, N // tn),
        in_specs=in_specs,
        out_specs=out_specs,
        out_shape=out_shape,
        scratch_shapes=[] if prenormed else [pltpu.VMEM((tm, D), BF16)],
        compiler_params=pltpu.CompilerParams(dimension_semantics=("parallel", "arbitrary"),
                                             vmem_limit_bytes=BIG_VMEM_LIMIT),
        name="norm_matmul_" + str(act),
    )(*args)
    return outs if f32_col is not None else outs[0]


def _sg_kernel(u_ref, v_ref, lng_ref, lnb_ref, ws_ref, bst_ref, y_ref, vn_ref, *, heads):
    v = v_ref[...].astype(F32)
    mu = jnp.mean(v, axis=-1, keepdims=True)
    vc = v - mu
    var = jnp.mean(vc * vc, axis=-1, keepdims=True)
    vn_ref[...] = (vc * lax.rsqrt(var + LN_EPS) * lng_ref[...] + lnb_ref[...]).astype(BF16)

    n_chunks = v_ref.shape[0] // SG_CHUNK
    row = lax.broadcasted_iota(I32, (SG_CHUNK, SG_CHUNK), 0)
    col = lax.broadcasted_iota(I32, (SG_CHUNK, SG_CHUNK), 1)
    causal = row >= col
    for hd in range(heads):
        cs = slice(hd * HEAD_DIM, (hd + 1) * HEAD_DIM)
        wm = jnp.where(causal, ws_ref[hd], 0.0).astype(BF16)
        bias = bst_ref[:, hd:hd + 1]
        for ci in range(n_chunks):
            rs = slice(ci * SG_CHUNK, (ci + 1) * SG_CHUNK)
            mixed = _dot(wm, vn_ref[rs, cs]) + bias
            y_ref[rs, cs] = (u_ref[rs, cs].astype(F32) * mixed).astype(BF16)


def _spatial_gate(z, ln_g, ln_b, w_s, b_s, *, tm=512):
    T, two_w = z.shape
    W = two_w // 2
    heads = w_s.shape[0]
    return pl.pallas_call(
        functools.partial(_sg_kernel, heads=heads),
        grid=(T // tm,),
        in_specs=[pl.BlockSpec((tm, W), lambda i: (i, 0)),
                  pl.BlockSpec((tm, W), lambda i: (i, 1)),
                  pl.BlockSpec((1, W), lambda i: (0, 0)),
                  pl.BlockSpec((1, W), lambda i: (0, 0)),
                  pl.BlockSpec((heads, SG_CHUNK, SG_CHUNK), lambda i: (0, 0, 0)),
                  pl.BlockSpec((SG_CHUNK, heads), lambda i: (0, 0))],
        out_specs=pl.BlockSpec((tm, W), lambda i: (i, 0)),
        out_shape=jax.ShapeDtypeStruct((T, W), BF16),
        scratch_shapes=[pltpu.VMEM((tm, W), BF16)],
        compiler_params=_cparams(("parallel",)),
        name="spatial_gate",
    )(z, z, ln_g, ln_b, w_s, jnp.transpose(b_s))


def _sg_fused_kernel(*refs, heads, prenormed):
    if prenormed:
        h_ref, w_ref = refs[:2]
        rest = refs[2:]
        h = h_ref[...]
    else:
        x_ref, g_ref, sc_ref, sh_ref, w_ref = refs[:5]
        rest = refs[5:]
        h = _modulated_norm(x_ref[...], g_ref[...], sc_ref[0], sh_ref[0]).astype(BF16)
    lng_ref, lnb_ref, ws_ref, bst_ref, y_ref, u_ref, v_ref, vn_ref = rest
    width = u_ref.shape[1]
    inv_sqrt2 = np.float32(1.0 / np.sqrt(2.0))
    for dst, cols in ((v_ref, slice(width, 2 * width)), (u_ref, slice(0, width))):
        acc = _dot(h, w_ref[:, cols])
        dst[...] = (0.5 * acc * (1.0 + lax.erf(acc * inv_sqrt2))).astype(BF16)
    _sg_kernel(u_ref, v_ref, lng_ref, lnb_ref, ws_ref, bst_ref, y_ref, vn_ref, heads=heads)


def _spatial_mixer(x, norm, w_in, ln_g, ln_b, w_s, b_s, seq, *, tm=256):
    T, D = x.shape
    W = w_in.shape[1] // 2
    heads = w_s.shape[0]
    prenormed = norm is None
    bmap = lambda i: ((i * tm) // seq, 0, 0)
    const = lambda i: (0, 0)
    in_specs = [pl.BlockSpec((tm, D), lambda i: (i, 0))]
    args = [x]
    if not prenormed:
        in_specs += [pl.BlockSpec((1, D), const), pl.BlockSpec((1, 1, D), bmap),
                     pl.BlockSpec((1, 1, D), bmap)]
        args += list(norm)
    in_specs += [pl.BlockSpec((D, 2 * W), const, pipeline_mode=pl.Buffered(1)),
                 pl.BlockSpec((1, W), const),
                 pl.BlockSpec((1, W), const),
                 pl.BlockSpec((heads, SG_CHUNK, SG_CHUNK), lambda i: (0, 0, 0)),
                 pl.BlockSpec((SG_CHUNK, heads), const)]
    args += [w_in, ln_g, ln_b, w_s, jnp.transpose(b_s)]
    return pl.pallas_call(
        functools.partial(_sg_fused_kernel, heads=heads, prenormed=prenormed),
        grid=(T // tm,),
        in_specs=in_specs,
        out_specs=pl.BlockSpec((tm, W), lambda i: (i, 0)),
        out_shape=jax.ShapeDtypeStruct((T, W), BF16),
        scratch_shapes=[pltpu.VMEM((tm, W), BF16), pltpu.VMEM((tm, W), BF16),
                        pltpu.VMEM((tm, W), BF16)],
        compiler_params=pltpu.CompilerParams(dimension_semantics=("parallel",),
                                             vmem_limit_bytes=BIG_VMEM_LIMIT),
        name="spatial_mixer",
    )(*args)


POOL_HALO = 16


def _pool_kernel(z_ref, halo_ref, wg_ref, ls_ref, y_ref, *, seq):
    tm = z_ref.shape[0]
    gdim = wg_ref.shape[1]
    pos0 = (pl.program_id(0) * tm) % seq
    halo_on = (pos0 > 0).astype(F32)
    pos = pos0 + lax.broadcasted_iota(I32, (tm, 1), 0)
    for gi, wnd in enumerate(POOL_WINDOWS):
        cs = slice(gi * gdim, (gi + 1) * gdim)
        zg = z_ref[:, cs]
        s = jnp.concatenate([halo_ref[:, cs] * halo_on, zg], axis=0)
        k = 1
        while k < wnd:
            s = s + pltpu.roll(s, k, 0)
            k *= 2
        cnt = jnp.minimum(pos + 1, wnd).astype(F32)
        pooled = s[POOL_HALO:, :] / cnt - zg
        yg = _dot(pooled.astype(BF16), wg_ref[gi]) * ls_ref[:, cs]
        y_ref[:, cs] = yg.astype(BF16)


def _pool_mix(z, w_grp, ls, seq, *, tm=512):
    T, D = z.shape
    G, gdim, _ = w_grp.shape
    assert max(POOL_WINDOWS) <= POOL_HALO and tm % POOL_HALO == 0
    ratio = tm // POOL_HALO
    return pl.pallas_call(
        functools.partial(_pool_kernel, seq=seq),
        grid=(T // tm,),
        in_specs=[pl.BlockSpec((tm, D), lambda i: (i, 0)),
                  pl.BlockSpec((POOL_HALO, D), lambda i: (jnp.maximum(i * ratio - 1, 0), 0)),
                  pl.BlockSpec((G, gdim, gdim), lambda i: (0, 0, 0)),
                  pl.BlockSpec((1, D), lambda i: (0, 0))],
        out_specs=pl.BlockSpec((tm, D), lambda i: (i, 0)),
        out_shape=jax.ShapeDtypeStruct((T, D), BF16),
        compiler_params=_cparams(("parallel",)),
        name="pool_mix",
    )(z, z, w_grp, ls)


HGRN_CHUNK = 128
HGRN_BASE = 32
HGRN_GUARD_LOG2 = 100.0
LOG2_E = 1.4426950408889634


def _hgrn_tables(c=HGRN_CHUNK, base=HGRN_BASE):
    t = np.arange(c)
    j = t[None, :]
    sums = [j <= t[:, None]]
    masks = []
    n_shallow = 0
    h = c // 2
    while h >= 1:
        off = t % (2 * h)
        mid = (t // (2 * h)) * (2 * h) + h
        second = off >= h
        if h >= 2:
            m_q = second[:, None] & (j >= mid[:, None]) & (j <= t[:, None])
            m_k = (~second)[:, None] & (j > t[:, None]) & (j < mid[:, None])
            sums.append(m_q | m_k)
        same = (t[:, None] // (2 * h)) == (t[None, :] // (2 * h))
        masks.append(same & second[:, None] & (~second)[None, :])
        n_shallow += h >= base
        h //= 2
    same_base = (t[:, None] // base) == (t[None, :] // base)
    sums.append(same_base & (j <= t[:, None]))
    sums = np.stack(sums).astype(np.float32)
    base_mask = (same_base & (j <= t[:, None])).astype(np.float32)
    return (np.concatenate([sums, sums], axis=2), np.stack(masks).astype(np.float32), base_mask,
            n_shallow)


def _hgrn_kernel(q_ref, f_ref, i_ref, g_ref, lb_ref, gn_ref, sums_ref, masks_ref, bmask_ref,
                 o_ref, st_ref, qb_ref, ke_ref, z_ref, qd_ref, kd_ref, vb_ref, eb_ref, sc_ref,
                 *, heads, n_shallow):
    c = q_ref.shape[0]
    n_sum = sums_ref.shape[0] - 1
    n_lvl = masks_ref.shape[0]

    @pl.when(pl.program_id(1) == 0)
    def _():
        st_ref[...] = jnp.zeros_like(st_ref)

    lb = lb_ref[...]
    sig = jax.nn.sigmoid(f_ref[...])
    fg = lb + (1.0 - lb) * sig
    lf2 = jnp.log(fg) * np.float32(LOG2_E)
    kk = (1.0 - lb) * (1.0 - sig)
    lf_hi, lf_lo = _split_bf16(lf2)
    lf_cat = jnp.concatenate([lf_hi, lf_lo], axis=0)
    qr = q_ref[...].astype(F32)
    q = qr * jax.nn.sigmoid(qr)
    vb_ref[...] = i_ref[...].astype(BF16)
    q16 = q.astype(BF16)
    k16 = kk.astype(BF16)

    b2 = _dot(sums_ref[0], lf_cat)
    eb = jnp.exp2(b2)
    qb_ref[...] = (q * eb).astype(BF16)
    eb_ref[...] = eb[c - 1:c, :]
    ke_ref[...] = (kk * jnp.exp2(b2[c - 1:c, :] - b2)).astype(BF16)
    w2 = _dot(sums_ref[n_sum], lf_cat)
    shallow = jnp.min(w2) >= -HGRN_GUARD_LOG2

    row = lax.broadcasted_iota(I32, (c, 1), 0)

    def level(l):
        half = c >> (l + 1)
        second = (row & (2 * half - 1)) >= half
        if l + 1 < n_sum:
            x = jnp.exp2(_dot(sums_ref[l + 1], lf_cat).astype(BF16))
        else:
            x = jnp.where(second, fg, 1.0).astype(BF16)
        z_ref[l] = jnp.where(second, q16, k16) * x

    gn = gn_ref[...]

    def finish(hd, cs, o):
        st = st_ref[hd]
        vh = vb_ref[:, cs]
        o = o + _dot_nt(qb_ref[:, cs], st.astype(BF16))
        st_ref[hd] = st * eb_ref[:, cs] + _dot_tn(vh, ke_ref[:, cs])
        on = o * lax.rsqrt(jnp.mean(o * o, axis=-1, keepdims=True) + RMS_EPS) * gn
        gv = g_ref[:, cs].astype(F32)
        o_ref[:, cs] = (on * (gv * jax.nn.sigmoid(gv))).astype(BF16)

    def level_scores(cs, n):
        scores = None
        for l in range(n):
            half = c >> (l + 1)
            zl = z_ref[l, :, cs]
            if half >= 16:
                starts = list(range(half, c, 2 * half))
                zq = jnp.concatenate([z_ref[l, r:r + half, cs] for r in starts], axis=0)
                p = _dot_nt(zq, zl).astype(BF16)
                parts = []
                for n_i, r in enumerate(starts):
                    parts.append(jnp.zeros((half, c), BF16))
                    parts.append(masks_ref[l, r:r + half, :] * p[n_i * half:(n_i + 1) * half])
                term = jnp.concatenate(parts, axis=0)
            else:
                term = masks_ref[l] * _dot_nt(zl, zl).astype(BF16)
            scores = term if scores is None else scores + term
        return scores

    @pl.when(shallow)
    def _():
        for l in range(n_shallow):
            level(l)
        qd_ref[...] = (q * jnp.exp2(w2)).astype(BF16)
        kd_ref[...] = (kk * jnp.exp2(-w2)).astype(BF16)

        def head_scores(hd, carry):
            cs = pl.ds(pl.multiple_of(hd * HEAD_DIM, HEAD_DIM), HEAD_DIM)
            sc_ref[hd] = (level_scores(cs, n_shallow)
                          + bmask_ref[...] * _dot_nt(qd_ref[:, cs], kd_ref[:, cs]).astype(BF16))
            return carry

        def head_out(hd, carry):
            cs = pl.ds(pl.multiple_of(hd * HEAD_DIM, HEAD_DIM), HEAD_DIM)
            finish(hd, cs, _dot(sc_ref[hd], vb_ref[:, cs]))
            return carry

        lax.fori_loop(0, heads, head_scores, 0, unroll=16)
        lax.fori_loop(0, heads, head_out, 0, unroll=16)

    @pl.when(jnp.logical_not(shallow))
    def _():
        for l in range(n_lvl):
            level(l)
        qd_ref[...] = q16
        kd_ref[...] = k16

        def head(hd, carry):
            cs = pl.ds(pl.multiple_of(hd * HEAD_DIM, HEAD_DIM), HEAD_DIM)
            vh = vb_ref[:, cs]
            diag = jnp.sum(qd_ref[:, cs].astype(F32) * kd_ref[:, cs].astype(F32), axis=-1,
                           keepdims=True)
            finish(hd, cs, _dot(level_scores(cs, n_lvl), vh) + diag * vh.astype(F32))
            return carry

        lax.fori_loop(0, heads, head, 0, unroll=4)


def _hgrn_mix(proj, f_pre, lb, gnorm_g, batch, seq):
    T, four_d = proj.shape
    D = four_d // 4
    heads = D // HEAD_DIM
    c = HGRN_CHUNK
    n_chunks = seq // c
    sums, masks, base_mask, n_shallow = _hgrn_tables(c)
    n_sum, n_mask = sums.shape[0], masks.shape[0]
    row = lambda b, i: b * n_chunks + i
    return pl.pallas_call(
        functools.partial(_hgrn_kernel, heads=heads, n_shallow=n_shallow),
        grid=(batch, n_chunks),
        in_specs=[pl.BlockSpec((c, D), lambda b, i: (row(b, i), 0)),
                  pl.BlockSpec((c, D), lambda b, i: (row(b, i), 0)),
                  pl.BlockSpec((c, D), lambda b, i: (row(b, i), 2)),
                  pl.BlockSpec((c, D), lambda b, i: (row(b, i), 3)),
                  pl.BlockSpec((1, D), lambda b, i: (0, 0)),
                  pl.BlockSpec((1, HEAD_DIM), lambda b, i: (0, 0)),
                  pl.BlockSpec((n_sum, c, 2 * c), lambda b, i: (0, 0, 0)),
                  pl.BlockSpec((n_mask, c, c), lambda b, i: (0, 0, 0)),
                  pl.BlockSpec((c, c), lambda b, i: (0, 0))],
        out_specs=pl.BlockSpec((c, D), lambda b, i: (row(b, i), 0)),
        out_shape=jax.ShapeDtypeStruct((T, D), BF16),
        scratch_shapes=[pltpu.VMEM((heads, HEAD_DIM, HEAD_DIM), F32),
                        pltpu.VMEM((c, D), BF16),
                        pltpu.VMEM((c, D), BF16),
                        pltpu.VMEM((n_mask, c, D), BF16),
                        pltpu.VMEM((c, D), BF16),
                        pltpu.VMEM((c, D), BF16),
                        pltpu.VMEM((c, D), BF16),
                        pltpu.VMEM((1, D), F32),
                        pltpu.VMEM((heads, c, c), BF16)],
        compiler_params=_cparams(("parallel", "arbitrary")),
        name="hgrn_mix",
    )(proj, f_pre, proj, proj, lb, gnorm_g, jnp.asarray(sums, BF16), jnp.asarray(masks, BF16),
      jnp.asarray(base_mask, BF16))


def _route_kernel(y_ref, w_ref, xin_ref, mgate_ref, g_ref, sc_ref, sh_ref, rw_ref,
                  rb_ref, tri_ref, x_ref, hp_ref, idx_ref, gate_ref, rank_ref, cnt_ref, carry_ref,
                  *, n_experts):
    @pl.when(pl.program_id(0) == 0)
    def _():
        carry_ref[...] = jnp.zeros_like(carry_ref)

    x = xin_ref[...] + mgate_ref[0] * _dot(y_ref[...], w_ref[...])
    x_ref[...] = x
    h = _modulated_norm(x, g_ref[...], sc_ref[0], sh_ref[0])
    hp_ref[...] = _pack_halves(h)

    h_hi, h_lo = _split_bf16(h)
    both = _dot(h_hi, rw_ref[...])
    logits = (both[:, :LANES] + both[:, LANES:] + _dot(h_lo, rw_ref[:, :LANES])) + rb_ref[...]
    lt = jnp.transpose(logits)[:n_experts, :]
    eidx = lax.broadcasted_iota(I32, lt.shape, 0)
    vals, idxs, hits = [], [], []
    for _ in range(TOP_K):
        m = jnp.max(lt, axis=0, keepdims=True)
        sel = jnp.min(jnp.where(lt == m, eidx, n_experts), axis=0, keepdims=True)
        hit = eidx == sel
        vals.append(m)
        idxs.append(sel)
        hits.append(hit)
        lt = jnp.where(hit, -jnp.inf, lt)
    exps = [jnp.exp(v - vals[0]) for v in vals]
    denom = exps[0]
    for e in exps[1:]:
        denom = denom + e
    idx_ref[...] = jnp.concatenate(idxs, axis=0)
    gate_ref[...] = jnp.concatenate([e / denom for e in exps], axis=0)

    onehot = hits[0].astype(F32)
    for hit in hits[1:]:
        onehot = onehot + hit.astype(F32)
    before = carry_ref[...] + _dot(onehot.astype(BF16), tri_ref[...])
    rank_ref[...] = jnp.concatenate(
        [jnp.sum(jnp.where(hit, before, 0.0), axis=0, keepdims=True) for hit in hits],
        axis=0).astype(I32)
    carry_ref[...] += jnp.sum(onehot, axis=1, keepdims=True)
    cnt_ref[...] = carry_ref[...].astype(I32)


def _mixer_out_route(y, w, x, mix_gate, g, scale, shift, r_w, r_b, seq, *, tm=512):
    T, K = y.shape
    D = w.shape[1]
    E = r_w.shape[1]
    rw_pad = jnp.zeros((D, LANES), F32).at[:, :E].set(r_w)
    rw_cat = jnp.concatenate(_split_bf16(rw_pad), axis=1)
    rb_pad = jnp.full((1, LANES), -1e30, F32).at[0, :E].set(r_b)
    tri = jnp.asarray(np.triu(np.ones((tm, tm), np.float32), k=1), BF16)
    pw, pdt = _packed_layout(D)
    bmap = lambda i: ((i * tm) // seq, 0, 0)
    const = lambda i: (0, 0)
    rows = lambda i: (i, 0)
    kt = pl.BlockSpec((TOP_K, tm), lambda i: (0, i))
    return pl.pallas_call(
        functools.partial(_route_kernel, n_experts=E),
        grid=(T // tm,),
        in_specs=[pl.BlockSpec((tm, K), rows),
                  pl.BlockSpec((K, D), const, pipeline_mode=pl.Buffered(1)),
                  pl.BlockSpec((tm, D), rows),
                  pl.BlockSpec((1, 1, D), bmap),
                  pl.BlockSpec((1, D), const),
                  pl.BlockSpec((1, 1, D), bmap),
                  pl.BlockSpec((1, 1, D), bmap),
                  pl.BlockSpec((D, 2 * LANES), const, pipeline_mode=pl.Buffered(1)),
                  pl.BlockSpec((1, LANES), const),
                  pl.BlockSpec((tm, tm), const, pipeline_mode=pl.Buffered(1))],
        out_specs=[pl.BlockSpec((tm, D), rows), pl.BlockSpec((tm, pw), rows), kt, kt, kt,
                   pl.BlockSpec((E, 1), const)],
        out_shape=[jax.ShapeDtypeStruct((T, D), F32),
                   jax.ShapeDtypeStruct((T, pw), pdt),
                   jax.ShapeDtypeStruct((TOP_K, T), I32),
                   jax.ShapeDtypeStruct((TOP_K, T), F32),
                   jax.ShapeDtypeStruct((TOP_K, T), I32),
                   jax.ShapeDtypeStruct((E, 1), I32)],
        scratch_shapes=[pltpu.VMEM((E, 1), F32)],
        compiler_params=_cparams(("arbitrary",)),
        name="mixer_out_route",
    )(y, w, x, mix_gate, g, scale, shift, rw_cat, rb_pad, tri)


def _sc_gather(table, idx):
    R = idx.shape[0]
    W = table.shape[1]
    rows = SC_ROWS_PER_STEP // 2
    per_worker = R // SC_WORKERS
    steps = per_worker // rows
    assert steps * rows * SC_WORKERS == R and steps % 2 == 0
    mesh = plsc.VectorSubcoreMesh(core_axis_name="c", subcore_axis_name="s")
    n_cores = mesh.num_cores

    @functools.partial(
        pl.kernel, mesh=mesh,
        out_type=jax.ShapeDtypeStruct((R, W), table.dtype),
        scratch_types=[pltpu.VMEM((steps, rows), I32),
                       pltpu.VMEM((rows, W), table.dtype),
                       pltpu.VMEM((rows, W), table.dtype),
                       pltpu.SemaphoreType.DMA((2,)),
                       pltpu.SemaphoreType.DMA((2,))],
    )
    def gather(table_hbm, idx_hbm, out_hbm, idx_v, buf0, buf1, gsem, wsem):
        wid = lax.axis_index("s") * n_cores + lax.axis_index("c")
        base = wid * per_worker
        bufs = (buf0, buf1)
        pltpu.sync_copy(idx_hbm.at[pl.ds(wid * steps, steps)], idx_v)

        def fetch(j, b):
            return pltpu.make_async_copy(table_hbm.at[idx_v.at[j]], bufs[b], gsem.at[b])

        def flush(j, b):
            off = pl.multiple_of(base + j * rows, 8)
            return pltpu.make_async_copy(bufs[b], out_hbm.at[pl.ds(off, rows)], wsem.at[b])

        fetch(0, 0).start()

        @pl.loop(0, steps, step=2)
        def _(j0):
            for b in range(2):
                j = j0 + b
                fetch(j, b).wait()
                flush(j, b).start()

                @pl.when(j + 1 < steps)
                def _():
                    @pl.when(j >= 1)
                    def _():
                        flush(j - 1, 1 - b).wait()
                    fetch(j + 1, 1 - b).start()

        flush(steps - 2, 0).wait()
        flush(steps - 1, 1).wait()

    return gather(table, idx.reshape(R // rows, rows))


def _sc_scatter_rows(table, slots, n_out):
    K, T = slots.shape
    W = table.shape[1]
    rows = SC_ROWS_PER_STEP
    per_worker = T // SC_WORKERS
    steps = per_worker // rows
    assert steps * rows * SC_WORKERS == T
    mesh = plsc.VectorSubcoreMesh(core_axis_name="c", subcore_axis_name="s")
    n_cores = mesh.num_cores

    @functools.partial(
        pl.kernel, mesh=mesh,
        out_type=jax.ShapeDtypeStruct((n_out, W), table.dtype),
        scratch_types=[pltpu.VMEM((K, rows), I32),
                       pltpu.VMEM((rows, W), table.dtype)],
    )
    def scatter(table_hbm, slots_hbm, out_hbm, idx_v, rows_v):
        wid = lax.axis_index("s") * n_cores + lax.axis_index("c")
        base = wid * per_worker

        @pl.loop(0, steps)
        def _(j):
            off = pl.multiple_of(base + j * rows, 8)
            pltpu.sync_copy(table_hbm.at[pl.ds(off, rows)], rows_v)
            for k in range(K):
                pltpu.sync_copy(slots_hbm.at[pl.ds(pl.multiple_of(k * T + off, 8), rows)],
                                idx_v.at[k])
                pltpu.sync_copy(rows_v, out_hbm.at[idx_v.at[k]])

    return scatter(table, slots.reshape(K * T))


EXPERT_SUB_ROWS = 256
CAST_ELEMS = 32 * 1024
BF16_TILE_ROWS = 16


def _expert_kernel(blk_e_ref, n_valid_ref, next_e_ref, xs_ref, wg_hbm, bg_ref, wl_hbm, bl_ref,
                   wo_hbm, bo_ref, y_ref, wg_st, wl_st, wo_st, wgb_ref, wlb_ref, wob_ref, sem,
                   *, expert_base):
    i = pl.program_id(0)
    e = blk_e_ref[i]
    n_valid = n_valid_ref[i]
    new_expert = jnp.logical_or(i == 0, e != blk_e_ref[jnp.maximum(i - 1, 0)])

    def weight_copies(expert):
        idx = expert_base + expert
        return (pltpu.make_async_copy(wg_hbm.at[idx], wg_st, sem.at[0]),
                pltpu.make_async_copy(wl_hbm.at[idx], wl_st, sem.at[1]),
                pltpu.make_async_copy(wo_hbm.at[idx], wo_st, sem.at[2]))

    @pl.when(i == 0)
    def _():
        for cp in weight_copies(e):
            cp.start()

    @pl.when(jnp.logical_and(new_expert, n_valid > 0))
    def _():
        for cp in weight_copies(e):
            cp.wait()
        for src, dst in ((wg_st, wgb_ref), (wl_st, wlb_ref), (wo_st, wob_ref)):
            n_rows = CAST_ELEMS // src.shape[1] // BF16_TILE_ROWS * BF16_TILE_ROWS
            assert n_rows > 0 and src.shape[0] % n_rows == 0

            def cast_rows(r, carry, src=src, dst=dst, n_rows=n_rows):
                rows = pl.ds(pl.multiple_of(r * n_rows, n_rows), n_rows)
                dst[rows, :] = src[rows, :].astype(BF16)
                return carry

            lax.fori_loop(0, src.shape[0] // n_rows, cast_rows, 0, unroll=4)
        nxt = next_e_ref[i]

        @pl.when(nxt >= 0)
        def _():
            for cp in weight_copies(nxt):
                cp.start()

    half = wgb_ref.shape[0] // 2
    for first in range(0, xs_ref.shape[0], EXPERT_SUB_ROWS):
        rs = slice(first, first + EXPERT_SUB_ROWS)

        @pl.when(n_valid > first)
        def _():
            live = first + lax.broadcasted_iota(I32, (EXPERT_SUB_ROWS, 1), 0) < n_valid
            lo, hi = _unpack_halves(jnp.where(live, xs_ref[rs, :], 0))
            lo = lo.astype(BF16)
            hi = hi.astype(BF16)
            a = _dot(lo, wgb_ref[:half, :]) + _dot(hi, wgb_ref[half:, :]) + bg_ref[0]
            l = _dot(lo, wlb_ref[:half, :]) + _dot(hi, wlb_ref[half:, :]) + bl_ref[0]
            a = jnp.minimum(a, SWIGLU_LIMIT)
            l = jnp.clip(l, -SWIGLU_LIMIT, SWIGLU_LIMIT)
            act = a * jax.nn.sigmoid(SWIGLU_ALPHA * a) * (l + 1.0)
            y_ref[rs, :] = _pack_halves(_dot(act.astype(BF16), wob_ref[...]) + bo_ref[0])

        @pl.when(n_valid <= first)
        def _():
            y_ref[rs, :] = jnp.zeros((EXPERT_SUB_ROWS, y_ref.shape[1]), y_ref.dtype)


def _experts(xs, blk_e, n_valid, next_e, layer, w_glu, b_glu, w_lin, b_lin, w_out, b_out, *,
             blk):
    P, pw = xs.shape
    L, E, D, F = w_glu.shape
    n_blocks = P // blk
    bmap = lambda i, be, nv, ne: (layer * E + be[i], 0, 0)
    rows = lambda i, be, nv, ne: (i, 0)
    w_glu, w_lin = w_glu.reshape(L * E, D, F), w_lin.reshape(L * E, D, F)
    w_out = w_out.reshape(L * E, F, D)
    b_glu, b_lin = b_glu.reshape(L * E, 1, F), b_lin.reshape(L * E, 1, F)
    b_out = b_out.reshape(L * E, 1, D)
    hbm = pl.BlockSpec(memory_space=pl.ANY)
    grid_spec = pltpu.PrefetchScalarGridSpec(
        num_scalar_prefetch=3,
        grid=(n_blocks,),
        in_specs=[pl.BlockSpec((blk, pw), rows),
                  hbm, pl.BlockSpec((1, 1, F), bmap),
                  hbm, pl.BlockSpec((1, 1, F), bmap),
                  hbm, pl.BlockSpec((1, 1, D), bmap)],
        out_specs=pl.BlockSpec((blk, pw), rows),
        scratch_shapes=[pltpu.VMEM((D, F), F32), pltpu.VMEM((D, F), F32),
                        pltpu.VMEM((F, D), F32),
                        pltpu.VMEM((D, F), BF16), pltpu.VMEM((D, F), BF16),
                        pltpu.VMEM((F, D), BF16),
                        pltpu.SemaphoreType.DMA((3,))],
    )
    return pl.pallas_call(
        functools.partial(_expert_kernel, expert_base=layer * E),
        grid_spec=grid_spec,
        out_shape=jax.ShapeDtypeStruct((P, pw), xs.dtype),
        compiler_params=_cparams(("arbitrary",)),
        name="moe_experts",
    )(blk_e, n_valid, next_e, xs, w_glu, b_glu, w_lin, b_lin, w_out, b_out)


def _combine_kernel(ys_ref, tg_ref, x_ref, gate_ref, *rest, tail):
    half = x_ref.shape[1] // 2
    tg = tg_ref[...]
    lo_sum = hi_sum = None
    for k in range(ys_ref.shape[0]):
        lo, hi = _unpack_halves(ys_ref[k])
        gk = tg[:, k:k + 1]
        lo_sum = gk * lo if lo_sum is None else lo_sum + gk * lo
        hi_sum = gk * hi if hi_sum is None else hi_sum + gk * hi
    x_lo = x_ref[:, :half] + gate_ref[0, :, :half] * lo_sum
    x_hi = x_ref[:, half:] + gate_ref[0, :, half:] * hi_sum
    ms = (jnp.sum(x_lo * x_lo, axis=-1, keepdims=True)
          + jnp.sum(x_hi * x_hi, axis=-1, keepdims=True)) / x_ref.shape[1]
    inv = lax.rsqrt(ms + RMS_EPS)
    if tail == "final":
        g_ref, o_ref = rest
        o_ref[:, :half] = x_lo * inv * g_ref[:, :half]
        o_ref[:, half:] = x_hi * inv * g_ref[:, half:]
    else:
        g_ref, sc_ref, sh_ref, o_ref, h_ref = rest
        o_ref[:, :half] = x_lo
        o_ref[:, half:] = x_hi
        for cols, xs in ((slice(0, half), x_lo), (slice(half, 2 * half), x_hi)):
            h = xs * inv * g_ref[:, cols] * (1.0 + sc_ref[0, :, cols]) + sh_ref[0, :, cols]
            h_ref[:, cols] = h.astype(BF16)


def _combine(ys, tok_gate, x, gate, seq, *, final_g=None, next_norm=None, tm=512):
    K, T, pw = ys.shape
    D = x.shape[1]
    bmap = lambda i: ((i * tm) // seq, 0, 0)
    rows = pl.BlockSpec((tm, D), lambda i: (i, 0))
    vec = pl.BlockSpec((1, D), lambda i: (0, 0))
    per_batch = pl.BlockSpec((1, 1, D), bmap)
    in_specs = [pl.BlockSpec((K, tm, pw), lambda i: (0, i, 0)),
                pl.BlockSpec((tm, K), lambda i: (i, 0)), rows, per_batch]
    args = [ys, tok_gate, x, gate]
    if final_g is not None:
        tail = "final"
        in_specs.append(vec)
        args.append(final_g)
        out_specs, out_shape = rows, jax.ShapeDtypeStruct((T, D), F32)
    else:
        tail = "next"
        in_specs += [vec, per_batch, per_batch]
        args += list(next_norm)
        out_specs = [rows, rows]
        out_shape = [jax.ShapeDtypeStruct((T, D), F32), jax.ShapeDtypeStruct((T, D), BF16)]
    return pl.pallas_call(
        functools.partial(_combine_kernel, tail=tail),
        grid=(T // tm,),
        in_specs=in_specs,
        out_specs=out_specs,
        out_shape=out_shape,
        compiler_params=_cparams(("parallel",)),
        name="moe_combine",
    )(*args)


MOE_BLOCK = 512


def _slots_kernel(start_ref, idx_ref, rank_ref, o_ref):
    idx = idx_ref[...]
    acc = rank_ref[...]
    for e in range(start_ref.shape[0]):
        acc = acc + jnp.where(idx == e, start_ref[e], 0)
    o_ref[...] = acc


def _slots(idx_t, rank_t, group_start):
    full = pl.BlockSpec(idx_t.shape, lambda i, gs: (0, 0))
    return pl.pallas_call(
        _slots_kernel,
        grid_spec=pltpu.PrefetchScalarGridSpec(num_scalar_prefetch=1, grid=(1,),
                                               in_specs=[full, full], out_specs=full),
        out_shape=jax.ShapeDtypeStruct(idx_t.shape, I32),
        compiler_params=_cparams(("arbitrary",)),
        name="moe_slots",
    )(group_start, idx_t, rank_t)


def _dispatch_plan(idx_t, rank_t, counts, blk):
    K, T = idx_t.shape
    E = counts.shape[0]
    padded = (counts + blk - 1) // blk * blk
    pad_end = jnp.cumsum(padded)
    pad_start = pad_end - padded
    experts = jnp.arange(E, dtype=I32)
    slots = _slots(idx_t, rank_t, pad_start.astype(I32))
    n_blocks = -(-(K * T + E * (blk - 1)) // blk)
    blk_start = jnp.arange(n_blocks, dtype=I32) * blk
    blk_e = jnp.minimum(jnp.sum(pad_end[None, :] <= blk_start[:, None], axis=1), E - 1).astype(I32)
    n_valid = jnp.clip(pad_start[blk_e] + counts[blk_e] - blk_start, 0, blk).astype(I32)
    later = jnp.where(counts > 0, experts, E)
    after = lax.cummin(jnp.concatenate([later[1:], jnp.full((1,), E, I32)]), reverse=True)
    next_e = jnp.where(after < E, after, -1).astype(I32)[blk_e]
    return slots, blk_e, n_valid, next_e, n_blocks * blk


def _mixer_out_moe(y, w_mix, x, mix_gate, g, scale, shift, gate, r_w, r_b, layer,
                   w_glu, b_glu, w_lin, b_lin, w_out, b_out, seq, **tail):
    T, D = x.shape
    x, hp, idx_t, gate_t, rank_t, counts = _mixer_out_route(
        y, w_mix, x, mix_gate, g, scale, shift, r_w, r_b, seq)
    slots, blk_e, n_valid, next_e, n_rows = _dispatch_plan(idx_t, rank_t, counts[:, 0],
                                                           MOE_BLOCK)
    xs = _sc_scatter_rows(hp, slots, n_rows)
    yp = _experts(xs, blk_e, n_valid, next_e, layer, w_glu, b_glu, w_lin, b_lin, w_out, b_out,
                  blk=MOE_BLOCK)
    ys = _sc_gather(yp, slots.reshape(TOP_K * T)).reshape(TOP_K, T, yp.shape[1])
    return _combine(ys, jnp.transpose(gate_t), x, gate, seq, **tail)


def kernel(x, c, norm_g, ada_w, final_norm_g, sg_w_in, sg_ln_g, sg_ln_b, sg_w_s, sg_b_s, sg_w_out, pool_w_in, pool_w_grp, pool_ls, pool_w_out, hgrn_w_in, hgrn_lb_logits, hgrn_gnorm_g, hgrn_w_out, router_w, router_b, expert_w_glu, expert_b_glu, expert_w_lin, expert_b_lin, expert_w_out, expert_b_out):
    B, S, D = x.shape
    depth = norm_g.shape[0]
    T = B * S
    xf = x.reshape(T, D)

    ada_w = ada_w.reshape(depth * 2, D, 3 * D)

    def mod_parts(layer, sub):
        m = _ada_mod(c, ada_w, layer * 2 + sub)[:, None, :]
        return m[..., :D], m[..., D:2 * D], m[..., 2 * D:]

    def mixer_norm(layer, shift, scale):
        return norm_g[layer, 0][None, :], scale, shift

    lb_cum = jnp.cumsum(jax.nn.softmax(hgrn_lb_logits.astype(F32), axis=0), axis=0)
    lower_bounds = lb_cum - lb_cum[0]

    shift, scale, mix_gate = mod_parts(0, 0)
    src, norm = xf, mixer_norm(0, shift, scale)
    for layer in range(depth):
        kind, slot = layer % 3, layer // 3
        if kind == 0:
            y = _spatial_mixer(src, norm, sg_w_in[slot].astype(BF16), sg_ln_g[slot][None, :],
                               sg_ln_b[slot][None, :], sg_w_s[slot], sg_b_s[slot], S)
            w_out = sg_w_out[slot]
        elif kind == 1:
            z = _norm_matmul(src, norm, pool_w_in[slot].astype(BF16), S, act=None,
                             out_dtype=F32)
            y = _pool_mix(z, pool_w_grp[slot].astype(BF16), pool_ls[slot][None, :], S)
            w_out = pool_w_out[slot]
        else:
            proj, f_pre = _norm_matmul(src, norm, hgrn_w_in[slot].astype(BF16), S, act=None,
                                       out_dtype=BF16, f32_col=1, tn=D)
            y = _hgrn_mix(proj, f_pre, lower_bounds[layer][None, :],
                          hgrn_gnorm_g[slot][None, :], B, S)
            w_out = hgrn_w_out[slot]

        moe_shift, moe_scale, moe_gate = mod_parts(layer, 1)
        moe_args = (y, w_out.astype(BF16), xf, mix_gate, norm_g[layer, 1][None, :],
                    moe_scale, moe_shift, moe_gate, router_w[layer], router_b[layer], layer,
                    expert_w_glu, expert_b_glu, expert_w_lin, expert_b_lin,
                    expert_w_out, expert_b_out, S)
        if layer == depth - 1:
            xf = _mixer_out_moe(*moe_args, final_g=final_norm_g[None, :])
        else:
            shift, scale, mix_gate = mod_parts(layer + 1, 0)
            xf, src = _mixer_out_moe(*moe_args, next_norm=mixer_norm(layer + 1, shift, scale))
            norm = None

    return xf.reshape(B, S, D)
```

```python
import functools

import numpy as np
import jax
import jax.numpy as jnp
from jax import lax
from jax.experimental import pallas as pl
from jax.experimental.pallas import tpu as pltpu
from jax.experimental.pallas import tpu_sc as plsc

F32 = jnp.float32
BF16 = jnp.bfloat16
I32 = jnp.int32
U32 = jnp.uint32

RMS_EPS = 1e-5
LN_EPS = 1e-5
SWIGLU_ALPHA = 1.702
SWIGLU_LIMIT = 7.0
TOP_K = 4
SG_CHUNK = 128
POOL_WINDOWS = (2, 4, 8, 16)
HEAD_DIM = 128
LANES = 128
SC_WORKERS = 32
SC_ROWS_PER_STEP = 64

VMEM_LIMIT = 48 * 1024 * 1024
BIG_VMEM_LIMIT = 56 * 1024 * 1024


def _cparams(sem):
    return pltpu.CompilerParams(dimension_semantics=sem, vmem_limit_bytes=VMEM_LIMIT)


def _dot(a, b):
    return jnp.dot(a, b, preferred_element_type=F32)


def _dot_nt(a, b):
    return lax.dot_general(a, b, (((1,), (1,)), ((), ())), preferred_element_type=F32)


def _dot_tn(a, b):
    return lax.dot_general(a, b, (((0,), (0,)), ((), ())), preferred_element_type=F32)


def _split_bf16(x):
    hi = x.astype(BF16)
    lo = (x - hi.astype(F32)).astype(BF16)
    return hi, lo


def _packed_layout(d):
    return d // 2, I32


def _pack_halves(y):
    n = y.shape[1] // 2
    lo = lax.bitcast_convert_type(y[:, :n].astype(BF16).astype(F32), U32)
    hi = lax.bitcast_convert_type(y[:, n:].astype(BF16).astype(F32), U32)
    word = lax.shift_right_logical(lo, jnp.uint32(16)) | (hi & jnp.uint32(0xFFFF0000))
    return lax.bitcast_convert_type(word, I32)


def _unpack_halves(w):
    u = lax.bitcast_convert_type(w, U32)
    lo = lax.bitcast_convert_type(lax.shift_left(u, jnp.uint32(16)), F32)
    hi = lax.bitcast_convert_type(u & jnp.uint32(0xFFFF0000), F32)
    return lo, hi


def _modulated_norm(x, g, scale, shift):
    ms = jnp.mean(x * x, axis=-1, keepdims=True)
    y = x * lax.rsqrt(ms + RMS_EPS) * g
    return y * (1.0 + scale) + shift


def _ada_kernel(c_ref, w_ref, o_ref):
    w = w_ref[0]
    reps = w.shape[1] // LANES
    for b in range(c_ref.shape[0]):
        cb = c_ref[b]
        s = jnp.tile(cb * jax.nn.sigmoid(cb), (1, reps))
        o_ref[b] = jnp.sum(w * s, axis=0, keepdims=True)


def _ada_mod(c, ada_w, index):
    L, D, N = ada_w.shape
    B = c.shape[0]
    tn = 512
    c_cols = jnp.broadcast_to(c[:, :, None], (B, D, LANES))
    out = pl.pallas_call(
        _ada_kernel,
        grid=(N // tn,),
        in_specs=[pl.BlockSpec((B, D, LANES), lambda j: (0, 0, 0)),
                  pl.BlockSpec((1, D, tn), lambda j: (index, 0, j))],
        out_specs=pl.BlockSpec((B, 1, tn), lambda j: (0, 0, j)),
        out_shape=jax.ShapeDtypeStruct((B, 1, N), F32),
        compiler_params=_cparams(("parallel",)),
        name="ada_mod",
    )(c_cols, ada_w)
    return out[:, 0]


def _nm_kernel(*refs, act, f32_col, prenormed):
    if prenormed:
        h_ref, w_ref, o_ref = refs[:3]
        outs = refs[3:]
    else:
        x_ref, g_ref, sc_ref, sh_ref, w_ref, o_ref = refs[:6]
        outs, h_ref = refs[6:-1], refs[-1]

        @pl.when(pl.program_id(1) == 0)
        def _():
            h = _modulated_norm(x_ref[...], g_ref[...], sc_ref[0], sh_ref[0])
            h_ref[...] = h.astype(BF16)

    acc = _dot(h_ref[...], w_ref[...])
    if act == "gelu":
        acc = 0.5 * acc * (1.0 + lax.erf(acc * np.float32(1.0 / np.sqrt(2.0))))
    o_ref[...] = acc.astype(o_ref.dtype)
    if f32_col is not None:
        @pl.when(pl.program_id(1) == f32_col)
        def _():
            outs[0][...] = acc


def _norm_matmul(x, norm, w, seq, *, act, out_dtype, f32_col=None, tm=512, tn=2048):
    T, D = x.shape
    N = w.shape[1]
    prenormed = norm is None
    bmap = lambda i, j: ((i * tm) // seq, 0, 0)
    out_specs = [pl.BlockSpec((tm, tn), lambda i, j: (i, j))]
    out_shape = [jax.ShapeDtypeStruct((T, N), out_dtype)]
    if f32_col is not None:
        out_specs.append(pl.BlockSpec((tm, tn), lambda i, j: (i, 0)))
        out_shape.append(jax.ShapeDtypeStruct((T, tn), F32))
    in_specs = [pl.BlockSpec((tm, D), lambda i, j: (i, 0))]
    args = [x]
    if not prenormed:
        in_specs += [pl.BlockSpec((1, D), lambda i, j: (0, 0)),
                     pl.BlockSpec((1, 1, D), bmap),
                     pl.BlockSpec((1, 1, D), bmap)]
        args += list(norm)
    in_specs.append(pl.BlockSpec((D, tn), lambda i, j: (0, j)))
    args.append(w)
    outs = pl.pallas_call(
        functools.partial(_nm_kernel, act=act, f32_col=f32_col, prenormed=prenormed),
        grid=(T // tm, N // tn),
        in_specs=in_specs,
        out_specs=out_specs,
        out_shape=out_shape,
        scratch_shapes=[] if prenormed else [pltpu.VMEM((tm, D), BF16)],
        compiler_params=pltpu.CompilerParams(dimension_semantics=("parallel", "arbitrary"),
                                             vmem_limit_bytes=BIG_VMEM_LIMIT),
        name="norm_matmul_" + str(act),
    )(*args)
    return outs if f32_col is not None else outs[0]


def _sg_kernel(u_ref, v_ref, lng_ref, lnb_ref, ws_ref, bst_ref, y_ref, vn_ref, *, heads):
    v = v_ref[...].astype(F32)
    mu = jnp.mean(v, axis=-1, keepdims=True)
    vc = v - mu
    var = jnp.mean(vc * vc, axis=-1, keepdims=True)
    vn_ref[...] = (vc * lax.rsqrt(var + LN_EPS) * lng_ref[...] + lnb_ref[...]).astype(BF16)

    n_chunks = v_ref.shape[0] // SG_CHUNK
    row = lax.broadcasted_iota(I32, (SG_CHUNK, SG_CHUNK), 0)
    col = lax.broadcasted_iota(I32, (SG_CHUNK, SG_CHUNK), 1)
    causal = row >= col
    for hd in range(heads):
        cs = slice(hd * HEAD_DIM, (hd + 1) * HEAD_DIM)
        wm = jnp.where(causal, ws_ref[hd], 0.0).astype(BF16)
        bias = bst_ref[:, hd:hd + 1]
        for ci in range(n_chunks):
            rs = slice(ci * SG_CHUNK, (ci + 1) * SG_CHUNK)
            mixed = _dot(wm, vn_ref[rs, cs]) + bias
            y_ref[rs, cs] = (u_ref[rs, cs].astype(F32) * mixed).astype(BF16)


def _spatial_gate(z, ln_g, ln_b, w_s, b_s, *, tm=512):
    T, two_w = z.shape
    W = two_w // 2
    heads = w_s.shape[0]
    return pl.pallas_call(
        functools.partial(_sg_kernel, heads=heads),
        grid=(T // tm,),
        in_specs=[pl.BlockSpec((tm, W), lambda i: (i, 0)),
                  pl.BlockSpec((tm, W), lambda i: (i, 1)),
                  pl.BlockSpec((1, W), lambda i: (0, 0)),
                  pl.BlockSpec((1, W), lambda i: (0, 0)),
                  pl.BlockSpec((heads, SG_CHUNK, SG_CHUNK), lambda i: (0, 0, 0)),
                  pl.BlockSpec((SG_CHUNK, heads), lambda i: (0, 0))],
        out_specs=pl.BlockSpec((tm, W), lambda i: (i, 0)),
        out_shape=jax.ShapeDtypeStruct((T, W), BF16),
        scratch_shapes=[pltpu.VMEM((tm, W), BF16)],
        compiler_params=_cparams(("parallel",)),
        name="spatial_gate",
    )(z, z, ln_g, ln_b, w_s, jnp.transpose(b_s))


def _sg_fused_kernel(*refs, heads, prenormed):
    if prenormed:
        h_ref, w_ref = refs[:2]
        rest = refs[2:]
        h = h_ref[...]
    else:
        x_ref, g_ref, sc_ref, sh_ref, w_ref = refs[:5]
        rest = refs[5:]
        h = _modulated_norm(x_ref[...], g_ref[...], sc_ref[0], sh_ref[0]).astype(BF16)
    lng_ref, lnb_ref, ws_ref, bst_ref, y_ref, u_ref, v_ref, vn_ref = rest
    width = u_ref.shape[1]
    inv_sqrt2 = np.float32(1.0 / np.sqrt(2.0))
    for dst, cols in ((v_ref, slice(width, 2 * width)), (u_ref, slice(0, width))):
        acc = _dot(h, w_ref[:, cols])
        dst[...] = (0.5 * acc * (1.0 + lax.erf(acc * inv_sqrt2))).astype(BF16)
    _sg_kernel(u_ref, v_ref, lng_ref, lnb_ref, ws_ref, bst_ref, y_ref, vn_ref, heads=heads)


def _spatial_mixer(x, norm, w_in, ln_g, ln_b, w_s, b_s, seq, *, tm=256):
    T, D = x.shape
    W = w_in.shape[1] // 2
    heads = w_s.shape[0]
    prenormed = norm is None
    bmap = lambda i: ((i * tm) // seq, 0, 0)
    const = lambda i: (0, 0)
    in_specs = [pl.BlockSpec((tm, D), lambda i: (i, 0))]
    args = [x]
    if not prenormed:
        in_specs += [pl.BlockSpec((1, D), const), pl.BlockSpec((1, 1, D), bmap),
                     pl.BlockSpec((1, 1, D), bmap)]
        args += list(norm)
    in_specs += [pl.BlockSpec((D, 2 * W), const, pipeline_mode=pl.Buffered(1)),
                 pl.BlockSpec((1, W), const),
                 pl.BlockSpec((1, W), const),
                 pl.BlockSpec((heads, SG_CHUNK, SG_CHUNK), lambda i: (0, 0, 0)),
                 pl.BlockSpec((SG_CHUNK, heads), const)]
    args += [w_in, ln_g, ln_b, w_s, jnp.transpose(b_s)]
    return pl.pallas_call(
        functools.partial(_sg_fused_kernel, heads=heads, prenormed=prenormed),
        grid=(T // tm,),
        in_specs=in_specs,
        out_specs=pl.BlockSpec((tm, W), lambda i: (i, 0)),
        out_shape=jax.ShapeDtypeStruct((T, W), BF16),
        scratch_shapes=[pltpu.VMEM((tm, W), BF16), pltpu.VMEM((tm, W), BF16),
                        pltpu.VMEM((tm, W), BF16)],
        compiler_params=pltpu.CompilerParams(dimension_semantics=("parallel",),
                                             vmem_limit_bytes=BIG_VMEM_LIMIT),
        name="spatial_mixer",
    )(*args)


POOL_HALO = 16


def _pool_kernel(z_ref, halo_ref, wg_ref, ls_ref, y_ref, *, seq):
    tm = z_ref.shape[0]
    gdim = wg_ref.shape[1]
    pos0 = (pl.program_id(0) * tm) % seq
    halo_on = (pos0 > 0).astype(F32)
    pos = pos0 + lax.broadcasted_iota(I32, (tm, 1), 0)
    for gi, wnd in enumerate(POOL_WINDOWS):
        cs = slice(gi * gdim, (gi + 1) * gdim)
        zg = z_ref[:, cs]
        s = jnp.concatenate([halo_ref[:, cs] * halo_on, zg], axis=0)
        k = 1
        while k < wnd:
            s = s + pltpu.roll(s, k, 0)
            k *= 2
        cnt = jnp.minimum(pos + 1, wnd).astype(F32)
        pooled = s[POOL_HALO:, :] / cnt - zg
        yg = _dot(pooled.astype(BF16), wg_ref[gi]) * ls_ref[:, cs]
        y_ref[:, cs] = yg.astype(BF16)


def _pool_mix(z, w_grp, ls, seq, *, tm=512):
    T, D = z.shape
    G, gdim, _ = w_grp.shape
    assert max(POOL_WINDOWS) <= POOL_HALO and tm % POOL_HALO == 0
    ratio = tm // POOL_HALO
    return pl.pallas_call(
        functools.partial(_pool_kernel, seq=seq),
        grid=(T // tm,),
        in_specs=[pl.BlockSpec((tm, D), lambda i: (i, 0)),
                  pl.BlockSpec((POOL_HALO, D), lambda i: (jnp.maximum(i * ratio - 1, 0), 0)),
                  pl.BlockSpec((G, gdim, gdim), lambda i: (0, 0, 0)),
                  pl.BlockSpec((1, D), lambda i: (0, 0))],
        out_specs=pl.BlockSpec((tm, D), lambda i: (i, 0)),
        out_shape=jax.ShapeDtypeStruct((T, D), BF16),
        compiler_params=_cparams(("parallel",)),
        name="pool_mix",
    )(z, z, w_grp, ls)


def _pool_fused_kernel(*refs, seq, prenormed):
    if prenormed:
        h_ref, w_ref = refs[:2]
        rest = refs[2:]
        h = h_ref[...]
    else:
        x_ref, g_ref, sc_ref, sh_ref, w_ref = refs[:5]
        rest = refs[5:]
        h = _modulated_norm(x_ref[...], g_ref[...], sc_ref[0], sh_ref[0]).astype(BF16)
    wg_ref, ls_ref, y_ref, z_ref, halo_ref = rest
    tm = z_ref.shape[0]

    @pl.when((pl.program_id(0) * tm) % seq == 0)
    def _():
        halo_ref[...] = jnp.zeros_like(halo_ref)

    z_ref[...] = _dot(h, w_ref[...])
    _pool_kernel(z_ref, halo_ref, wg_ref, ls_ref, y_ref, seq=seq)
    halo_ref[...] = z_ref[tm - POOL_HALO:, :]


def _pool_mixer(x, norm, w_in, w_grp, ls, seq, *, tm=512):
    T, D = x.shape
    G, gdim, _ = w_grp.shape
    assert max(POOL_WINDOWS) <= POOL_HALO and tm % POOL_HALO == 0 and seq % tm == 0
    prenormed = norm is None
    bmap = lambda i: ((i * tm) // seq, 0, 0)
    const = lambda i: (0, 0)
    in_specs = [pl.BlockSpec((tm, D), lambda i: (i, 0))]
    args = [x]
    if not prenormed:
        in_specs += [pl.BlockSpec((1, D), const), pl.BlockSpec((1, 1, D), bmap),
                     pl.BlockSpec((1, 1, D), bmap)]
        args += list(norm)
    in_specs += [pl.BlockSpec((D, D), const, pipeline_mode=pl.Buffered(1)),
                 pl.BlockSpec((G, gdim, gdim), lambda i: (0, 0, 0)),
                 pl.BlockSpec((1, D), const)]
    args += [w_in, w_grp, ls]
    return pl.pallas_call(
        functools.partial(_pool_fused_kernel, seq=seq, prenormed=prenormed),
        grid=(T // tm,),
        in_specs=in_specs,
        out_specs=pl.BlockSpec((tm, D), lambda i: (i, 0)),
        out_shape=jax.ShapeDtypeStruct((T, D), BF16),
        scratch_shapes=[pltpu.VMEM((tm, D), F32), pltpu.VMEM((POOL_HALO, D), F32)],
        compiler_params=_cparams(("arbitrary",)),
        name="pool_mixer",
    )(*args)


HGRN_CHUNK = 128
HGRN_BASE = 32
HGRN_GUARD_LOG2 = 100.0
LOG2_E = 1.4426950408889634


def _hgrn_tables(c=HGRN_CHUNK, base=HGRN_BASE):
    t = np.arange(c)
    j = t[None, :]
    sums = [j <= t[:, None]]
    masks = []
    n_shallow = 0
    h = c // 2
    while h >= 1:
        off = t % (2 * h)
        mid = (t // (2 * h)) * (2 * h) + h
        second = off >= h
        if h >= 2:
            m_q = second[:, None] & (j >= mid[:, None]) & (j <= t[:, None])
            m_k = (~second)[:, None] & (j > t[:, None]) & (j < mid[:, None])
            sums.append(m_q | m_k)
        same = (t[:, None] // (2 * h)) == (t[None, :] // (2 * h))
        masks.append(same & second[:, None] & (~second)[None, :])
        n_shallow += h >= base
        h //= 2
    same_base = (t[:, None] // base) == (t[None, :] // base)
    sums.append(same_base & (j <= t[:, None]))
    sums = np.stack(sums).astype(np.float32)
    base_mask = (same_base & (j <= t[:, None])).astype(np.float32)
    return (np.concatenate([sums, sums], axis=2), np.stack(masks).astype(np.float32), base_mask,
            n_shallow)


def _hgrn_kernel(q_ref, f_ref, i_ref, g_ref, lb_ref, gn_ref, sums_ref, masks_ref, bmask_ref,
                 o_ref, st_ref, qb_ref, ke_ref, z_ref, qd_ref, kd_ref, vb_ref, eb_ref, sc_ref,
                 *, heads, n_shallow):
    c = q_ref.shape[0]
    n_sum = sums_ref.shape[0] - 1
    n_lvl = masks_ref.shape[0]

    @pl.when(pl.program_id(1) == 0)
    def _():
        st_ref[...] = jnp.zeros_like(st_ref)

    lb = lb_ref[...]
    sig = jax.nn.sigmoid(f_ref[...])
    fg = lb + (1.0 - lb) * sig
    lf2 = jnp.log(fg) * np.float32(LOG2_E)
    kk = (1.0 - lb) * (1.0 - sig)
    lf_hi, lf_lo = _split_bf16(lf2)
    lf_cat = jnp.concatenate([lf_hi, lf_lo], axis=0)
    qr = q_ref[...].astype(F32)
    q = qr * jax.nn.sigmoid(qr)
    vb_ref[...] = i_ref[...].astype(BF16)
    q16 = q.astype(BF16)
    k16 = kk.astype(BF16)

    b2 = _dot(sums_ref[0], lf_cat)
    eb = jnp.exp2(b2)
    qb_ref[...] = (q * eb).astype(BF16)
    eb_ref[...] = eb[c - 1:c, :]
    ke_ref[...] = (kk * jnp.exp2(b2[c - 1:c, :] - b2)).astype(BF16)
    w2 = _dot(sums_ref[n_sum], lf_cat)
    shallow = jnp.min(w2) >= -HGRN_GUARD_LOG2

    row = lax.broadcasted_iota(I32, (c, 1), 0)

    def level(l):
        half = c >> (l + 1)
        second = (row & (2 * half - 1)) >= half
        if l + 1 < n_sum:
            x = jnp.exp2(_dot(sums_ref[l + 1], lf_cat).astype(BF16))
        else:
            x = jnp.where(second, fg, 1.0).astype(BF16)
        z_ref[l] = jnp.where(second, q16, k16) * x

    gn = gn_ref[...]

    def finish(hd, cs, o):
        st = st_ref[hd]
        vh = vb_ref[:, cs]
        o = o + _dot_nt(qb_ref[:, cs], st.astype(BF16))
        st_ref[hd] = st * eb_ref[:, cs] + _dot_tn(vh, ke_ref[:, cs])
        on = o * lax.rsqrt(jnp.mean(o * o, axis=-1, keepdims=True) + RMS_EPS) * gn
        gv = g_ref[:, cs].astype(F32)
        o_ref[:, cs] = (on * (gv * jax.nn.sigmoid(gv))).astype(BF16)

    def level_scores(cs, n):
        scores = None
        for l in range(n):
            half = c >> (l + 1)
            zl = z_ref[l, :, cs]
            if half >= 16:
                starts = list(range(half, c, 2 * half))
                zq = jnp.concatenate([z_ref[l, r:r + half, cs] for r in starts], axis=0)
                p = _dot_nt(zq, zl).astype(BF16)
                parts = []
                for n_i, r in enumerate(starts):
                    parts.append(jnp.zeros((half, c), BF16))
                    parts.append(masks_ref[l, r:r + half, :] * p[n_i * half:(n_i + 1) * half])
                term = jnp.concatenate(parts, axis=0)
            else:
                term = masks_ref[l] * _dot_nt(zl, zl).astype(BF16)
            scores = term if scores is None else scores + term
        return scores

    @pl.when(shallow)
    def _():
        for l in range(n_shallow):
            level(l)
        qd_ref[...] = (q * jnp.exp2(w2)).astype(BF16)
        kd_ref[...] = (kk * jnp.exp2(-w2)).astype(BF16)

        def head_scores(hd, carry):
            cs = pl.ds(pl.multiple_of(hd * HEAD_DIM, HEAD_DIM), HEAD_DIM)
            sc_ref[hd] = (level_scores(cs, n_shallow)
                          + bmask_ref[...] * _dot_nt(qd_ref[:, cs], kd_ref[:, cs]).astype(BF16))
            return carry

        def head_out(hd, carry):
            cs = pl.ds(pl.multiple_of(hd * HEAD_DIM, HEAD_DIM), HEAD_DIM)
            finish(hd, cs, _dot(sc_ref[hd], vb_ref[:, cs]))
            return carry

        lax.fori_loop(0, heads, head_scores, 0, unroll=16)
        lax.fori_loop(0, heads, head_out, 0, unroll=16)

    @pl.when(jnp.logical_not(shallow))
    def _():
        for l in range(n_lvl):
            level(l)
        qd_ref[...] = q16
        kd_ref[...] = k16

        def head(hd, carry):
            cs = pl.ds(pl.multiple_of(hd * HEAD_DIM, HEAD_DIM), HEAD_DIM)
            vh = vb_ref[:, cs]
            diag = jnp.sum(qd_ref[:, cs].astype(F32) * kd_ref[:, cs].astype(F32), axis=-1,
                           keepdims=True)
            finish(hd, cs, _dot(level_scores(cs, n_lvl), vh) + diag * vh.astype(F32))
            return carry

        lax.fori_loop(0, heads, head, 0, unroll=4)


def _hgrn_mix(proj, f_pre, lb, gnorm_g, batch, seq):
    T, four_d = proj.shape
    D = four_d // 4
    heads = D // HEAD_DIM
    c = HGRN_CHUNK
    n_chunks = seq // c
    sums, masks, base_mask, n_shallow = _hgrn_tables(c)
    n_sum, n_mask = sums.shape[0], masks.shape[0]
    row = lambda b, i: b * n_chunks + i
    return pl.pallas_call(
        functools.partial(_hgrn_kernel, heads=heads, n_shallow=n_shallow),
        grid=(batch, n_chunks),
        in_specs=[pl.BlockSpec((c, D), lambda b, i: (row(b, i), 0)),
                  pl.BlockSpec((c, D), lambda b, i: (row(b, i), 0)),
                  pl.BlockSpec((c, D), lambda b, i: (row(b, i), 2)),
                  pl.BlockSpec((c, D), lambda b, i: (row(b, i), 3)),
                  pl.BlockSpec((1, D), lambda b, i: (0, 0)),
                  pl.BlockSpec((1, HEAD_DIM), lambda b, i: (0, 0)),
                  pl.BlockSpec((n_sum, c, 2 * c), lambda b, i: (0, 0, 0)),
                  pl.BlockSpec((n_mask, c, c), lambda b, i: (0, 0, 0)),
                  pl.BlockSpec((c, c), lambda b, i: (0, 0))],
        out_specs=pl.BlockSpec((c, D), lambda b, i: (row(b, i), 0)),
        out_shape=jax.ShapeDtypeStruct((T, D), BF16),
        scratch_shapes=[pltpu.VMEM((heads, HEAD_DIM, HEAD_DIM), F32),
                        pltpu.VMEM((c, D), BF16),
                        pltpu.VMEM((c, D), BF16),
                        pltpu.VMEM((n_mask, c, D), BF16),
                        pltpu.VMEM((c, D), BF16),
                        pltpu.VMEM((c, D), BF16),
                        pltpu.VMEM((c, D), BF16),
                        pltpu.VMEM((1, D), F32),
                        pltpu.VMEM((heads, c, c), BF16)],
        compiler_params=_cparams(("parallel", "arbitrary")),
        name="hgrn_mix",
    )(proj, f_pre, proj, proj, lb, gnorm_g, jnp.asarray(sums, BF16), jnp.asarray(masks, BF16),
      jnp.asarray(base_mask, BF16))


def _route_kernel(y_ref, w_ref, xin_ref, mgate_ref, g_ref, sc_ref, sh_ref, rw_ref,
                  rb_ref, tri_ref, x_ref, hp_ref, idx_ref, gate_ref, rank_ref, cnt_ref, carry_ref,
                  *, n_experts):
    @pl.when(pl.program_id(0) == 0)
    def _():
        carry_ref[...] = jnp.zeros_like(carry_ref)

    x = xin_ref[...] + mgate_ref[0] * _dot(y_ref[...], w_ref[...])
    x_ref[...] = x
    h = _modulated_norm(x, g_ref[...], sc_ref[0], sh_ref[0])
    hp_ref[...] = _pack_halves(h)

    h_hi, h_lo = _split_bf16(h)
    both = _dot(h_hi, rw_ref[...])
    logits = (both[:, :LANES] + both[:, LANES:] + _dot(h_lo, rw_ref[:, :LANES])) + rb_ref[...]
    lt = jnp.transpose(logits)[:n_experts, :]
    eidx = lax.broadcasted_iota(I32, lt.shape, 0)
    vals, idxs, hits = [], [], []
    for _ in range(TOP_K):
        m = jnp.max(lt, axis=0, keepdims=True)
        sel = jnp.min(jnp.where(lt == m, eidx, n_experts), axis=0, keepdims=True)
        hit = eidx == sel
        vals.append(m)
        idxs.append(sel)
        hits.append(hit)
        lt = jnp.where(hit, -jnp.inf, lt)
    exps = [jnp.exp(v - vals[0]) for v in vals]
    denom = exps[0]
    for e in exps[1:]:
        denom = denom + e
    idx_ref[...] = jnp.concatenate(idxs, axis=0)
    gate_ref[...] = jnp.concatenate([e / denom for e in exps], axis=0)

    onehot = hits[0].astype(F32)
    for hit in hits[1:]:
        onehot = onehot + hit.astype(F32)
    before = carry_ref[...] + _dot(onehot.astype(BF16), tri_ref[...])
    rank_ref[...] = jnp.concatenate(
        [jnp.sum(jnp.where(hit, before, 0.0), axis=0, keepdims=True) for hit in hits],
        axis=0).astype(I32)
    carry_ref[...] += jnp.sum(onehot, axis=1, keepdims=True)
    cnt_ref[...] = carry_ref[...].astype(I32)


def _mixer_out_route(y, w, x, mix_gate, g, scale, shift, r_w, r_b, seq, *, tm=512):
    T, K = y.shape
    D = w.shape[1]
    E = r_w.shape[1]
    rw_pad = jnp.zeros((D, LANES), F32).at[:, :E].set(r_w)
    rw_cat = jnp.concatenate(_split_bf16(rw_pad), axis=1)
    rb_pad = jnp.full((1, LANES), -1e30, F32).at[0, :E].set(r_b)
    tri = jnp.asarray(np.triu(np.ones((tm, tm), np.float32), k=1), BF16)
    pw, pdt = _packed_layout(D)
    bmap = lambda i: ((i * tm) // seq, 0, 0)
    const = lambda i: (0, 0)
    rows = lambda i: (i, 0)
    kt = pl.BlockSpec((TOP_K, tm), lambda i: (0, i))
    return pl.pallas_call(
        functools.partial(_route_kernel, n_experts=E),
        grid=(T // tm,),
        in_specs=[pl.BlockSpec((tm, K), rows),
                  pl.BlockSpec((K, D), const, pipeline_mode=pl.Buffered(1)),
                  pl.BlockSpec((tm, D), rows),
                  pl.BlockSpec((1, 1, D), bmap),
                  pl.BlockSpec((1, D), const),
                  pl.BlockSpec((1, 1, D), bmap),
                  pl.BlockSpec((1, 1, D), bmap),
                  pl.BlockSpec((D, 2 * LANES), const, pipeline_mode=pl.Buffered(1)),
                  pl.BlockSpec((1, LANES), const),
                  pl.BlockSpec((tm, tm), const, pipeline_mode=pl.Buffered(1))],
        out_specs=[pl.BlockSpec((tm, D), rows), pl.BlockSpec((tm, pw), rows), kt, kt, kt,
                   pl.BlockSpec((E, 1), const)],
        out_shape=[jax.ShapeDtypeStruct((T, D), F32),
                   jax.ShapeDtypeStruct((T, pw), pdt),
                   jax.ShapeDtypeStruct((TOP_K, T), I32),
                   jax.ShapeDtypeStruct((TOP_K, T), F32),
                   jax.ShapeDtypeStruct((TOP_K, T), I32),
                   jax.ShapeDtypeStruct((E, 1), I32)],
        scratch_shapes=[pltpu.VMEM((E, 1), F32)],
        compiler_params=_cparams(("arbitrary",)),
        name="mixer_out_route",
    )(y, w, x, mix_gate, g, scale, shift, rw_cat, rb_pad, tri)


def _sc_gather(table, idx):
    R = idx.shape[0]
    W = table.shape[1]
    rows = SC_ROWS_PER_STEP // 2
    per_worker = R // SC_WORKERS
    steps = per_worker // rows
    assert steps * rows * SC_WORKERS == R and steps % 2 == 0
    mesh = plsc.VectorSubcoreMesh(core_axis_name="c", subcore_axis_name="s")
    n_cores = mesh.num_cores

    @functools.partial(
        pl.kernel, mesh=mesh,
        out_type=jax.ShapeDtypeStruct((R, W), table.dtype),
        scratch_types=[pltpu.VMEM((steps, rows), I32),
                       pltpu.VMEM((rows, W), table.dtype),
                       pltpu.VMEM((rows, W), table.dtype),
                       pltpu.SemaphoreType.DMA((2,)),
                       pltpu.SemaphoreType.DMA((2,))],
    )
    def gather(table_hbm, idx_hbm, out_hbm, idx_v, buf0, buf1, gsem, wsem):
        wid = lax.axis_index("s") * n_cores + lax.axis_index("c")
        base = wid * per_worker
        bufs = (buf0, buf1)
        pltpu.sync_copy(idx_hbm.at[pl.ds(wid * steps, steps)], idx_v)

        def fetch(j, b):
            return pltpu.make_async_copy(table_hbm.at[idx_v.at[j]], bufs[b], gsem.at[b])

        def flush(j, b):
            off = pl.multiple_of(base + j * rows, 8)
            return pltpu.make_async_copy(bufs[b], out_hbm.at[pl.ds(off, rows)], wsem.at[b])

        fetch(0, 0).start()

        @pl.loop(0, steps, step=2)
        def _(j0):
            for b in range(2):
                j = j0 + b
                fetch(j, b).wait()
                flush(j, b).start()

                @pl.when(j + 1 < steps)
                def _():
                    @pl.when(j >= 1)
                    def _():
                        flush(j - 1, 1 - b).wait()
                    fetch(j + 1, 1 - b).start()

        flush(steps - 2, 0).wait()
        flush(steps - 1, 1).wait()

    return gather(table, idx.reshape(R // rows, rows))


def _sc_scatter_rows(table, slots, n_out):
    K, T = slots.shape
    W = table.shape[1]
    rows = SC_ROWS_PER_STEP
    per_worker = T // SC_WORKERS
    steps = per_worker // rows
    assert steps * rows * SC_WORKERS == T
    mesh = plsc.VectorSubcoreMesh(core_axis_name="c", subcore_axis_name="s")
    n_cores = mesh.num_cores

    @functools.partial(
        pl.kernel, mesh=mesh,
        out_type=jax.ShapeDtypeStruct((n_out, W), table.dtype),
        scratch_types=[pltpu.VMEM((K, rows), I32),
                       pltpu.VMEM((rows, W), table.dtype)],
    )
    def scatter(table_hbm, slots_hbm, out_hbm, idx_v, rows_v):
        wid = lax.axis_index("s") * n_cores + lax.axis_index("c")
        base = wid * per_worker

        @pl.loop(0, steps)
        def _(j):
            off = pl.multiple_of(base + j * rows, 8)
            pltpu.sync_copy(table_hbm.at[pl.ds(off, rows)], rows_v)
            for k in range(K):
                pltpu.sync_copy(slots_hbm.at[pl.ds(pl.multiple_of(k * T + off, 8), rows)],
                                idx_v.at[k])
                pltpu.sync_copy(rows_v, out_hbm.at[idx_v.at[k]])

    return scatter(table, slots.reshape(K * T))


EXPERT_SUB_ROWS = 256
CAST_ELEMS = 32 * 1024
BF16_TILE_ROWS = 16


def _expert_kernel(blk_e_ref, n_valid_ref, next_e_ref, xs_ref, wg_hbm, bg_ref, wl_hbm, bl_ref,
                   wo_hbm, bo_ref, y_ref, wg_st, wl_st, wo_st, wgb_ref, wlb_ref, wob_ref, sem,
                   *, expert_base):
    i = pl.program_id(0)
    e = blk_e_ref[i]
    n_valid = n_valid_ref[i]
    new_expert = jnp.logical_or(i == 0, e != blk_e_ref[jnp.maximum(i - 1, 0)])

    def weight_copies(expert):
        idx = expert_base + expert
        return (pltpu.make_async_copy(wg_hbm.at[idx], wg_st, sem.at[0]),
                pltpu.make_async_copy(wl_hbm.at[idx], wl_st, sem.at[1]),
                pltpu.make_async_copy(wo_hbm.at[idx], wo_st, sem.at[2]))

    @pl.when(i == 0)
    def _():
        for cp in weight_copies(e):
            cp.start()

    @pl.when(jnp.logical_and(new_expert, n_valid > 0))
    def _():
        for cp in weight_copies(e):
            cp.wait()
        for src, dst in ((wg_st, wgb_ref), (wl_st, wlb_ref), (wo_st, wob_ref)):
            n_rows = CAST_ELEMS // src.shape[1] // BF16_TILE_ROWS * BF16_TILE_ROWS
            assert n_rows > 0 and src.shape[0] % n_rows == 0

            def cast_rows(r, carry, src=src, dst=dst, n_rows=n_rows):
                rows = pl.ds(pl.multiple_of(r * n_rows, n_rows), n_rows)
                dst[rows, :] = src[rows, :].astype(BF16)
                return carry

            lax.fori_loop(0, src.shape[0] // n_rows, cast_rows, 0, unroll=4)
        nxt = next_e_ref[i]

        @pl.when(nxt >= 0)
        def _():
            for cp in weight_copies(nxt):
                cp.start()

    half = wgb_ref.shape[0] // 2
    for first in range(0, xs_ref.shape[0], EXPERT_SUB_ROWS):
        rs = slice(first, first + EXPERT_SUB_ROWS)

        @pl.when(n_valid > first)
        def _():
            live = first + lax.broadcasted_iota(I32, (EXPERT_SUB_ROWS, 1), 0) < n_valid
            lo, hi = _unpack_halves(jnp.where(live, xs_ref[rs, :], 0))
            lo = lo.astype(BF16)
            hi = hi.astype(BF16)
            a = _dot(lo, wgb_ref[:half, :]) + _dot(hi, wgb_ref[half:, :]) + bg_ref[0]
            l = _dot(lo, wlb_ref[:half, :]) + _dot(hi, wlb_ref[half:, :]) + bl_ref[0]
            a = jnp.minimum(a, SWIGLU_LIMIT)
            l = jnp.clip(l, -SWIGLU_LIMIT, SWIGLU_LIMIT)
            act = a * jax.nn.sigmoid(SWIGLU_ALPHA * a) * (l + 1.0)
            y_ref[rs, :] = _pack_halves(_dot(act.astype(BF16), wob_ref[...]) + bo_ref[0])

        @pl.when(n_valid <= first)
        def _():
            y_ref[rs, :] = jnp.zeros((EXPERT_SUB_ROWS, y_ref.shape[1]), y_ref.dtype)


def _experts(xs, blk_e, n_valid, next_e, layer, w_glu, b_glu, w_lin, b_lin, w_out, b_out, *,
             blk):
    P, pw = xs.shape
    L, E, D, F = w_glu.shape
    n_blocks = P // blk
    bmap = lambda i, be, nv, ne: (layer * E + be[i], 0, 0)
    rows = lambda i, be, nv, ne: (i, 0)
    w_glu, w_lin = w_glu.reshape(L * E, D, F), w_lin.reshape(L * E, D, F)
    w_out = w_out.reshape(L * E, F, D)
    b_glu, b_lin = b_glu.reshape(L * E, 1, F), b_lin.reshape(L * E, 1, F)
    b_out = b_out.reshape(L * E, 1, D)
    hbm = pl.BlockSpec(memory_space=pl.ANY)
    grid_spec = pltpu.PrefetchScalarGridSpec(
        num_scalar_prefetch=3,
        grid=(n_blocks,),
        in_specs=[pl.BlockSpec((blk, pw), rows),
                  hbm, pl.BlockSpec((1, 1, F), bmap),
                  hbm, pl.BlockSpec((1, 1, F), bmap),
                  hbm, pl.BlockSpec((1, 1, D), bmap)],
        out_specs=pl.BlockSpec((blk, pw), rows),
        scratch_shapes=[pltpu.VMEM((D, F), F32), pltpu.VMEM((D, F), F32),
                        pltpu.VMEM((F, D), F32),
                        pltpu.VMEM((D, F), BF16), pltpu.VMEM((D, F), BF16),
                        pltpu.VMEM((F, D), BF16),
                        pltpu.SemaphoreType.DMA((3,))],
    )
    return pl.pallas_call(
        functools.partial(_expert_kernel, expert_base=layer * E),
        grid_spec=grid_spec,
        out_shape=jax.ShapeDtypeStruct((P, pw), xs.dtype),
        compiler_params=_cparams(("arbitrary",)),
        name="moe_experts",
    )(blk_e, n_valid, next_e, xs, w_glu, b_glu, w_lin, b_lin, w_out, b_out)


def _combine_kernel(ys_ref, tg_ref, x_ref, gate_ref, *rest, tail):
    half = x_ref.shape[1] // 2
    tg = tg_ref[...]
    lo_sum = hi_sum = None
    for k in range(ys_ref.shape[0]):
        lo, hi = _unpack_halves(ys_ref[k])
        gk = tg[:, k:k + 1]
        lo_sum = gk * lo if lo_sum is None else lo_sum + gk * lo
        hi_sum = gk * hi if hi_sum is None else hi_sum + gk * hi
    x_lo = x_ref[:, :half] + gate_ref[0, :, :half] * lo_sum
    x_hi = x_ref[:, half:] + gate_ref[0, :, half:] * hi_sum
    ms = (jnp.sum(x_lo * x_lo, axis=-1, keepdims=True)
          + jnp.sum(x_hi * x_hi, axis=-1, keepdims=True)) / x_ref.shape[1]
    inv = lax.rsqrt(ms + RMS_EPS)
    if tail == "final":
        g_ref, o_ref = rest
        o_ref[:, :half] = x_lo * inv * g_ref[:, :half]
        o_ref[:, half:] = x_hi * inv * g_ref[:, half:]
    else:
        g_ref, sc_ref, sh_ref, o_ref, h_ref = rest
        o_ref[:, :half] = x_lo
        o_ref[:, half:] = x_hi
        for cols, xs in ((slice(0, half), x_lo), (slice(half, 2 * half), x_hi)):
            h = xs * inv * g_ref[:, cols] * (1.0 + sc_ref[0, :, cols]) + sh_ref[0, :, cols]
            h_ref[:, cols] = h.astype(BF16)


def _combine(ys, tok_gate, x, gate, seq, *, final_g=None, next_norm=None, tm=512):
    K, T, pw = ys.shape
    D = x.shape[1]
    bmap = lambda i: ((i * tm) // seq, 0, 0)
    rows = pl.BlockSpec((tm, D), lambda i: (i, 0))
    vec = pl.BlockSpec((1, D), lambda i: (0, 0))
    per_batch = pl.BlockSpec((1, 1, D), bmap)
    in_specs = [pl.BlockSpec((K, tm, pw), lambda i: (0, i, 0)),
                pl.BlockSpec((tm, K), lambda i: (i, 0)), rows, per_batch]
    args = [ys, tok_gate, x, gate]
    if final_g is not None:
        tail = "final"
        in_specs.append(vec)
        args.append(final_g)
        out_specs, out_shape = rows, jax.ShapeDtypeStruct((T, D), F32)
    else:
        tail = "next"
        in_specs += [vec, per_batch, per_batch]
        args += list(next_norm)
        out_specs = [rows, rows]
        out_shape = [jax.ShapeDtypeStruct((T, D), F32), jax.ShapeDtypeStruct((T, D), BF16)]
    return pl.pallas_call(
        functools.partial(_combine_kernel, tail=tail),
        grid=(T // tm,),
        in_specs=in_specs,
        out_specs=out_specs,
        out_shape=out_shape,
        compiler_params=_cparams(("parallel",)),
        name="moe_combine",
    )(*args)


MOE_BLOCK = 512


def _slots_kernel(start_ref, idx_ref, rank_ref, o_ref):
    idx = idx_ref[...]
    acc = rank_ref[...]
    for e in range(start_ref.shape[0]):
        acc = acc + jnp.where(idx == e, start_ref[e], 0)
    o_ref[...] = acc


def _slots(idx_t, rank_t, group_start):
    full = pl.BlockSpec(idx_t.shape, lambda i, gs: (0, 0))
    return pl.pallas_call(
        _slots_kernel,
        grid_spec=pltpu.PrefetchScalarGridSpec(num_scalar_prefetch=1, grid=(1,),
                                               in_specs=[full, full], out_specs=full),
        out_shape=jax.ShapeDtypeStruct(idx_t.shape, I32),
        compiler_params=_cparams(("arbitrary",)),
        name="moe_slots",
    )(group_start, idx_t, rank_t)


def _dispatch_plan(idx_t, rank_t, counts, blk):
    K, T = idx_t.shape
    E = counts.shape[0]
    padded = (counts + blk - 1) // blk * blk
    pad_end = jnp.cumsum(padded)
    pad_start = pad_end - padded
    experts = jnp.arange(E, dtype=I32)
    slots = _slots(idx_t, rank_t, pad_start.astype(I32))
    n_blocks = -(-(K * T + E * (blk - 1)) // blk)
    blk_start = jnp.arange(n_blocks, dtype=I32) * blk
    blk_e = jnp.minimum(jnp.sum(pad_end[None, :] <= blk_start[:, None], axis=1), E - 1).astype(I32)
    n_valid = jnp.clip(pad_start[blk_e] + counts[blk_e] - blk_start, 0, blk).astype(I32)
    later = jnp.where(counts > 0, experts, E)
    after = lax.cummin(jnp.concatenate([later[1:], jnp.full((1,), E, I32)]), reverse=True)
    next_e = jnp.where(after < E, after, -1).astype(I32)[blk_e]
    return slots, blk_e, n_valid, next_e, n_blocks * blk


def _mixer_out_moe(y, w_mix, x, mix_gate, g, scale, shift, gate, r_w, r_b, layer,
                   w_glu, b_glu, w_lin, b_lin, w_out, b_out, seq, **tail):
    T, D = x.shape
    x, hp, idx_t, gate_t, rank_t, counts = _mixer_out_route(
        y, w_mix, x, mix_gate, g, scale, shift, r_w, r_b, seq)
    slots, blk_e, n_valid, next_e, n_rows = _dispatch_plan(idx_t, rank_t, counts[:, 0],
                                                           MOE_BLOCK)
    xs = _sc_scatter_rows(hp, slots, n_rows)
    yp = _experts(xs, blk_e, n_valid, next_e, layer, w_glu, b_glu, w_lin, b_lin, w_out, b_out,
                  blk=MOE_BLOCK)
    ys = _sc_gather(yp, slots.reshape(TOP_K * T)).reshape(TOP_K, T, yp.shape[1])
    return _combine(ys, jnp.transpose(gate_t), x, gate, seq, **tail)


def kernel(x, c, norm_g, ada_w, final_norm_g, sg_w_in, sg_ln_g, sg_ln_b, sg_w_s, sg_b_s, sg_w_out, pool_w_in, pool_w_grp, pool_ls, pool_w_out, hgrn_w_in, hgrn_lb_logits, hgrn_gnorm_g, hgrn_w_out, router_w, router_b, expert_w_glu, expert_b_glu, expert_w_lin, expert_b_lin, expert_w_out, expert_b_out):
    B, S, D = x.shape
    depth = norm_g.shape[0]
    T = B * S
    xf = x.reshape(T, D)

    ada_w = ada_w.reshape(depth * 2, D, 3 * D)

    def mod_parts(layer, sub):
        m = _ada_mod(c, ada_w, layer * 2 + sub)[:, None, :]
        return m[..., :D], m[..., D:2 * D], m[..., 2 * D:]

    def mixer_norm(layer, shift, scale):
        return norm_g[layer, 0][None, :], scale, shift

    lb_cum = jnp.cumsum(jax.nn.softmax(hgrn_lb_logits.astype(F32), axis=0), axis=0)
    lower_bounds = lb_cum - lb_cum[0]

    shift, scale, mix_gate = mod_parts(0, 0)
    src, norm = xf, mixer_norm(0, shift, scale)
    for layer in range(depth):
        kind, slot = layer % 3, layer // 3
        if kind == 0:
            y = _spatial_mixer(src, norm, sg_w_in[slot].astype(BF16), sg_ln_g[slot][None, :],
                               sg_ln_b[slot][None, :], sg_w_s[slot], sg_b_s[slot], S)
            w_out = sg_w_out[slot]
        elif kind == 1:
            y = _pool_mixer(src, norm, pool_w_in[slot].astype(BF16),
                            pool_w_grp[slot].astype(BF16), pool_ls[slot][None, :], S)
            w_out = pool_w_out[slot]
        else:
            proj, f_pre = _norm_matmul(src, norm, hgrn_w_in[slot].astype(BF16), S, act=None,
                                       out_dtype=BF16, f32_col=1, tn=D)
            y = _hgrn_mix(proj, f_pre, lower_bounds[layer][None, :],
                          hgrn_gnorm_g[slot][None, :], B, S)
            w_out = hgrn_w_out[slot]

        moe_shift, moe_scale, moe_gate = mod_parts(layer, 1)
        moe_args = (y, w_out.astype(BF16), xf, mix_gate, norm_g[layer, 1][None, :],
                    moe_scale, moe_shift, moe_gate, router_w[layer], router_b[layer], layer,
                    expert_w_glu, expert_b_glu, expert_w_lin, expert_b_lin,
                    expert_w_out, expert_b_out, S)
        if layer == depth - 1:
            xf = _mixer_out_moe(*moe_args, final_g=final_norm_g[None, :])
        else:
            shift, scale, mix_gate = mod_parts(layer + 1, 0)
            xf, src = _mixer_out_moe(*moe_args, next_norm=mixer_norm(layer + 1, shift, scale))
            norm = None

    return xf.reshape(B, S, D)
```

```python
import functools

import numpy as np
import jax
import jax.numpy as jnp
from jax import lax
from jax.experimental import pallas as pl
from jax.experimental.pallas import tpu as pltpu
from jax.experimental.pallas import tpu_sc as plsc

F32 = jnp.float32
BF16 = jnp.bfloat16
I32 = jnp.int32
U32 = jnp.uint32

RMS_EPS = 1e-5
LN_EPS = 1e-5
SWIGLU_ALPHA = 1.702
SWIGLU_LIMIT = 7.0
TOP_K = 4
SG_CHUNK = 128
POOL_WINDOWS = (2, 4, 8, 16)
HEAD_DIM = 128
LANES = 128
SC_WORKERS = 32
SC_ROWS_PER_STEP = 64

VMEM_LIMIT = 48 * 1024 * 1024
BIG_VMEM_LIMIT = 56 * 1024 * 1024


def _cparams(sem):
    return pltpu.CompilerParams(dimension_semantics=sem, vmem_limit_bytes=VMEM_LIMIT)


def _dot(a, b):
    return jnp.dot(a, b, preferred_element_type=F32)


def _dot_nt(a, b):
    return lax.dot_general(a, b, (((1,), (1,)), ((), ())), preferred_element_type=F32)


def _dot_tn(a, b):
    return lax.dot_general(a, b, (((0,), (0,)), ((), ())), preferred_element_type=F32)


def _split_bf16(x):
    hi = x.astype(BF16)
    lo = (x - hi.astype(F32)).astype(BF16)
    return hi, lo


def _packed_layout(d):
    return d // 2, I32


def _pack_halves(y):
    n = y.shape[1] // 2
    lo = lax.bitcast_convert_type(y[:, :n].astype(BF16).astype(F32), U32)
    hi = lax.bitcast_convert_type(y[:, n:].astype(BF16).astype(F32), U32)
    word = lax.shift_right_logical(lo, jnp.uint32(16)) | (hi & jnp.uint32(0xFFFF0000))
    return lax.bitcast_convert_type(word, I32)


def _unpack_halves(w):
    u = lax.bitcast_convert_type(w, U32)
    lo = lax.bitcast_convert_type(lax.shift_left(u, jnp.uint32(16)), F32)
    hi = lax.bitcast_convert_type(u & jnp.uint32(0xFFFF0000), F32)
    return lo, hi


def _modulated_norm(x, g, scale, shift):
    ms = jnp.mean(x * x, axis=-1, keepdims=True)
    y = x * lax.rsqrt(ms + RMS_EPS) * g
    return y * (1.0 + scale) + shift


def _ada_kernel(c_ref, w_ref, o_ref):
    w = w_ref[0]
    reps = w.shape[1] // LANES
    for b in range(c_ref.shape[0]):
        cb = c_ref[b]
        s = jnp.tile(cb * jax.nn.sigmoid(cb), (1, reps))
        o_ref[b] = jnp.sum(w * s, axis=0, keepdims=True)


def _ada_mod(c, ada_w, index):
    L, D, N = ada_w.shape
    B = c.shape[0]
    tn = 512
    c_cols = jnp.broadcast_to(c[:, :, None], (B, D, LANES))
    out = pl.pallas_call(
        _ada_kernel,
        grid=(N // tn,),
        in_specs=[pl.BlockSpec((B, D, LANES), lambda j: (0, 0, 0)),
                  pl.BlockSpec((1, D, tn), lambda j: (index, 0, j))],
        out_specs=pl.BlockSpec((B, 1, tn), lambda j: (0, 0, j)),
        out_shape=jax.ShapeDtypeStruct((B, 1, N), F32),
        compiler_params=_cparams(("parallel",)),
        name="ada_mod",
    )(c_cols, ada_w)
    return out[:, 0]


def _nm_kernel(*refs, act, f32_col, prenormed):
    if prenormed:
        h_ref, w_ref, o_ref = refs[:3]
        outs = refs[3:]
    else:
        x_ref, g_ref, sc_ref, sh_ref, w_ref, o_ref = refs[:6]
        outs, h_ref = refs[6:-1], refs[-1]

        @pl.when(pl.program_id(1) == 0)
        def _():
            h = _modulated_norm(x_ref[...], g_ref[...], sc_ref[0], sh_ref[0])
            h_ref[...] = h.astype(BF16)

    acc = _dot(h_ref[...], w_ref[...])
    if act == "gelu":
        acc = 0.5 * acc * (1.0 + lax.erf(acc * np.float32(1.0 / np.sqrt(2.0))))
    o_ref[...] = acc.astype(o_ref.dtype)
    if f32_col is not None:
        @pl.when(pl.program_id(1) == f32_col)
        def _():
            outs[0][...] = acc


def _norm_matmul(x, norm, w, seq, *, act, out_dtype, f32_col=None, tm=512, tn=2048):
    T, D = x.shape
    N = w.shape[1]
    prenormed = norm is None
    bmap = lambda i, j: ((i * tm) // seq, 0, 0)
    out_specs = [pl.BlockSpec((tm, tn), lambda i, j: (i, j))]
    out_shape = [jax.ShapeDtypeStruct((T, N), out_dtype)]
    if f32_col is not None:
        out_specs.append(pl.BlockSpec((tm, tn), lambda i, j: (i, 0)))
        out_shape.append(jax.ShapeDtypeStruct((T, tn), F32))
    in_specs = [pl.BlockSpec((tm, D), lambda i, j: (i, 0))]
    args = [x]
    if not prenormed:
        in_specs += [pl.BlockSpec((1, D), lambda i, j: (0, 0)),
                     pl.BlockSpec((1, 1, D), bmap),
                     pl.BlockSpec((1, 1, D), bmap)]
        args += list(norm)
    in_specs.append(pl.BlockSpec((D, tn), lambda i, j: (0, j)))
    args.append(w)
    outs = pl.pallas_call(
        functools.partial(_nm_kernel, act=act, f32_col=f32_col, prenormed=prenormed),
        grid=(T // tm, N // tn),
        in_specs=in_specs,
        out_specs=out_specs,
        out_shape=out_shape,
        scratch_shapes=[] if prenormed else [pltpu.VMEM((tm, D), BF16)],
        compiler_params=pltpu.CompilerParams(dimension_semantics=("parallel", "arbitrary"),
                                             vmem_limit_bytes=BIG_VMEM_LIMIT),
        name="norm_matmul_" + str(act),
    )(*args)
    return outs if f32_col is not None else outs[0]


def _sg_kernel(u_ref, v_ref, lng_ref, lnb_ref, ws_ref, bst_ref, y_ref, vn_ref, *, heads):
    v = v_ref[...].astype(F32)
    mu = jnp.mean(v, axis=-1, keepdims=True)
    vc = v - mu
    var = jnp.mean(vc * vc, axis=-1, keepdims=True)
    vn_ref[...] = (vc * lax.rsqrt(var + LN_EPS) * lng_ref[...] + lnb_ref[...]).astype(BF16)

    n_chunks = v_ref.shape[0] // SG_CHUNK
    row = lax.broadcasted_iota(I32, (SG_CHUNK, SG_CHUNK), 0)
    col = lax.broadcasted_iota(I32, (SG_CHUNK, SG_CHUNK), 1)
    causal = row >= col
    for hd in range(heads):
        cs = slice(hd * HEAD_DIM, (hd + 1) * HEAD_DIM)
        wm = jnp.where(causal, ws_ref[hd], 0.0).astype(BF16)
        bias = bst_ref[:, hd:hd + 1]
        for ci in range(n_chunks):
            rs = slice(ci * SG_CHUNK, (ci + 1) * SG_CHUNK)
            mixed = _dot(wm, vn_ref[rs, cs]) + bias
            y_ref[rs, cs] = (u_ref[rs, cs].astype(F32) * mixed).astype(BF16)


def _spatial_gate(z, ln_g, ln_b, w_s, b_s, *, tm=512):
    T, two_w = z.shape
    W = two_w // 2
    heads = w_s.shape[0]
    return pl.pallas_call(
        functools.partial(_sg_kernel, heads=heads),
        grid=(T // tm,),
        in_specs=[pl.BlockSpec((tm, W), lambda i: (i, 0)),
                  pl.BlockSpec((tm, W), lambda i: (i, 1)),
                  pl.BlockSpec((1, W), lambda i: (0, 0)),
                  pl.BlockSpec((1, W), lambda i: (0, 0)),
                  pl.BlockSpec((heads, SG_CHUNK, SG_CHUNK), lambda i: (0, 0, 0)),
                  pl.BlockSpec((SG_CHUNK, heads), lambda i: (0, 0))],
        out_specs=pl.BlockSpec((tm, W), lambda i: (i, 0)),
        out_shape=jax.ShapeDtypeStruct((T, W), BF16),
        scratch_shapes=[pltpu.VMEM((tm, W), BF16)],
        compiler_params=_cparams(("parallel",)),
        name="spatial_gate",
    )(z, z, ln_g, ln_b, w_s, jnp.transpose(b_s))


def _sg_fused_kernel(*refs, heads, prenormed):
    if prenormed:
        h_ref, w_ref = refs[:2]
        rest = refs[2:]
        h = h_ref[...]
    else:
        x_ref, g_ref, sc_ref, sh_ref, w_ref = refs[:5]
        rest = refs[5:]
        h = _modulated_norm(x_ref[...], g_ref[...], sc_ref[0], sh_ref[0]).astype(BF16)
    lng_ref, lnb_ref, ws_ref, bst_ref, y_ref, u_ref, v_ref, vn_ref = rest
    width = u_ref.shape[1]
    inv_sqrt2 = np.float32(1.0 / np.sqrt(2.0))
    for dst, cols in ((v_ref, slice(width, 2 * width)), (u_ref, slice(0, width))):
        acc = _dot(h, w_ref[:, cols])
        dst[...] = (0.5 * acc * (1.0 + lax.erf(acc * inv_sqrt2))).astype(BF16)
    _sg_kernel(u_ref, v_ref, lng_ref, lnb_ref, ws_ref, bst_ref, y_ref, vn_ref, heads=heads)


def _spatial_mixer(x, norm, w_in, ln_g, ln_b, w_s, b_s, seq, *, tm=256):
    T, D = x.shape
    W = w_in.shape[1] // 2
    heads = w_s.shape[0]
    prenormed = norm is None
    bmap = lambda i: ((i * tm) // seq, 0, 0)
    const = lambda i: (0, 0)
    in_specs = [pl.BlockSpec((tm, D), lambda i: (i, 0))]
    args = [x]
    if not prenormed:
        in_specs += [pl.BlockSpec((1, D), const), pl.BlockSpec((1, 1, D), bmap),
                     pl.BlockSpec((1, 1, D), bmap)]
        args += list(norm)
    in_specs += [pl.BlockSpec((D, 2 * W), const, pipeline_mode=pl.Buffered(1)),
                 pl.BlockSpec((1, W), const),
                 pl.BlockSpec((1, W), const),
                 pl.BlockSpec((heads, SG_CHUNK, SG_CHUNK), lambda i: (0, 0, 0)),
                 pl.BlockSpec((SG_CHUNK, heads), const)]
    args += [w_in, ln_g, ln_b, w_s, jnp.transpose(b_s)]
    return pl.pallas_call(
        functools.partial(_sg_fused_kernel, heads=heads, prenormed=prenormed),
        grid=(T // tm,),
        in_specs=in_specs,
        out_specs=pl.BlockSpec((tm, W), lambda i: (i, 0)),
        out_shape=jax.ShapeDtypeStruct((T, W), BF16),
        scratch_shapes=[pltpu.VMEM((tm, W), BF16), pltpu.VMEM((tm, W), BF16),
                        pltpu.VMEM((tm, W), BF16)],
        compiler_params=pltpu.CompilerParams(dimension_semantics=("parallel",),
                                             vmem_limit_bytes=BIG_VMEM_LIMIT),
        name="spatial_mixer",
    )(*args)


POOL_HALO = 16


def _pool_kernel(z_ref, halo_ref, wg_ref, ls_ref, y_ref, *, seq):
    tm = z_ref.shape[0]
    gdim = wg_ref.shape[1]
    pos0 = (pl.program_id(0) * tm) % seq
    halo_on = (pos0 > 0).astype(F32)
    pos = pos0 + lax.broadcasted_iota(I32, (tm, 1), 0)
    for gi, wnd in enumerate(POOL_WINDOWS):
        cs = slice(gi * gdim, (gi + 1) * gdim)
        zg = z_ref[:, cs]
        s = jnp.concatenate([halo_ref[:, cs] * halo_on, zg], axis=0)
        k = 1
        while k < wnd:
            s = s + pltpu.roll(s, k, 0)
            k *= 2
        cnt = jnp.minimum(pos + 1, wnd).astype(F32)
        pooled = s[POOL_HALO:, :] / cnt - zg
        yg = _dot(pooled.astype(BF16), wg_ref[gi]) * ls_ref[:, cs]
        y_ref[:, cs] = yg.astype(BF16)


def _pool_mix(z, w_grp, ls, seq, *, tm=512):
    T, D = z.shape
    G, gdim, _ = w_grp.shape
    assert max(POOL_WINDOWS) <= POOL_HALO and tm % POOL_HALO == 0
    ratio = tm // POOL_HALO
    return pl.pallas_call(
        functools.partial(_pool_kernel, seq=seq),
        grid=(T // tm,),
        in_specs=[pl.BlockSpec((tm, D), lambda i: (i, 0)),
                  pl.BlockSpec((POOL_HALO, D), lambda i: (jnp.maximum(i * ratio - 1, 0), 0)),
                  pl.BlockSpec((G, gdim, gdim), lambda i: (0, 0, 0)),
                  pl.BlockSpec((1, D), lambda i: (0, 0))],
        out_specs=pl.BlockSpec((tm, D), lambda i: (i, 0)),
        out_shape=jax.ShapeDtypeStruct((T, D), BF16),
        compiler_params=_cparams(("parallel",)),
        name="pool_mix",
    )(z, z, w_grp, ls)


def _pool_fused_kernel(*refs, seq, prenormed):
    if prenormed:
        h_ref, w_ref = refs[:2]
        rest = refs[2:]
        h = h_ref[...]
    else:
        x_ref, g_ref, sc_ref, sh_ref, w_ref = refs[:5]
        rest = refs[5:]
        h = _modulated_norm(x_ref[...], g_ref[...], sc_ref[0], sh_ref[0]).astype(BF16)
    wg_ref, ls_ref, y_ref, z_ref, halo_ref = rest
    tm = z_ref.shape[0]

    @pl.when((pl.program_id(0) * tm) % seq == 0)
    def _():
        halo_ref[...] = jnp.zeros_like(halo_ref)

    z_ref[...] = _dot(h, w_ref[...])
    _pool_kernel(z_ref, halo_ref, wg_ref, ls_ref, y_ref, seq=seq)
    halo_ref[...] = z_ref[tm - POOL_HALO:, :]


def _pool_mixer(x, norm, w_in, w_grp, ls, seq, *, tm=512):
    T, D = x.shape
    G, gdim, _ = w_grp.shape
    assert max(POOL_WINDOWS) <= POOL_HALO and tm % POOL_HALO == 0 and seq % tm == 0
    prenormed = norm is None
    bmap = lambda i: ((i * tm) // seq, 0, 0)
    const = lambda i: (0, 0)
    in_specs = [pl.BlockSpec((tm, D), lambda i: (i, 0))]
    args = [x]
    if not prenormed:
        in_specs += [pl.BlockSpec((1, D), const), pl.BlockSpec((1, 1, D), bmap),
                     pl.BlockSpec((1, 1, D), bmap)]
        args += list(norm)
    in_specs += [pl.BlockSpec((D, D), const, pipeline_mode=pl.Buffered(1)),
                 pl.BlockSpec((G, gdim, gdim), lambda i: (0, 0, 0)),
                 pl.BlockSpec((1, D), const)]
    args += [w_in, w_grp, ls]
    return pl.pallas_call(
        functools.partial(_pool_fused_kernel, seq=seq, prenormed=prenormed),
        grid=(T // tm,),
        in_specs=in_specs,
        out_specs=pl.BlockSpec((tm, D), lambda i: (i, 0)),
        out_shape=jax.ShapeDtypeStruct((T, D), BF16),
        scratch_shapes=[pltpu.VMEM((tm, D), F32), pltpu.VMEM((POOL_HALO, D), F32)],
        compiler_params=_cparams(("arbitrary",)),
        name="pool_mixer",
    )(*args)


HGRN_CHUNK = 128
HGRN_BASE = 32
HGRN_GUARD_LOG2 = 100.0
LOG2_E = 1.4426950408889634


def _hgrn_tables(c=HGRN_CHUNK, base=HGRN_BASE):
    t = np.arange(c)
    j = t[None, :]
    sums = [j <= t[:, None]]
    masks = []
    n_shallow = 0
    h = c // 2
    while h >= 1:
        off = t % (2 * h)
        mid = (t // (2 * h)) * (2 * h) + h
        second = off >= h
        if h >= 2:
            m_q = second[:, None] & (j >= mid[:, None]) & (j <= t[:, None])
            m_k = (~second)[:, None] & (j > t[:, None]) & (j < mid[:, None])
            sums.append(m_q | m_k)
        same = (t[:, None] // (2 * h)) == (t[None, :] // (2 * h))
        masks.append(same & second[:, None] & (~second)[None, :])
        n_shallow += h >= base
        h //= 2
    same_base = (t[:, None] // base) == (t[None, :] // base)
    sums.append(same_base & (j <= t[:, None]))
    sums = np.stack(sums).astype(np.float32)
    base_mask = (same_base & (j <= t[:, None])).astype(np.float32)
    return (np.concatenate([sums, sums], axis=2), np.stack(masks).astype(np.float32), base_mask,
            n_shallow)


def _hgrn_kernel(q_ref, f_ref, i_ref, g_ref, lb_ref, gn_ref, sums_ref, masks_ref, bmask_ref,
                 o_ref, st_ref, qb_ref, ke_ref, z_ref, qd_ref, kd_ref, vb_ref, eb_ref, sc_ref,
                 *, heads, n_shallow):
    c = q_ref.shape[0]
    n_sum = sums_ref.shape[0] - 1
    n_lvl = masks_ref.shape[0]

    @pl.when(pl.program_id(1) == 0)
    def _():
        st_ref[...] = jnp.zeros_like(st_ref)

    lb = lb_ref[...]
    sig = jax.nn.sigmoid(f_ref[...])
    fg = lb + (1.0 - lb) * sig
    lf2 = jnp.log(fg) * np.float32(LOG2_E)
    kk = (1.0 - lb) * (1.0 - sig)
    lf_hi, lf_lo = _split_bf16(lf2)
    lf_cat = jnp.concatenate([lf_hi, lf_lo], axis=0)
    qr = q_ref[...].astype(F32)
    q = qr * jax.nn.sigmoid(qr)
    vb_ref[...] = i_ref[...].astype(BF16)
    q16 = q.astype(BF16)
    k16 = kk.astype(BF16)

    b2 = _dot(sums_ref[0], lf_cat)
    eb = jnp.exp2(b2)
    qb_ref[...] = (q * eb).astype(BF16)
    eb_ref[...] = eb[c - 1:c, :]
    ke_ref[...] = (kk * jnp.exp2(b2[c - 1:c, :] - b2)).astype(BF16)
    w2 = _dot(sums_ref[n_sum], lf_cat)
    shallow = jnp.min(w2) >= -HGRN_GUARD_LOG2

    row = lax.broadcasted_iota(I32, (c, 1), 0)

    def level(l):
        half = c >> (l + 1)
        second = (row & (2 * half - 1)) >= half
        if l + 1 < n_sum:
            x = jnp.exp2(_dot(sums_ref[l + 1], lf_cat).astype(BF16))
        else:
            x = jnp.where(second, fg, 1.0).astype(BF16)
        z_ref[l] = jnp.where(second, q16, k16) * x

    gn = gn_ref[...]

    def finish(hd, cs, o):
        st = st_ref[hd]
        vh = vb_ref[:, cs]
        o = o + _dot_nt(qb_ref[:, cs], st.astype(BF16))
        st_ref[hd] = st * eb_ref[:, cs] + _dot_tn(vh, ke_ref[:, cs])
        on = o * lax.rsqrt(jnp.mean(o * o, axis=-1, keepdims=True) + RMS_EPS) * gn
        gv = g_ref[:, cs].astype(F32)
        o_ref[:, cs] = (on * (gv * jax.nn.sigmoid(gv))).astype(BF16)

    def level_scores(cs, n):
        scores = None
        for l in range(n):
            half = c >> (l + 1)
            zl = z_ref[l, :, cs]
            if half >= 16:
                starts = list(range(half, c, 2 * half))
                zq = jnp.concatenate([z_ref[l, r:r + half, cs] for r in starts], axis=0)
                p = _dot_nt(zq, zl).astype(BF16)
                parts = []
                for n_i, r in enumerate(starts):
                    parts.append(jnp.zeros((half, c), BF16))
                    parts.append(masks_ref[l, r:r + half, :] * p[n_i * half:(n_i + 1) * half])
                term = jnp.concatenate(parts, axis=0)
            else:
                term = masks_ref[l] * _dot_nt(zl, zl).astype(BF16)
            scores = term if scores is None else scores + term
        return scores

    @pl.when(shallow)
    def _():
        for l in range(n_shallow):
            level(l)
        qd_ref[...] = (q * jnp.exp2(w2)).astype(BF16)
        kd_ref[...] = (kk * jnp.exp2(-w2)).astype(BF16)

        def head_scores(hd, carry):
            cs = pl.ds(pl.multiple_of(hd * HEAD_DIM, HEAD_DIM), HEAD_DIM)
            sc_ref[hd] = (level_scores(cs, n_shallow)
                          + bmask_ref[...] * _dot_nt(qd_ref[:, cs], kd_ref[:, cs]).astype(BF16))
            return carry

        def head_out(hd, carry):
            cs = pl.ds(pl.multiple_of(hd * HEAD_DIM, HEAD_DIM), HEAD_DIM)
            finish(hd, cs, _dot(sc_ref[hd], vb_ref[:, cs]))
            return carry

        lax.fori_loop(0, heads, head_scores, 0, unroll=16)
        lax.fori_loop(0, heads, head_out, 0, unroll=16)

    @pl.when(jnp.logical_not(shallow))
    def _():
        for l in range(n_lvl):
            level(l)
        qd_ref[...] = q16
        kd_ref[...] = k16

        def head(hd, carry):
            cs = pl.ds(pl.multiple_of(hd * HEAD_DIM, HEAD_DIM), HEAD_DIM)
            vh = vb_ref[:, cs]
            diag = jnp.sum(qd_ref[:, cs].astype(F32) * kd_ref[:, cs].astype(F32), axis=-1,
                           keepdims=True)
            finish(hd, cs, _dot(level_scores(cs, n_lvl), vh) + diag * vh.astype(F32))
            return carry

        lax.fori_loop(0, heads, head, 0, unroll=4)


def _hgrn_mix(proj, f_pre, lb, gnorm_g, batch, seq):
    T, four_d = proj.shape
    D = four_d // 4
    heads = D // HEAD_DIM
    c = HGRN_CHUNK
    n_chunks = seq // c
    sums, masks, base_mask, n_shallow = _hgrn_tables(c)
    n_sum, n_mask = sums.shape[0], masks.shape[0]
    row = lambda b, i: b * n_chunks + i
    return pl.pallas_call(
        functools.partial(_hgrn_kernel, heads=heads, n_shallow=n_shallow),
        grid=(batch, n_chunks),
        in_specs=[pl.BlockSpec((c, D), lambda b, i: (row(b, i), 0)),
                  pl.BlockSpec((c, D), lambda b, i: (row(b, i), 0)),
                  pl.BlockSpec((c, D), lambda b, i: (row(b, i), 2)),
                  pl.BlockSpec((c, D), lambda b, i: (row(b, i), 3)),
                  pl.BlockSpec((1, D), lambda b, i: (0, 0)),
                  pl.BlockSpec((1, HEAD_DIM), lambda b, i: (0, 0)),
                  pl.BlockSpec((n_sum, c, 2 * c), lambda b, i: (0, 0, 0)),
                  pl.BlockSpec((n_mask, c, c), lambda b, i: (0, 0, 0)),
                  pl.BlockSpec((c, c), lambda b, i: (0, 0))],
        out_specs=pl.BlockSpec((c, D), lambda b, i: (row(b, i), 0)),
        out_shape=jax.ShapeDtypeStruct((T, D), BF16),
        scratch_shapes=[pltpu.VMEM((heads, HEAD_DIM, HEAD_DIM), F32),
                        pltpu.VMEM((c, D), BF16),
                        pltpu.VMEM((c, D), BF16),
                        pltpu.VMEM((n_mask, c, D), BF16),
                        pltpu.VMEM((c, D), BF16),
                        pltpu.VMEM((c, D), BF16),
                        pltpu.VMEM((c, D), BF16),
                        pltpu.VMEM((1, D), F32),
                        pltpu.VMEM((heads, c, c), BF16)],
        compiler_params=_cparams(("parallel", "arbitrary")),
        name="hgrn_mix",
    )(proj, f_pre, proj, proj, lb, gnorm_g, jnp.asarray(sums, BF16), jnp.asarray(masks, BF16),
      jnp.asarray(base_mask, BF16))


def _route_kernel(y_ref, w_ref, xin_ref, mgate_ref, g_ref, sc_ref, sh_ref, rw_ref,
                  rb_ref, tri_ref, x_ref, hp_ref, idx_ref, gate_ref, rank_ref, cnt_ref, carry_ref,
                  *, n_experts):
    @pl.when(pl.program_id(0) == 0)
    def _():
        carry_ref[...] = jnp.zeros_like(carry_ref)

    x = xin_ref[...] + mgate_ref[0] * _dot(y_ref[...], w_ref[...])
    x_ref[...] = x
    h = _modulated_norm(x, g_ref[...], sc_ref[0], sh_ref[0])
    hp_ref[...] = _pack_halves(h)

    h_hi, h_lo = _split_bf16(h)
    both = _dot(h_hi, rw_ref[...])
    logits = (both[:, :LANES] + both[:, LANES:] + _dot(h_lo, rw_ref[:, :LANES])) + rb_ref[...]
    lt = jnp.transpose(logits)[:n_experts, :]
    eidx = lax.broadcasted_iota(I32, lt.shape, 0)
    vals, idxs, hits = [], [], []
    for _ in range(TOP_K):
        m = jnp.max(lt, axis=0, keepdims=True)
        sel = jnp.min(jnp.where(lt == m, eidx, n_experts), axis=0, keepdims=True)
        hit = eidx == sel
        vals.append(m)
        idxs.append(sel)
        hits.append(hit)
        lt = jnp.where(hit, -jnp.inf, lt)
    exps = [jnp.exp(v - vals[0]) for v in vals]
    denom = exps[0]
    for e in exps[1:]:
        denom = denom + e
    idx_ref[...] = jnp.concatenate(idxs, axis=0)
    gate_ref[...] = jnp.concatenate([e / denom for e in exps], axis=0)

    onehot = hits[0].astype(F32)
    for hit in hits[1:]:
        onehot = onehot + hit.astype(F32)
    before = carry_ref[...] + _dot(onehot.astype(BF16), tri_ref[...])
    rank_ref[...] = jnp.concatenate(
        [jnp.sum(jnp.where(hit, before, 0.0), axis=0, keepdims=True) for hit in hits],
        axis=0).astype(I32)
    carry_ref[...] += jnp.sum(onehot, axis=1, keepdims=True)
    cnt_ref[...] = carry_ref[...].astype(I32)


def _mixer_out_route(y, w, x, mix_gate, g, scale, shift, r_w, r_b, seq, *, tm=512):
    T, K = y.shape
    D = w.shape[1]
    E = r_w.shape[1]
    rw_pad = jnp.zeros((D, LANES), F32).at[:, :E].set(r_w)
    rw_cat = jnp.concatenate(_split_bf16(rw_pad), axis=1)
    rb_pad = jnp.full((1, LANES), -1e30, F32).at[0, :E].set(r_b)
    tri = jnp.asarray(np.triu(np.ones((tm, tm), np.float32), k=1), BF16)
    pw, pdt = _packed_layout(D)
    bmap = lambda i: ((i * tm) // seq, 0, 0)
    const = lambda i: (0, 0)
    rows = lambda i: (i, 0)
    kt = pl.BlockSpec((TOP_K, tm), lambda i: (0, i))
    return pl.pallas_call(
        functools.partial(_route_kernel, n_experts=E),
        grid=(T // tm,),
        in_specs=[pl.BlockSpec((tm, K), rows),
                  pl.BlockSpec((K, D), const, pipeline_mode=pl.Buffered(1)),
                  pl.BlockSpec((tm, D), rows),
                  pl.BlockSpec((1, 1, D), bmap),
                  pl.BlockSpec((1, D), const),
                  pl.BlockSpec((1, 1, D), bmap),
                  pl.BlockSpec((1, 1, D), bmap),
                  pl.BlockSpec((D, 2 * LANES), const, pipeline_mode=pl.Buffered(1)),
                  pl.BlockSpec((1, LANES), const),
                  pl.BlockSpec((tm, tm), const, pipeline_mode=pl.Buffered(1))],
        out_specs=[pl.BlockSpec((tm, D), rows), pl.BlockSpec((tm, pw), rows), kt, kt, kt,
                   pl.BlockSpec((E, 1), const)],
        out_shape=[jax.ShapeDtypeStruct((T, D), F32),
                   jax.ShapeDtypeStruct((T, pw), pdt),
                   jax.ShapeDtypeStruct((TOP_K, T), I32),
                   jax.ShapeDtypeStruct((TOP_K, T), F32),
                   jax.ShapeDtypeStruct((TOP_K, T), I32),
                   jax.ShapeDtypeStruct((E, 1), I32)],
        scratch_shapes=[pltpu.VMEM((E, 1), F32)],
        compiler_params=_cparams(("arbitrary",)),
        name="mixer_out_route",
    )(y, w, x, mix_gate, g, scale, shift, rw_cat, rb_pad, tri)


def _sc_gather(table, idx):
    R = idx.shape[0]
    W = table.shape[1]
    rows = SC_ROWS_PER_STEP // 2
    per_worker = R // SC_WORKERS
    steps = per_worker // rows
    assert steps * rows * SC_WORKERS == R and steps % 2 == 0
    mesh = plsc.VectorSubcoreMesh(core_axis_name="c", subcore_axis_name="s")
    n_cores = mesh.num_cores

    @functools.partial(
        pl.kernel, mesh=mesh,
        out_type=jax.ShapeDtypeStruct((R, W), table.dtype),
        scratch_types=[pltpu.VMEM((steps, rows), I32),
                       pltpu.VMEM((rows, W), table.dtype),
                       pltpu.VMEM((rows, W), table.dtype),
                       pltpu.SemaphoreType.DMA((2,)),
                       pltpu.SemaphoreType.DMA((2,))],
    )
    def gather(table_hbm, idx_hbm, out_hbm, idx_v, buf0, buf1, gsem, wsem):
        wid = lax.axis_index("s") * n_cores + lax.axis_index("c")
        base = wid * per_worker
        bufs = (buf0, buf1)
        pltpu.sync_copy(idx_hbm.at[pl.ds(wid * steps, steps)], idx_v)

        def fetch(j, b):
            return pltpu.make_async_copy(table_hbm.at[idx_v.at[j]], bufs[b], gsem.at[b])

        def flush(j, b):
            off = pl.multiple_of(base + j * rows, 8)
            return pltpu.make_async_copy(bufs[b], out_hbm.at[pl.ds(off, rows)], wsem.at[b])

        fetch(0, 0).start()

        @pl.loop(0, steps, step=2)
        def _(j0):
            for b in range(2):
                j = j0 + b
                fetch(j, b).wait()
                flush(j, b).start()

                @pl.when(j + 1 < steps)
                def _():
                    @pl.when(j >= 1)
                    def _():
                        flush(j - 1, 1 - b).wait()
                    fetch(j + 1, 1 - b).start()

        flush(steps - 2, 0).wait()
        flush(steps - 1, 1).wait()

    return gather(table, idx.reshape(R // rows, rows))


def _sc_scatter_rows(table, slots, n_out):
    K, T = slots.shape
    W = table.shape[1]
    rows = SC_ROWS_PER_STEP
    per_worker = T // SC_WORKERS
    steps = per_worker // rows
    assert steps * rows * SC_WORKERS == T
    mesh = plsc.VectorSubcoreMesh(core_axis_name="c", subcore_axis_name="s")
    n_cores = mesh.num_cores

    @functools.partial(
        pl.kernel, mesh=mesh,
        out_type=jax.ShapeDtypeStruct((n_out, W), table.dtype),
        scratch_types=[pltpu.VMEM((K, rows), I32),
                       pltpu.VMEM((rows, W), table.dtype)],
    )
    def scatter(table_hbm, slots_hbm, out_hbm, idx_v, rows_v):
        wid = lax.axis_index("s") * n_cores + lax.axis_index("c")
        base = wid * per_worker

        @pl.loop(0, steps)
        def _(j):
            off = pl.multiple_of(base + j * rows, 8)
            pltpu.sync_copy(table_hbm.at[pl.ds(off, rows)], rows_v)
            for k in range(K):
                pltpu.sync_copy(slots_hbm.at[pl.ds(pl.multiple_of(k * T + off, 8), rows)],
                                idx_v.at[k])
                pltpu.sync_copy(rows_v, out_hbm.at[idx_v.at[k]])

    return scatter(table, slots.reshape(K * T))


EXPERT_SUB_ROWS = 256
CAST_ELEMS = 32 * 1024
BF16_TILE_ROWS = 16
WEIGHT_DMA_PRIORITY = 1


def _expert_kernel(blk_e_ref, n_valid_ref, next_e_ref, xs_ref, wg_hbm, bg_ref, wl_hbm, bl_ref,
                   wo_hbm, bo_ref, y_ref, wg_st, wl_st, wo_st, wgb_ref, wlb_ref, wob_ref, sem,
                   *, expert_base):
    i = pl.program_id(0)
    e = blk_e_ref[i]
    n_valid = n_valid_ref[i]
    new_expert = jnp.logical_or(i == 0, e != blk_e_ref[jnp.maximum(i - 1, 0)])

    def weight_copies(expert):
        idx = expert_base + expert
        return (pltpu.make_async_copy(wg_hbm.at[idx], wg_st, sem.at[0]),
                pltpu.make_async_copy(wl_hbm.at[idx], wl_st, sem.at[1]),
                pltpu.make_async_copy(wo_hbm.at[idx], wo_st, sem.at[2]))

    @pl.when(i == 0)
    def _():
        for cp in weight_copies(e):
            cp.start(priority=WEIGHT_DMA_PRIORITY)

    @pl.when(jnp.logical_and(new_expert, n_valid > 0))
    def _():
        for cp in weight_copies(e):
            cp.wait()
        for src, dst in ((wg_st, wgb_ref), (wl_st, wlb_ref), (wo_st, wob_ref)):
            n_rows = CAST_ELEMS // src.shape[1] // BF16_TILE_ROWS * BF16_TILE_ROWS
            assert n_rows > 0 and src.shape[0] % n_rows == 0

            def cast_rows(r, carry, src=src, dst=dst, n_rows=n_rows):
                rows = pl.ds(pl.multiple_of(r * n_rows, n_rows), n_rows)
                dst[rows, :] = src[rows, :].astype(BF16)
                return carry

            lax.fori_loop(0, src.shape[0] // n_rows, cast_rows, 0, unroll=4)
        nxt = next_e_ref[i]

        @pl.when(nxt >= 0)
        def _():
            for cp in weight_copies(nxt):
                cp.start(priority=WEIGHT_DMA_PRIORITY)

    half = wgb_ref.shape[0] // 2
    for first in range(0, xs_ref.shape[0], EXPERT_SUB_ROWS):
        rs = slice(first, first + EXPERT_SUB_ROWS)

        @pl.when(n_valid > first)
        def _():
            live = first + lax.broadcasted_iota(I32, (EXPERT_SUB_ROWS, 1), 0) < n_valid
            lo, hi = _unpack_halves(jnp.where(live, xs_ref[rs, :], 0))
            lo = lo.astype(BF16)
            hi = hi.astype(BF16)
            a = _dot(lo, wgb_ref[:half, :]) + _dot(hi, wgb_ref[half:, :]) + bg_ref[0]
            l = _dot(lo, wlb_ref[:half, :]) + _dot(hi, wlb_ref[half:, :]) + bl_ref[0]
            a = jnp.minimum(a, SWIGLU_LIMIT)
            l = jnp.clip(l, -SWIGLU_LIMIT, SWIGLU_LIMIT)
            act = a * jax.nn.sigmoid(SWIGLU_ALPHA * a) * (l + 1.0)
            y_ref[rs, :] = _pack_halves(_dot(act.astype(BF16), wob_ref[...]) + bo_ref[0])

        @pl.when(n_valid <= first)
        def _():
            y_ref[rs, :] = jnp.zeros((EXPERT_SUB_ROWS, y_ref.shape[1]), y_ref.dtype)


def _experts(xs, blk_e, n_valid, next_e, layer, w_glu, b_glu, w_lin, b_lin, w_out, b_out, *,
             blk):
    P, pw = xs.shape
    L, E, D, F = w_glu.shape
    n_blocks = P // blk
    bmap = lambda i, be, nv, ne: (layer * E + be[i], 0, 0)
    rows = lambda i, be, nv, ne: (i, 0)
    w_glu, w_lin = w_glu.reshape(L * E, D, F), w_lin.reshape(L * E, D, F)
    w_out = w_out.reshape(L * E, F, D)
    b_glu, b_lin = b_glu.reshape(L * E, 1, F), b_lin.reshape(L * E, 1, F)
    b_out = b_out.reshape(L * E, 1, D)
    hbm = pl.BlockSpec(memory_space=pl.ANY)
    grid_spec = pltpu.PrefetchScalarGridSpec(
        num_scalar_prefetch=3,
        grid=(n_blocks,),
        in_specs=[pl.BlockSpec((blk, pw), rows),
                  hbm, pl.BlockSpec((1, 1, F), bmap),
                  hbm, pl.BlockSpec((1, 1, F), bmap),
                  hbm, pl.BlockSpec((1, 1, D), bmap)],
        out_specs=pl.BlockSpec((blk, pw), rows),
        scratch_shapes=[pltpu.VMEM((D, F), F32), pltpu.VMEM((D, F), F32),
                        pltpu.VMEM((F, D), F32),
                        pltpu.VMEM((D, F), BF16), pltpu.VMEM((D, F), BF16),
                        pltpu.VMEM((F, D), BF16),
                        pltpu.SemaphoreType.DMA((3,))],
    )
    return pl.pallas_call(
        functools.partial(_expert_kernel, expert_base=layer * E),
        grid_spec=grid_spec,
        out_shape=jax.ShapeDtypeStruct((P, pw), xs.dtype),
        compiler_params=_cparams(("arbitrary",)),
        name="moe_experts",
    )(blk_e, n_valid, next_e, xs, w_glu, b_glu, w_lin, b_lin, w_out, b_out)


def _combine_kernel(ys_ref, tg_ref, x_ref, gate_ref, *rest, tail):
    half = x_ref.shape[1] // 2
    tg = tg_ref[...]
    lo_sum = hi_sum = None
    for k in range(ys_ref.shape[0]):
        lo, hi = _unpack_halves(ys_ref[k])
        gk = tg[:, k:k + 1]
        lo_sum = gk * lo if lo_sum is None else lo_sum + gk * lo
        hi_sum = gk * hi if hi_sum is None else hi_sum + gk * hi
    x_lo = x_ref[:, :half] + gate_ref[0, :, :half] * lo_sum
    x_hi = x_ref[:, half:] + gate_ref[0, :, half:] * hi_sum
    ms = (jnp.sum(x_lo * x_lo, axis=-1, keepdims=True)
          + jnp.sum(x_hi * x_hi, axis=-1, keepdims=True)) / x_ref.shape[1]
    inv = lax.rsqrt(ms + RMS_EPS)
    if tail == "final":
        g_ref, o_ref = rest
        o_ref[:, :half] = x_lo * inv * g_ref[:, :half]
        o_ref[:, half:] = x_hi * inv * g_ref[:, half:]
    else:
        g_ref, sc_ref, sh_ref, o_ref, h_ref = rest
        o_ref[:, :half] = x_lo
        o_ref[:, half:] = x_hi
        for cols, xs in ((slice(0, half), x_lo), (slice(half, 2 * half), x_hi)):
            h = xs * inv * g_ref[:, cols] * (1.0 + sc_ref[0, :, cols]) + sh_ref[0, :, cols]
            h_ref[:, cols] = h.astype(BF16)


def _combine(ys, tok_gate, x, gate, seq, *, final_g=None, next_norm=None, tm=512):
    K, T, pw = ys.shape
    D = x.shape[1]
    bmap = lambda i: ((i * tm) // seq, 0, 0)
    rows = pl.BlockSpec((tm, D), lambda i: (i, 0))
    vec = pl.BlockSpec((1, D), lambda i: (0, 0))
    per_batch = pl.BlockSpec((1, 1, D), bmap)
    in_specs = [pl.BlockSpec((K, tm, pw), lambda i: (0, i, 0)),
                pl.BlockSpec((tm, K), lambda i: (i, 0)), rows, per_batch]
    args = [ys, tok_gate, x, gate]
    if final_g is not None:
        tail = "final"
        in_specs.append(vec)
        args.append(final_g)
        out_specs, out_shape = rows, jax.ShapeDtypeStruct((T, D), F32)
    else:
        tail = "next"
        in_specs += [vec, per_batch, per_batch]
        args += list(next_norm)
        out_specs = [rows, rows]
        out_shape = [jax.ShapeDtypeStruct((T, D), F32), jax.ShapeDtypeStruct((T, D), BF16)]
    return pl.pallas_call(
        functools.partial(_combine_kernel, tail=tail),
        grid=(T // tm,),
        in_specs=in_specs,
        out_specs=out_specs,
        out_shape=out_shape,
        compiler_params=_cparams(("parallel",)),
        name="moe_combine",
    )(*args)


MOE_BLOCK = 512


def _slots_kernel(start_ref, idx_ref, rank_ref, o_ref):
    idx = idx_ref[...]
    acc = rank_ref[...]
    for e in range(start_ref.shape[0]):
        acc = acc + jnp.where(idx == e, start_ref[e], 0)
    o_ref[...] = acc


def _slots(idx_t, rank_t, group_start):
    full = pl.BlockSpec(idx_t.shape, lambda i, gs: (0, 0))
    return pl.pallas_call(
        _slots_kernel,
        grid_spec=pltpu.PrefetchScalarGridSpec(num_scalar_prefetch=1, grid=(1,),
                                               in_specs=[full, full], out_specs=full),
        out_shape=jax.ShapeDtypeStruct(idx_t.shape, I32),
        compiler_params=_cparams(("arbitrary",)),
        name="moe_slots",
    )(group_start, idx_t, rank_t)


def _dispatch_plan(idx_t, rank_t, counts, blk):
    K, T = idx_t.shape
    E = counts.shape[0]
    padded = (counts + blk - 1) // blk * blk
    pad_end = jnp.cumsum(padded)
    pad_start = pad_end - padded
    experts = jnp.arange(E, dtype=I32)
    slots = _slots(idx_t, rank_t, pad_start.astype(I32))
    n_blocks = -(-(K * T + E * (blk - 1)) // blk)
    blk_start = jnp.arange(n_blocks, dtype=I32) * blk
    blk_e = jnp.minimum(jnp.sum(pad_end[None, :] <= blk_start[:, None], axis=1), E - 1).astype(I32)
    n_valid = jnp.clip(pad_start[blk_e] + counts[blk_e] - blk_start, 0, blk).astype(I32)
    later = jnp.where(counts > 0, experts, E)
    after = lax.cummin(jnp.concatenate([later[1:], jnp.full((1,), E, I32)]), reverse=True)
    next_e = jnp.where(after < E, after, -1).astype(I32)[blk_e]
    return slots, blk_e, n_valid, next_e, n_blocks * blk


def _mixer_out_moe(y, w_mix, x, mix_gate, g, scale, shift, gate, r_w, r_b, layer,
                   w_glu, b_glu, w_lin, b_lin, w_out, b_out, seq, **tail):
    T, D = x.shape
    x, hp, idx_t, gate_t, rank_t, counts = _mixer_out_route(
        y, w_mix, x, mix_gate, g, scale, shift, r_w, r_b, seq)
    slots, blk_e, n_valid, next_e, n_rows = _dispatch_plan(idx_t, rank_t, counts[:, 0],
                                                           MOE_BLOCK)
    xs = _sc_scatter_rows(hp, slots, n_rows)
    yp = _experts(xs, blk_e, n_valid, next_e, layer, w_glu, b_glu, w_lin, b_lin, w_out, b_out,
                  blk=MOE_BLOCK)
    ys = _sc_gather(yp, slots.reshape(TOP_K * T)).reshape(TOP_K, T, yp.shape[1])
    return _combine(ys, jnp.transpose(gate_t), x, gate, seq, **tail)


def kernel(x, c, norm_g, ada_w, final_norm_g, sg_w_in, sg_ln_g, sg_ln_b, sg_w_s, sg_b_s, sg_w_out, pool_w_in, pool_w_grp, pool_ls, pool_w_out, hgrn_w_in, hgrn_lb_logits, hgrn_gnorm_g, hgrn_w_out, router_w, router_b, expert_w_glu, expert_b_glu, expert_w_lin, expert_b_lin, expert_w_out, expert_b_out):
    B, S, D = x.shape
    depth = norm_g.shape[0]
    T = B * S
    xf = x.reshape(T, D)

    ada_w = ada_w.reshape(depth * 2, D, 3 * D)

    def mod_parts(layer, sub):
        m = _ada_mod(c, ada_w, layer * 2 + sub)[:, None, :]
        return m[..., :D], m[..., D:2 * D], m[..., 2 * D:]

    def mixer_norm(layer, shift, scale):
        return norm_g[layer, 0][None, :], scale, shift

    lb_cum = jnp.cumsum(jax.nn.softmax(hgrn_lb_logits.astype(F32), axis=0), axis=0)
    lower_bounds = lb_cum - lb_cum[0]

    shift, scale, mix_gate = mod_parts(0, 0)
    src, norm = xf, mixer_norm(0, shift, scale)
    for layer in range(depth):
        kind, slot = layer % 3, layer // 3
        if kind == 0:
            y = _spatial_mixer(src, norm, sg_w_in[slot].astype(BF16), sg_ln_g[slot][None, :],
                               sg_ln_b[slot][None, :], sg_w_s[slot], sg_b_s[slot], S)
            w_out = sg_w_out[slot]
        elif kind == 1:
            y = _pool_mixer(src, norm, pool_w_in[slot].astype(BF16),
                            pool_w_grp[slot].astype(BF16), pool_ls[slot][None, :], S)
            w_out = pool_w_out[slot]
        else:
            proj, f_pre = _norm_matmul(src, norm, hgrn_w_in[slot].astype(BF16), S, act=None,
                                       out_dtype=BF16, f32_col=1, tn=D)
            y = _hgrn_mix(proj, f_pre, lower_bounds[layer][None, :],
                          hgrn_gnorm_g[slot][None, :], B, S)
            w_out = hgrn_w_out[slot]

        moe_shift, moe_scale, moe_gate = mod_parts(layer, 1)
        moe_args = (y, w_out.astype(BF16), xf, mix_gate, norm_g[layer, 1][None, :],
                    moe_scale, moe_shift, moe_gate, router_w[layer], router_b[layer], layer,
                    expert_w_glu, expert_b_glu, expert_w_lin, expert_b_lin,
                    expert_w_out, expert_b_out, S)
        if layer == depth - 1:
            xf = _mixer_out_moe(*moe_args, final_g=final_norm_g[None, :])
        else:
            shift, scale, mix_gate = mod_parts(layer + 1, 0)
            xf, src = _mixer_out_moe(*moe_args, next_norm=mixer_norm(layer + 1, shift, scale))
            norm = None

    return xf.reshape(B, S, D)
```
